```python
import math
import jax, jax.numpy as jnp
from jax import lax
import numpy as np

D_MODEL = 2048
BATCH = 8
SEQ = 4096
DEPTH = 4

HEAD_DIM = 128
MIX_WIDTH = D_MODEL
A_HEADS = 4
A_NOPE = 128
A_ROPE = 64
A_V = 128
Q_LORA_RANK = 448
KV_LORA_RANK = 512
B_HEADS = 6
B_KV_HEADS = 2
B_GROUP = B_HEADS // B_KV_HEADS
GRID_W = 64
C_HEADS = 6
C_BRANCHES = ((128, 1), (512, 4), (2048, 16))
D_FF = 4 * D_MODEL
ROPE_THETA = 10000.0
Q_BLOCK = 128
EPS = 1e-6
NEG_INF = -1e30
IN_SIZES = (Q_LORA_RANK, KV_LORA_RANK, A_ROPE,
            B_HEADS * HEAD_DIM, B_KV_HEADS * HEAD_DIM, B_KV_HEADS * HEAD_DIM,
            C_HEADS * HEAD_DIM, C_HEADS * HEAD_DIM, C_HEADS * HEAD_DIM)
IN_WIDTH = sum(IN_SIZES)
OUT_SIZES = (A_HEADS * A_V, B_HEADS * HEAD_DIM, C_HEADS * HEAD_DIM)

kernel_name = "hymba_style_mla_gqa2d_dilated_encoder"


def _rms(x, g=None):
    xf = x.astype(jnp.float32)
    y = xf * lax.rsqrt(jnp.mean(xf * xf, axis=-1, keepdims=True) + EPS)
    if g is not None:
        y = y * g.astype(jnp.float32)
    return y.astype(x.dtype)


def _rope_angles(pos, dim):
    inv = jnp.power(ROPE_THETA, -jnp.arange(0, dim, 2, dtype=jnp.float32) / dim)
    ang = pos.astype(jnp.float32)[:, None] * inv[None, :]
    return jnp.cos(ang), jnp.sin(ang)


def _apply_rope(x, cs):
    cos, sin = cs
    cos = cos[:, None, :]
    sin = sin[:, None, :]
    xf = x.astype(jnp.float32)
    half = x.shape[-1] // 2
    x1, x2 = xf[..., :half], xf[..., half:]
    return jnp.concatenate([x1 * cos - x2 * sin, x1 * sin + x2 * cos], axis=-1).astype(x.dtype)


def _split_cols(a, sizes):
    out = []
    start = 0
    for s in sizes:
        out.append(a[..., start:start + s])
        start += s
    return out


def _dense_attn_blocked(q, k, v, scale):
    B, S, Hkv, G, Dk = q.shape
    nb = S // Q_BLOCK
    qb = q.reshape(B, nb, Q_BLOCK, Hkv, G, Dk).transpose(1, 0, 2, 3, 4, 5)

    def one_block(qblk):
        s = jnp.einsum('bqhgd,bkhd->bhgqk', qblk, k).astype(jnp.float32) * scale
        p = jax.nn.softmax(s, axis=-1)
        return jnp.einsum('bhgqk,bkhd->bqhgd', p.astype(v.dtype), v)

    ob = lax.map(one_block, qb)
    return ob.transpose(1, 0, 2, 3, 4, 5).reshape(B, S, Hkv, G, v.shape[-1])


def _band_attn(q, k, v, half, scale):
    N, L, H, D = q.shape
    blk = half
    nb = -(-L // blk)
    pad = nb * blk - L
    qp = jnp.pad(q, ((0, 0), (0, pad), (0, 0), (0, 0)))
    kp = jnp.pad(k, ((0, 0), (blk, pad + blk), (0, 0), (0, 0))).reshape(N, nb + 2, blk, H, D)
    vp = jnp.pad(v, ((0, 0), (blk, pad + blk), (0, 0), (0, 0))).reshape(N, nb + 2, blk, H, D)
    kwin = jnp.concatenate([kp[:, :-2], kp[:, 1:-1], kp[:, 2:]], axis=2)
    vwin = jnp.concatenate([vp[:, :-2], vp[:, 1:-1], vp[:, 2:]], axis=2)
    qb = qp.reshape(N, nb, blk, H, D)
    s = jnp.einsum('nbqhd,nbkhd->nbhqk', qb, kwin).astype(jnp.float32) * scale
    qpos = jnp.arange(nb)[:, None] * blk + jnp.arange(blk)[None, :]
    kpos = (jnp.arange(nb)[:, None] - 1) * blk + jnp.arange(3 * blk)[None, :]
    rel = kpos[:, None, :] - qpos[:, :, None]
    mask = (jnp.abs(rel) <= half) & (kpos[:, None, :] >= 0) & (kpos[:, None, :] < L)
    s = jnp.where(mask[None, :, None, :, :], s, NEG_INF)
    m = jnp.max(s, axis=-1, keepdims=True)
    e = jnp.exp(s - m)
    den = jnp.sum(e, axis=-1, keepdims=True)
    o = jnp.einsum('nbhqk,nbkhd->nbqhd', (e / den).astype(v.dtype), vwin)
    lse = (m + jnp.log(den))[..., 0]
    o = o.reshape(N, nb * blk, H, D)[:, :L]
    lse = lse.transpose(0, 1, 3, 2).reshape(N, nb * blk, H)[:, :L]
    return o, lse


def _dilated_mixture(q, k, v, scale):
    B, S, H, D = q.shape
    outs = []
    lses = []
    for window, dil in C_BRANCHES:
        half = window // (2 * dil)
        L = S // dil

        def to_sub(t):
            return t.reshape(B, L, dil, H, D).transpose(0, 2, 1, 3, 4).reshape(B * dil, L, H, D)

        o, lse = _band_attn(to_sub(q), to_sub(k), to_sub(v), half, scale)
        outs.append(o.reshape(B, dil, L, H, D).transpose(0, 2, 1, 3, 4).reshape(B, S, H, D))
        lses.append(lse.reshape(B, dil, L, H).transpose(0, 2, 1, 3).reshape(B, S, H))
    w = jax.nn.softmax(jnp.stack(lses, axis=0), axis=0)
    o = jnp.stack(outs, axis=0).astype(jnp.float32)
    return jnp.sum(w[..., None] * o, axis=0).astype(q.dtype)


def _fwd_setup_inputs(seed: int = 0) -> dict:
    key = jax.random.key(seed)
    ks = jax.random.split(key, 16)
    f32 = jnp.float32

    def nrm(k, shape, scale):
        return jax.random.normal(k, shape, f32) * scale

    def gain(k, shape):
        return 1.0 + 0.02 * jax.random.normal(k, shape, f32)

    return {
        "x": jax.random.normal(ks[0], (BATCH, SEQ, D_MODEL), f32),
        "ln1_g": gain(ks[1], (DEPTH, D_MODEL)),
        "w_in": nrm(ks[2], (DEPTH, D_MODEL, IN_WIDTH), D_MODEL ** -0.5),
        "g_q_a": gain(ks[3], (DEPTH, Q_LORA_RANK)),
        "w_uq": nrm(ks[4], (DEPTH, Q_LORA_RANK, A_HEADS * (A_NOPE + A_ROPE)), Q_LORA_RANK ** -0.5),
        "g_kv_a": gain(ks[5], (DEPTH, KV_LORA_RANK)),
        "w_ukv": nrm(ks[6], (DEPTH, KV_LORA_RANK, A_HEADS * (A_NOPE + A_V)), KV_LORA_RANK ** -0.5),
        "g_qn_b": gain(ks[7], (DEPTH, HEAD_DIM)),
        "g_kn_b": gain(ks[8], (DEPTH, HEAD_DIM)),
        "g_out": gain(ks[9], (DEPTH, MIX_WIDTH)),
        "w_out": nrm(ks[10], (DEPTH, MIX_WIDTH, D_MODEL), MIX_WIDTH ** -0.5),
        "ln2_g": gain(ks[11], (DEPTH, D_MODEL)),
        "w_ff1": nrm(ks[12], (DEPTH, D_MODEL, D_FF), D_MODEL ** -0.5),
        "w_ff2": nrm(ks[13], (DEPTH, D_FF, D_MODEL), D_FF ** -0.5),
        "ln_f_g": gain(ks[14], (D_MODEL,)),
    }


def _fwd_reference(x, ln1_g, w_in, g_q_a, w_uq, g_kv_a, w_ukv, g_qn_b, g_kn_b, g_out, w_out,
              ln2_g, w_ff1, w_ff2, ln_f_g):
    B, S, _ = x.shape
    ROWS = S // GRID_W
    pos = jnp.arange(S, dtype=jnp.float32)
    row = jnp.repeat(jnp.arange(ROWS, dtype=jnp.float32), GRID_W)
    col = jnp.tile(jnp.arange(GRID_W, dtype=jnp.float32), ROWS)
    cs_a = _rope_angles(pos, A_ROPE)
    cs_c = _rope_angles(pos, HEAD_DIM)
    cs_row = _rope_angles(row, HEAD_DIM // 2)
    cs_col = _rope_angles(col, HEAD_DIM // 2)
    scale_a = 1.0 / math.sqrt(A_NOPE + A_ROPE)
    scale_h = 1.0 / math.sqrt(HEAD_DIM)

    def axial(t):
        hd = HEAD_DIM // 2
        return jnp.concatenate([_apply_rope(t[..., :hd], cs_row), _apply_rope(t[..., hd:], cs_col)], axis=-1)

    for l in range(DEPTH):
        h = _rms(x, ln1_g[l])
        proj = h @ w_in[l]
        a_cq, a_ckv, a_kr, b_q, b_k, b_v, c_q, c_k, c_v = _split_cols(proj, IN_SIZES)

        qa = (_rms(a_cq, g_q_a[l]) @ w_uq[l]).reshape(B, S, A_HEADS, A_NOPE + A_ROPE)
        qa = jnp.concatenate([qa[..., :A_NOPE], _apply_rope(qa[..., A_NOPE:], cs_a)], axis=-1)
        kva = (_rms(a_ckv, g_kv_a[l]) @ w_ukv[l]).reshape(B, S, A_HEADS, A_NOPE + A_V)
        k_pe = _apply_rope(a_kr[:, :, None, :], cs_a)
        ka = jnp.concatenate([kva[..., :A_NOPE], jnp.broadcast_to(k_pe, (B, S, A_HEADS, A_ROPE))], axis=-1)
        va = kva[..., A_NOPE:]
        o_a = _dense_attn_blocked(qa[:, :, :, None, :], ka, va, scale_a).reshape(B, S, OUT_SIZES[0])

        qb = axial(_rms(b_q.reshape(B, S, B_HEADS, HEAD_DIM), g_qn_b[l]))
        kb = axial(_rms(b_k.reshape(B, S, B_KV_HEADS, HEAD_DIM), g_kn_b[l]))
        vb = b_v.reshape(B, S, B_KV_HEADS, HEAD_DIM)
        qb = qb.reshape(B, S, B_KV_HEADS, B_GROUP, HEAD_DIM)
        o_b = _dense_attn_blocked(qb, kb, vb, scale_h).reshape(B, S, OUT_SIZES[1])

        qc = _apply_rope(c_q.reshape(B, S, C_HEADS, HEAD_DIM), cs_c)
        kc = _apply_rope(c_k.reshape(B, S, C_HEADS, HEAD_DIM), cs_c)
        vc = c_v.reshape(B, S, C_HEADS, HEAD_DIM)
        o_c = _dilated_mixture(qc, kc, vc, scale_h).reshape(B, S, OUT_SIZES[2])

        mixed = jnp.concatenate([_rms(o_a), _rms(o_b), _rms(o_c)], axis=-1) * g_out[l]
        x = x + mixed @ w_out[l]

        u = jnp.square(jax.nn.relu(_rms(x, ln2_g[l]) @ w_ff1[l]))
        x = x + u @ w_ff2[l]

    return _rms(x, ln_f_g)


import jax as _jax
import jax.numpy as _jnp

TWIN_FORMAT = 'train_step'
FWD_PARAMS = ['x', 'ln1_g', 'w_in', 'g_q_a', 'w_uq', 'g_kv_a', 'w_ukv', 'g_qn_b', 'g_kn_b', 'g_out', 'w_out', 'ln2_g', 'w_ff1', 'w_ff2', 'ln_f_g']
TWIN_WEIGHTS = ['ln1_g', 'w_in', 'g_q_a', 'w_uq', 'g_kv_a', 'w_ukv', 'g_qn_b', 'g_kn_b', 'g_out', 'w_out', 'ln2_g', 'w_ff1', 'w_ff2', 'ln_f_g']
TWIN_DIFF_INPUT = 'x'
TWIN_INPUTS = ['x', 'ln1_g', 'w_in', 'g_q_a', 'w_uq', 'g_kv_a', 'w_ukv', 'g_qn_b', 'g_kn_b', 'g_out', 'w_out', 'ln2_g', 'w_ff1', 'w_ff2', 'ln_f_g', 'loss_target', 'm_ln1_g', 'm_w_in', 'm_g_q_a', 'm_w_uq', 'm_g_kv_a', 'm_w_ukv', 'm_g_qn_b', 'm_g_kn_b', 'm_g_out', 'm_w_out', 'm_ln2_g', 'm_w_ff1', 'm_w_ff2', 'm_ln_f_g', 'v_ln1_g', 'v_w_in', 'v_g_q_a', 'v_w_uq', 'v_g_kv_a', 'v_w_ukv', 'v_g_qn_b', 'v_g_kn_b', 'v_g_out', 'v_w_out', 'v_ln2_g', 'v_w_ff1', 'v_w_ff2', 'v_ln_f_g']
TWIN_OUTPUTS = ['loss', 'grad_x', 'grad_ln1_g', 'grad_w_in', 'grad_g_q_a', 'grad_w_uq', 'grad_g_kv_a', 'grad_w_ukv', 'grad_g_qn_b', 'grad_g_kn_b', 'grad_g_out', 'grad_w_out', 'grad_ln2_g', 'grad_w_ff1', 'grad_w_ff2', 'grad_ln_f_g', 'delta_ln1_g', 'delta_w_in', 'delta_g_q_a', 'delta_w_uq', 'delta_g_kv_a', 'delta_w_ukv', 'delta_g_qn_b', 'delta_g_kn_b', 'delta_g_out', 'delta_w_out', 'delta_ln2_g', 'delta_w_ff1', 'delta_w_ff2', 'delta_ln_f_g', 'new_m_ln1_g', 'new_m_w_in', 'new_m_g_q_a', 'new_m_w_uq', 'new_m_g_kv_a', 'new_m_w_ukv', 'new_m_g_qn_b', 'new_m_g_kn_b', 'new_m_g_out', 'new_m_w_out', 'new_m_ln2_g', 'new_m_w_ff1', 'new_m_w_ff2', 'new_m_ln_f_g', 'new_v_ln1_g', 'new_v_w_in', 'new_v_g_q_a', 'new_v_w_uq', 'new_v_g_kv_a', 'new_v_w_ukv', 'new_v_g_qn_b', 'new_v_g_kn_b', 'new_v_g_out', 'new_v_w_out', 'new_v_ln2_g', 'new_v_w_ff1', 'new_v_w_ff2', 'new_v_ln_f_g']
TWIN_LEAF_KINDS = {'loss': 'loss', 'grad_x': 'grad_x', 'grad_ln1_g': 'grad_w', 'grad_w_in': 'grad_w', 'grad_g_q_a': 'grad_w', 'grad_w_uq': 'grad_w', 'grad_g_kv_a': 'grad_w', 'grad_w_ukv': 'grad_w', 'grad_g_qn_b': 'grad_w', 'grad_g_kn_b': 'grad_w', 'grad_g_out': 'grad_w', 'grad_w_out': 'grad_w', 'grad_ln2_g': 'grad_w', 'grad_w_ff1': 'grad_w', 'grad_w_ff2': 'grad_w', 'grad_ln_f_g': 'grad_w', 'delta_ln1_g': 'delta_w', 'delta_w_in': 'delta_w', 'delta_g_q_a': 'delta_w', 'delta_w_uq': 'delta_w', 'delta_g_kv_a': 'delta_w', 'delta_w_ukv': 'delta_w', 'delta_g_qn_b': 'delta_w', 'delta_g_kn_b': 'delta_w', 'delta_g_out': 'delta_w', 'delta_w_out': 'delta_w', 'delta_ln2_g': 'delta_w', 'delta_w_ff1': 'delta_w', 'delta_w_ff2': 'delta_w', 'delta_ln_f_g': 'delta_w', 'new_m_ln1_g': 'new_m', 'new_m_w_in': 'new_m', 'new_m_g_q_a': 'new_m', 'new_m_w_uq': 'new_m', 'new_m_g_kv_a': 'new_m', 'new_m_w_ukv': 'new_m', 'new_m_g_qn_b': 'new_m', 'new_m_g_kn_b': 'new_m', 'new_m_g_out': 'new_m', 'new_m_w_out': 'new_m', 'new_m_ln2_g': 'new_m', 'new_m_w_ff1': 'new_m', 'new_m_w_ff2': 'new_m', 'new_m_ln_f_g': 'new_m', 'new_v_ln1_g': 'new_v', 'new_v_w_in': 'new_v', 'new_v_g_q_a': 'new_v', 'new_v_w_uq': 'new_v', 'new_v_g_kv_a': 'new_v', 'new_v_w_ukv': 'new_v', 'new_v_g_qn_b': 'new_v', 'new_v_g_kn_b': 'new_v', 'new_v_g_out': 'new_v', 'new_v_w_out': 'new_v', 'new_v_ln2_g': 'new_v', 'new_v_w_ff1': 'new_v', 'new_v_w_ff2': 'new_v', 'new_v_ln_f_g': 'new_v'}


def _forward(args):
    return _fwd_reference(*[args[k] for k in FWD_PARAMS])


def _output_shape():
    def fwd():
        inp = _fwd_setup_inputs(0)
        return _fwd_reference(*[inp[k] for k in FWD_PARAMS])
    out = _jax.eval_shape(fwd)
    return out.shape, out.dtype

N_MICROBATCH = 1
ADAM_LR = 0.001
ADAM_B1 = 0.9
ADAM_B2 = 0.999
ADAM_EPS = 1e-08
ADAM_WD = 0.01
ADAM_STEP = 10
PER_EXAMPLE_BATCH_AXIS = {'x': 0, 'loss_target': 0}
SHARED_INPUTS = []
_WEIGHT_DTYPES = {'ln1_g': _jnp.float32, 'w_in': _jnp.float32, 'g_q_a': _jnp.float32, 'w_uq': _jnp.float32, 'g_kv_a': _jnp.float32, 'w_ukv': _jnp.float32, 'g_qn_b': _jnp.float32, 'g_kn_b': _jnp.float32, 'g_out': _jnp.float32, 'w_out': _jnp.float32, 'ln2_g': _jnp.float32, 'w_ff1': _jnp.float32, 'w_ff2': _jnp.float32, 'ln_f_g': _jnp.float32}
MOMENT_SCALE = {'ln1_g': 1.022158e-01, 'w_in': 6.611890e-02, 'g_q_a': 3.315800e-02, 'w_uq': 2.366614e-02, 'g_kv_a': 1.060648e-01, 'w_ukv': 6.866599e-02, 'g_qn_b': 8.227310e-02, 'g_kn_b': 8.004315e-02, 'g_out': 9.714859e-02, 'w_out': 9.565670e-02, 'ln2_g': 6.438069e-02, 'w_ff1': 3.229023e-02, 'w_ff2': 1.001119e-01, 'ln_f_g': 1.732708e+01}


def _to_microbatches(a, axis):
    t = _jnp.moveaxis(a, axis, 0)
    t = t.reshape((N_MICROBATCH, t.shape[0] // N_MICROBATCH) + t.shape[1:])
    return _jnp.moveaxis(t, 1, axis + 1)


def setup_inputs(seed: int = 0) -> dict:
    inp = _fwd_setup_inputs(seed)
    key = _jax.random.fold_in(_jax.random.key(seed), 7919)
    shape, _ = _output_shape()
    out = dict(inp)
    out["loss_target"] = _jax.random.normal(_jax.random.fold_in(key, 0), shape, _jnp.float32)
    for i, name in enumerate(TWIN_WEIGHTS):
        w = inp[name].astype(_jnp.float32)
        if MOMENT_SCALE is None:
            s = _jnp.sqrt(_jnp.mean(_jnp.square(w)) + 1e-30)
        else:
            s = MOMENT_SCALE[name]
        km, kv = _jax.random.split(_jax.random.fold_in(key, i + 1))
        out[name] = w
        out["m_" + name] = s * _jax.random.normal(km, w.shape, _jnp.float32)
        out["v_" + name] = (s * s) * _jax.random.uniform(kv, w.shape, _jnp.float32, 0.5, 1.5)
    if N_MICROBATCH > 1:
        for name, axis in PER_EXAMPLE_BATCH_AXIS.items():
            out[name] = _to_microbatches(out[name], axis)
    return {'x': out['x'], 'ln1_g': out['ln1_g'], 'w_in': out['w_in'], 'g_q_a': out['g_q_a'], 'w_uq': out['w_uq'], 'g_kv_a': out['g_kv_a'], 'w_ukv': out['w_ukv'], 'g_qn_b': out['g_qn_b'], 'g_kn_b': out['g_kn_b'], 'g_out': out['g_out'], 'w_out': out['w_out'], 'ln2_g': out['ln2_g'], 'w_ff1': out['w_ff1'], 'w_ff2': out['w_ff2'], 'ln_f_g': out['ln_f_g'], 'loss_target': out['loss_target'], 'm_ln1_g': out['m_ln1_g'], 'm_w_in': out['m_w_in'], 'm_g_q_a': out['m_g_q_a'], 'm_w_uq': out['m_w_uq'], 'm_g_kv_a': out['m_g_kv_a'], 'm_w_ukv': out['m_w_ukv'], 'm_g_qn_b': out['m_g_qn_b'], 'm_g_kn_b': out['m_g_kn_b'], 'm_g_out': out['m_g_out'], 'm_w_out': out['m_w_out'], 'm_ln2_g': out['m_ln2_g'], 'm_w_ff1': out['m_w_ff1'], 'm_w_ff2': out['m_w_ff2'], 'm_ln_f_g': out['m_ln_f_g'], 'v_ln1_g': out['v_ln1_g'], 'v_w_in': out['v_w_in'], 'v_g_q_a': out['v_g_q_a'], 'v_w_uq': out['v_w_uq'], 'v_g_kv_a': out['v_g_kv_a'], 'v_w_ukv': out['v_w_ukv'], 'v_g_qn_b': out['v_g_qn_b'], 'v_g_kn_b': out['v_g_kn_b'], 'v_g_out': out['v_g_out'], 'v_w_out': out['v_w_out'], 'v_ln2_g': out['v_ln2_g'], 'v_w_ff1': out['v_w_ff1'], 'v_w_ff2': out['v_w_ff2'], 'v_ln_f_g': out['v_ln_f_g']}


def _loss(weights, diff, rest, loss_target):
    with _jax.named_scope("forward"):
        args = {**rest, TWIN_DIFF_INPUT: diff, **{k: w.astype(_WEIGHT_DTYPES[k]) for k, w in weights.items()}}
        y = _forward(args)
    with _jax.named_scope("loss_head"):
        err = _jnp.square(y.astype(_jnp.float32) - loss_target)
        return 0.5 * _jnp.sum(_jnp.mean(err, axis=-1)) if err.ndim else 0.5 * err


def _adamw(w, g, m, v):
    m = ADAM_B1 * m + (1.0 - ADAM_B1) * g
    v = ADAM_B2 * v + (1.0 - ADAM_B2) * _jnp.square(g)
    m_hat = m / (1.0 - ADAM_B1 ** ADAM_STEP)
    v_hat = v / (1.0 - ADAM_B2 ** ADAM_STEP)
    delta = -ADAM_LR * (m_hat / (_jnp.sqrt(v_hat) + ADAM_EPS) + ADAM_WD * w)
    return delta, m, v


def reference(x, ln1_g, w_in, g_q_a, w_uq, g_kv_a, w_ukv, g_qn_b, g_kn_b, g_out, w_out, ln2_g, w_ff1, w_ff2, ln_f_g, loss_target, m_ln1_g, m_w_in, m_g_q_a, m_w_uq, m_g_kv_a, m_w_ukv, m_g_qn_b, m_g_kn_b, m_g_out, m_w_out, m_ln2_g, m_w_ff1, m_w_ff2, m_ln_f_g, v_ln1_g, v_w_in, v_g_q_a, v_w_uq, v_g_kv_a, v_w_ukv, v_g_qn_b, v_g_kn_b, v_g_out, v_w_out, v_ln2_g, v_w_ff1, v_w_ff2, v_ln_f_g):
    given = dict(x=x, ln1_g=ln1_g, w_in=w_in, g_q_a=g_q_a, w_uq=w_uq, g_kv_a=g_kv_a, w_ukv=w_ukv, g_qn_b=g_qn_b, g_kn_b=g_kn_b, g_out=g_out, w_out=w_out, ln2_g=ln2_g, w_ff1=w_ff1, w_ff2=w_ff2, ln_f_g=ln_f_g, loss_target=loss_target, m_ln1_g=m_ln1_g, m_w_in=m_w_in, m_g_q_a=m_g_q_a, m_w_uq=m_w_uq, m_g_kv_a=m_g_kv_a, m_w_ukv=m_w_ukv, m_g_qn_b=m_g_qn_b, m_g_kn_b=m_g_kn_b, m_g_out=m_g_out, m_w_out=m_w_out, m_ln2_g=m_ln2_g, m_w_ff1=m_w_ff1, m_w_ff2=m_w_ff2, m_ln_f_g=m_ln_f_g, v_ln1_g=v_ln1_g, v_w_in=v_w_in, v_g_q_a=v_g_q_a, v_w_uq=v_w_uq, v_g_kv_a=v_g_kv_a, v_w_ukv=v_w_ukv, v_g_qn_b=v_g_qn_b, v_g_kn_b=v_g_kn_b, v_g_out=v_g_out, v_w_out=v_w_out, v_ln2_g=v_ln2_g, v_w_ff1=v_w_ff1, v_w_ff2=v_w_ff2, v_ln_f_g=v_ln_f_g)
    weights = {n: given[n] for n in TWIN_WEIGHTS}
    shared = {n: given[n] for n in SHARED_INPUTS}
    per_example = {n: given[n] for n in ['x']}
    grad_fn = _jax.value_and_grad(_loss, argnums=(0, 1))

    def one_microbatch(ex, loss_target):
        ex = dict(ex)
        diff = ex.pop(TWIN_DIFF_INPUT)
        return grad_fn(weights, diff, {**shared, **ex}, loss_target)

    if N_MICROBATCH == 1:
        loss, (grad_w, grad_x) = one_microbatch(per_example, given["loss_target"])
    else:
        def body(carry, xs):
            loss_sum, grad_sum = carry
            l_k, (gw_k, gx_k) = one_microbatch(xs[0], xs[1])
            with _jax.named_scope("update"):
                return (loss_sum + l_k, _jax.tree.map(_jnp.add, grad_sum, gw_k)), gx_k

        init = (_jnp.zeros((), _jnp.float32), _jax.tree.map(_jnp.zeros_like, weights))
        (loss, grad_w), grad_x = _jax.lax.scan(body, init, (per_example, given["loss_target"]))
    with _jax.named_scope("update"):
        delta_w, new_m, new_v = {}, {}, {}
        for n in TWIN_WEIGHTS:
            delta_w[n], new_m[n], new_v[n] = _adamw(weights[n], grad_w[n], given["m_" + n], given["v_" + n])
    return (loss, grad_x, *[grad_w[n] for n in TWIN_WEIGHTS], *[delta_w[n] for n in TWIN_WEIGHTS],
            *[new_m[n] for n in TWIN_WEIGHTS], *[new_v[n] for n in TWIN_WEIGHTS])
```

```python
import functools
import math

import jax
import jax.numpy as jnp
from jax import lax
from jax.experimental import pallas as pl
from jax.experimental.pallas import tpu as pltpu

D_MODEL = 2048
D_FF = 8192
EPS = 1e-6
NEG_INF = -1e30
Q_LORA = 448
ROPE_THETA = 10000.0
GRID_W = 64
DILATIONS = (1, 4, 16)
BAND_HALF = 64
SCALE_A = 1.0 / math.sqrt(192.0)
SCALE_H = 1.0 / math.sqrt(128.0)
ADAM_LR, ADAM_B1, ADAM_B2, ADAM_EPS, ADAM_WD, ADAM_STEP = 0.001, 0.9, 0.999, 1e-08, 0.01, 10

CDT = jnp.bfloat16
F32 = jnp.float32
LANES = 128
VMEM_LIMIT = 56 * 1024 * 1024

PROJ_W = 4608

NN = ((1,), (0,))
NT = ((1,), (1,))
TN = ((0,), (0,))

BIG = ("w_in", "w_uq", "w_ukv", "w_out", "w_ff1", "w_ff2")
SHARD_SHAPE = {"w_in": (2048, 576), "w_uq": (448, 96), "w_ukv": (512, 128), "w_out": (256, 2048),
               "w_ff1": (2048, 1024), "w_ff2": (1024, 2048)}
SHARD_ROWS = {n: s[0] * s[1] // LANES for n, s in SHARD_SHAPE.items()}
PACK_ROWS = sum(SHARD_ROWS.values())
SMALL = ("ln1_g", "g_q_a", "g_kv_a", "g_qn_b", "g_kn_b", "g_out", "ln2_g")


def _dot(a, b, dims):
    return lax.dot_general(a, b, (dims, ((), ())), preferred_element_type=F32)


def _cparams(dims=None):
    return pltpu.CompilerParams(dimension_semantics=dims, vmem_limit_bytes=VMEM_LIMIT)


def _sds(shape, dtype):
    return jax.ShapeDtypeStruct(shape, dtype)


def _matmul(a, b, *, mode, tm, tn, tk, out_dtype, name, epi=None, extra=None):
    if mode == "nn":
        (M, K), (K2, N) = a.shape, b.shape
    elif mode == "nt":
        (M, K), (N, K2) = a.shape, b.shape
    else:
        (K, M), (K2, N) = a.shape, b.shape
    tm, tn, tk = min(tm, M), min(tn, N), min(tk, K)
    assert K == K2 and M % tm == 0 and N % tn == 0 and K % tk == 0, (name, a.shape, b.shape)
    nk = K // tk
    dims = {"nn": NN, "nt": NT, "tn": TN}[mode]
    if mode == "tn":
        a_spec = pl.BlockSpec((tk, tm), lambda i, j, k: (k, i))
    else:
        a_spec = pl.BlockSpec((tm, tk), lambda i, j, k: (i, k))
    if mode == "nt":
        b_spec = pl.BlockSpec((tn, tk), lambda i, j, k: (j, k))
    else:
        b_spec = pl.BlockSpec((tk, tn), lambda i, j, k: (k, j))
    tile = pl.BlockSpec((tm, tn), lambda i, j, k: (i, j))
    n_extra = 1 if epi in ("residual", "drelu2") else 0
    n_out = 2 if epi == "relu2" else 1

    def body(*refs):
        a_ref, b_ref = refs[0], refs[1]
        extra_refs = refs[2:2 + n_extra]
        out_refs = refs[2 + n_extra:2 + n_extra + n_out]

        def finish(acc):
            if epi is None:
                out_refs[0][...] = acc.astype(out_dtype)
            elif epi == "residual":
                out_refs[0][...] = (extra_refs[0][...] + acc).astype(out_dtype)
            elif epi == "relu2":
                out_refs[0][...] = acc.astype(out_dtype)
                r = jnp.maximum(acc, 0.0)
                out_refs[1][...] = (r * r).astype(out_dtype)
            else:
                z = extra_refs[0][...].astype(F32)
                out_refs[0][...] = (acc * (2.0 * jnp.maximum(z, 0.0))).astype(out_dtype)

        part = _dot(a_ref[...], b_ref[...], dims)
        if nk == 1:
            finish(part)
        else:
            acc_ref = refs[-1]
            k = pl.program_id(2)

            @pl.when(k == 0)
            def _():
                acc_ref[...] = part

            @pl.when(k > 0)
            def _():
                acc_ref[...] += part

            @pl.when(k == nk - 1)
            def _():
                finish(acc_ref[...])

    out_shape = [_sds((M, N), out_dtype)] * n_out
    res = pl.pallas_call(
        body, name=name, grid=(M // tm, N // tn, nk),
        in_specs=[a_spec, b_spec] + [tile] * n_extra,
        out_specs=[tile] * n_out, out_shape=out_shape,
        scratch_shapes=[pltpu.VMEM((tm, tn), F32)] if nk > 1 else [],
        compiler_params=_cparams(("parallel", "parallel", "arbitrary")),
    )(a, b, *([extra] if n_extra else []))
    return res if n_out > 1 else res[0]


def _rms_val(x, g, n):
    r = lax.rsqrt(jnp.sum(x * x, axis=-1, keepdims=True) * (1.0 / n) + EPS)
    y = x * r
    return (y if g is None else y * g), r


def _rms_bwd_val(x, g, dy, n):
    r = lax.rsqrt(jnp.sum(x * x, axis=-1, keepdims=True) * (1.0 / n) + EPS)
    xhat = x * r
    dyg = dy if g is None else dy * g
    dx = r * (dyg - xhat * (jnp.sum(dyg * xhat, axis=-1, keepdims=True) * (1.0 / n)))
    return dx, dy * xhat


def _rope_val(x, c, sa, sb, shift):
    return x * c + pltpu.roll(x, LANES - shift, 1) * sa + pltpu.roll(x, shift, 1) * sb


def _rope_t_val(dy, c, sa, sb, shift):
    return dy * c + pltpu.roll(dy * sa, shift, 1) + pltpu.roll(dy * sb, LANES - shift, 1)


def _colsum(x):
    return jnp.sum(x, axis=0, keepdims=True)


def _rope_tables(S):
    pos = jnp.arange(S, dtype=F32)

    def ang(p, dim):
        inv = jnp.power(ROPE_THETA, -jnp.arange(0, dim, 2, dtype=F32) / dim)
        a = p[:, None] * inv[None, :]
        return jnp.cos(a), jnp.sin(a)

    z32 = jnp.zeros((S, 32), F32)
    z64 = jnp.zeros((S, 64), F32)
    ca, sa_ = ang(pos, 64)
    tab_a = (jnp.concatenate([z64, ca, ca], 1), jnp.concatenate([z64, -sa_, z32], 1),
             jnp.concatenate([z64, z32, sa_], 1))
    row = jnp.floor(pos / GRID_W)
    col = pos - row * GRID_W
    cr, sr = ang(row, 64)
    cc, sc = ang(col, 64)
    tab_b = (jnp.concatenate([cr, cr, cc, cc], 1), jnp.concatenate([-sr, z32, -sc, z32], 1),
             jnp.concatenate([z32, sr, z32, sc], 1))
    c128, s128 = ang(pos, 128)
    tab_c = (jnp.concatenate([c128, c128], 1), jnp.concatenate([-s128, z64], 1),
             jnp.concatenate([z64, s128], 1))
    return tab_a, tab_b, tab_c


ROPE_SHIFT_AB = 32
ROPE_SHIFT_C = 64


def _rms_fwd(x, g, *, name, tr=512):
    S, W = x.shape
    tr = min(tr, S)

    def body(x_ref, g_ref, o_ref):
        y, _ = _rms_val(x_ref[...], g_ref[...], W)
        o_ref[...] = y.astype(CDT)

    return pl.pallas_call(
        body, name=name, grid=(S // tr,),
        in_specs=[pl.BlockSpec((tr, W), lambda i: (i, 0)), pl.BlockSpec((1, W), lambda i: (0, 0))],
        out_specs=pl.BlockSpec((tr, W), lambda i: (i, 0)), out_shape=_sds((S, W), CDT),
        compiler_params=_cparams(("parallel",)),
    )(x, g.reshape(1, W))


def _rms_bwd(x, g, dy, res, *, name, tr=256):
    S, W = x.shape
    tr = min(tr, S)

    def body(x_ref, g_ref, dy_ref, res_ref, dx_ref, dxb_ref, dg_ref):
        dx, dgt = _rms_bwd_val(x_ref[...], g_ref[...], dy_ref[...], W)
        dx = res_ref[...] + dx
        dx_ref[...] = dx
        dxb_ref[...] = dx.astype(CDT)

        @pl.when(pl.program_id(0) == 0)
        def _():
            dg_ref[...] = jnp.zeros_like(dg_ref)

        dg_ref[...] += _colsum(dgt)

    row = pl.BlockSpec((tr, W), lambda i: (i, 0))
    vec = pl.BlockSpec((1, W), lambda i: (0, 0))
    return pl.pallas_call(
        body, name=name, grid=(S // tr,),
        in_specs=[row, vec, row, row], out_specs=[row, row, vec],
        out_shape=[_sds((S, W), F32), _sds((S, W), CDT), _sds((1, W), F32)],
        compiler_params=_cparams(("arbitrary",)),
    )(x, g.reshape(1, W), dy, res)


def _loss_head(x, g, tgt, *, tr=256):
    S, W = x.shape
    tr = min(tr, S)

    def body(x_ref, g_ref, t_ref, loss_ref, dx_ref, dxb_ref, dg_ref):
        xv, gv = x_ref[...], g_ref[...]
        y, _ = _rms_val(xv, gv, W)
        err = y - t_ref[...]
        part = 0.5 * jnp.sum(jnp.sum(err * err, axis=-1, keepdims=True) * (1.0 / W), axis=0, keepdims=True)
        dx, dgt = _rms_bwd_val(xv, gv, err * (1.0 / W), W)
        dx_ref[...] = dx
        dxb_ref[...] = dx.astype(CDT)

        @pl.when(pl.program_id(0) == 0)
        def _():
            dg_ref[...] = jnp.zeros_like(dg_ref)
            loss_ref[...] = jnp.zeros_like(loss_ref)

        dg_ref[...] += _colsum(dgt)
        loss_ref[...] += jnp.broadcast_to(part, (1, LANES))

    row = pl.BlockSpec((tr, W), lambda i: (i, 0))
    vec = pl.BlockSpec((1, W), lambda i: (0, 0))
    return pl.pallas_call(
        body, name="loss_head", grid=(S // tr,),
        in_specs=[row, vec, row], out_specs=[pl.BlockSpec((1, LANES), lambda i: (0, 0)), row, row, vec],
        out_shape=[_sds((1, LANES), F32), _sds((S, W), F32), _sds((S, W), CDT), _sds((1, W), F32)],
        compiler_params=_cparams(("arbitrary",)),
    )(x, g.reshape(1, W), tgt)


def _tab_specs(tr):
    return [pl.BlockSpec((tr, LANES), lambda i: (i, 0))] * 9


def _prep_fwd(proj, gq, gkv, gqn, gkn, tabs, *, tr=256):
    S = proj.shape[0]
    tr = min(tr, S)

    def body(bq_ref, cq_ref, ck_ref, cv_ref, acq_ref, ackv_ref, bk_ref, bv_ref, gq_ref, gkv_ref, gqn_ref, gkn_ref,
             ac, aa, ab, bc, ba, bb, cc, ca, cb,
             cqn_ref, ckvn_ref, kpe_ref, qb_ref, kb_ref, vb_ref, qc_ref, kc_ref, vc_ref):
        x = acq_ref[...]
        lane = lax.broadcasted_iota(jnp.int32, x.shape, 1)
        xm = jnp.where(lane < Q_LORA, x, 0.0)
        cqn_ref[...] = _rms_val(xm, gq_ref[...], Q_LORA)[0].astype(CDT)
        ckvn_ref[...] = _rms_val(ackv_ref[...], gkv_ref[...], 512)[0].astype(CDT)
        kpe_ref[...] = _rope_val(x[:, 384:512], ac[...], aa[...], ab[...], ROPE_SHIFT_AB).astype(CDT)
        for h in range(6):
            xh = bq_ref[:, LANES * h:LANES * (h + 1)]
            y = _rope_val(_rms_val(xh, gqn_ref[...], LANES)[0], bc[...], ba[...], bb[...], ROPE_SHIFT_AB)
            qb_ref[:, LANES * h:LANES * (h + 1)] = y.astype(CDT)
        for h in range(2):
            xh = bk_ref[:, LANES * h:LANES * (h + 1)]
            y = _rope_val(_rms_val(xh, gkn_ref[...], LANES)[0], bc[...], ba[...], bb[...], ROPE_SHIFT_AB)
            kb_ref[:, LANES * h:LANES * (h + 1)] = y.astype(CDT)
        vb_ref[...] = bv_ref[...].astype(CDT)
        for h in range(6):
            sl = slice(LANES * h, LANES * (h + 1))
            qc_ref[:, sl] = _rope_val(cq_ref[:, sl], cc[...], ca[...], cb[...], ROPE_SHIFT_C).astype(CDT)
            kc_ref[:, sl] = _rope_val(ck_ref[:, sl], cc[...], ca[...], cb[...], ROPE_SHIFT_C).astype(CDT)
        vc_ref[...] = cv_ref[...].astype(CDT)

    def blk(w, j):
        return pl.BlockSpec((tr, w), lambda i: (i, j))

    vec = lambda w: pl.BlockSpec((1, w), lambda i: (0, 0))
    row = lambda w: pl.BlockSpec((tr, w), lambda i: (i, 0))
    return pl.pallas_call(
        body, name="prep_fwd", grid=(S // tr,),
        in_specs=[blk(768, 0), blk(768, 1), blk(768, 2), blk(768, 3), blk(512, 6), blk(512, 7), blk(256, 16),
                  blk(256, 17), vec(512), vec(512), vec(128), vec(128)] + _tab_specs(tr),
        out_specs=[row(512), row(512), row(128), row(768), row(256), row(256), row(768), row(768), row(768)],
        out_shape=[_sds((S, 512), CDT), _sds((S, 512), CDT), _sds((S, 128), CDT), _sds((S, 768), CDT),
                   _sds((S, 256), CDT), _sds((S, 256), CDT), _sds((S, 768), CDT), _sds((S, 768), CDT),
                   _sds((S, 768), CDT)],
        compiler_params=_cparams(("parallel",)),
    )(proj, proj, proj, proj, proj, proj, proj, proj, gq, gkv, gqn, gkn, *tabs[0], *tabs[1], *tabs[2])


def _prep_a2_fwd(qa_raw, kva, kpe, tab_a, *, tr=512):
    S = qa_raw.shape[0]
    tr = min(tr, S)

    def body(q_ref, kva_ref, kpe_ref, ac, aa, ab, qa_ref, ka_ref):
        for h in range(4):
            lo, hi = 2 * LANES * h, 2 * LANES * h + LANES
            qa_ref[:, lo:hi] = q_ref[:, lo:hi].astype(CDT)
            qa_ref[:, hi:hi + LANES] = _rope_val(q_ref[:, hi:hi + LANES], ac[...], aa[...], ab[...],
                                                 ROPE_SHIFT_AB).astype(CDT)
            ka_ref[:, lo:hi] = kva_ref[:, LANES * h:LANES * (h + 1)]
            ka_ref[:, hi:hi + LANES] = kpe_ref[...]

    row = lambda w: pl.BlockSpec((tr, w), lambda i: (i, 0))
    return pl.pallas_call(
        body, name="prep_a2_fwd", grid=(S // tr,),
        in_specs=[row(1024), pl.BlockSpec((tr, 512), lambda i: (i, 0)), row(128)] + _tab_specs(tr)[:3],
        out_specs=[row(1024), row(1024)], out_shape=[_sds((S, 1024), CDT), _sds((S, 1024), CDT)],
        compiler_params=_cparams(("parallel",)),
    )(qa_raw, kva, kpe, *tab_a)


def _prep_a2_bwd(dqa, dka, dva, tab_a, *, tr=512):
    S = dqa.shape[0]
    tr = min(tr, S)

    def body(dq_ref, dk_ref, dv_ref, ac, aa, ab, dqr_ref, dkva_ref, dkr_ref):
        dkpe = jnp.zeros((tr, LANES), F32)
        for h in range(4):
            lo, hi = 2 * LANES * h, 2 * LANES * h + LANES
            dqr_ref[:, lo:hi] = dq_ref[:, lo:hi].astype(CDT)
            dqr_ref[:, hi:hi + LANES] = _rope_t_val(dq_ref[:, hi:hi + LANES], ac[...], aa[...], ab[...],
                                                    ROPE_SHIFT_AB).astype(CDT)
            dkva_ref[:, LANES * h:LANES * (h + 1)] = dk_ref[:, lo:hi].astype(CDT)
            dkpe = dkpe + dk_ref[:, hi:hi + LANES]
        dkva_ref[:, 512:1024] = dv_ref[...].astype(CDT)
        dkr_ref[...] = _rope_t_val(dkpe, ac[...], aa[...], ab[...], ROPE_SHIFT_AB)

    row = lambda w: pl.BlockSpec((tr, w), lambda i: (i, 0))
    return pl.pallas_call(
        body, name="prep_a2_bwd", grid=(S // tr,),
        in_specs=[row(1024), row(1024), row(512)] + _tab_specs(tr)[:3],
        out_specs=[row(1024), row(1024), row(128)],
        out_shape=[_sds((S, 1024), CDT), _sds((S, 1024), CDT), _sds((S, 128), F32)],
        compiler_params=_cparams(("parallel",)),
    )(dqa, dka, dva, *tab_a)


def _prep_bwd(proj, gq, gkv, gqn, gkn, tabs, dcqn, dckvn, dkr, dqb, dkb, dvb, dqc, dkc, dvc, *, tr=128):
    S = proj.shape[0]
    tr = min(tr, S)

    def body(bq_ref, acq_ref, ackv_ref, bk_ref, gq_ref, gkv_ref, gqn_ref, gkn_ref,
             ac, aa, ab, bc, ba, bb, cc, ca, cb,
             dcqn_ref, dckvn_ref, dkr_ref, dqb_ref, dkb_ref, dvb_ref, dqc_ref, dkc_ref, dvc_ref,
             dp_ref, dgq_ref, dgkv_ref, dgqn_ref, dgkn_ref):
        @pl.when(pl.program_id(0) == 0)
        def _():
            dgq_ref[...] = jnp.zeros_like(dgq_ref)
            dgkv_ref[...] = jnp.zeros_like(dgkv_ref)
            dgqn_ref[...] = jnp.zeros_like(dgqn_ref)
            dgkn_ref[...] = jnp.zeros_like(dgkn_ref)

        dgqn = jnp.zeros((1, LANES), F32)
        for h in range(6):
            sl = slice(LANES * h, LANES * (h + 1))
            dy = _rope_t_val(dqb_ref[:, sl], bc[...], ba[...], bb[...], ROPE_SHIFT_AB)
            dx, dgt = _rms_bwd_val(bq_ref[:, sl], gqn_ref[...], dy, LANES)
            dp_ref[:, sl] = dx.astype(CDT)
            dgqn = dgqn + _colsum(dgt)
        dgqn_ref[...] += dgqn
        dgkn = jnp.zeros((1, LANES), F32)
        for h in range(2):
            sl = slice(LANES * h, LANES * (h + 1))
            dy = _rope_t_val(dkb_ref[:, sl], bc[...], ba[...], bb[...], ROPE_SHIFT_AB)
            dx, dgt = _rms_bwd_val(bk_ref[:, sl], gkn_ref[...], dy, LANES)
            dp_ref[:, 4096 + LANES * h:4096 + LANES * (h + 1)] = dx.astype(CDT)
            dgkn = dgkn + _colsum(dgt)
        dgkn_ref[...] += dgkn
        dp_ref[:, 4352:4608] = dvb_ref[...].astype(CDT)
        for h in range(6):
            sl = slice(LANES * h, LANES * (h + 1))
            dq = dqc_ref[0, :, sl] + dqc_ref[1, :, sl] + dqc_ref[2, :, sl]
            dk = dkc_ref[0, :, sl] + dkc_ref[1, :, sl] + dkc_ref[2, :, sl]
            dv = dvc_ref[0, :, sl] + dvc_ref[1, :, sl] + dvc_ref[2, :, sl]
            dp_ref[:, 768 + LANES * h:768 + LANES * (h + 1)] = _rope_t_val(
                dq, cc[...], ca[...], cb[...], ROPE_SHIFT_C).astype(CDT)
            dp_ref[:, 1536 + LANES * h:1536 + LANES * (h + 1)] = _rope_t_val(
                dk, cc[...], ca[...], cb[...], ROPE_SHIFT_C).astype(CDT)
            dp_ref[:, 2304 + LANES * h:2304 + LANES * (h + 1)] = dv.astype(CDT)
        x = acq_ref[...]
        lane = lax.broadcasted_iota(jnp.int32, x.shape, 1)
        xm = jnp.where(lane < Q_LORA, x, 0.0)
        dx, dgt = _rms_bwd_val(xm, gq_ref[...], dcqn_ref[...], Q_LORA)
        dgq_ref[...] += _colsum(dgt)
        dp_ref[:, 3072:3456] = dx[:, 0:384].astype(CDT)
        dp_ref[:, 3456:3584] = (dx[:, 384:512] + dkr_ref[...]).astype(CDT)
        dx, dgt = _rms_bwd_val(ackv_ref[...], gkv_ref[...], dckvn_ref[...], 512)
        dgkv_ref[...] += _colsum(dgt)
        dp_ref[:, 3584:4096] = dx.astype(CDT)

    def blk(w, j):
        return pl.BlockSpec((tr, w), lambda i: (i, j))

    vec = lambda w: pl.BlockSpec((1, w), lambda i: (0, 0))
    row = lambda w: pl.BlockSpec((tr, w), lambda i: (i, 0))
    row3 = pl.BlockSpec((3, tr, 768), lambda i: (0, i, 0))
    return pl.pallas_call(
        body, name="prep_bwd", grid=(S // tr,),
        in_specs=[blk(768, 0), blk(512, 6), blk(512, 7), blk(256, 16), vec(512), vec(512), vec(128), vec(128)]
        + _tab_specs(tr) + [row(512), row(512), row(128), row(768), row(256), row(256), row3, row3, row3],
        out_specs=[row(PROJ_W), vec(512), vec(512), vec(128), vec(128)],
        out_shape=[_sds((S, PROJ_W), CDT), _sds((1, 512), F32), _sds((1, 512), F32), _sds((1, 128), F32),
                   _sds((1, 128), F32)],
        compiler_params=_cparams(("arbitrary",)),
    )(proj, proj, proj, proj, gq, gkv, gqn, gkn, *tabs[0], *tabs[1], *tabs[2],
      dcqn, dckvn, dkr, dqb, dkb, dvb, dqc, dkc, dvc)


def _attn_fwd(q, k, v, *, H, G, dk, dv, v_off, scale, name, tq=512):
    S = q.shape[0]
    tq = min(tq, S)

    def body(q_ref, k_ref, v_ref, o_ref, l_ref):
        s = _dot(q_ref[...], k_ref[...], NT) * scale
        m = jnp.max(s, axis=-1, keepdims=True)
        e = jnp.exp(s - m)
        den = jnp.sum(e, axis=-1, keepdims=True)
        o_ref[...] = _dot(e.astype(CDT), v_ref[...], NN) * (1.0 / den)
        l_ref[...] = jnp.broadcast_to(m + jnp.log(den), (tq, LANES))

    return pl.pallas_call(
        body, name=name, grid=(H, S // tq),
        in_specs=[pl.BlockSpec((tq, dk), lambda h, i: (i, h)), pl.BlockSpec((S, dk), lambda h, i: (0, h // G)),
                  pl.BlockSpec((S, dv), lambda h, i: (0, v_off + h // G))],
        out_specs=[pl.BlockSpec((tq, dv), lambda h, i: (i, h)), pl.BlockSpec((tq, LANES), lambda h, i: (i, h))],
        out_shape=[_sds((S, H * dv), F32), _sds((S, H * LANES), F32)],
        compiler_params=_cparams(("parallel", "parallel")),
    )(q, k, v)


def _attn_bwd(q, k, v, do, lse, delta, *, H, G, dk, dv, v_off, scale, name, tq=256):
    S = q.shape[0]
    tq = min(tq, S)
    Hkv = H // G

    def body(q_ref, k_ref, v_ref, do_ref, l_ref, d_ref, dq_ref, dk_ref, dv_ref):
        @pl.when((pl.program_id(1) == 0) & (pl.program_id(2) == 0))
        def _():
            dk_ref[...] = jnp.zeros_like(dk_ref)
            dv_ref[...] = jnp.zeros_like(dv_ref)

        qv, kv, dov = q_ref[...], k_ref[...], do_ref[...]
        s = _dot(qv, kv, NT) * scale
        p = jnp.exp(s - l_ref[:, 0:1])
        dp = _dot(dov, v_ref[...], NT)
        ds = (p * (dp - d_ref[:, 0:1]) * scale).astype(CDT)
        dq_ref[...] = _dot(ds, kv, NN)
        dk_ref[...] += _dot(ds, qv, TN)
        dv_ref[...] += _dot(p.astype(CDT), dov, TN)

    qi = lambda hk, g, i: (i, hk * G + g)
    return pl.pallas_call(
        body, name=name, grid=(Hkv, G, S // tq),
        in_specs=[pl.BlockSpec((tq, dk), qi), pl.BlockSpec((S, dk), lambda hk, g, i: (0, hk)),
                  pl.BlockSpec((S, dv), lambda hk, g, i: (0, v_off + hk)), pl.BlockSpec((tq, dv), qi),
                  pl.BlockSpec((tq, LANES), qi), pl.BlockSpec((tq, LANES), qi)],
        out_specs=[pl.BlockSpec((tq, dk), qi), pl.BlockSpec((S, dk), lambda hk, g, i: (0, hk)),
                   pl.BlockSpec((S, dv), lambda hk, g, i: (0, hk))],
        out_shape=[_sds((S, H * dk), F32), _sds((S, Hkv * dk), F32), _sds((S, Hkv * dv), F32)],
        compiler_params=_cparams(("parallel", "arbitrary", "arbitrary")),
    )(q, k, v, do, lse, delta)


BAND_TQ = 512
BAND_SUB = 128
BAND_WIN = 384


def _band_geometry(S):
    logs = [int(math.log2(S // d)) for d in DILATIONS]
    assert all(S // d == 1 << lg for d, lg in zip(DILATIONS, logs)) and S >= BAND_WIN and S % BAND_SUB == 0
    return logs


def _band_mask(r0, w0, shift):
    rpos = r0 + lax.broadcasted_iota(jnp.int32, (BAND_SUB, BAND_WIN), 0)
    cpos = w0 + lax.broadcasted_iota(jnp.int32, (BAND_SUB, BAND_WIN), 1)
    return (jnp.abs(rpos - cpos) <= BAND_HALF) & (jnp.right_shift(rpos, shift) == jnp.right_shift(cpos, shift))


def _band_fwd(qs, ks, vs):
    _, S, W = qs.shape
    logs = _band_geometry(S)
    tq = min(BAND_TQ, S)

    def body(q_ref, k_ref, v_ref, o_ref, l_ref):
        b, i = pl.program_id(0), pl.program_id(2)
        shift = jnp.where(b == 0, logs[0], jnp.where(b == 1, logs[1], logs[2]))
        for j in range(tq // BAND_SUB):
            rows = slice(BAND_SUB * j, BAND_SUB * (j + 1))
            r0 = i * tq + BAND_SUB * j
            w0 = pl.multiple_of(jnp.clip(r0 - BAND_SUB, 0, S - BAND_WIN), BAND_SUB)
            kw, vw = k_ref[pl.ds(w0, BAND_WIN), :], v_ref[pl.ds(w0, BAND_WIN), :]
            s = jnp.where(_band_mask(r0, w0, shift), _dot(q_ref[rows, :], kw, NT) * SCALE_H, NEG_INF)
            m = jnp.max(s, axis=-1, keepdims=True)
            e = jnp.exp(s - m)
            den = jnp.sum(e, axis=-1, keepdims=True)
            o_ref[rows, :] = _dot((e * (1.0 / den)).astype(CDT), vw, NN)
            l_ref[rows, :] = jnp.broadcast_to(m + jnp.log(den), (BAND_SUB, LANES))

    tile = pl.BlockSpec((None, tq, LANES), lambda b, h, i: (b, i, h))
    full = pl.BlockSpec((None, S, LANES), lambda b, h, i: (b, 0, h))
    return pl.pallas_call(
        body, name="band_fwd", grid=(3, W // LANES, S // tq),
        in_specs=[tile, full, full], out_specs=[tile, tile],
        out_shape=[_sds((3, S, W), F32), _sds((3, S, W), F32)],
        compiler_params=_cparams(("parallel", "parallel", "parallel")),
    )(qs, ks, vs)


def _band_bwd(qs, ks, vs, dos, lses, dds):
    _, S, W = qs.shape
    logs = _band_geometry(S)
    tq = min(BAND_TQ, S)

    def body(q_ref, k_ref, v_ref, do_ref, l_ref, d_ref, dq_ref, dk_ref, dv_ref):
        b, i = pl.program_id(0), pl.program_id(2)

        @pl.when(i == 0)
        def _():
            dk_ref[...] = jnp.zeros_like(dk_ref)
            dv_ref[...] = jnp.zeros_like(dv_ref)

        shift = jnp.where(b == 0, logs[0], jnp.where(b == 1, logs[1], logs[2]))
        for j in range(tq // BAND_SUB):
            rows = slice(BAND_SUB * j, BAND_SUB * (j + 1))
            r0 = i * tq + BAND_SUB * j
            w0 = pl.multiple_of(jnp.clip(r0 - BAND_SUB, 0, S - BAND_WIN), BAND_SUB)
            win = pl.ds(w0, BAND_WIN)
            qv, dov, kw = q_ref[rows, :], do_ref[rows, :], k_ref[win, :]
            s = jnp.where(_band_mask(r0, w0, shift), _dot(qv, kw, NT) * SCALE_H, NEG_INF)
            p = jnp.exp(s - l_ref[rows, 0:1])
            dp = _dot(dov, v_ref[win, :], NT)
            ds = (p * (dp - d_ref[rows, 0:1]) * SCALE_H).astype(CDT)
            dq_ref[rows, :] = _dot(ds, kw, NN)
            dk_ref[win, :] += _dot(ds, qv, TN)
            dv_ref[win, :] += _dot(p.astype(CDT), dov, TN)

    tile = pl.BlockSpec((None, tq, LANES), lambda b, h, i: (b, i, h))
    full = pl.BlockSpec((None, S, LANES), lambda b, h, i: (b, 0, h))
    return pl.pallas_call(
        body, name="band_bwd", grid=(3, W // LANES, S // tq),
        in_specs=[tile, full, full, tile, tile, tile], out_specs=[tile, full, full],
        out_shape=[_sds((3, S, W), F32)] * 3,
        compiler_params=_cparams(("parallel", "parallel", "arbitrary")),
    )(qs, ks, vs, dos, lses, dds)


def _combine(ob, lb, *, tr=256):
    _, S, W = ob.shape
    tr = min(tr, S)

    def body(o_ref, l_ref, out_ref, lse_ref):
        l0, l1, l2 = l_ref[0], l_ref[1], l_ref[2]
        m = jnp.maximum(jnp.maximum(l0, l1), l2)
        e0, e1, e2 = jnp.exp(l0 - m), jnp.exp(l1 - m), jnp.exp(l2 - m)
        den = e0 + e1 + e2
        inv = 1.0 / den
        out_ref[...] = (e0 * inv) * o_ref[0] + (e1 * inv) * o_ref[1] + (e2 * inv) * o_ref[2]
        lse_ref[...] = m + jnp.log(den)

    blk3 = pl.BlockSpec((3, tr, W), lambda i: (0, i, 0))
    row = pl.BlockSpec((tr, W), lambda i: (i, 0))
    return pl.pallas_call(
        body, name="combine", grid=(S // tr,), in_specs=[blk3, blk3], out_specs=[row, row],
        out_shape=[_sds((S, W), F32), _sds((S, W), F32)], compiler_params=_cparams(("parallel",)),
    )(ob, lb)


def _to_branches(a):
    S, C = a.shape
    return jnp.stack([a.reshape(S // d, d, C).transpose(1, 0, 2).reshape(S, C) for d in DILATIONS])


def _from_branches(a):
    _, S, C = a.shape
    return jnp.stack([a[b].reshape(d, S // d, C).transpose(1, 0, 2).reshape(S, C) for b, d in enumerate(DILATIONS)])


def _outnorm_fwd(oa, ob, oc, g, *, tr=256):
    S = oa.shape[0]
    tr = min(tr, S)

    def body(a_ref, b_ref, c_ref, g_ref, m_ref):
        m_ref[:, 0:512] = (_rms_val(a_ref[...], None, 512)[0] * g_ref[:, 0:512]).astype(CDT)
        m_ref[:, 512:1280] = (_rms_val(b_ref[...], None, 768)[0] * g_ref[:, 512:1280]).astype(CDT)
        m_ref[:, 1280:2048] = (_rms_val(c_ref[...], None, 768)[0] * g_ref[:, 1280:2048]).astype(CDT)

    row = lambda w: pl.BlockSpec((tr, w), lambda i: (i, 0))
    return pl.pallas_call(
        body, name="outnorm_fwd", grid=(S // tr,),
        in_specs=[row(512), row(768), row(768), pl.BlockSpec((1, 2048), lambda i: (0, 0))],
        out_specs=row(2048), out_shape=_sds((S, 2048), CDT), compiler_params=_cparams(("parallel",)),
    )(oa, ob, oc, g)


def _outnorm_bwd(oa, ob, oc, g, dm, *, tr=256):
    S = oa.shape[0]
    tr = min(tr, S)

    def body(a_ref, b_ref, c_ref, g_ref, dm_ref, doa_ref, dob_ref, doc_ref, da_ref, db_ref, dc_ref, dg_ref):
        @pl.when(pl.program_id(0) == 0)
        def _():
            dg_ref[...] = jnp.zeros_like(dg_ref)

        for o_ref, do_ref, d_ref, lo, w in ((a_ref, doa_ref, da_ref, 0, 512), (b_ref, dob_ref, db_ref, 512, 768),
                                            (c_ref, doc_ref, dc_ref, 1280, 768)):
            o = o_ref[...]
            dmv = dm_ref[:, lo:lo + w]
            do, dgt = _rms_bwd_val(o, None, dmv * g_ref[:, lo:lo + w], w)
            r = lax.rsqrt(jnp.sum(o * o, axis=-1, keepdims=True) * (1.0 / w) + EPS)
            dg_ref[:, lo:lo + w] += _colsum(dmv * (o * r))
            do_ref[...] = do.astype(CDT)
            for h in range(w // LANES):
                sl = slice(LANES * h, LANES * (h + 1))
                d_ref[:, sl] = jnp.broadcast_to(jnp.sum(do[:, sl] * o[:, sl], axis=-1, keepdims=True), (tr, LANES))

    row = lambda w: pl.BlockSpec((tr, w), lambda i: (i, 0))
    vec = pl.BlockSpec((1, 2048), lambda i: (0, 0))
    return pl.pallas_call(
        body, name="outnorm_bwd", grid=(S // tr,),
        in_specs=[row(512), row(768), row(768), vec, row(2048)],
        out_specs=[row(512), row(768), row(768), row(512), row(768), row(768), vec],
        out_shape=[_sds((S, 512), CDT), _sds((S, 768), CDT), _sds((S, 768), CDT), _sds((S, 512), F32),
                   _sds((S, 768), F32), _sds((S, 768), F32), _sds((1, 2048), F32)],
        compiler_params=_cparams(("arbitrary",)),
    )(oa, ob, oc, g, dm)


MESH = pl.DeviceIdType.MESH
ANY = pl.BlockSpec(memory_space=pl.ANY)


def _all_gather(shard):
    R = shard.shape[0]

    def body(x_ref, out_ref, send_sems, recv_sems, local_sem):
        x, y, c = lax.axis_index("x"), lax.axis_index("y"), lax.axis_index("c")
        me, sibling = (x, y, c), (x, y, 1 - c)
        chips = [(1 - x, y), (x, 1 - y), (1 - x, 1 - y)]

        def rows(px, py, pc):
            return out_ref.at[4 * px + 2 * py + pc]

        def copy(k, block, to, src=None):
            return pltpu.make_async_remote_copy(
                src_ref=rows(*block) if src is None else src, dst_ref=rows(*block),
                send_sem=send_sems.at[k], recv_sem=recv_sems.at[k], device_id=to, device_id_type=MESH)

        mine = pltpu.make_async_copy(x_ref, rows(*me), local_sem)
        mine.start()
        first = [copy(0, me, sibling, src=x_ref)]
        first += [copy(1 + j, me, (*chip, c), src=x_ref) for j, chip in enumerate(chips)]
        for cp in first:
            cp.start()
        passed = [copy(4 + j, (*chip, c), sibling) for j, chip in enumerate(chips)]
        for j, chip in enumerate(chips):
            copy(1 + j, (*chip, c), me).wait_recv()
            passed[j].start()
        copy(0, sibling, me).wait_recv()
        for j, chip in enumerate(chips):
            copy(4 + j, (*chip, 1 - c), me).wait_recv()
        for cp in first + passed:
            cp.wait_send()
        mine.wait()

    return pl.pallas_call(
        body, name="all_gather", out_shape=_sds((8, R, LANES), shard.dtype), in_specs=[ANY], out_specs=ANY,
        scratch_shapes=[pltpu.SemaphoreType.DMA((7,)), pltpu.SemaphoreType.DMA((7,)), pltpu.SemaphoreType.DMA],
    )(shard)


def _rs_sibling_exchange(gs):
    _, _, R, _ = gs.shape

    def body(g_ref, out_ref, send_sem, recv_sem):
        x, y, c = lax.axis_index("x"), lax.axis_index("y"), lax.axis_index("c")
        cp = pltpu.make_async_remote_copy(src_ref=g_ref.at[1 - c], dst_ref=out_ref, send_sem=send_sem,
                                          recv_sem=recv_sem, device_id=(x, y, 1 - c), device_id_type=MESH)
        cp.start()
        cp.wait()

    return pl.pallas_call(
        body, name="rs_sibling_exchange", out_shape=_sds((4, R, LANES), gs.dtype), in_specs=[ANY], out_specs=ANY,
        scratch_shapes=[pltpu.SemaphoreType.DMA, pltpu.SemaphoreType.DMA],
    )(gs)


def _rs_chip_sum(gs, got, *, tr=2048):
    _, _, R, _ = gs.shape
    tr = math.gcd(R, tr)
    core = lax.axis_index("c").astype(jnp.int32).reshape(1)

    def body(c_ref, a_ref, b_ref, o_ref):
        o_ref[...] = (a_ref[...].astype(F32) + b_ref[...].astype(F32)).astype(o_ref.dtype)

    spec = pltpu.PrefetchScalarGridSpec(
        num_scalar_prefetch=1, grid=(4, R // tr),
        in_specs=[pl.BlockSpec((None, None, tr, LANES), lambda k, i, c: (c[0], k, i, 0)),
                  pl.BlockSpec((None, tr, LANES), lambda k, i, c: (k, i, 0))],
        out_specs=pl.BlockSpec((None, tr, LANES), lambda k, i, c: (k, i, 0)))
    return pl.pallas_call(body, name="rs_chip_sum", grid_spec=spec, out_shape=_sds((4, R, LANES), gs.dtype),
                          compiler_params=_cparams(("parallel", "parallel")))(core, gs, got)


def _rs_chip_exchange(p):
    _, R, _ = p.shape

    def body(p_ref, out_ref, send_sems, recv_sems, local_sem):
        x, y, c = lax.axis_index("x"), lax.axis_index("y"), lax.axis_index("c")
        chips = [(1 - x, y), (x, 1 - y), (1 - x, 1 - y)]
        my_chip = 2 * x + y
        mine = pltpu.make_async_copy(p_ref.at[my_chip], out_ref.at[my_chip], local_sem)
        mine.start()
        sends = [pltpu.make_async_remote_copy(
            src_ref=p_ref.at[2 * cx + cy], dst_ref=out_ref.at[my_chip], send_sem=send_sems.at[j],
            recv_sem=recv_sems.at[j], device_id=(cx, cy, c), device_id_type=MESH) for j, (cx, cy) in enumerate(chips)]
        for cp in sends:
            cp.start()
        for j, (cx, cy) in enumerate(chips):
            pltpu.make_async_remote_copy(
                src_ref=p_ref.at[my_chip], dst_ref=out_ref.at[2 * cx + cy], send_sem=send_sems.at[j],
                recv_sem=recv_sems.at[j], device_id=(cx, cy, c), device_id_type=MESH).wait_recv()
        for cp in sends:
            cp.wait_send()
        mine.wait()

    return pl.pallas_call(
        body, name="rs_chip_exchange", out_shape=_sds((4, R, LANES), p.dtype), in_specs=[ANY], out_specs=ANY,
        scratch_shapes=[pltpu.SemaphoreType.DMA((3,)), pltpu.SemaphoreType.DMA((3,)), pltpu.SemaphoreType.DMA],
    )(p)


def _rs_final_sum(r, *, tr=2048):
    _, R, _ = r.shape
    tr = math.gcd(R, tr)

    def body(r_ref, o_ref):
        o_ref[...] = ((r_ref[0].astype(F32) + r_ref[1].astype(F32)) + r_ref[2].astype(F32)) + r_ref[3].astype(F32)

    return pl.pallas_call(
        body, name="rs_final_sum", grid=(R // tr,), in_specs=[pl.BlockSpec((4, tr, LANES), lambda i: (0, i, 0))],
        out_specs=pl.BlockSpec((tr, LANES), lambda i: (i, 0)), out_shape=_sds((R, LANES), F32),
        compiler_params=_cparams(("parallel",)))(r)


def _all_reduce_small(v):
    R = v.shape[0]

    def body(v_ref, out_ref, buf_ref, send_sems, recv_sems):
        x, y, c = lax.axis_index("x"), lax.axis_index("y"), lax.axis_index("c")
        me = 4 * x + 2 * y + c
        buf_ref[me] = v_ref[...]
        peers = []
        for r in range(1, 8):
            px, py, pc = x ^ (r >> 2), y ^ ((r >> 1) & 1), c ^ (r & 1)
            peers.append((r, (px, py, pc), 4 * px + 2 * py + pc))
        sends = [pltpu.make_async_remote_copy(
            src_ref=v_ref, dst_ref=buf_ref.at[me], send_sem=send_sems.at[r - 1], recv_sem=recv_sems.at[r - 1],
            device_id=dev, device_id_type=MESH) for r, dev, _ in peers]
        for cp in sends:
            cp.start()
        for r, dev, idx in peers:
            pltpu.make_async_remote_copy(
                src_ref=v_ref, dst_ref=buf_ref.at[idx], send_sem=send_sems.at[r - 1], recv_sem=recv_sems.at[r - 1],
                device_id=dev, device_id_type=MESH).wait_recv()
        for cp in sends:
            cp.wait_send()
        acc = buf_ref[0]
        for k in range(1, 8):
            acc = acc + buf_ref[k]
        out_ref[...] = acc

    vm = pl.BlockSpec(memory_space=pltpu.VMEM)
    return pl.pallas_call(
        body, name="all_reduce_small", out_shape=_sds((R, LANES), F32), in_specs=[vm], out_specs=vm,
        scratch_shapes=[pltpu.VMEM((8, R, LANES), F32), pltpu.SemaphoreType.DMA((7,)), pltpu.SemaphoreType.DMA((7,))],
    )(v)


def _adamw(w, g, m, v, *, name):
    R, C = w.shape
    tr = R
    for cand in (1024, 512, 256, 128, 64, 32, 16, 8):
        if R % cand == 0 and cand * C * 4 <= 2 * 1024 * 1024:
            tr = cand
            break

    def body(w_ref, g_ref, m_ref, v_ref, d_ref, nm_ref, nv_ref):
        gv = g_ref[...]
        mn = ADAM_B1 * m_ref[...] + (1.0 - ADAM_B1) * gv
        vn = ADAM_B2 * v_ref[...] + (1.0 - ADAM_B2) * (gv * gv)
        m_hat = mn / (1.0 - ADAM_B1 ** ADAM_STEP)
        v_hat = vn / (1.0 - ADAM_B2 ** ADAM_STEP)
        d_ref[...] = -ADAM_LR * (m_hat / (jnp.sqrt(v_hat) + ADAM_EPS) + ADAM_WD * w_ref[...])
        nm_ref[...] = mn
        nv_ref[...] = vn

    blk = pl.BlockSpec((tr, C), lambda i: (i, 0))
    return pl.pallas_call(
        body, name=name, grid=(R // tr,), in_specs=[blk] * 4, out_specs=[blk] * 3,
        out_shape=[_sds((R, C), F32)] * 3, compiler_params=_cparams(("parallel",)))(w, g, m, v)


def _win_perm(w):
    return jnp.concatenate([w[..., 1024:1792], w[..., 2304:4608], w[..., 0:448], w[..., 960:1024],
                            w[..., 448:960], w[..., 1792:2304]], axis=-1)


def _win_unperm(w):
    return jnp.concatenate([w[..., 3072:3520], w[..., 3584:4096], w[..., 3520:3584], w[..., 0:768],
                            w[..., 4096:4608], w[..., 768:3072]], axis=-1)


def _wuq_pad(w):
    w = w.reshape(448, 4, 192)
    z = jnp.zeros((448, 4, 64), w.dtype)
    w = jnp.concatenate([w[:, :, 0:128], z, w[:, :, 128:192]], axis=2).reshape(448, 1024)
    return jnp.concatenate([w, jnp.zeros((64, 1024), w.dtype)], axis=0)


def _wuq_unpad(w):
    w = w[0:448].reshape(448, 4, 256)
    return jnp.concatenate([w[:, :, 0:128], w[:, :, 192:256]], axis=2).reshape(448, 768)


def _wukv_perm(w):
    return w.reshape(512, 4, 2, 128).transpose(0, 2, 1, 3).reshape(512, 1024)


def _wukv_unperm(w):
    return w.reshape(512, 2, 4, 128).transpose(0, 2, 1, 3).reshape(512, 1024)


def _pack_shards(shards):
    return jnp.concatenate([shards[n].reshape(SHARD_ROWS[n], LANES) for n in BIG], axis=0)


def _unpack_gathered(g):
    out, lo = {}, 0
    for n in BIG:
        r, c = SHARD_SHAPE[n]
        blk = g[:, lo:lo + SHARD_ROWS[n]].reshape(8, r, c)
        lo += SHARD_ROWS[n]
        if n in ("w_out", "w_ff2"):
            out[n] = blk.reshape(8 * r, c)
        else:
            out[n] = blk.transpose(1, 0, 2).reshape(r, 8 * c)
    out["w_in"] = _win_perm(out["w_in"])
    out["w_uq"] = _wuq_pad(out["w_uq"])
    out["w_ukv"] = _wukv_perm(out["w_ukv"])
    return out


def _pack_grads(dw):
    dw = dict(dw)
    dw["w_in"] = _win_unperm(dw["w_in"])
    dw["w_uq"] = _wuq_unpad(dw["w_uq"])
    dw["w_ukv"] = _wukv_unperm(dw["w_ukv"])
    parts = []
    for n in BIG:
        r, c = SHARD_SHAPE[n]
        if n in ("w_out", "w_ff2"):
            blk = dw[n].reshape(8, r, c)
        else:
            blk = dw[n].reshape(r, 8, c).transpose(1, 0, 2)
        parts.append(blk.reshape(8, SHARD_ROWS[n], LANES))
    packed = jnp.concatenate(parts, axis=1)
    return packed.reshape(4, 2, PACK_ROWS, LANES).transpose(1, 0, 2, 3)


def _unpack_shard_grads(g):
    out, lo = {}, 0
    for n in BIG:
        out[n] = g[lo:lo + SHARD_ROWS[n]].reshape(SHARD_SHAPE[n])
        lo += SHARD_ROWS[n]
    return out


def _layer_fwd(x, W, G, tabs):
    s = {"x0": x}
    s["h1"] = _rms_fwd(x, G["ln1_g"], name="rms1_fwd")
    s["proj"] = _matmul(s["h1"], W["w_in"], mode="nn", tm=1024, tn=768, tk=2048, out_dtype=F32, name="mm_in")
    (s["cqn"], s["ckvn"], kpe, s["qb"], s["kb"], s["vb"], qc, kc, vc) = _prep_fwd(
        s["proj"], G["gq"], G["gkv"], G["gqn"], G["gkn"], tabs)
    qa_raw = _matmul(s["cqn"], W["w_uq"], mode="nn", tm=1024, tn=1024, tk=512, out_dtype=F32, name="mm_uq")
    s["kva"] = _matmul(s["ckvn"], W["w_ukv"], mode="nn", tm=1024, tn=1024, tk=512, out_dtype=CDT, name="mm_ukv")
    s["qa"], s["ka"] = _prep_a2_fwd(qa_raw, s["kva"], kpe, tabs[0])
    s["oa"], s["lse_a"] = _attn_fwd(s["qa"], s["ka"], s["kva"], H=4, G=1, dk=256, dv=128, v_off=4, scale=SCALE_A,
                                    name="attn_a_fwd")
    s["ob"], s["lse_b"] = _attn_fwd(s["qb"], s["kb"], s["vb"], H=6, G=3, dk=128, dv=128, v_off=0, scale=SCALE_H,
                                    name="attn_b_fwd")
    s["qs"], s["ks"], s["vs"] = _to_branches(qc), _to_branches(kc), _to_branches(vc)
    o_br, l_br = _band_fwd(s["qs"], s["ks"], s["vs"])
    s["oc"], s["lse_c"] = _combine(_from_branches(o_br), _from_branches(l_br))
    s["mixed"] = _outnorm_fwd(s["oa"], s["ob"], s["oc"], G["g_out"])
    s["x1"] = _matmul(s["mixed"], W["w_out"], mode="nn", tm=1024, tn=1024, tk=2048, out_dtype=F32, name="mm_out",
                      epi="residual", extra=x)
    s["h2"] = _rms_fwd(s["x1"], G["ln2_g"], name="rms2_fwd")
    s["z"], s["u"] = _matmul(s["h2"], W["w_ff1"], mode="nn", tm=1024, tn=1024, tk=2048, out_dtype=CDT, name="mm_ff1",
                             epi="relu2")
    x2 = _matmul(s["u"], W["w_ff2"], mode="nn", tm=1024, tn=1024, tk=1024, out_dtype=F32, name="mm_ff2",
                 epi="residual", extra=s["x1"])
    return x2, s


def _layer_bwd(dx2, dx2b, s, W, G, tabs):
    dw, dg = {}, {}
    dz = _matmul(dx2b, W["w_ff2"], mode="nt", tm=1024, tn=1024, tk=2048, out_dtype=CDT, name="mm_ff2_dx",
                 epi="drelu2", extra=s["z"])
    dw["w_ff2"] = _matmul(s["u"], dx2b, mode="tn", tm=2048, tn=2048, tk=512, out_dtype=CDT, name="mm_ff2_dw")
    dh2 = _matmul(dz, W["w_ff1"], mode="nt", tm=1024, tn=1024, tk=1024, out_dtype=F32, name="mm_ff1_dx")
    dw["w_ff1"] = _matmul(s["h2"], dz, mode="tn", tm=2048, tn=2048, tk=512, out_dtype=CDT, name="mm_ff1_dw")
    dx1, dx1b, dg["ln2_g"] = _rms_bwd(s["x1"], G["ln2_g"], dh2, dx2, name="rms2_bwd")
    dmixed = _matmul(dx1b, W["w_out"], mode="nt", tm=1024, tn=1024, tk=2048, out_dtype=F32, name="mm_out_dx")
    dw["w_out"] = _matmul(s["mixed"], dx1b, mode="tn", tm=2048, tn=2048, tk=512, out_dtype=CDT, name="mm_out_dw")
    doa, dob, doc, dla, dlb, dlc, dg["g_out"] = _outnorm_bwd(s["oa"], s["ob"], s["oc"], G["g_out"], dmixed)
    dqa, dka, dva = _attn_bwd(s["qa"], s["ka"], s["kva"], doa, s["lse_a"], dla, H=4, G=1, dk=256, dv=128, v_off=4,
                              scale=SCALE_A, name="attn_a_bwd")
    dqb, dkb, dvb = _attn_bwd(s["qb"], s["kb"], s["vb"], dob, s["lse_b"], dlb, H=6, G=3, dk=128, dv=128, v_off=0,
                              scale=SCALE_H, name="attn_b_bwd")
    dq_br, dk_br, dv_br = _band_bwd(s["qs"], s["ks"], s["vs"], _to_branches(doc), _to_branches(s["lse_c"]),
                                    _to_branches(dlc))
    dqa_raw, dkva, dkr = _prep_a2_bwd(dqa, dka, dva, tabs[0])
    dcqn = _matmul(dqa_raw, W["w_uq"], mode="nt", tm=1024, tn=512, tk=1024, out_dtype=F32, name="mm_uq_dx")
    dw["w_uq"] = _matmul(s["cqn"], dqa_raw, mode="tn", tm=512, tn=1024, tk=512, out_dtype=CDT, name="mm_uq_dw")
    dckvn = _matmul(dkva, W["w_ukv"], mode="nt", tm=1024, tn=512, tk=1024, out_dtype=F32, name="mm_ukv_dx")
    dw["w_ukv"] = _matmul(s["ckvn"], dkva, mode="tn", tm=512, tn=1024, tk=512, out_dtype=CDT, name="mm_ukv_dw")
    dproj, dg["gq"], dg["gkv"], dg["gqn"], dg["gkn"] = _prep_bwd(
        s["proj"], G["gq"], G["gkv"], G["gqn"], G["gkn"], tabs, dcqn, dckvn, dkr, dqb, dkb, dvb,
        _from_branches(dq_br), _from_branches(dk_br), _from_branches(dv_br))
    dh1 = _matmul(dproj, W["w_in"], mode="nt", tm=1024, tn=1024, tk=1536, out_dtype=F32, name="mm_in_dx")
    dw["w_in"] = _matmul(s["h1"], dproj, mode="tn", tm=2048, tn=1536, tk=512, out_dtype=CDT, name="mm_in_dw")
    dx0, dx0b, dg["ln1_g"] = _rms_bwd(s["x0"], G["ln1_g"], dh1, dx1, name="rms1_bwd")
    return dx0, dx0b, dw, dg


def _layer_gains(l, ln1_g, g_q_a, g_kv_a, g_qn_b, g_kn_b, g_out, ln2_g):
    return {"ln1_g": ln1_g[l], "ln2_g": ln2_g[l], "g_out": g_out[l].reshape(1, 2048),
            "gq": jnp.concatenate([g_q_a[l], jnp.zeros((64,), F32)]).reshape(1, 512),
            "gkv": g_kv_a[l].reshape(1, 512), "gqn": g_qn_b[l].reshape(1, 128), "gkn": g_kn_b[l].reshape(1, 128)}


def _local_step(x, tgt, weights, gains, ln_f_g):
    S = x.shape[0]
    tabs = _rope_tables(S)
    depth = len(weights)
    saved = []
    for l in range(depth):
        x, s = _layer_fwd(x, weights[l], gains[l], tabs)
        saved.append(s)
    loss, dx, dxb, dlnf = _loss_head(x, ln_f_g, tgt)
    dws, dgs = [None] * depth, [None] * depth
    for l in reversed(range(depth)):
        dx, dxb, dws[l], dgs[l] = _layer_bwd(dx, dxb, saved[l], weights[l], gains[l], tabs)
    return loss, dx, dws, dgs, dlnf


SMALL_SIZES = (("ln1_g", 2048), ("g_q_a", 448), ("g_kv_a", 512), ("g_qn_b", 128), ("g_kn_b", 128), ("g_out", 2048),
               ("ln2_g", 2048))


def _pack_small(per_layer, ln_f):
    flat = jnp.concatenate([per_layer[n].reshape(-1) for n, _ in SMALL_SIZES] + [ln_f.reshape(-1)])
    rows = -(-flat.shape[0] // (8 * LANES)) * 8
    return jnp.concatenate([flat, jnp.zeros((rows * LANES - flat.shape[0],), F32)]).reshape(rows, LANES)


def _unpack_small(packed, depth):
    flat, out, lo = packed.reshape(-1), {}, 0
    for n, w in SMALL_SIZES:
        out[n] = flat[lo:lo + depth * w].reshape(depth, w)
        lo += depth * w
    out["ln_f_g"] = flat[lo:lo + 2048]
    return out


def kernel(x, ln1_g, w_in, g_q_a, w_uq, g_kv_a, w_ukv, g_qn_b, g_kn_b, g_out, w_out, ln2_g, w_ff1, w_ff2, ln_f_g, loss_target, m_ln1_g, m_w_in, m_g_q_a, m_w_uq, m_g_kv_a, m_w_ukv, m_g_qn_b, m_g_kn_b, m_g_out, m_w_out, m_ln2_g, m_w_ff1, m_w_ff2, m_ln_f_g, v_ln1_g, v_w_in, v_g_q_a, v_w_uq, v_g_kv_a, v_w_ukv, v_g_qn_b, v_g_kn_b, v_g_out, v_w_out, v_ln2_g, v_w_ff1, v_w_ff2, v_ln_f_g):
    depth = w_in.shape[0]
    S = x.shape[1]
    big_w = {"w_in": w_in, "w_uq": w_uq, "w_ukv": w_ukv, "w_out": w_out, "w_ff1": w_ff1, "w_ff2": w_ff2}
    big_m = {"w_in": m_w_in, "w_uq": m_w_uq, "w_ukv": m_w_ukv, "w_out": m_w_out, "w_ff1": m_w_ff1, "w_ff2": m_w_ff2}
    big_v = {"w_in": v_w_in, "w_uq": v_w_uq, "w_ukv": v_w_ukv, "w_out": v_w_out, "w_ff1": v_w_ff1, "w_ff2": v_w_ff2}
    small_w = {"ln1_g": ln1_g, "g_q_a": g_q_a, "g_kv_a": g_kv_a, "g_qn_b": g_qn_b, "g_kn_b": g_kn_b, "g_out": g_out,
               "ln2_g": ln2_g}
    small_m = {"ln1_g": m_ln1_g, "g_q_a": m_g_q_a, "g_kv_a": m_g_kv_a, "g_qn_b": m_g_qn_b, "g_kn_b": m_g_kn_b,
               "g_out": m_g_out, "ln2_g": m_ln2_g}
    small_v = {"ln1_g": v_ln1_g, "g_q_a": v_g_q_a, "g_kv_a": v_g_kv_a, "g_qn_b": v_g_qn_b, "g_kn_b": v_g_kn_b,
               "g_out": v_g_out, "ln2_g": v_ln2_g}

    weights = []
    for l in range(depth):
        packed = _pack_shards({n: big_w[n][l].astype(CDT) for n in BIG})
        weights.append(_unpack_gathered(_all_gather(packed)))
    gains = [_layer_gains(l, ln1_g, g_q_a, g_kv_a, g_qn_b, g_kn_b, g_out, ln2_g) for l in range(depth)]

    loss_part, dx, dws, dgs, dlnf = _local_step(x.reshape(S, D_MODEL), loss_target.reshape(S, D_MODEL), weights,
                                                 gains, ln_f_g)
    loss = lax.psum(loss_part[0, 0], ("x", "y", "c"))
    grad_x = dx.reshape(1, S, D_MODEL)

    shard_grads = []
    for l in range(depth):
        gs = _pack_grads(dws[l])
        chip = _rs_chip_sum(gs, _rs_sibling_exchange(gs))
        shard_grads.append(_unpack_shard_grads(_rs_final_sum(_rs_chip_exchange(chip))))
    big_g = {n: jnp.stack([shard_grads[l][n] for l in range(depth)]) for n in BIG}

    gain_key = {"ln1_g": "ln1_g", "g_q_a": "gq", "g_kv_a": "gkv", "g_qn_b": "gqn", "g_kn_b": "gkn", "g_out": "g_out",
                "ln2_g": "ln2_g"}
    per_layer = {n: jnp.stack([dgs[l][gain_key[n]].reshape(-1)[:w] for l in range(depth)]) for n, w in SMALL_SIZES}
    small_g = _unpack_small(_all_reduce_small(_pack_small(per_layer, dlnf.reshape(-1))), depth)

    upd = {}
    for n in BIG:
        shp = big_w[n].shape
        two_d = (shp[0] * shp[1], shp[2])
        d, nm, nv = _adamw(big_w[n].reshape(two_d), big_g[n].reshape(two_d), big_m[n].reshape(two_d),
                           big_v[n].reshape(two_d), name="adamw_" + n)
        upd[n] = (d.reshape(shp), nm.reshape(shp), nv.reshape(shp))
    small_w["ln_f_g"], small_m["ln_f_g"], small_v["ln_f_g"] = ln_f_g, m_ln_f_g, v_ln_f_g
    names_small = [n for n, _ in SMALL_SIZES]
    pw = _pack_small({n: small_w[n] for n in names_small}, small_w["ln_f_g"])
    pg = _pack_small({n: small_g[n] for n in names_small}, small_g["ln_f_g"])
    pm = _pack_small({n: small_m[n] for n in names_small}, small_m["ln_f_g"])
    pv = _pack_small({n: small_v[n] for n in names_small}, small_v["ln_f_g"])
    d, nm, nv = _adamw(pw, pg, pm, pv, name="adamw_small")
    sd, snm, snv = _unpack_small(d, depth), _unpack_small(nm, depth), _unpack_small(nv, depth)
    for n in names_small + ["ln_f_g"]:
        upd[n] = (sd[n], snm[n], snv[n])

    order = ["ln1_g", "w_in", "g_q_a", "w_uq", "g_kv_a", "w_ukv", "g_qn_b", "g_kn_b", "g_out", "w_out", "ln2_g", "w_ff1",
             "w_ff2", "ln_f_g"]
    grads = {**big_g, **small_g}
    return (loss, grad_x, *[grads[n] for n in order], *[upd[n][0] for n in order], *[upd[n][1] for n in order],
            *[upd[n][2] for n in order])
```

```python
import functools
import math

import jax
import jax.numpy as jnp
from jax import lax
from jax.experimental import pallas as pl
from jax.experimental.pallas import tpu as pltpu

D_MODEL = 2048
D_FF = 8192
EPS = 1e-6
NEG_INF = -1e30
Q_LORA = 448
ROPE_THETA = 10000.0
GRID_W = 64
DILATIONS = (1, 4, 16)
BAND_HALF = 64
SCALE_A = 1.0 / math.sqrt(192.0)
SCALE_H = 1.0 / math.sqrt(128.0)
ADAM_LR, ADAM_B1, ADAM_B2, ADAM_EPS, ADAM_WD, ADAM_STEP = 0.001, 0.9, 0.999, 1e-08, 0.01, 10

CDT = jnp.bfloat16
F32 = jnp.float32
LANES = 128
VMEM_LIMIT = 56 * 1024 * 1024

PROJ_W = 4608

NN = ((1,), (0,))
NT = ((1,), (1,))
TN = ((0,), (0,))

BIG = ("w_in", "w_uq", "w_ukv", "w_out", "w_ff1", "w_ff2")
SHARD_SHAPE = {"w_in": (2048, 576), "w_uq": (448, 96), "w_ukv": (512, 128), "w_out": (256, 2048),
               "w_ff1": (2048, 1024), "w_ff2": (1024, 2048)}
SHARD_ROWS = {n: s[0] * s[1] // LANES for n, s in SHARD_SHAPE.items()}
PACK_TILE = 2048
PACK_ROWS = -(-sum(SHARD_ROWS.values()) // PACK_TILE) * PACK_TILE
SMALL = ("ln1_g", "g_q_a", "g_kv_a", "g_qn_b", "g_kn_b", "g_out", "ln2_g")


def _dot(a, b, dims):
    return lax.dot_general(a, b, (dims, ((), ())), preferred_element_type=F32)


def _cparams(dims=None):
    return pltpu.CompilerParams(dimension_semantics=dims, vmem_limit_bytes=VMEM_LIMIT)


def _sds(shape, dtype):
    return jax.ShapeDtypeStruct(shape, dtype)


def _matmul(a, b, *, mode, tm, tn, tk, out_dtype, name, epi=None, extra=None):
    if mode == "nn":
        (M, K), (K2, N) = a.shape, b.shape
    elif mode == "nt":
        (M, K), (N, K2) = a.shape, b.shape
    else:
        (K, M), (K2, N) = a.shape, b.shape
    tm, tn, tk = min(tm, M), min(tn, N), min(tk, K)
    assert K == K2 and M % tm == 0 and N % tn == 0 and K % tk == 0, (name, a.shape, b.shape)
    nk = K // tk
    dims = {"nn": NN, "nt": NT, "tn": TN}[mode]
    if mode == "tn":
        a_spec = pl.BlockSpec((tk, tm), lambda i, j, k: (k, i))
    else:
        a_spec = pl.BlockSpec((tm, tk), lambda i, j, k: (i, k))
    if mode == "nt":
        b_spec = pl.BlockSpec((tn, tk), lambda i, j, k: (j, k))
    else:
        b_spec = pl.BlockSpec((tk, tn), lambda i, j, k: (k, j))
    tile = pl.BlockSpec((tm, tn), lambda i, j, k: (i, j))
    n_extra = 1 if epi in ("residual", "drelu2") else 0
    n_out = 2 if epi == "relu2" else 1

    def body(*refs):
        a_ref, b_ref = refs[0], refs[1]
        extra_refs = refs[2:2 + n_extra]
        out_refs = refs[2 + n_extra:2 + n_extra + n_out]

        def finish(acc):
            if epi is None:
                out_refs[0][...] = acc.astype(out_dtype)
            elif epi == "residual":
                out_refs[0][...] = (extra_refs[0][...] + acc).astype(out_dtype)
            elif epi == "relu2":
                out_refs[0][...] = acc.astype(out_dtype)
                r = jnp.maximum(acc, 0.0)
                out_refs[1][...] = (r * r).astype(out_dtype)
            else:
                z = extra_refs[0][...].astype(F32)
                out_refs[0][...] = (acc * (2.0 * jnp.maximum(z, 0.0))).astype(out_dtype)

        part = _dot(a_ref[...], b_ref[...], dims)
        if nk == 1:
            finish(part)
        else:
            acc_ref = refs[-1]
            k = pl.program_id(2)

            @pl.when(k == 0)
            def _():
                acc_ref[...] = part

            @pl.when(k > 0)
            def _():
                acc_ref[...] += part

            @pl.when(k == nk - 1)
            def _():
                finish(acc_ref[...])

    out_shape = [_sds((M, N), out_dtype)] * n_out
    res = pl.pallas_call(
        body, name=name, grid=(M // tm, N // tn, nk),
        in_specs=[a_spec, b_spec] + [tile] * n_extra,
        out_specs=[tile] * n_out, out_shape=out_shape,
        scratch_shapes=[pltpu.VMEM((tm, tn), F32)] if nk > 1 else [],
        compiler_params=_cparams(("parallel", "parallel", "arbitrary")),
    )(a, b, *([extra] if n_extra else []))
    return res if n_out > 1 else res[0]


def _rms_val(x, g, n):
    r = lax.rsqrt(jnp.sum(x * x, axis=-1, keepdims=True) * (1.0 / n) + EPS)
    y = x * r
    return (y if g is None else y * g), r


def _rms_bwd_val(x, g, dy, n):
    r = lax.rsqrt(jnp.sum(x * x, axis=-1, keepdims=True) * (1.0 / n) + EPS)
    xhat = x * r
    dyg = dy if g is None else dy * g
    dx = r * (dyg - xhat * (jnp.sum(dyg * xhat, axis=-1, keepdims=True) * (1.0 / n)))
    return dx, dy * xhat


def _rope_val(x, c, sa, sb, shift):
    return x * c + pltpu.roll(x, LANES - shift, 1) * sa + pltpu.roll(x, shift, 1) * sb


def _rope_t_val(dy, c, sa, sb, shift):
    return dy * c + pltpu.roll(dy * sa, shift, 1) + pltpu.roll(dy * sb, LANES - shift, 1)


def _colsum(x):
    return jnp.sum(x, axis=0, keepdims=True)


def _rope_tables(S):
    pos = lax.broadcasted_iota(jnp.int32, (S, LANES), 0)
    lane = lax.broadcasted_iota(jnp.int32, (S, LANES), 1)

    def tables(p, dim, active):
        half = dim // 2
        inv = jnp.power(ROPE_THETA, -(2 * (lane % half)).astype(F32) / dim)
        a = p.astype(F32) * inv
        first = (lane % dim) < half
        zero = jnp.zeros((S, LANES), F32)
        return (jnp.where(active, jnp.cos(a), zero), jnp.where(active & first, -jnp.sin(a), zero),
                jnp.where(active & ~first, jnp.sin(a), zero))

    tab_a = tables(pos, 64, lane >= 64)
    tab_b = tables(jnp.where(lane < 64, pos // GRID_W, pos % GRID_W), 64, lane >= 0)
    tab_c = tables(pos, 128, lane >= 0)
    return tab_a, tab_b, tab_c


ROPE_SHIFT_AB = 32
ROPE_SHIFT_C = 64


def _rms_fwd(x, g, *, name, tr=512):
    S, W = x.shape
    tr = min(tr, S)

    def body(x_ref, g_ref, o_ref):
        y, _ = _rms_val(x_ref[...], g_ref[...], W)
        o_ref[...] = y.astype(CDT)

    return pl.pallas_call(
        body, name=name, grid=(S // tr,),
        in_specs=[pl.BlockSpec((tr, W), lambda i: (i, 0)), pl.BlockSpec((1, W), lambda i: (0, 0))],
        out_specs=pl.BlockSpec((tr, W), lambda i: (i, 0)), out_shape=_sds((S, W), CDT),
        compiler_params=_cparams(("parallel",)),
    )(x, g.reshape(1, W))


def _rms_bwd(x, g, dy, res, *, name, tr=256):
    S, W = x.shape
    tr = min(tr, S)

    def body(x_ref, g_ref, dy_ref, res_ref, dx_ref, dxb_ref, dg_ref):
        dx, dgt = _rms_bwd_val(x_ref[...], g_ref[...], dy_ref[...], W)
        dx = res_ref[...] + dx
        dx_ref[...] = dx
        dxb_ref[...] = dx.astype(CDT)

        @pl.when(pl.program_id(0) == 0)
        def _():
            dg_ref[...] = jnp.zeros_like(dg_ref)

        dg_ref[...] += _colsum(dgt)

    row = pl.BlockSpec((tr, W), lambda i: (i, 0))
    vec = pl.BlockSpec((1, W), lambda i: (0, 0))
    return pl.pallas_call(
        body, name=name, grid=(S // tr,),
        in_specs=[row, vec, row, row], out_specs=[row, row, vec],
        out_shape=[_sds((S, W), F32), _sds((S, W), CDT), _sds((1, W), F32)],
        compiler_params=_cparams(("arbitrary",)),
    )(x, g.reshape(1, W), dy, res)


def _loss_head(x, g, tgt, *, tr=256):
    S, W = x.shape
    tr = min(tr, S)

    def body(x_ref, g_ref, t_ref, loss_ref, dx_ref, dxb_ref, dg_ref):
        xv, gv = x_ref[...], g_ref[...]
        y, _ = _rms_val(xv, gv, W)
        err = y - t_ref[...]
        part = 0.5 * jnp.sum(jnp.sum(err * err, axis=-1, keepdims=True) * (1.0 / W), axis=0, keepdims=True)
        dx, dgt = _rms_bwd_val(xv, gv, err * (1.0 / W), W)
        dx_ref[...] = dx
        dxb_ref[...] = dx.astype(CDT)

        @pl.when(pl.program_id(0) == 0)
        def _():
            dg_ref[...] = jnp.zeros_like(dg_ref)
            loss_ref[...] = jnp.zeros_like(loss_ref)

        dg_ref[...] += _colsum(dgt)
        loss_ref[...] += jnp.broadcast_to(part, (1, LANES))

    row = pl.BlockSpec((tr, W), lambda i: (i, 0))
    vec = pl.BlockSpec((1, W), lambda i: (0, 0))
    return pl.pallas_call(
        body, name="loss_head", grid=(S // tr,),
        in_specs=[row, vec, row], out_specs=[pl.BlockSpec((1, LANES), lambda i: (0, 0)), row, row, vec],
        out_shape=[_sds((1, LANES), F32), _sds((S, W), F32), _sds((S, W), CDT), _sds((1, W), F32)],
        compiler_params=_cparams(("arbitrary",)),
    )(x, g.reshape(1, W), tgt)


def _tab_specs(tr):
    return [pl.BlockSpec((tr, LANES), lambda i: (i, 0))] * 9


def _prep_fwd(proj, gq, gkv, gqn, gkn, tabs, *, tr=256):
    S = proj.shape[0]
    tr = min(tr, S)

    def body(bq_ref, cq_ref, ck_ref, cv_ref, acq_ref, ackv_ref, bk_ref, bv_ref, gq_ref, gkv_ref, gqn_ref, gkn_ref,
             ac, aa, ab, bc, ba, bb, cc, ca, cb,
             cqn_ref, ckvn_ref, kpe_ref, qb_ref, kb_ref, vb_ref, qc_ref, kc_ref, vc_ref):
        x = acq_ref[...]
        lane = lax.broadcasted_iota(jnp.int32, x.shape, 1)
        xm = jnp.where(lane < Q_LORA, x, 0.0)
        cqn_ref[...] = _rms_val(xm, gq_ref[...], Q_LORA)[0].astype(CDT)
        ckvn_ref[...] = _rms_val(ackv_ref[...], gkv_ref[...], 512)[0].astype(CDT)
        kpe_ref[...] = _rope_val(x[:, 384:512], ac[...], aa[...], ab[...], ROPE_SHIFT_AB).astype(CDT)
        for h in range(6):
            xh = bq_ref[:, LANES * h:LANES * (h + 1)]
            y = _rope_val(_rms_val(xh, gqn_ref[...], LANES)[0], bc[...], ba[...], bb[...], ROPE_SHIFT_AB)
            qb_ref[:, LANES * h:LANES * (h + 1)] = y.astype(CDT)
        for h in range(2):
            xh = bk_ref[:, LANES * h:LANES * (h + 1)]
            y = _rope_val(_rms_val(xh, gkn_ref[...], LANES)[0], bc[...], ba[...], bb[...], ROPE_SHIFT_AB)
            kb_ref[:, LANES * h:LANES * (h + 1)] = y.astype(CDT)
        vb_ref[...] = bv_ref[...].astype(CDT)
        for h in range(6):
            sl = slice(LANES * h, LANES * (h + 1))
            qc_ref[:, sl] = _rope_val(cq_ref[:, sl], cc[...], ca[...], cb[...], ROPE_SHIFT_C).astype(CDT)
            kc_ref[:, sl] = _rope_val(ck_ref[:, sl], cc[...], ca[...], cb[...], ROPE_SHIFT_C).astype(CDT)
        vc_ref[...] = cv_ref[...].astype(CDT)

    def blk(w, j):
        return pl.BlockSpec((tr, w), lambda i: (i, j))

    vec = lambda w: pl.BlockSpec((1, w), lambda i: (0, 0))
    row = lambda w: pl.BlockSpec((tr, w), lambda i: (i, 0))
    return pl.pallas_call(
        body, name="prep_fwd", grid=(S // tr,),
        in_specs=[blk(768, 0), blk(768, 1), blk(768, 2), blk(768, 3), blk(512, 6), blk(512, 7), blk(256, 16),
                  blk(256, 17), vec(512), vec(512), vec(128), vec(128)] + _tab_specs(tr),
        out_specs=[row(512), row(512), row(128), row(768), row(256), row(256), row(768), row(768), row(768)],
        out_shape=[_sds((S, 512), CDT), _sds((S, 512), CDT), _sds((S, 128), CDT), _sds((S, 768), CDT),
                   _sds((S, 256), CDT), _sds((S, 256), CDT), _sds((S, 768), CDT), _sds((S, 768), CDT),
                   _sds((S, 768), CDT)],
        compiler_params=_cparams(("parallel",)),
    )(proj, proj, proj, proj, proj, proj, proj, proj, gq, gkv, gqn, gkn, *tabs[0], *tabs[1], *tabs[2])


def _prep_a2_fwd(qa_raw, kva, kpe, tab_a, *, tr=512):
    S = qa_raw.shape[0]
    tr = min(tr, S)

    def body(q_ref, kva_ref, kpe_ref, ac, aa, ab, qa_ref, ka_ref):
        for h in range(4):
            lo, hi = 2 * LANES * h, 2 * LANES * h + LANES
            qa_ref[:, lo:hi] = q_ref[:, lo:hi].astype(CDT)
            qa_ref[:, hi:hi + LANES] = _rope_val(q_ref[:, hi:hi + LANES], ac[...], aa[...], ab[...],
                                                 ROPE_SHIFT_AB).astype(CDT)
            ka_ref[:, lo:hi] = kva_ref[:, LANES * h:LANES * (h + 1)]
            ka_ref[:, hi:hi + LANES] = kpe_ref[...]

    row = lambda w: pl.BlockSpec((tr, w), lambda i: (i, 0))
    return pl.pallas_call(
        body, name="prep_a2_fwd", grid=(S // tr,),
        in_specs=[row(1024), pl.BlockSpec((tr, 512), lambda i: (i, 0)), row(128)] + _tab_specs(tr)[:3],
        out_specs=[row(1024), row(1024)], out_shape=[_sds((S, 1024), CDT), _sds((S, 1024), CDT)],
        compiler_params=_cparams(("parallel",)),
    )(qa_raw, kva, kpe, *tab_a)


def _prep_a2_bwd(dqa, dka, dva, tab_a, *, tr=512):
    S = dqa.shape[0]
    tr = min(tr, S)

    def body(dq_ref, dk_ref, dv_ref, ac, aa, ab, dqr_ref, dkva_ref, dkr_ref):
        dkpe = jnp.zeros((tr, LANES), F32)
        for h in range(4):
            lo, hi = 2 * LANES * h, 2 * LANES * h + LANES
            dqr_ref[:, lo:hi] = dq_ref[:, lo:hi].astype(CDT)
            dqr_ref[:, hi:hi + LANES] = _rope_t_val(dq_ref[:, hi:hi + LANES], ac[...], aa[...], ab[...],
                                                    ROPE_SHIFT_AB).astype(CDT)
            dkva_ref[:, LANES * h:LANES * (h + 1)] = dk_ref[:, lo:hi].astype(CDT)
            dkpe = dkpe + dk_ref[:, hi:hi + LANES]
        dkva_ref[:, 512:1024] = dv_ref[...].astype(CDT)
        dkr_ref[...] = _rope_t_val(dkpe, ac[...], aa[...], ab[...], ROPE_SHIFT_AB)

    row = lambda w: pl.BlockSpec((tr, w), lambda i: (i, 0))
    return pl.pallas_call(
        body, name="prep_a2_bwd", grid=(S // tr,),
        in_specs=[row(1024), row(1024), row(512)] + _tab_specs(tr)[:3],
        out_specs=[row(1024), row(1024), row(128)],
        out_shape=[_sds((S, 1024), CDT), _sds((S, 1024), CDT), _sds((S, 128), F32)],
        compiler_params=_cparams(("parallel",)),
    )(dqa, dka, dva, *tab_a)


def _prep_bwd(proj, gq, gkv, gqn, gkn, tabs, dcqn, dckvn, dkr, dqb, dkb, dvb, dqc, dkc, dvc, *, tr=128):
    S = proj.shape[0]
    tr = min(tr, S)

    def body(bq_ref, acq_ref, ackv_ref, bk_ref, gq_ref, gkv_ref, gqn_ref, gkn_ref,
             ac, aa, ab, bc, ba, bb, cc, ca, cb,
             dcqn_ref, dckvn_ref, dkr_ref, dqb_ref, dkb_ref, dvb_ref, dqc_ref, dkc_ref, dvc_ref,
             dp_ref, dgq_ref, dgkv_ref, dgqn_ref, dgkn_ref):
        @pl.when(pl.program_id(0) == 0)
        def _():
            dgq_ref[...] = jnp.zeros_like(dgq_ref)
            dgkv_ref[...] = jnp.zeros_like(dgkv_ref)
            dgqn_ref[...] = jnp.zeros_like(dgqn_ref)
            dgkn_ref[...] = jnp.zeros_like(dgkn_ref)

        dgqn = jnp.zeros((1, LANES), F32)
        for h in range(6):
            sl = slice(LANES * h, LANES * (h + 1))
            dy = _rope_t_val(dqb_ref[:, sl], bc[...], ba[...], bb[...], ROPE_SHIFT_AB)
            dx, dgt = _rms_bwd_val(bq_ref[:, sl], gqn_ref[...], dy, LANES)
            dp_ref[:, sl] = dx.astype(CDT)
            dgqn = dgqn + _colsum(dgt)
        dgqn_ref[...] += dgqn
        dgkn = jnp.zeros((1, LANES), F32)
        for h in range(2):
            sl = slice(LANES * h, LANES * (h + 1))
            dy = _rope_t_val(dkb_ref[:, sl], bc[...], ba[...], bb[...], ROPE_SHIFT_AB)
            dx, dgt = _rms_bwd_val(bk_ref[:, sl], gkn_ref[...], dy, LANES)
            dp_ref[:, 4096 + LANES * h:4096 + LANES * (h + 1)] = dx.astype(CDT)
            dgkn = dgkn + _colsum(dgt)
        dgkn_ref[...] += dgkn
        dp_ref[:, 4352:4608] = dvb_ref[...].astype(CDT)
        for h in range(6):
            sl = slice(LANES * h, LANES * (h + 1))
            dq = dqc_ref[0, :, sl] + dqc_ref[1, :, sl] + dqc_ref[2, :, sl]
            dk = dkc_ref[0, :, sl] + dkc_ref[1, :, sl] + dkc_ref[2, :, sl]
            dv = dvc_ref[0, :, sl] + dvc_ref[1, :, sl] + dvc_ref[2, :, sl]
            dp_ref[:, 768 + LANES * h:768 + LANES * (h + 1)] = _rope_t_val(
                dq, cc[...], ca[...], cb[...], ROPE_SHIFT_C).astype(CDT)
            dp_ref[:, 1536 + LANES * h:1536 + LANES * (h + 1)] = _rope_t_val(
                dk, cc[...], ca[...], cb[...], ROPE_SHIFT_C).astype(CDT)
            dp_ref[:, 2304 + LANES * h:2304 + LANES * (h + 1)] = dv.astype(CDT)
        x = acq_ref[...]
        lane = lax.broadcasted_iota(jnp.int32, x.shape, 1)
        xm = jnp.where(lane < Q_LORA, x, 0.0)
        dx, dgt = _rms_bwd_val(xm, gq_ref[...], dcqn_ref[...], Q_LORA)
        dgq_ref[...] += _colsum(dgt)
        dp_ref[:, 3072:3456] = dx[:, 0:384].astype(CDT)
        dp_ref[:, 3456:3584] = (dx[:, 384:512] + dkr_ref[...]).astype(CDT)
        dx, dgt = _rms_bwd_val(ackv_ref[...], gkv_ref[...], dckvn_ref[...], 512)
        dgkv_ref[...] += _colsum(dgt)
        dp_ref[:, 3584:4096] = dx.astype(CDT)

    def blk(w, j):
        return pl.BlockSpec((tr, w), lambda i: (i, j))

    vec = lambda w: pl.BlockSpec((1, w), lambda i: (0, 0))
    row = lambda w: pl.BlockSpec((tr, w), lambda i: (i, 0))
    row3 = pl.BlockSpec((3, tr, 768), lambda i: (0, i, 0))
    return pl.pallas_call(
        body, name="prep_bwd", grid=(S // tr,),
        in_specs=[blk(768, 0), blk(512, 6), blk(512, 7), blk(256, 16), vec(512), vec(512), vec(128), vec(128)]
        + _tab_specs(tr) + [row(512), row(512), row(128), row(768), row(256), row(256), row3, row3, row3],
        out_specs=[row(PROJ_W), vec(512), vec(512), vec(128), vec(128)],
        out_shape=[_sds((S, PROJ_W), CDT), _sds((1, 512), F32), _sds((1, 512), F32), _sds((1, 128), F32),
                   _sds((1, 128), F32)],
        compiler_params=_cparams(("arbitrary",)),
    )(proj, proj, proj, proj, gq, gkv, gqn, gkn, *tabs[0], *tabs[1], *tabs[2],
      dcqn, dckvn, dkr, dqb, dkb, dvb, dqc, dkc, dvc)


def _attn_fwd(q, k, v, *, H, G, dk, dv, v_off, scale, name, tq=512):
    S = q.shape[0]
    tq = min(tq, S)

    def body(q_ref, k_ref, v_ref, o_ref, l_ref):
        s = _dot(q_ref[...], k_ref[...], NT) * scale
        m = jnp.max(s, axis=-1, keepdims=True)
        e = jnp.exp(s - m)
        den = jnp.sum(e, axis=-1, keepdims=True)
        o_ref[...] = _dot(e.astype(CDT), v_ref[...], NN) * (1.0 / den)
        l_ref[...] = jnp.broadcast_to(m + jnp.log(den), (tq, LANES))

    return pl.pallas_call(
        body, name=name, grid=(H, S // tq),
        in_specs=[pl.BlockSpec((tq, dk), lambda h, i: (i, h)), pl.BlockSpec((S, dk), lambda h, i: (0, h // G)),
                  pl.BlockSpec((S, dv), lambda h, i: (0, v_off + h // G))],
        out_specs=[pl.BlockSpec((tq, dv), lambda h, i: (i, h)), pl.BlockSpec((tq, LANES), lambda h, i: (i, h))],
        out_shape=[_sds((S, H * dv), F32), _sds((S, H * LANES), F32)],
        compiler_params=_cparams(("parallel", "parallel")),
    )(q, k, v)


def _attn_bwd(q, k, v, do, lse, delta, *, H, G, dk, dv, v_off, scale, name, tq=256):
    S = q.shape[0]
    tq = min(tq, S)
    Hkv = H // G

    def body(q_ref, k_ref, v_ref, do_ref, l_ref, d_ref, dq_ref, dk_ref, dv_ref):
        @pl.when((pl.program_id(1) == 0) & (pl.program_id(2) == 0))
        def _():
            dk_ref[...] = jnp.zeros_like(dk_ref)
            dv_ref[...] = jnp.zeros_like(dv_ref)

        qv, kv, dov = q_ref[...], k_ref[...], do_ref[...]
        s = _dot(qv, kv, NT) * scale
        p = jnp.exp(s - l_ref[:, 0:1])
        dp = _dot(dov, v_ref[...], NT)
        ds = (p * (dp - d_ref[:, 0:1]) * scale).astype(CDT)
        dq_ref[...] = _dot(ds, kv, NN)
        dk_ref[...] += _dot(ds, qv, TN)
        dv_ref[...] += _dot(p.astype(CDT), dov, TN)

    qi = lambda hk, g, i: (i, hk * G + g)
    return pl.pallas_call(
        body, name=name, grid=(Hkv, G, S // tq),
        in_specs=[pl.BlockSpec((tq, dk), qi), pl.BlockSpec((S, dk), lambda hk, g, i: (0, hk)),
                  pl.BlockSpec((S, dv), lambda hk, g, i: (0, v_off + hk)), pl.BlockSpec((tq, dv), qi),
                  pl.BlockSpec((tq, LANES), qi), pl.BlockSpec((tq, LANES), qi)],
        out_specs=[pl.BlockSpec((tq, dk), qi), pl.BlockSpec((S, dk), lambda hk, g, i: (0, hk)),
                   pl.BlockSpec((S, dv), lambda hk, g, i: (0, hk))],
        out_shape=[_sds((S, H * dk), F32), _sds((S, Hkv * dk), F32), _sds((S, Hkv * dv), F32)],
        compiler_params=_cparams(("parallel", "arbitrary", "arbitrary")),
    )(q, k, v, do, lse, delta)


BAND_TQ = 512
BAND_SUB = 128
BAND_WIN = 384


def _band_geometry(S):
    logs = [int(math.log2(S // d)) for d in DILATIONS]
    assert all(S // d == 1 << lg for d, lg in zip(DILATIONS, logs)) and S >= BAND_WIN and S % BAND_SUB == 0
    return logs


def _band_mask(r0, w0, shift):
    rpos = r0 + lax.broadcasted_iota(jnp.int32, (BAND_SUB, BAND_WIN), 0)
    cpos = w0 + lax.broadcasted_iota(jnp.int32, (BAND_SUB, BAND_WIN), 1)
    return (jnp.abs(rpos - cpos) <= BAND_HALF) & (jnp.right_shift(rpos, shift) == jnp.right_shift(cpos, shift))


def _band_fwd(qs, ks, vs):
    _, S, W = qs.shape
    logs = _band_geometry(S)
    tq = min(BAND_TQ, S)

    def body(q_ref, k_ref, v_ref, o_ref, l_ref):
        b, i = pl.program_id(0), pl.program_id(2)
        shift = jnp.where(b == 0, logs[0], jnp.where(b == 1, logs[1], logs[2]))
        for j in range(tq // BAND_SUB):
            rows = slice(BAND_SUB * j, BAND_SUB * (j + 1))
            r0 = i * tq + BAND_SUB * j
            w0 = pl.multiple_of(jnp.clip(r0 - BAND_SUB, 0, S - BAND_WIN), BAND_SUB)
            kw, vw = k_ref[pl.ds(w0, BAND_WIN), :], v_ref[pl.ds(w0, BAND_WIN), :]
            s = jnp.where(_band_mask(r0, w0, shift), _dot(q_ref[rows, :], kw, NT) * SCALE_H, NEG_INF)
            m = jnp.max(s, axis=-1, keepdims=True)
            e = jnp.exp(s - m)
            den = jnp.sum(e, axis=-1, keepdims=True)
            o_ref[rows, :] = _dot((e * (1.0 / den)).astype(CDT), vw, NN)
            l_ref[rows, :] = jnp.broadcast_to(m + jnp.log(den), (BAND_SUB, LANES))

    tile = pl.BlockSpec((None, tq, LANES), lambda b, h, i: (b, i, h))
    full = pl.BlockSpec((None, S, LANES), lambda b, h, i: (b, 0, h))
    return pl.pallas_call(
        body, name="band_fwd", grid=(3, W // LANES, S // tq),
        in_specs=[tile, full, full], out_specs=[tile, tile],
        out_shape=[_sds((3, S, W), F32), _sds((3, S, W), F32)],
        compiler_params=_cparams(("parallel", "parallel", "parallel")),
    )(qs, ks, vs)


def _band_bwd(qs, ks, vs, dos, lses, dds):
    _, S, W = qs.shape
    logs = _band_geometry(S)
    tq = min(BAND_TQ, S)

    def body(q_ref, k_ref, v_ref, do_ref, l_ref, d_ref, dq_ref, dk_ref, dv_ref):
        b, i = pl.program_id(0), pl.program_id(2)

        @pl.when(i == 0)
        def _():
            dk_ref[...] = jnp.zeros_like(dk_ref)
            dv_ref[...] = jnp.zeros_like(dv_ref)

        shift = jnp.where(b == 0, logs[0], jnp.where(b == 1, logs[1], logs[2]))
        for j in range(tq // BAND_SUB):
            rows = slice(BAND_SUB * j, BAND_SUB * (j + 1))
            r0 = i * tq + BAND_SUB * j
            w0 = pl.multiple_of(jnp.clip(r0 - BAND_SUB, 0, S - BAND_WIN), BAND_SUB)
            win = pl.ds(w0, BAND_WIN)
            qv, dov, kw = q_ref[rows, :], do_ref[rows, :], k_ref[win, :]
            s = jnp.where(_band_mask(r0, w0, shift), _dot(qv, kw, NT) * SCALE_H, NEG_INF)
            p = jnp.exp(s - l_ref[rows, 0:1])
            dp = _dot(dov, v_ref[win, :], NT)
            ds = (p * (dp - d_ref[rows, 0:1]) * SCALE_H).astype(CDT)
            dq_ref[rows, :] = _dot(ds, kw, NN)
            dk_ref[win, :] += _dot(ds, qv, TN)
            dv_ref[win, :] += _dot(p.astype(CDT), dov, TN)

    tile = pl.BlockSpec((None, tq, LANES), lambda b, h, i: (b, i, h))
    full = pl.BlockSpec((None, S, LANES), lambda b, h, i: (b, 0, h))
    return pl.pallas_call(
        body, name="band_bwd", grid=(3, W // LANES, S // tq),
        in_specs=[tile, full, full, tile, tile, tile], out_specs=[tile, full, full],
        out_shape=[_sds((3, S, W), F32)] * 3,
        compiler_params=_cparams(("parallel", "parallel", "arbitrary")),
    )(qs, ks, vs, dos, lses, dds)


def _combine(ob, lb, *, tr=256):
    _, S, W = ob.shape
    tr = min(tr, S)

    def body(o_ref, l_ref, out_ref, lse_ref):
        l0, l1, l2 = l_ref[0], l_ref[1], l_ref[2]
        m = jnp.maximum(jnp.maximum(l0, l1), l2)
        e0, e1, e2 = jnp.exp(l0 - m), jnp.exp(l1 - m), jnp.exp(l2 - m)
        den = e0 + e1 + e2
        inv = 1.0 / den
        out_ref[...] = (e0 * inv) * o_ref[0] + (e1 * inv) * o_ref[1] + (e2 * inv) * o_ref[2]
        lse_ref[...] = m + jnp.log(den)

    blk3 = pl.BlockSpec((3, tr, W), lambda i: (0, i, 0))
    row = pl.BlockSpec((tr, W), lambda i: (i, 0))
    return pl.pallas_call(
        body, name="combine", grid=(S // tr,), in_specs=[blk3, blk3], out_specs=[row, row],
        out_shape=[_sds((S, W), F32), _sds((S, W), F32)], compiler_params=_cparams(("parallel",)),
    )(ob, lb)


def _to_branches(a):
    S, C = a.shape
    return jnp.stack([a.reshape(S // d, d, C).transpose(1, 0, 2).reshape(S, C) for d in DILATIONS])


def _from_branches(a):
    _, S, C = a.shape
    return jnp.stack([a[b].reshape(d, S // d, C).transpose(1, 0, 2).reshape(S, C) for b, d in enumerate(DILATIONS)])


def _outnorm_fwd(oa, ob, oc, g, *, tr=256):
    S = oa.shape[0]
    tr = min(tr, S)

    def body(a_ref, b_ref, c_ref, g_ref, m_ref):
        m_ref[:, 0:512] = (_rms_val(a_ref[...], None, 512)[0] * g_ref[:, 0:512]).astype(CDT)
        m_ref[:, 512:1280] = (_rms_val(b_ref[...], None, 768)[0] * g_ref[:, 512:1280]).astype(CDT)
        m_ref[:, 1280:2048] = (_rms_val(c_ref[...], None, 768)[0] * g_ref[:, 1280:2048]).astype(CDT)

    row = lambda w: pl.BlockSpec((tr, w), lambda i: (i, 0))
    return pl.pallas_call(
        body, name="outnorm_fwd", grid=(S // tr,),
        in_specs=[row(512), row(768), row(768), pl.BlockSpec((1, 2048), lambda i: (0, 0))],
        out_specs=row(2048), out_shape=_sds((S, 2048), CDT), compiler_params=_cparams(("parallel",)),
    )(oa, ob, oc, g)


def _outnorm_bwd(oa, ob, oc, g, dm, *, tr=256):
    S = oa.shape[0]
    tr = min(tr, S)

    def body(a_ref, b_ref, c_ref, g_ref, dm_ref, doa_ref, dob_ref, doc_ref, da_ref, db_ref, dc_ref, dg_ref):
        @pl.when(pl.program_id(0) == 0)
        def _():
            dg_ref[...] = jnp.zeros_like(dg_ref)

        for o_ref, do_ref, d_ref, lo, w in ((a_ref, doa_ref, da_ref, 0, 512), (b_ref, dob_ref, db_ref, 512, 768),
                                            (c_ref, doc_ref, dc_ref, 1280, 768)):
            o = o_ref[...]
            dmv = dm_ref[:, lo:lo + w]
            do, dgt = _rms_bwd_val(o, None, dmv * g_ref[:, lo:lo + w], w)
            r = lax.rsqrt(jnp.sum(o * o, axis=-1, keepdims=True) * (1.0 / w) + EPS)
            dg_ref[:, lo:lo + w] += _colsum(dmv * (o * r))
            do_ref[...] = do.astype(CDT)
            for h in range(w // LANES):
                sl = slice(LANES * h, LANES * (h + 1))
                d_ref[:, sl] = jnp.broadcast_to(jnp.sum(do[:, sl] * o[:, sl], axis=-1, keepdims=True), (tr, LANES))

    row = lambda w: pl.BlockSpec((tr, w), lambda i: (i, 0))
    vec = pl.BlockSpec((1, 2048), lambda i: (0, 0))
    return pl.pallas_call(
        body, name="outnorm_bwd", grid=(S // tr,),
        in_specs=[row(512), row(768), row(768), vec, row(2048)],
        out_specs=[row(512), row(768), row(768), row(512), row(768), row(768), vec],
        out_shape=[_sds((S, 512), CDT), _sds((S, 768), CDT), _sds((S, 768), CDT), _sds((S, 512), F32),
                   _sds((S, 768), F32), _sds((S, 768), F32), _sds((1, 2048), F32)],
        compiler_params=_cparams(("arbitrary",)),
    )(oa, ob, oc, g, dm)


MESH = pl.DeviceIdType.MESH
ANY = pl.BlockSpec(memory_space=pl.ANY)


def _all_gather(shard):
    R = shard.shape[0]

    def body(x_ref, out_ref, send_sems, recv_sems, local_sem):
        x, y, c = lax.axis_index("x"), lax.axis_index("y"), lax.axis_index("c")
        me, sibling = (x, y, c), (x, y, 1 - c)
        chips = [(1 - x, y), (x, 1 - y), (1 - x, 1 - y)]

        def rows(px, py, pc):
            return out_ref.at[4 * px + 2 * py + pc]

        def copy(k, block, to, src=None):
            return pltpu.make_async_remote_copy(
                src_ref=rows(*block) if src is None else src, dst_ref=rows(*block),
                send_sem=send_sems.at[k], recv_sem=recv_sems.at[k], device_id=to, device_id_type=MESH)

        mine = pltpu.make_async_copy(x_ref, rows(*me), local_sem)
        mine.start()
        first = [copy(0, me, sibling, src=x_ref)]
        first += [copy(1 + j, me, (*chip, c), src=x_ref) for j, chip in enumerate(chips)]
        for cp in first:
            cp.start()
        passed = [copy(4 + j, (*chip, c), sibling) for j, chip in enumerate(chips)]
        for j, chip in enumerate(chips):
            copy(1 + j, (*chip, c), me).wait_recv()
            passed[j].start()
        copy(0, sibling, me).wait_recv()
        for j, chip in enumerate(chips):
            copy(4 + j, (*chip, 1 - c), me).wait_recv()
        for cp in first + passed:
            cp.wait_send()
        mine.wait()

    return pl.pallas_call(
        body, name="all_gather", out_shape=_sds((8, R, LANES), shard.dtype), in_specs=[ANY], out_specs=ANY,
        scratch_shapes=[pltpu.SemaphoreType.DMA((7,)), pltpu.SemaphoreType.DMA((7,)), pltpu.SemaphoreType.DMA],
    )(shard)


def _rs_sibling_exchange(gs):
    _, _, R, _ = gs.shape

    def body(g_ref, out_ref, send_sem, recv_sem):
        x, y, c = lax.axis_index("x"), lax.axis_index("y"), lax.axis_index("c")
        cp = pltpu.make_async_remote_copy(src_ref=g_ref.at[1 - c], dst_ref=out_ref, send_sem=send_sem,
                                          recv_sem=recv_sem, device_id=(x, y, 1 - c), device_id_type=MESH)
        cp.start()
        cp.wait()

    return pl.pallas_call(
        body, name="rs_sibling_exchange", out_shape=_sds((4, R, LANES), gs.dtype), in_specs=[ANY], out_specs=ANY,
        scratch_shapes=[pltpu.SemaphoreType.DMA, pltpu.SemaphoreType.DMA],
    )(gs)


def _rs_chip_sum(gs, got, *, tr=PACK_TILE):
    _, _, R, _ = gs.shape
    tr = math.gcd(R, tr)
    core = lax.axis_index("c").astype(jnp.int32).reshape(1)

    def body(c_ref, a_ref, b_ref, o_ref):
        o_ref[...] = (a_ref[...].astype(F32) + b_ref[...].astype(F32)).astype(o_ref.dtype)

    spec = pltpu.PrefetchScalarGridSpec(
        num_scalar_prefetch=1, grid=(4, R // tr),
        in_specs=[pl.BlockSpec((None, None, tr, LANES), lambda k, i, c: (c[0], k, i, 0)),
                  pl.BlockSpec((None, tr, LANES), lambda k, i, c: (k, i, 0))],
        out_specs=pl.BlockSpec((None, tr, LANES), lambda k, i, c: (k, i, 0)))
    return pl.pallas_call(body, name="rs_chip_sum", grid_spec=spec, out_shape=_sds((4, R, LANES), gs.dtype),
                          compiler_params=_cparams(("parallel", "parallel")))(core, gs, got)


def _rs_chip_exchange(p):
    _, R, _ = p.shape

    def body(p_ref, out_ref, send_sems, recv_sems, local_sem):
        x, y, c = lax.axis_index("x"), lax.axis_index("y"), lax.axis_index("c")
        chips = [(1 - x, y), (x, 1 - y), (1 - x, 1 - y)]
        my_chip = 2 * x + y
        mine = pltpu.make_async_copy(p_ref.at[my_chip], out_ref.at[my_chip], local_sem)
        mine.start()
        sends = [pltpu.make_async_remote_copy(
            src_ref=p_ref.at[2 * cx + cy], dst_ref=out_ref.at[my_chip], send_sem=send_sems.at[j],
            recv_sem=recv_sems.at[j], device_id=(cx, cy, c), device_id_type=MESH) for j, (cx, cy) in enumerate(chips)]
        for cp in sends:
            cp.start()
        for j, (cx, cy) in enumerate(chips):
            pltpu.make_async_remote_copy(
                src_ref=p_ref.at[my_chip], dst_ref=out_ref.at[2 * cx + cy], send_sem=send_sems.at[j],
                recv_sem=recv_sems.at[j], device_id=(cx, cy, c), device_id_type=MESH).wait_recv()
        for cp in sends:
            cp.wait_send()
        mine.wait()

    return pl.pallas_call(
        body, name="rs_chip_exchange", out_shape=_sds((4, R, LANES), p.dtype), in_specs=[ANY], out_specs=ANY,
        scratch_shapes=[pltpu.SemaphoreType.DMA((3,)), pltpu.SemaphoreType.DMA((3,)), pltpu.SemaphoreType.DMA],
    )(p)


def _rs_final_sum(r, *, tr=PACK_TILE):
    _, R, _ = r.shape
    tr = math.gcd(R, tr)

    def body(r_ref, o_ref):
        o_ref[...] = ((r_ref[0].astype(F32) + r_ref[1].astype(F32)) + r_ref[2].astype(F32)) + r_ref[3].astype(F32)

    return pl.pallas_call(
        body, name="rs_final_sum", grid=(R // tr,), in_specs=[pl.BlockSpec((4, tr, LANES), lambda i: (0, i, 0))],
        out_specs=pl.BlockSpec((tr, LANES), lambda i: (i, 0)), out_shape=_sds((R, LANES), F32),
        compiler_params=_cparams(("parallel",)))(r)


def _all_reduce_small(v):
    R = v.shape[0]

    def body(v_ref, out_ref, buf_ref, send_sems, recv_sems):
        x, y, c = lax.axis_index("x"), lax.axis_index("y"), lax.axis_index("c")
        me = 4 * x + 2 * y + c
        buf_ref[me] = v_ref[...]
        peers = []
        for r in range(1, 8):
            px, py, pc = x ^ (r >> 2), y ^ ((r >> 1) & 1), c ^ (r & 1)
            peers.append((r, (px, py, pc), 4 * px + 2 * py + pc))
        sends = [pltpu.make_async_remote_copy(
            src_ref=v_ref, dst_ref=buf_ref.at[me], send_sem=send_sems.at[r - 1], recv_sem=recv_sems.at[r - 1],
            device_id=dev, device_id_type=MESH) for r, dev, _ in peers]
        for cp in sends:
            cp.start()
        for r, dev, idx in peers:
            pltpu.make_async_remote_copy(
                src_ref=v_ref, dst_ref=buf_ref.at[idx], send_sem=send_sems.at[r - 1], recv_sem=recv_sems.at[r - 1],
                device_id=dev, device_id_type=MESH).wait_recv()
        for cp in sends:
            cp.wait_send()
        acc = buf_ref[0]
        for k in range(1, 8):
            acc = acc + buf_ref[k]
        out_ref[...] = acc

    vm = pl.BlockSpec(memory_space=pltpu.VMEM)
    return pl.pallas_call(
        body, name="all_reduce_small", out_shape=_sds((R, LANES), F32), in_specs=[vm], out_specs=vm,
        scratch_shapes=[pltpu.VMEM((8, R, LANES), F32), pltpu.SemaphoreType.DMA((7,)), pltpu.SemaphoreType.DMA((7,))],
    )(v)


def _adamw(w, g, m, v, *, name):
    R, C = w.shape
    tr = R
    for cand in (1024, 512, 256, 128, 64, 32, 16, 8):
        if R % cand == 0 and cand * C * 4 <= 2 * 1024 * 1024:
            tr = cand
            break

    def body(w_ref, g_ref, m_ref, v_ref, d_ref, nm_ref, nv_ref):
        gv = g_ref[...]
        mn = ADAM_B1 * m_ref[...] + (1.0 - ADAM_B1) * gv
        vn = ADAM_B2 * v_ref[...] + (1.0 - ADAM_B2) * (gv * gv)
        m_hat = mn / (1.0 - ADAM_B1 ** ADAM_STEP)
        v_hat = vn / (1.0 - ADAM_B2 ** ADAM_STEP)
        d_ref[...] = -ADAM_LR * (m_hat / (jnp.sqrt(v_hat) + ADAM_EPS) + ADAM_WD * w_ref[...])
        nm_ref[...] = mn
        nv_ref[...] = vn

    blk = pl.BlockSpec((tr, C), lambda i: (i, 0))
    return pl.pallas_call(
        body, name=name, grid=(R // tr,), in_specs=[blk] * 4, out_specs=[blk] * 3,
        out_shape=[_sds((R, C), F32)] * 3, compiler_params=_cparams(("parallel",)))(w, g, m, v)


def _win_perm(w):
    return jnp.concatenate([w[..., 1024:1792], w[..., 2304:4608], w[..., 0:448], w[..., 960:1024],
                            w[..., 448:960], w[..., 1792:2304]], axis=-1)


def _win_unperm(w):
    return jnp.concatenate([w[..., 3072:3520], w[..., 3584:4096], w[..., 3520:3584], w[..., 0:768],
                            w[..., 4096:4608], w[..., 768:3072]], axis=-1)


def _wuq_pad(w):
    w = w.reshape(448, 4, 192)
    z = jnp.zeros((448, 4, 64), w.dtype)
    w = jnp.concatenate([w[:, :, 0:128], z, w[:, :, 128:192]], axis=2).reshape(448, 1024)
    return jnp.concatenate([w, jnp.zeros((64, 1024), w.dtype)], axis=0)


def _wuq_unpad(w):
    w = w[0:448].reshape(448, 4, 256)
    return jnp.concatenate([w[:, :, 0:128], w[:, :, 192:256]], axis=2).reshape(448, 768)


def _wukv_perm(w):
    return w.reshape(512, 4, 2, 128).transpose(0, 2, 1, 3).reshape(512, 1024)


def _wukv_unperm(w):
    return w.reshape(512, 2, 4, 128).transpose(0, 2, 1, 3).reshape(512, 1024)


def _pack_shards(shards):
    parts = [shards[n].reshape(SHARD_ROWS[n], LANES) for n in BIG]
    pad = jnp.zeros((PACK_ROWS - sum(SHARD_ROWS.values()), LANES), parts[0].dtype)
    return jnp.concatenate(parts + [pad], axis=0)


def _unpack_gathered(g):
    out, lo = {}, 0
    for n in BIG:
        r, c = SHARD_SHAPE[n]
        blk = g[:, lo:lo + SHARD_ROWS[n]].reshape(8, r, c)
        lo += SHARD_ROWS[n]
        if n in ("w_out", "w_ff2"):
            out[n] = blk.reshape(8 * r, c)
        else:
            out[n] = blk.transpose(1, 0, 2).reshape(r, 8 * c)
    out["w_in"] = _win_perm(out["w_in"])
    out["w_uq"] = _wuq_pad(out["w_uq"])
    out["w_ukv"] = _wukv_perm(out["w_ukv"])
    return out


def _pack_grads(dw):
    dw = dict(dw)
    dw["w_in"] = _win_unperm(dw["w_in"])
    dw["w_uq"] = _wuq_unpad(dw["w_uq"])
    dw["w_ukv"] = _wukv_unperm(dw["w_ukv"])
    parts = []
    for n in BIG:
        r, c = SHARD_SHAPE[n]
        if n in ("w_out", "w_ff2"):
            blk = dw[n].reshape(8, r, c)
        else:
            blk = dw[n].reshape(r, 8, c).transpose(1, 0, 2)
        parts.append(blk.reshape(8, SHARD_ROWS[n], LANES))
    parts.append(jnp.zeros((8, PACK_ROWS - sum(SHARD_ROWS.values()), LANES), parts[0].dtype))
    packed = jnp.concatenate(parts, axis=1)
    return packed.reshape(4, 2, PACK_ROWS, LANES).transpose(1, 0, 2, 3)


def _unpack_shard_grads(g):
    out, lo = {}, 0
    for n in BIG:
        out[n] = g[lo:lo + SHARD_ROWS[n]].reshape(SHARD_SHAPE[n])
        lo += SHARD_ROWS[n]
    return out


def _layer_fwd(x, W, G, tabs):
    s = {"x0": x}
    s["h1"] = _rms_fwd(x, G["ln1_g"], name="rms1_fwd")
    s["proj"] = _matmul(s["h1"], W["w_in"], mode="nn", tm=1024, tn=768, tk=2048, out_dtype=F32, name="mm_in")
    (s["cqn"], s["ckvn"], kpe, s["qb"], s["kb"], s["vb"], qc, kc, vc) = _prep_fwd(
        s["proj"], G["gq"], G["gkv"], G["gqn"], G["gkn"], tabs)
    qa_raw = _matmul(s["cqn"], W["w_uq"], mode="nn", tm=1024, tn=1024, tk=512, out_dtype=F32, name="mm_uq")
    s["kva"] = _matmul(s["ckvn"], W["w_ukv"], mode="nn", tm=1024, tn=1024, tk=512, out_dtype=CDT, name="mm_ukv")
    s["qa"], s["ka"] = _prep_a2_fwd(qa_raw, s["kva"], kpe, tabs[0])
    s["oa"], s["lse_a"] = _attn_fwd(s["qa"], s["ka"], s["kva"], H=4, G=1, dk=256, dv=128, v_off=4, scale=SCALE_A,
                                    name="attn_a_fwd")
    s["ob"], s["lse_b"] = _attn_fwd(s["qb"], s["kb"], s["vb"], H=6, G=3, dk=128, dv=128, v_off=0, scale=SCALE_H,
                                    name="attn_b_fwd")
    s["qs"], s["ks"], s["vs"] = _to_branches(qc), _to_branches(kc), _to_branches(vc)
    o_br, l_br = _band_fwd(s["qs"], s["ks"], s["vs"])
    s["oc"], s["lse_c"] = _combine(_from_branches(o_br), _from_branches(l_br))
    s["mixed"] = _outnorm_fwd(s["oa"], s["ob"], s["oc"], G["g_out"])
    s["x1"] = _matmul(s["mixed"], W["w_out"], mode="nn", tm=1024, tn=1024, tk=2048, out_dtype=F32, name="mm_out",
                      epi="residual", extra=x)
    s["h2"] = _rms_fwd(s["x1"], G["ln2_g"], name="rms2_fwd")
    s["z"], s["u"] = _matmul(s["h2"], W["w_ff1"], mode="nn", tm=1024, tn=1024, tk=2048, out_dtype=CDT, name="mm_ff1",
                             epi="relu2")
    x2 = _matmul(s["u"], W["w_ff2"], mode="nn", tm=1024, tn=1024, tk=1024, out_dtype=F32, name="mm_ff2",
                 epi="residual", extra=s["x1"])
    return x2, s


def _layer_bwd(dx2, dx2b, s, W, G, tabs):
    dw, dg = {}, {}
    dz = _matmul(dx2b, W["w_ff2"], mode="nt", tm=1024, tn=1024, tk=2048, out_dtype=CDT, name="mm_ff2_dx",
                 epi="drelu2", extra=s["z"])
    dw["w_ff2"] = _matmul(s["u"], dx2b, mode="tn", tm=2048, tn=2048, tk=512, out_dtype=CDT, name="mm_ff2_dw")
    dh2 = _matmul(dz, W["w_ff1"], mode="nt", tm=1024, tn=1024, tk=1024, out_dtype=F32, name="mm_ff1_dx")
    dw["w_ff1"] = _matmul(s["h2"], dz, mode="tn", tm=2048, tn=2048, tk=512, out_dtype=CDT, name="mm_ff1_dw")
    dx1, dx1b, dg["ln2_g"] = _rms_bwd(s["x1"], G["ln2_g"], dh2, dx2, name="rms2_bwd")
    dmixed = _matmul(dx1b, W["w_out"], mode="nt", tm=1024, tn=1024, tk=2048, out_dtype=F32, name="mm_out_dx")
    dw["w_out"] = _matmul(s["mixed"], dx1b, mode="tn", tm=2048, tn=2048, tk=512, out_dtype=CDT, name="mm_out_dw")
    doa, dob, doc, dla, dlb, dlc, dg["g_out"] = _outnorm_bwd(s["oa"], s["ob"], s["oc"], G["g_out"], dmixed)
    dqa, dka, dva = _attn_bwd(s["qa"], s["ka"], s["kva"], doa, s["lse_a"], dla, H=4, G=1, dk=256, dv=128, v_off=4,
                              scale=SCALE_A, name="attn_a_bwd")
    dqb, dkb, dvb = _attn_bwd(s["qb"], s["kb"], s["vb"], dob, s["lse_b"], dlb, H=6, G=3, dk=128, dv=128, v_off=0,
                              scale=SCALE_H, name="attn_b_bwd")
    dq_br, dk_br, dv_br = _band_bwd(s["qs"], s["ks"], s["vs"], _to_branches(doc), _to_branches(s["lse_c"]),
                                    _to_branches(dlc))
    dqa_raw, dkva, dkr = _prep_a2_bwd(dqa, dka, dva, tabs[0])
    dcqn = _matmul(dqa_raw, W["w_uq"], mode="nt", tm=1024, tn=512, tk=1024, out_dtype=F32, name="mm_uq_dx")
    dw["w_uq"] = _matmul(s["cqn"], dqa_raw, mode="tn", tm=512, tn=1024, tk=512, out_dtype=CDT, name="mm_uq_dw")
    dckvn = _matmul(dkva, W["w_ukv"], mode="nt", tm=1024, tn=512, tk=1024, out_dtype=F32, name="mm_ukv_dx")
    dw["w_ukv"] = _matmul(s["ckvn"], dkva, mode="tn", tm=512, tn=1024, tk=512, out_dtype=CDT, name="mm_ukv_dw")
    dproj, dg["gq"], dg["gkv"], dg["gqn"], dg["gkn"] = _prep_bwd(
        s["proj"], G["gq"], G["gkv"], G["gqn"], G["gkn"], tabs, dcqn, dckvn, dkr, dqb, dkb, dvb,
        _from_branches(dq_br), _from_branches(dk_br), _from_branches(dv_br))
    dh1 = _matmul(dproj, W["w_in"], mode="nt", tm=1024, tn=1024, tk=1536, out_dtype=F32, name="mm_in_dx")
    dw["w_in"] = _matmul(s["h1"], dproj, mode="tn", tm=2048, tn=1536, tk=512, out_dtype=CDT, name="mm_in_dw")
    dx0, dx0b, dg["ln1_g"] = _rms_bwd(s["x0"], G["ln1_g"], dh1, dx1, name="rms1_bwd")
    return dx0, dx0b, dw, dg


def _layer_gains(l, ln1_g, g_q_a, g_kv_a, g_qn_b, g_kn_b, g_out, ln2_g):
    return {"ln1_g": ln1_g[l], "ln2_g": ln2_g[l], "g_out": g_out[l].reshape(1, 2048),
            "gq": jnp.concatenate([g_q_a[l], jnp.zeros((64,), F32)]).reshape(1, 512),
            "gkv": g_kv_a[l].reshape(1, 512), "gqn": g_qn_b[l].reshape(1, 128), "gkn": g_kn_b[l].reshape(1, 128)}


def _local_step(x, tgt, weights, gains, ln_f_g):
    S = x.shape[0]
    tabs = _rope_tables(S)
    depth = len(weights)
    saved = []
    for l in range(depth):
        x, s = _layer_fwd(x, weights[l], gains[l], tabs)
        saved.append(s)
    loss, dx, dxb, dlnf = _loss_head(x, ln_f_g, tgt)
    dws, dgs = [None] * depth, [None] * depth
    for l in reversed(range(depth)):
        dx, dxb, dws[l], dgs[l] = _layer_bwd(dx, dxb, saved[l], weights[l], gains[l], tabs)
    return loss, dx, dws, dgs, dlnf


SMALL_SIZES = (("ln1_g", 2048), ("g_q_a", 448), ("g_kv_a", 512), ("g_qn_b", 128), ("g_kn_b", 128), ("g_out", 2048),
               ("ln2_g", 2048))


def _pack_small(per_layer, ln_f):
    flat = jnp.concatenate([per_layer[n].reshape(-1) for n, _ in SMALL_SIZES] + [ln_f.reshape(-1)])
    rows = -(-flat.shape[0] // (8 * LANES)) * 8
    return jnp.concatenate([flat, jnp.zeros((rows * LANES - flat.shape[0],), F32)]).reshape(rows, LANES)


def _unpack_small(packed, depth):
    flat, out, lo = packed.reshape(-1), {}, 0
    for n, w in SMALL_SIZES:
        out[n] = flat[lo:lo + depth * w].reshape(depth, w)
        lo += depth * w
    out["ln_f_g"] = flat[lo:lo + 2048]
    return out


def kernel(x, ln1_g, w_in, g_q_a, w_uq, g_kv_a, w_ukv, g_qn_b, g_kn_b, g_out, w_out, ln2_g, w_ff1, w_ff2, ln_f_g, loss_target, m_ln1_g, m_w_in, m_g_q_a, m_w_uq, m_g_kv_a, m_w_ukv, m_g_qn_b, m_g_kn_b, m_g_out, m_w_out, m_ln2_g, m_w_ff1, m_w_ff2, m_ln_f_g, v_ln1_g, v_w_in, v_g_q_a, v_w_uq, v_g_kv_a, v_w_ukv, v_g_qn_b, v_g_kn_b, v_g_out, v_w_out, v_ln2_g, v_w_ff1, v_w_ff2, v_ln_f_g):
    depth = w_in.shape[0]
    S = x.shape[1]
    big_w = {"w_in": w_in, "w_uq": w_uq, "w_ukv": w_ukv, "w_out": w_out, "w_ff1": w_ff1, "w_ff2": w_ff2}
    big_m = {"w_in": m_w_in, "w_uq": m_w_uq, "w_ukv": m_w_ukv, "w_out": m_w_out, "w_ff1": m_w_ff1, "w_ff2": m_w_ff2}
    big_v = {"w_in": v_w_in, "w_uq": v_w_uq, "w_ukv": v_w_ukv, "w_out": v_w_out, "w_ff1": v_w_ff1, "w_ff2": v_w_ff2}
    small_w = {"ln1_g": ln1_g, "g_q_a": g_q_a, "g_kv_a": g_kv_a, "g_qn_b": g_qn_b, "g_kn_b": g_kn_b, "g_out": g_out,
               "ln2_g": ln2_g}
    small_m = {"ln1_g": m_ln1_g, "g_q_a": m_g_q_a, "g_kv_a": m_g_kv_a, "g_qn_b": m_g_qn_b, "g_kn_b": m_g_kn_b,
               "g_out": m_g_out, "ln2_g": m_ln2_g}
    small_v = {"ln1_g": v_ln1_g, "g_q_a": v_g_q_a, "g_kv_a": v_g_kv_a, "g_qn_b": v_g_qn_b, "g_kn_b": v_g_kn_b,
               "g_out": v_g_out, "ln2_g": v_ln2_g}

    weights = []
    for l in range(depth):
        packed = _pack_shards({n: big_w[n][l].astype(CDT) for n in BIG})
        weights.append(_unpack_gathered(_all_gather(packed)))
    gains = [_layer_gains(l, ln1_g, g_q_a, g_kv_a, g_qn_b, g_kn_b, g_out, ln2_g) for l in range(depth)]

    loss_part, dx, dws, dgs, dlnf = _local_step(x.reshape(S, D_MODEL), loss_target.reshape(S, D_MODEL), weights,
                                                 gains, ln_f_g)
    loss = lax.psum(loss_part[0, 0], ("x", "y", "c"))
    grad_x = dx.reshape(1, S, D_MODEL)

    shard_grads = []
    for l in range(depth):
        gs = _pack_grads(dws[l])
        chip = _rs_chip_sum(gs, _rs_sibling_exchange(gs))
        shard_grads.append(_unpack_shard_grads(_rs_final_sum(_rs_chip_exchange(chip))))
    big_g = {n: jnp.stack([shard_grads[l][n] for l in range(depth)]) for n in BIG}

    gain_key = {"ln1_g": "ln1_g", "g_q_a": "gq", "g_kv_a": "gkv", "g_qn_b": "gqn", "g_kn_b": "gkn", "g_out": "g_out",
                "ln2_g": "ln2_g"}
    per_layer = {n: jnp.stack([dgs[l][gain_key[n]].reshape(-1)[:w] for l in range(depth)]) for n, w in SMALL_SIZES}
    small_g = _unpack_small(_all_reduce_small(_pack_small(per_layer, dlnf.reshape(-1))), depth)

    upd = {}
    for n in BIG:
        shp = big_w[n].shape
        two_d = (shp[0] * shp[1], shp[2])
        d, nm, nv = _adamw(big_w[n].reshape(two_d), big_g[n].reshape(two_d), big_m[n].reshape(two_d),
                           big_v[n].reshape(two_d), name="adamw_" + n)
        upd[n] = (d.reshape(shp), nm.reshape(shp), nv.reshape(shp))
    small_w["ln_f_g"], small_m["ln_f_g"], small_v["ln_f_g"] = ln_f_g, m_ln_f_g, v_ln_f_g
    names_small = [n for n, _ in SMALL_SIZES]
    pw = _pack_small({n: small_w[n] for n in names_small}, small_w["ln_f_g"])
    pg = _pack_small({n: small_g[n] for n in names_small}, small_g["ln_f_g"])
    pm = _pack_small({n: small_m[n] for n in names_small}, small_m["ln_f_g"])
    pv = _pack_small({n: small_v[n] for n in names_small}, small_v["ln_f_g"])
    d, nm, nv = _adamw(pw, pg, pm, pv, name="adamw_small")
    sd, snm, snv = _unpack_small(d, depth), _unpack_small(nm, depth), _unpack_small(nv, depth)
    for n in names_small + ["ln_f_g"]:
        upd[n] = (sd[n], snm[n], snv[n])

    order = ["ln1_g", "w_in", "g_q_a", "w_uq", "g_kv_a", "w_ukv", "g_qn_b", "g_kn_b", "g_out", "w_out", "ln2_g", "w_ff1",
             "w_ff2", "ln_f_g"]
    grads = {**big_g, **small_g}
    return (loss, grad_x, *[grads[n] for n in order], *[upd[n][0] for n in order], *[upd[n][1] for n in order],
            *[upd[n][2] for n in order])
```

```python
import functools
import math

import jax
import jax.numpy as jnp
from jax import lax
from jax.experimental import pallas as pl
from jax.experimental.pallas import tpu as pltpu

D_MODEL = 2048
D_FF = 8192
EPS = 1e-6
NEG_INF = -1e30
Q_LORA = 448
ROPE_THETA = 10000.0
GRID_W = 64
DILATIONS = (1, 4, 16)
BAND_HALF = 64
SCALE_A = 1.0 / math.sqrt(192.0)
SCALE_H = 1.0 / math.sqrt(128.0)
ADAM_LR, ADAM_B1, ADAM_B2, ADAM_EPS, ADAM_WD, ADAM_STEP = 0.001, 0.9, 0.999, 1e-08, 0.01, 10

CDT = jnp.bfloat16
F32 = jnp.float32
LANES = 128
VMEM_LIMIT = 56 * 1024 * 1024

PROJ_W = 4608
LAT_W = 1024
KV_LO, KV_HI = 448, 960
OFF_BQ, OFF_BK, OFF_BV, OFF_CQ, OFF_CK, OFF_CV = 1024, 1792, 2048, 2304, 3072, 3840

NN = ((1,), (0,))
NT = ((1,), (1,))
TN = ((0,), (0,))

BIG = ("w_in", "w_uq", "w_ukv", "w_out", "w_ff1", "w_ff2")
COMM = ("w_in_t", "w_ff1_t", "w_out", "w_ff2", "lat")
COMM_SHAPE = {"w_in_t": (576, 2048), "w_ff1_t": (1024, 2048), "w_out": (256, 2048), "w_ff2": (1024, 2048),
              "lat": (848, 128)}
UQ_ROWS = 448 * 96 // LANES


def _dot(a, b, dims):
    return lax.dot_general(a, b, (dims, ((), ())), preferred_element_type=F32)


def _cparams(dims=None):
    return pltpu.CompilerParams(dimension_semantics=dims, vmem_limit_bytes=VMEM_LIMIT)


def _sds(shape, dtype):
    return jax.ShapeDtypeStruct(shape, dtype)


def _matmul(a, b, *, mode, tm, tn, tk, out_dtype, name, epi=None, extra=None):
    if mode == "nn":
        (M, K), (K2, N) = a.shape, b.shape
    elif mode == "nt":
        (M, K), (N, K2) = a.shape, b.shape
    else:
        (K, M), (K2, N) = a.shape, b.shape
    tm, tn, tk = min(tm, M), min(tn, N), min(tk, K)
    assert K == K2 and M % tm == 0 and N % tn == 0 and K % tk == 0, (name, a.shape, b.shape)
    nk = K // tk
    dims = {"nn": NN, "nt": NT, "tn": TN}[mode]
    if mode == "tn":
        a_spec = pl.BlockSpec((tk, tm), lambda i, j, k: (k, i))
    else:
        a_spec = pl.BlockSpec((tm, tk), lambda i, j, k: (i, k))
    if mode == "nt":
        b_spec = pl.BlockSpec((tn, tk), lambda i, j, k: (j, k))
    else:
        b_spec = pl.BlockSpec((tk, tn), lambda i, j, k: (k, j))
    tile = pl.BlockSpec((tm, tn), lambda i, j, k: (i, j))
    n_extra = 1 if epi in ("residual", "drelu2") else 0
    n_out = 2 if epi == "relu2" else 1

    def body(*refs):
        a_ref, b_ref = refs[0], refs[1]
        extra_refs = refs[2:2 + n_extra]
        out_refs = refs[2 + n_extra:2 + n_extra + n_out]

        def finish(acc):
            if epi is None:
                out_refs[0][...] = acc.astype(out_dtype)
            elif epi == "residual":
                out_refs[0][...] = (extra_refs[0][...] + acc).astype(out_dtype)
            elif epi == "relu2":
                out_refs[0][...] = acc.astype(out_dtype)
                r = jnp.maximum(acc, 0.0)
                out_refs[1][...] = (r * r).astype(out_dtype)
            else:
                z = extra_refs[0][...].astype(F32)
                out_refs[0][...] = (acc * (2.0 * jnp.maximum(z, 0.0))).astype(out_dtype)

        part = _dot(a_ref[...], b_ref[...], dims)
        if nk == 1:
            finish(part)
        else:
            acc_ref = refs[-1]
            k = pl.program_id(2)

            @pl.when(k == 0)
            def _():
                acc_ref[...] = part

            @pl.when(k > 0)
            def _():
                acc_ref[...] += part

            @pl.when(k == nk - 1)
            def _():
                finish(acc_ref[...])

    out_shape = [_sds((M, N), out_dtype)] * n_out
    res = pl.pallas_call(
        body, name=name, grid=(M // tm, N // tn, nk),
        in_specs=[a_spec, b_spec] + [tile] * n_extra,
        out_specs=[tile] * n_out, out_shape=out_shape,
        scratch_shapes=[pltpu.VMEM((tm, tn), F32)] if nk > 1 else [],
        compiler_params=_cparams(("parallel", "parallel", "arbitrary")),
    )(a, b, *([extra] if n_extra else []))
    return res if n_out > 1 else res[0]


def _rms_val(x, g, n):
    r = lax.rsqrt(jnp.sum(x * x, axis=-1, keepdims=True) * (1.0 / n) + EPS)
    y = x * r
    return (y if g is None else y * g), r


def _rms_bwd_val(x, g, dy, n):
    r = lax.rsqrt(jnp.sum(x * x, axis=-1, keepdims=True) * (1.0 / n) + EPS)
    xhat = x * r
    dyg = dy if g is None else dy * g
    dx = r * (dyg - xhat * (jnp.sum(dyg * xhat, axis=-1, keepdims=True) * (1.0 / n)))
    return dx, dy * xhat


def _rope_val(x, c, sa, sb, shift):
    return x * c + pltpu.roll(x, LANES - shift, 1) * sa + pltpu.roll(x, shift, 1) * sb


def _rope_t_val(dy, c, sa, sb, shift):
    return dy * c + pltpu.roll(dy * sa, shift, 1) + pltpu.roll(dy * sb, LANES - shift, 1)


def _colsum(x):
    return jnp.sum(x, axis=0, keepdims=True)


def _rope_tables(S):
    pos = lax.broadcasted_iota(jnp.int32, (S, LANES), 0)
    lane = lax.broadcasted_iota(jnp.int32, (S, LANES), 1)

    def tables(p, dim, active):
        half = dim // 2
        inv = jnp.power(ROPE_THETA, -(2 * (lane % half)).astype(F32) / dim)
        a = p.astype(F32) * inv
        first = (lane % dim) < half
        zero = jnp.zeros((S, LANES), F32)
        return (jnp.where(active, jnp.cos(a), zero), jnp.where(active & first, -jnp.sin(a), zero),
                jnp.where(active & ~first, jnp.sin(a), zero))

    tab_a = tables(pos, 64, lane >= 64)
    tab_b = tables(jnp.where(lane < 64, pos // GRID_W, pos % GRID_W), 64, lane >= 0)
    tab_c = tables(pos, 128, lane >= 0)
    return tab_a, tab_b, tab_c


ROPE_SHIFT_AB = 32
ROPE_SHIFT_C = 64


def _rms_fwd(x, g, *, name, tr=512):
    S, W = x.shape
    tr = min(tr, S)

    def body(x_ref, g_ref, o_ref):
        y, _ = _rms_val(x_ref[...], g_ref[...], W)
        o_ref[...] = y.astype(CDT)

    return pl.pallas_call(
        body, name=name, grid=(S // tr,),
        in_specs=[pl.BlockSpec((tr, W), lambda i: (i, 0)), pl.BlockSpec((1, W), lambda i: (0, 0))],
        out_specs=pl.BlockSpec((tr, W), lambda i: (i, 0)), out_shape=_sds((S, W), CDT),
        compiler_params=_cparams(("parallel",)),
    )(x, g.reshape(1, W))


def _rms_bwd(x, g, dy, res, *, name, tr=256):
    S, W = x.shape
    tr = min(tr, S)

    def body(x_ref, g_ref, dy_ref, res_ref, dx_ref, dxb_ref, dg_ref):
        dx, dgt = _rms_bwd_val(x_ref[...], g_ref[...], dy_ref[...], W)
        dx = res_ref[...] + dx
        dx_ref[...] = dx
        dxb_ref[...] = dx.astype(CDT)

        @pl.when(pl.program_id(0) == 0)
        def _():
            dg_ref[...] = jnp.zeros_like(dg_ref)

        dg_ref[...] += _colsum(dgt)

    row = pl.BlockSpec((tr, W), lambda i: (i, 0))
    vec = pl.BlockSpec((1, W), lambda i: (0, 0))
    return pl.pallas_call(
        body, name=name, grid=(S // tr,),
        in_specs=[row, vec, row, row], out_specs=[row, row, vec],
        out_shape=[_sds((S, W), F32), _sds((S, W), CDT), _sds((1, W), F32)],
        compiler_params=_cparams(("arbitrary",)),
    )(x, g.reshape(1, W), dy, res)


def _loss_head(x, g, tgt, *, tr=256):
    S, W = x.shape
    tr = min(tr, S)

    def body(x_ref, g_ref, t_ref, loss_ref, dx_ref, dxb_ref, dg_ref):
        xv, gv = x_ref[...], g_ref[...]
        y, _ = _rms_val(xv, gv, W)
        err = y - t_ref[...]
        part = 0.5 * jnp.sum(jnp.sum(err * err, axis=-1, keepdims=True) * (1.0 / W), axis=0, keepdims=True)
        dx, dgt = _rms_bwd_val(xv, gv, err * (1.0 / W), W)
        dx_ref[...] = dx
        dxb_ref[...] = dx.astype(CDT)

        @pl.when(pl.program_id(0) == 0)
        def _():
            dg_ref[...] = jnp.zeros_like(dg_ref)
            loss_ref[...] = jnp.zeros_like(loss_ref)

        dg_ref[...] += _colsum(dgt)
        loss_ref[...] += jnp.broadcast_to(part, (1, LANES))

    row = pl.BlockSpec((tr, W), lambda i: (i, 0))
    vec = pl.BlockSpec((1, W), lambda i: (0, 0))
    return pl.pallas_call(
        body, name="loss_head", grid=(S // tr,),
        in_specs=[row, vec, row], out_specs=[pl.BlockSpec((1, LANES), lambda i: (0, 0)), row, row, vec],
        out_shape=[_sds((1, LANES), F32), _sds((S, W), F32), _sds((S, W), CDT), _sds((1, W), F32)],
        compiler_params=_cparams(("arbitrary",)),
    )(x, g.reshape(1, W), tgt)


def _tab_specs(tr):
    return [pl.BlockSpec((tr, LANES), lambda i: (i, 0))] * 9


def _lat_masks(shape):
    lane = lax.broadcasted_iota(jnp.int32, shape, 1)
    return lane < KV_LO, (lane >= KV_LO) & (lane < KV_HI)


def _prep_fwd(proj, glat, gqn, gkn, tabs, *, tr=128):
    S = proj.shape[0]
    tr = min(tr, S)

    def body(p_ref, glat_ref, gqn_ref, gkn_ref, ac, aa, ab, bc, ba, bb, cc, ca, cb,
             lat_ref, kpe_ref, qb_ref, kb_ref, vb_ref, qc_ref, kc_ref, vc_ref):
        x = p_ref[:, 0:LAT_W]
        is_q, is_kv = _lat_masks(x.shape)
        yq, _ = _rms_val(jnp.where(is_q, x, 0.0), None, Q_LORA)
        ykv, _ = _rms_val(jnp.where(is_kv, x, 0.0), None, KV_HI - KV_LO)
        lat_ref[...] = ((yq + ykv) * glat_ref[...]).astype(CDT)
        kpe_ref[...] = _rope_val(x[:, LAT_W - LANES:LAT_W], ac[...], aa[...], ab[...], ROPE_SHIFT_AB).astype(CDT)
        for h in range(6):
            xh = p_ref[:, OFF_BQ + LANES * h:OFF_BQ + LANES * (h + 1)]
            y = _rope_val(_rms_val(xh, gqn_ref[...], LANES)[0], bc[...], ba[...], bb[...], ROPE_SHIFT_AB)
            qb_ref[:, LANES * h:LANES * (h + 1)] = y.astype(CDT)
        for h in range(2):
            xh = p_ref[:, OFF_BK + LANES * h:OFF_BK + LANES * (h + 1)]
            y = _rope_val(_rms_val(xh, gkn_ref[...], LANES)[0], bc[...], ba[...], bb[...], ROPE_SHIFT_AB)
            kb_ref[:, LANES * h:LANES * (h + 1)] = y.astype(CDT)
        vb_ref[...] = p_ref[:, OFF_BV:OFF_BV + 256].astype(CDT)
        for h in range(6):
            sl = slice(LANES * h, LANES * (h + 1))
            qc_ref[:, sl] = _rope_val(p_ref[:, OFF_CQ + LANES * h:OFF_CQ + LANES * (h + 1)], cc[...], ca[...], cb[...],
                                      ROPE_SHIFT_C)
            kc_ref[:, sl] = _rope_val(p_ref[:, OFF_CK + LANES * h:OFF_CK + LANES * (h + 1)], cc[...], ca[...], cb[...],
                                      ROPE_SHIFT_C)
        vc_ref[...] = p_ref[:, OFF_CV:OFF_CV + 768]

    vec = lambda w: pl.BlockSpec((1, w), lambda i: (0, 0))
    row = lambda w: pl.BlockSpec((tr, w), lambda i: (i, 0))
    return pl.pallas_call(
        body, name="prep_fwd", grid=(S // tr,),
        in_specs=[row(PROJ_W), vec(LAT_W), vec(128), vec(128)] + _tab_specs(tr),
        out_specs=[row(LAT_W), row(128), row(768), row(256), row(256), row(768), row(768), row(768)],
        out_shape=[_sds((S, LAT_W), CDT), _sds((S, 128), CDT), _sds((S, 768), CDT), _sds((S, 256), CDT),
                   _sds((S, 256), CDT), _sds((S, 768), F32), _sds((S, 768), F32), _sds((S, 768), F32)],
        compiler_params=_cparams(("parallel",)),
    )(proj, glat, gqn, gkn, *tabs[0], *tabs[1], *tabs[2])


def _prep_a2_fwd(qkva, kpe, tab_a, *, tr=512):
    S = qkva.shape[0]
    tr = min(tr, S)

    def body(x_ref, kpe_ref, ac, aa, ab, qa_ref, ka_ref, va_ref):
        for h in range(4):
            lo, hi = 2 * LANES * h, 2 * LANES * h + LANES
            qa_ref[:, lo:hi] = x_ref[:, lo:hi].astype(CDT)
            qa_ref[:, hi:hi + LANES] = _rope_val(x_ref[:, hi:hi + LANES], ac[...], aa[...], ab[...],
                                                 ROPE_SHIFT_AB).astype(CDT)
            ka_ref[:, lo:hi] = x_ref[:, 1024 + LANES * h:1024 + LANES * (h + 1)].astype(CDT)
            ka_ref[:, hi:hi + LANES] = kpe_ref[...]
        va_ref[...] = x_ref[:, 1536:2048].astype(CDT)

    row = lambda w: pl.BlockSpec((tr, w), lambda i: (i, 0))
    return pl.pallas_call(
        body, name="prep_a2_fwd", grid=(S // tr,),
        in_specs=[row(2048), row(128)] + _tab_specs(tr)[:3],
        out_specs=[row(1024), row(1024), row(512)],
        out_shape=[_sds((S, 1024), CDT), _sds((S, 1024), CDT), _sds((S, 512), CDT)],
        compiler_params=_cparams(("parallel",)),
    )(qkva, kpe, *tab_a)


def _prep_a2_bwd(dqa, dka, dva, tab_a, *, tr=512):
    S = dqa.shape[0]
    tr = min(tr, S)

    def body(dq_ref, dk_ref, dv_ref, ac, aa, ab, dx_ref, dkr_ref):
        dkpe = jnp.zeros((tr, LANES), F32)
        for h in range(4):
            lo, hi = 2 * LANES * h, 2 * LANES * h + LANES
            dx_ref[:, lo:hi] = dq_ref[:, lo:hi].astype(CDT)
            dx_ref[:, hi:hi + LANES] = _rope_t_val(dq_ref[:, hi:hi + LANES], ac[...], aa[...], ab[...],
                                                   ROPE_SHIFT_AB).astype(CDT)
            dx_ref[:, 1024 + LANES * h:1024 + LANES * (h + 1)] = dk_ref[:, lo:hi].astype(CDT)
            dkpe = dkpe + dk_ref[:, hi:hi + LANES]
        dx_ref[:, 1536:2048] = dv_ref[...].astype(CDT)
        dkr_ref[...] = _rope_t_val(dkpe, ac[...], aa[...], ab[...], ROPE_SHIFT_AB)

    row = lambda w: pl.BlockSpec((tr, w), lambda i: (i, 0))
    return pl.pallas_call(
        body, name="prep_a2_bwd", grid=(S // tr,),
        in_specs=[row(1024), row(1024), row(512)] + _tab_specs(tr)[:3],
        out_specs=[row(2048), row(128)],
        out_shape=[_sds((S, 2048), CDT), _sds((S, 128), F32)],
        compiler_params=_cparams(("parallel",)),
    )(dqa, dka, dva, *tab_a)


def _prep_bwd(proj, glat, gqn, gkn, tabs, dlat, dkr, dqb, dkb, dvb, dqc, dkc, dvc, *, tr=128):
    S = proj.shape[0]
    tr = min(tr, S)

    def body(p_ref, glat_ref, gqn_ref, gkn_ref, ac, aa, ab, bc, ba, bb, cc, ca, cb,
             dlat_ref, dkr_ref, dqb_ref, dkb_ref, dvb_ref, dqc_ref, dkc_ref, dvc_ref,
             dp_ref, dglat_ref, dgqn_ref, dgkn_ref):
        @pl.when(pl.program_id(0) == 0)
        def _():
            dglat_ref[...] = jnp.zeros_like(dglat_ref)
            dgqn_ref[...] = jnp.zeros_like(dgqn_ref)
            dgkn_ref[...] = jnp.zeros_like(dgkn_ref)

        x = p_ref[:, 0:LAT_W]
        is_q, is_kv = _lat_masks(x.shape)
        dy, g = dlat_ref[...], glat_ref[...]
        dxq, dgq = _rms_bwd_val(jnp.where(is_q, x, 0.0), g, jnp.where(is_q, dy, 0.0), Q_LORA)
        dxkv, dgkv = _rms_bwd_val(jnp.where(is_kv, x, 0.0), g, jnp.where(is_kv, dy, 0.0), KV_HI - KV_LO)
        dglat_ref[...] += _colsum(dgq + dgkv)
        dx = dxq + dxkv
        dp_ref[:, 0:LAT_W - LANES] = dx[:, 0:LAT_W - LANES].astype(CDT)
        dp_ref[:, LAT_W - LANES:LAT_W] = (dx[:, LAT_W - LANES:LAT_W] + dkr_ref[...]).astype(CDT)
        dgqn = jnp.zeros((1, LANES), F32)
        for h in range(6):
            sl = slice(LANES * h, LANES * (h + 1))
            po = slice(OFF_BQ + LANES * h, OFF_BQ + LANES * (h + 1))
            dyh = _rope_t_val(dqb_ref[:, sl], bc[...], ba[...], bb[...], ROPE_SHIFT_AB)
            dxh, dgt = _rms_bwd_val(p_ref[:, po], gqn_ref[...], dyh, LANES)
            dp_ref[:, po] = dxh.astype(CDT)
            dgqn = dgqn + _colsum(dgt)
        dgqn_ref[...] += dgqn
        dgkn = jnp.zeros((1, LANES), F32)
        for h in range(2):
            sl = slice(LANES * h, LANES * (h + 1))
            po = slice(OFF_BK + LANES * h, OFF_BK + LANES * (h + 1))
            dyh = _rope_t_val(dkb_ref[:, sl], bc[...], ba[...], bb[...], ROPE_SHIFT_AB)
            dxh, dgt = _rms_bwd_val(p_ref[:, po], gkn_ref[...], dyh, LANES)
            dp_ref[:, po] = dxh.astype(CDT)
            dgkn = dgkn + _colsum(dgt)
        dgkn_ref[...] += dgkn
        dp_ref[:, OFF_BV:OFF_BV + 256] = dvb_ref[...].astype(CDT)
        for h in range(6):
            sl = slice(LANES * h, LANES * (h + 1))
            dp_ref[:, OFF_CQ + LANES * h:OFF_CQ + LANES * (h + 1)] = _rope_t_val(
                dqc_ref[:, sl], cc[...], ca[...], cb[...], ROPE_SHIFT_C).astype(CDT)
            dp_ref[:, OFF_CK + LANES * h:OFF_CK + LANES * (h + 1)] = _rope_t_val(
                dkc_ref[:, sl], cc[...], ca[...], cb[...], ROPE_SHIFT_C).astype(CDT)
        dp_ref[:, OFF_CV:OFF_CV + 768] = dvc_ref[...].astype(CDT)

    vec = lambda w: pl.BlockSpec((1, w), lambda i: (0, 0))
    row = lambda w: pl.BlockSpec((tr, w), lambda i: (i, 0))
    return pl.pallas_call(
        body, name="prep_bwd", grid=(S // tr,),
        in_specs=[row(PROJ_W), vec(LAT_W), vec(128), vec(128)] + _tab_specs(tr)
        + [row(LAT_W), row(128), row(768), row(256), row(256), row(768), row(768), row(768)],
        out_specs=[row(PROJ_W), vec(LAT_W), vec(128), vec(128)],
        out_shape=[_sds((S, PROJ_W), CDT), _sds((1, LAT_W), F32), _sds((1, 128), F32), _sds((1, 128), F32)],
        compiler_params=_cparams(("arbitrary",)),
    )(proj, glat, gqn, gkn, *tabs[0], *tabs[1], *tabs[2], dlat, dkr, dqb, dkb, dvb, dqc, dkc, dvc)


def _attn_fwd(q, k, v, *, H, G, dk, dv, scale, name, tq=512):
    S = q.shape[0]
    tq = min(tq, S)

    def body(q_ref, k_ref, v_ref, o_ref, l_ref):
        s = _dot(q_ref[...], k_ref[...], NT) * scale
        m = jnp.max(s, axis=-1, keepdims=True)
        e = jnp.exp(s - m)
        den = jnp.sum(e, axis=-1, keepdims=True)
        o_ref[...] = _dot(e.astype(CDT), v_ref[...], NN) * (1.0 / den)
        l_ref[...] = jnp.broadcast_to(m + jnp.log(den), (tq, LANES))

    return pl.pallas_call(
        body, name=name, grid=(H, S // tq),
        in_specs=[pl.BlockSpec((tq, dk), lambda h, i: (i, h)), pl.BlockSpec((S, dk), lambda h, i: (0, h // G)),
                  pl.BlockSpec((S, dv), lambda h, i: (0, h // G))],
        out_specs=[pl.BlockSpec((tq, dv), lambda h, i: (i, h)), pl.BlockSpec((tq, LANES), lambda h, i: (i, h))],
        out_shape=[_sds((S, H * dv), F32), _sds((S, H * LANES), F32)],
        compiler_params=_cparams(("parallel", "parallel")),
    )(q, k, v)


def _attn_bwd(q, k, v, do, lse, delta, *, H, G, dk, dv, scale, name, tq=256):
    S = q.shape[0]
    tq = min(tq, S)
    Hkv = H // G

    def body(q_ref, k_ref, v_ref, do_ref, l_ref, d_ref, dq_ref, dk_ref, dv_ref):
        @pl.when((pl.program_id(1) == 0) & (pl.program_id(2) == 0))
        def _():
            dk_ref[...] = jnp.zeros_like(dk_ref)
            dv_ref[...] = jnp.zeros_like(dv_ref)

        qv, kv, dov = q_ref[...], k_ref[...], do_ref[...]
        s = _dot(qv, kv, NT) * scale
        p = jnp.exp(s - l_ref[:, 0:1])
        dp = _dot(dov, v_ref[...], NT)
        ds = (p * (dp - d_ref[:, 0:1]) * scale).astype(CDT)
        dq_ref[...] = _dot(ds, kv, NN)
        dk_ref[...] += _dot(ds, qv, TN)
        dv_ref[...] += _dot(p.astype(CDT), dov, TN)

    qi = lambda hk, g, i: (i, hk * G + g)
    return pl.pallas_call(
        body, name=name, grid=(Hkv, G, S // tq),
        in_specs=[pl.BlockSpec((tq, dk), qi), pl.BlockSpec((S, dk), lambda hk, g, i: (0, hk)),
                  pl.BlockSpec((S, dv), lambda hk, g, i: (0, hk)), pl.BlockSpec((tq, dv), qi),
                  pl.BlockSpec((tq, LANES), qi), pl.BlockSpec((tq, LANES), qi)],
        out_specs=[pl.BlockSpec((tq, dk), qi), pl.BlockSpec((S, dk), lambda hk, g, i: (0, hk)),
                   pl.BlockSpec((S, dv), lambda hk, g, i: (0, hk))],
        out_shape=[_sds((S, H * dk), F32), _sds((S, Hkv * dk), F32), _sds((S, Hkv * dv), F32)],
        compiler_params=_cparams(("parallel", "arbitrary", "arbitrary")),
    )(q, k, v, do, lse, delta)


BAND_SUB = 128
BAND_WIN = 384
BAND_UNROLL = 4


def _band_blocks(S, d):
    L = S // d
    assert L % BAND_SUB == 0 and S % (BAND_SUB * BAND_UNROLL) == 0
    return L, L // BAND_SUB, min(BAND_WIN, L)


def _band_index(blk, d, nb, L, win):
    r, jb = blk // nb, blk % nb
    l0 = jb * BAND_SUB
    w0 = jnp.clip(l0 - BAND_SUB, 0, L - win)
    return r + d * l0, r + d * w0, l0, w0


def _band_rows(start, size, d):
    return pl.ds(pl.multiple_of(start, BAND_SUB), size) if d == 1 else pl.ds(start, size, stride=d)


def _band_mask(l0, w0, win):
    rpos = l0 + lax.broadcasted_iota(jnp.int32, (BAND_SUB, win), 0)
    cpos = w0 + lax.broadcasted_iota(jnp.int32, (BAND_SUB, win), 1)
    return jnp.abs(rpos - cpos) <= BAND_HALF


def _mixc_fwd(q, k, v):
    S, W = q.shape

    def body(q_ref, k_ref, v_ref, o_ref, l_ref, *scratch):
        ob_refs, lb_refs = scratch[0:3], scratch[3:6]
        for b, d in enumerate(DILATIONS):
            L, nb, win = _band_blocks(S, d)

            def step(it, carry, b=b, d=d, L=L, nb=nb, win=win):
                for u in range(BAND_UNROLL):
                    qs, ks, l0, w0 = _band_index(it * BAND_UNROLL + u, d, nb, L, win)
                    qrows, krows = _band_rows(qs, BAND_SUB, d), _band_rows(ks, win, d)
                    kw, vw = k_ref[krows, :].astype(CDT), v_ref[krows, :].astype(CDT)
                    s = jnp.where(_band_mask(l0, w0, win), _dot(q_ref[qrows, :].astype(CDT), kw, NT) * SCALE_H, NEG_INF)
                    m = jnp.max(s, axis=-1, keepdims=True)
                    e = jnp.exp(s - m)
                    den = jnp.sum(e, axis=-1, keepdims=True)
                    ob_refs[b][qrows, :] = _dot((e * (1.0 / den)).astype(CDT), vw, NN)
                    lb_refs[b][qrows, :] = jnp.broadcast_to(m + jnp.log(den), (BAND_SUB, LANES))
                return carry

            lax.fori_loop(0, S // (BAND_SUB * BAND_UNROLL), step, 0)

        def combine(c, carry):
            rows = pl.ds(pl.multiple_of(c * 256, 256), 256)
            l0, l1, l2 = lb_refs[0][rows, :], lb_refs[1][rows, :], lb_refs[2][rows, :]
            m = jnp.maximum(jnp.maximum(l0, l1), l2)
            e0, e1, e2 = jnp.exp(l0 - m), jnp.exp(l1 - m), jnp.exp(l2 - m)
            den = e0 + e1 + e2
            inv = 1.0 / den
            o_ref[rows, :] = ((e0 * inv) * ob_refs[0][rows, :] + (e1 * inv) * ob_refs[1][rows, :]
                              + (e2 * inv) * ob_refs[2][rows, :])
            l_ref[rows, :] = m + jnp.log(den)
            return carry

        lax.fori_loop(0, S // 256, combine, 0)

    head = pl.BlockSpec((S, LANES), lambda h: (0, h))
    return pl.pallas_call(
        body, name="mixc_fwd", grid=(W // LANES,), in_specs=[head] * 3, out_specs=[head] * 2,
        out_shape=[_sds((S, W), F32)] * 2,
        scratch_shapes=[pltpu.VMEM((S, LANES), F32)] * 6,
        compiler_params=_cparams(("parallel",)),
    )(q, k, v)


def _mixc_bwd(q, k, v, do, lse, dd):
    S, W = q.shape

    def body(q_ref, k_ref, v_ref, do_ref, l_ref, d_ref, dq_ref, dk_ref, dv_ref):
        dq_ref[...] = jnp.zeros_like(dq_ref)
        dk_ref[...] = jnp.zeros_like(dk_ref)
        dv_ref[...] = jnp.zeros_like(dv_ref)
        for d in DILATIONS:
            L, nb, win = _band_blocks(S, d)

            def step(it, carry, d=d, L=L, nb=nb, win=win):
                for u in range(BAND_UNROLL):
                    qs, ks, l0, w0 = _band_index(it * BAND_UNROLL + u, d, nb, L, win)
                    qrows, krows = _band_rows(qs, BAND_SUB, d), _band_rows(ks, win, d)
                    qv, dov = q_ref[qrows, :].astype(CDT), do_ref[qrows, :].astype(CDT)
                    kw, vw = k_ref[krows, :].astype(CDT), v_ref[krows, :].astype(CDT)
                    s = jnp.where(_band_mask(l0, w0, win), _dot(qv, kw, NT) * SCALE_H, NEG_INF)
                    p = jnp.exp(s - l_ref[qrows, :][:, 0:1])
                    dp = _dot(dov, vw, NT)
                    ds = (p * (dp - d_ref[qrows, :][:, 0:1]) * SCALE_H).astype(CDT)
                    dq_ref[qrows, :] += _dot(ds, kw, NN)
                    dk_ref[krows, :] += _dot(ds, qv, TN)
                    dv_ref[krows, :] += _dot(p.astype(CDT), dov, TN)
                return carry

            lax.fori_loop(0, S // (BAND_SUB * BAND_UNROLL), step, 0)

    head = pl.BlockSpec((S, LANES), lambda h: (0, h))
    return pl.pallas_call(
        body, name="mixc_bwd", grid=(W // LANES,), in_specs=[head] * 6, out_specs=[head] * 3,
        out_shape=[_sds((S, W), F32)] * 3, compiler_params=_cparams(("parallel",)),
    )(q, k, v, do, lse, dd)


def _outnorm_fwd(oa, ob, oc, g, *, tr=256):
    S = oa.shape[0]
    tr = min(tr, S)

    def body(a_ref, b_ref, c_ref, g_ref, m_ref):
        m_ref[:, 0:512] = (_rms_val(a_ref[...], None, 512)[0] * g_ref[:, 0:512]).astype(CDT)
        m_ref[:, 512:1280] = (_rms_val(b_ref[...], None, 768)[0] * g_ref[:, 512:1280]).astype(CDT)
        m_ref[:, 1280:2048] = (_rms_val(c_ref[...], None, 768)[0] * g_ref[:, 1280:2048]).astype(CDT)

    row = lambda w: pl.BlockSpec((tr, w), lambda i: (i, 0))
    return pl.pallas_call(
        body, name="outnorm_fwd", grid=(S // tr,),
        in_specs=[row(512), row(768), row(768), pl.BlockSpec((1, 2048), lambda i: (0, 0))],
        out_specs=row(2048), out_shape=_sds((S, 2048), CDT), compiler_params=_cparams(("parallel",)),
    )(oa, ob, oc, g)


def _outnorm_bwd(oa, ob, oc, g, dm, *, tr=256):
    S = oa.shape[0]
    tr = min(tr, S)

    def body(a_ref, b_ref, c_ref, g_ref, dm_ref, doa_ref, dob_ref, doc_ref, da_ref, db_ref, dc_ref, dg_ref):
        @pl.when(pl.program_id(0) == 0)
        def _():
            dg_ref[...] = jnp.zeros_like(dg_ref)

        for o_ref, do_ref, d_ref, lo, w in ((a_ref, doa_ref, da_ref, 0, 512), (b_ref, dob_ref, db_ref, 512, 768),
                                            (c_ref, doc_ref, dc_ref, 1280, 768)):
            o = o_ref[...]
            dmv = dm_ref[:, lo:lo + w]
            do, _ = _rms_bwd_val(o, None, dmv * g_ref[:, lo:lo + w], w)
            r = lax.rsqrt(jnp.sum(o * o, axis=-1, keepdims=True) * (1.0 / w) + EPS)
            dg_ref[:, lo:lo + w] += _colsum(dmv * (o * r))
            do_ref[...] = do.astype(do_ref.dtype)
            for h in range(w // LANES):
                sl = slice(LANES * h, LANES * (h + 1))
                d_ref[:, sl] = jnp.broadcast_to(jnp.sum(do[:, sl] * o[:, sl], axis=-1, keepdims=True), (tr, LANES))

    row = lambda w: pl.BlockSpec((tr, w), lambda i: (i, 0))
    vec = pl.BlockSpec((1, 2048), lambda i: (0, 0))
    return pl.pallas_call(
        body, name="outnorm_bwd", grid=(S // tr,),
        in_specs=[row(512), row(768), row(768), vec, row(2048)],
        out_specs=[row(512), row(768), row(768), row(512), row(768), row(768), vec],
        out_shape=[_sds((S, 512), CDT), _sds((S, 768), CDT), _sds((S, 768), F32), _sds((S, 512), F32),
                   _sds((S, 768), F32), _sds((S, 768), F32), _sds((1, 2048), F32)],
        compiler_params=_cparams(("arbitrary",)),
    )(oa, ob, oc, g, dm)


MESH = pl.DeviceIdType.MESH
ANY = pl.BlockSpec(memory_space=pl.ANY)


def _all_gather(shards):
    n = len(shards)

    def body(*refs):
        x_refs, out_refs = refs[:n], refs[n:2 * n]
        send_sems, recv_sems, local_sems = refs[2 * n:]
        x, y, c = lax.axis_index("x"), lax.axis_index("y"), lax.axis_index("c")
        me, sibling = (x, y, c), (x, y, 1 - c)
        chips = [(1 - x, y), (x, 1 - y), (1 - x, 1 - y)]

        def copy(t, k, block, to, src=None):
            px, py, pc = block
            rows = out_refs[t].at[4 * px + 2 * py + pc]
            return pltpu.make_async_remote_copy(
                src_ref=rows if src is None else src, dst_ref=rows,
                send_sem=send_sems.at[t, k], recv_sem=recv_sems.at[t, k], device_id=to, device_id_type=MESH)

        started = []
        for t in range(n):
            mine = pltpu.make_async_copy(x_refs[t], out_refs[t].at[4 * x + 2 * y + c], local_sems.at[t])
            mine.start()
            started.append(mine)
        first = []
        for t in range(n):
            first.append(copy(t, 0, me, sibling, src=x_refs[t]))
            first += [copy(t, 1 + j, me, (*chip, c), src=x_refs[t]) for j, chip in enumerate(chips)]
        for cp in first:
            cp.start()
        passed = []
        for j, chip in enumerate(chips):
            for t in range(n):
                copy(t, 1 + j, (*chip, c), me).wait_recv()
                fwd = copy(t, 4 + j, (*chip, c), sibling)
                fwd.start()
                passed.append(fwd)
        for t in range(n):
            copy(t, 0, sibling, me).wait_recv()
            for j, chip in enumerate(chips):
                copy(t, 4 + j, (*chip, 1 - c), me).wait_recv()
        for cp in first + passed:
            cp.wait_send()
        for mine in started:
            mine.wait()

    return pl.pallas_call(
        body, name="all_gather", out_shape=[_sds((8,) + s.shape, s.dtype) for s in shards],
        in_specs=[ANY] * n, out_specs=[ANY] * n,
        scratch_shapes=[pltpu.SemaphoreType.DMA((n, 7)), pltpu.SemaphoreType.DMA((n, 7)), pltpu.SemaphoreType.DMA((n,))],
    )(*shards)


def _rs_sibling_exchange(gs):
    n = len(gs)

    def body(*refs):
        g_refs, out_refs = refs[:n], refs[n:2 * n]
        send_sems, recv_sems = refs[2 * n:]
        x, y, c = lax.axis_index("x"), lax.axis_index("y"), lax.axis_index("c")
        copies = [pltpu.make_async_remote_copy(
            src_ref=g_refs[t].at[k, 1 - c], dst_ref=out_refs[t].at[k], send_sem=send_sems.at[t, k],
            recv_sem=recv_sems.at[t, k], device_id=(x, y, 1 - c), device_id_type=MESH)
            for t in range(n) for k in range(4)]
        for cp in copies:
            cp.start()
        for cp in copies:
            cp.wait()

    return pl.pallas_call(
        body, name="rs_sibling_exchange", out_shape=[_sds((4,) + g.shape[2:], g.dtype) for g in gs],
        in_specs=[ANY] * n, out_specs=[ANY] * n,
        scratch_shapes=[pltpu.SemaphoreType.DMA((n, 4)), pltpu.SemaphoreType.DMA((n, 4))],
    )(*gs)


def _row_tile(r, c, itemsize, limit=1 << 20):
    best = 16
    for t in range(16, r + 1, 16):
        if r % t == 0 and t * c * itemsize <= limit:
            best = t
    return best


def _rs_chip_sum(g, got, *, name):
    _, _, r, c = g.shape
    tr = _row_tile(r, c, 2)
    core = lax.axis_index("c").astype(jnp.int32).reshape(1)

    def body(c_ref, a_ref, b_ref, o_ref):
        o_ref[...] = (a_ref[...].astype(F32) + b_ref[...].astype(F32)).astype(o_ref.dtype)

    spec = pltpu.PrefetchScalarGridSpec(
        num_scalar_prefetch=1, grid=(4, r // tr),
        in_specs=[pl.BlockSpec((None, None, tr, c), lambda k, i, cr: (k, cr[0], i, 0)),
                  pl.BlockSpec((None, tr, c), lambda k, i, cr: (k, i, 0))],
        out_specs=pl.BlockSpec((None, tr, c), lambda k, i, cr: (k, i, 0)))
    return pl.pallas_call(body, name=name, grid_spec=spec, out_shape=_sds((4, r, c), g.dtype),
                          compiler_params=_cparams(("parallel", "parallel")))(core, g, got)


def _rs_chip_exchange(ps):
    n = len(ps)

    def body(*refs):
        p_refs, out_refs = refs[:n], refs[n:2 * n]
        send_sems, recv_sems, local_sems = refs[2 * n:]
        x, y, c = lax.axis_index("x"), lax.axis_index("y"), lax.axis_index("c")
        chips = [(1 - x, y), (x, 1 - y), (1 - x, 1 - y)]
        my_chip = 2 * x + y
        locals_ = []
        for t in range(n):
            mine = pltpu.make_async_copy(p_refs[t].at[my_chip], out_refs[t].at[my_chip], local_sems.at[t])
            mine.start()
            locals_.append(mine)
        sends = [pltpu.make_async_remote_copy(
            src_ref=p_refs[t].at[2 * cx + cy], dst_ref=out_refs[t].at[my_chip], send_sem=send_sems.at[t, j],
            recv_sem=recv_sems.at[t, j], device_id=(cx, cy, c), device_id_type=MESH)
            for t in range(n) for j, (cx, cy) in enumerate(chips)]
        for cp in sends:
            cp.start()
        for t in range(n):
            for j, (cx, cy) in enumerate(chips):
                pltpu.make_async_remote_copy(
                    src_ref=p_refs[t].at[my_chip], dst_ref=out_refs[t].at[2 * cx + cy], send_sem=send_sems.at[t, j],
                    recv_sem=recv_sems.at[t, j], device_id=(cx, cy, c), device_id_type=MESH).wait_recv()
        for cp in sends:
            cp.wait_send()
        for mine in locals_:
            mine.wait()

    return pl.pallas_call(
        body, name="rs_chip_exchange", out_shape=[_sds(p.shape, p.dtype) for p in ps],
        in_specs=[ANY] * n, out_specs=[ANY] * n,
        scratch_shapes=[pltpu.SemaphoreType.DMA((n, 3)), pltpu.SemaphoreType.DMA((n, 3)), pltpu.SemaphoreType.DMA((n,))],
    )(*ps)


def _rs_final_sum(r4, *, name):
    _, r, c = r4.shape
    tr = _row_tile(r, c, 4)

    def body(r_ref, o_ref):
        o_ref[...] = ((r_ref[0].astype(F32) + r_ref[1].astype(F32)) + r_ref[2].astype(F32)) + r_ref[3].astype(F32)

    return pl.pallas_call(
        body, name=name, grid=(r // tr,), in_specs=[pl.BlockSpec((4, tr, c), lambda i: (0, i, 0))],
        out_specs=pl.BlockSpec((tr, c), lambda i: (i, 0)), out_shape=_sds((r, c), F32),
        compiler_params=_cparams(("parallel",)))(r4)


def _all_reduce_small(v):
    R = v.shape[0]

    def body(v_ref, out_ref, buf_ref, send_sems, recv_sems):
        x, y, c = lax.axis_index("x"), lax.axis_index("y"), lax.axis_index("c")
        me = 4 * x + 2 * y + c
        buf_ref[me] = v_ref[...]
        peers = []
        for r in range(1, 8):
            px, py, pc = x ^ (r >> 2), y ^ ((r >> 1) & 1), c ^ (r & 1)
            peers.append((r, (px, py, pc), 4 * px + 2 * py + pc))
        sends = [pltpu.make_async_remote_copy(
            src_ref=v_ref, dst_ref=buf_ref.at[me], send_sem=send_sems.at[r - 1], recv_sem=recv_sems.at[r - 1],
            device_id=dev, device_id_type=MESH) for r, dev, _ in peers]
        for cp in sends:
            cp.start()
        for r, dev, idx in peers:
            pltpu.make_async_remote_copy(
                src_ref=v_ref, dst_ref=buf_ref.at[idx], send_sem=send_sems.at[r - 1], recv_sem=recv_sems.at[r - 1],
                device_id=dev, device_id_type=MESH).wait_recv()
        for cp in sends:
            cp.wait_send()
        acc = buf_ref[0]
        for k in range(1, 8):
            acc = acc + buf_ref[k]
        out_ref[...] = acc

    vm = pl.BlockSpec(memory_space=pltpu.VMEM)
    return pl.pallas_call(
        body, name="all_reduce_small", out_shape=_sds((R, LANES), F32), in_specs=[vm], out_specs=vm,
        scratch_shapes=[pltpu.VMEM((8, R, LANES), F32), pltpu.SemaphoreType.DMA((7,)), pltpu.SemaphoreType.DMA((7,))],
    )(v)


def _adamw(w, g, m, v, *, name):
    R, C = w.shape
    tr = R
    for cand in (1024, 512, 256, 128, 64, 32, 16, 8):
        if R % cand == 0 and cand * C * 4 <= 2 * 1024 * 1024:
            tr = cand
            break

    def body(w_ref, g_ref, m_ref, v_ref, d_ref, nm_ref, nv_ref):
        gv = g_ref[...]
        mn = ADAM_B1 * m_ref[...] + (1.0 - ADAM_B1) * gv
        vn = ADAM_B2 * v_ref[...] + (1.0 - ADAM_B2) * (gv * gv)
        m_hat = mn / (1.0 - ADAM_B1 ** ADAM_STEP)
        v_hat = vn / (1.0 - ADAM_B2 ** ADAM_STEP)
        d_ref[...] = -ADAM_LR * (m_hat / (jnp.sqrt(v_hat) + ADAM_EPS) + ADAM_WD * w_ref[...])
        nm_ref[...] = mn
        nv_ref[...] = vn

    blk = pl.BlockSpec((tr, C), lambda i: (i, 0))
    return pl.pallas_call(
        body, name=name, grid=(R // tr,), in_specs=[blk] * 4, out_specs=[blk] * 3,
        out_shape=[_sds((R, C), F32)] * 3, compiler_params=_cparams(("parallel",)))(w, g, m, v)


def _wuq_pad(w):
    w = w.reshape(448, 4, 192)
    z = jnp.zeros((448, 4, 64), w.dtype)
    return jnp.concatenate([w[:, :, 0:128], z, w[:, :, 128:192]], axis=2).reshape(448, 1024)


def _wuq_unpad(w):
    w = w.reshape(448, 4, 256)
    return jnp.concatenate([w[:, :, 0:128], w[:, :, 192:256]], axis=2).reshape(448, 768)


def _wukv_perm(w):
    return w.reshape(512, 4, 2, 128).transpose(0, 2, 1, 3).reshape(512, 1024)


def _wukv_unperm(w):
    return w.reshape(512, 2, 4, 128).transpose(0, 2, 1, 3).reshape(512, 1024)


def _lat_weight(w_uq, w_ukv):
    z = lambda r, c: jnp.zeros((r, c), w_uq.dtype)
    top = jnp.concatenate([_wuq_pad(w_uq), z(448, 1024)], axis=1)
    mid = jnp.concatenate([z(512, 1024), _wukv_perm(w_ukv)], axis=1)
    return jnp.concatenate([top, mid, z(64, 2048)], axis=0)


def _lat_weight_grads(dw):
    return _wuq_unpad(dw[0:KV_LO, 0:1024]), _wukv_unperm(dw[KV_LO:KV_HI, 1024:2048])


def _comm_shards(w_in, w_uq, w_ukv, w_out, w_ff1, w_ff2):
    lat = jnp.concatenate([w_uq.reshape(UQ_ROWS, LANES), w_ukv.reshape(512, LANES)], axis=0)
    return [w_in.T, w_ff1.T, w_out, w_ff2, lat]


def _from_comm_shards(parts):
    w_in_t, w_ff1_t, w_out, w_ff2, lat = parts
    return {"w_in": w_in_t.T, "w_ff1": w_ff1_t.T, "w_out": w_out, "w_ff2": w_ff2,
            "w_uq": lat[0:UQ_ROWS].reshape(448, 96), "w_ukv": lat[UQ_ROWS:].reshape(512, 128)}


def _full_weights(gathered):
    w_in_t, w_ff1_t, w_out, w_ff2, lat = gathered
    w_uq = lat[:, 0:UQ_ROWS].reshape(8, 448, 96).transpose(1, 0, 2).reshape(448, 768)
    w_ukv = lat[:, UQ_ROWS:].reshape(8, 512, 128).transpose(1, 0, 2).reshape(512, 1024)
    return {"w_in_t": w_in_t.reshape(PROJ_W, D_MODEL), "w_ff1_t": w_ff1_t.reshape(D_FF, D_MODEL),
            "w_out": w_out.reshape(D_MODEL, D_MODEL), "w_ff2": w_ff2.reshape(D_FF, D_MODEL),
            "w_lat": _lat_weight(w_uq, w_ukv)}


def _grads_by_destination(dw):
    dw_uq, dw_ukv = _lat_weight_grads(dw["w_lat"])
    lat = jnp.concatenate([dw_uq.reshape(448, 8, 96).transpose(1, 0, 2).reshape(8, UQ_ROWS, LANES),
                           dw_ukv.reshape(512, 8, 128).transpose(1, 0, 2)], axis=1)
    full = [dw["w_in_t"], dw["w_ff1_t"], dw["w_out"], dw["w_ff2"], lat]
    return [a.reshape((4, 2) + COMM_SHAPE[n]) for a, n in zip(full, COMM)]


def _layer_fwd(x, W, G, tabs):
    s = {"x0": x}
    s["h1"] = _rms_fwd(x, G["ln1_g"], name="rms1_fwd")
    s["proj"] = _matmul(s["h1"], W["w_in_t"], mode="nt", tm=1024, tn=768, tk=2048, out_dtype=F32, name="mm_in")
    s["lat"], kpe, s["qb"], s["kb"], s["vb"], s["qc"], s["kc"], s["vc"] = _prep_fwd(
        s["proj"], G["glat"], G["gqn"], G["gkn"], tabs)
    qkva = _matmul(s["lat"], W["w_lat"], mode="nn", tm=1024, tn=1024, tk=1024, out_dtype=F32, name="mm_lat")
    s["qa"], s["ka"], s["va"] = _prep_a2_fwd(qkva, kpe, tabs[0])
    s["oa"], s["lse_a"] = _attn_fwd(s["qa"], s["ka"], s["va"], H=4, G=1, dk=256, dv=128, scale=SCALE_A,
                                    name="attn_a_fwd")
    s["ob"], s["lse_b"] = _attn_fwd(s["qb"], s["kb"], s["vb"], H=6, G=3, dk=128, dv=128, scale=SCALE_H,
                                    name="attn_b_fwd")
    s["oc"], s["lse_c"] = _mixc_fwd(s["qc"], s["kc"], s["vc"])
    s["mixed"] = _outnorm_fwd(s["oa"], s["ob"], s["oc"], G["g_out"])
    s["x1"] = _matmul(s["mixed"], W["w_out"], mode="nn", tm=1024, tn=1024, tk=2048, out_dtype=F32, name="mm_out",
                      epi="residual", extra=x)
    s["h2"] = _rms_fwd(s["x1"], G["ln2_g"], name="rms2_fwd")
    s["z"], s["u"] = _matmul(s["h2"], W["w_ff1_t"], mode="nt", tm=1024, tn=1024, tk=2048, out_dtype=CDT,
                             name="mm_ff1", epi="relu2")
    x2 = _matmul(s["u"], W["w_ff2"], mode="nn", tm=1024, tn=1024, tk=2048, out_dtype=F32, name="mm_ff2",
                 epi="residual", extra=s["x1"])
    return x2, s


def _layer_bwd(dx2, dx2b, s, W, G, tabs):
    dw, dg = {}, {}
    dz = _matmul(dx2b, W["w_ff2"], mode="nt", tm=1024, tn=1024, tk=2048, out_dtype=CDT, name="mm_ff2_dx",
                 epi="drelu2", extra=s["z"])
    dw["w_ff2"] = _matmul(s["u"], dx2b, mode="tn", tm=2048, tn=2048, tk=512, out_dtype=CDT, name="mm_ff2_dw")
    dh2 = _matmul(dz, W["w_ff1_t"], mode="nn", tm=1024, tn=1024, tk=2048, out_dtype=F32, name="mm_ff1_dx")
    dw["w_ff1_t"] = _matmul(dz, s["h2"], mode="tn", tm=2048, tn=2048, tk=512, out_dtype=CDT, name="mm_ff1_dw")
    dx1, dx1b, dg["ln2_g"] = _rms_bwd(s["x1"], G["ln2_g"], dh2, dx2, name="rms2_bwd")
    dmixed = _matmul(dx1b, W["w_out"], mode="nt", tm=1024, tn=1024, tk=2048, out_dtype=F32, name="mm_out_dx")
    dw["w_out"] = _matmul(s["mixed"], dx1b, mode="tn", tm=2048, tn=2048, tk=512, out_dtype=CDT, name="mm_out_dw")
    doa, dob, doc, dla, dlb, dlc, dg["g_out"] = _outnorm_bwd(s["oa"], s["ob"], s["oc"], G["g_out"], dmixed)
    dqa, dka, dva = _attn_bwd(s["qa"], s["ka"], s["va"], doa, s["lse_a"], dla, H=4, G=1, dk=256, dv=128,
                              scale=SCALE_A, name="attn_a_bwd")
    dqb, dkb, dvb = _attn_bwd(s["qb"], s["kb"], s["vb"], dob, s["lse_b"], dlb, H=6, G=3, dk=128, dv=128,
                              scale=SCALE_H, name="attn_b_bwd")
    dqc, dkc, dvc = _mixc_bwd(s["qc"], s["kc"], s["vc"], doc, s["lse_c"], dlc)
    dqkva, dkr = _prep_a2_bwd(dqa, dka, dva, tabs[0])
    dlat = _matmul(dqkva, W["w_lat"], mode="nt", tm=1024, tn=1024, tk=2048, out_dtype=F32, name="mm_lat_dx")
    dw["w_lat"] = _matmul(s["lat"], dqkva, mode="tn", tm=1024, tn=2048, tk=512, out_dtype=CDT, name="mm_lat_dw")
    dproj, dg["glat"], dg["gqn"], dg["gkn"] = _prep_bwd(
        s["proj"], G["glat"], G["gqn"], G["gkn"], tabs, dlat, dkr, dqb, dkb, dvb, dqc, dkc, dvc)
    dh1 = _matmul(dproj, W["w_in_t"], mode="nn", tm=1024, tn=1024, tk=1536, out_dtype=F32, name="mm_in_dx")
    dw["w_in_t"] = _matmul(dproj, s["h1"], mode="tn", tm=1536, tn=2048, tk=512, out_dtype=CDT, name="mm_in_dw")
    dx0, dx0b, dg["ln1_g"] = _rms_bwd(s["x0"], G["ln1_g"], dh1, dx1, name="rms1_bwd")
    return dx0, dx0b, dw, dg


def _layer_gains(l, ln1_g, g_q_a, g_kv_a, g_qn_b, g_kn_b, g_out, ln2_g):
    return {"ln1_g": ln1_g[l], "ln2_g": ln2_g[l], "g_out": g_out[l].reshape(1, 2048),
            "glat": jnp.concatenate([g_q_a[l], g_kv_a[l], jnp.zeros((LAT_W - KV_HI,), F32)]).reshape(1, LAT_W),
            "gqn": g_qn_b[l].reshape(1, 128), "gkn": g_kn_b[l].reshape(1, 128)}


def _gain_grads(dg):
    glat = dg["glat"].reshape(-1)
    return {"ln1_g": dg["ln1_g"].reshape(-1), "g_q_a": glat[0:KV_LO], "g_kv_a": glat[KV_LO:KV_HI],
            "g_qn_b": dg["gqn"].reshape(-1), "g_kn_b": dg["gkn"].reshape(-1), "g_out": dg["g_out"].reshape(-1),
            "ln2_g": dg["ln2_g"].reshape(-1)}


def _local_step(x, tgt, weights, gains, ln_f_g):
    S = x.shape[0]
    tabs = _rope_tables(S)
    depth = len(weights)
    saved = []
    for l in range(depth):
        x, s = _layer_fwd(x, weights[l], gains[l], tabs)
        saved.append(s)
    loss, dx, dxb, dlnf = _loss_head(x, ln_f_g, tgt)
    dws, dgs = [None] * depth, [None] * depth
    for l in reversed(range(depth)):
        dx, dxb, dws[l], dgs[l] = _layer_bwd(dx, dxb, saved[l], weights[l], gains[l], tabs)
    return loss, dx, dws, dgs, dlnf


SMALL_SIZES = (("ln1_g", 2048), ("g_q_a", 448), ("g_kv_a", 512), ("g_qn_b", 128), ("g_kn_b", 128), ("g_out", 2048),
               ("ln2_g", 2048))


def _pack_small(per_layer, ln_f):
    flat = jnp.concatenate([per_layer[n].reshape(-1) for n, _ in SMALL_SIZES] + [ln_f.reshape(-1)])
    rows = -(-flat.shape[0] // (8 * LANES)) * 8
    return jnp.concatenate([flat, jnp.zeros((rows * LANES - flat.shape[0],), F32)]).reshape(rows, LANES)


def _unpack_small(packed, depth):
    flat, out, lo = packed.reshape(-1), {}, 0
    for n, w in SMALL_SIZES:
        out[n] = flat[lo:lo + depth * w].reshape(depth, w)
        lo += depth * w
    out["ln_f_g"] = flat[lo:lo + 2048]
    return out


def kernel(x, ln1_g, w_in, g_q_a, w_uq, g_kv_a, w_ukv, g_qn_b, g_kn_b, g_out, w_out, ln2_g, w_ff1, w_ff2, ln_f_g, loss_target, m_ln1_g, m_w_in, m_g_q_a, m_w_uq, m_g_kv_a, m_w_ukv, m_g_qn_b, m_g_kn_b, m_g_out, m_w_out, m_ln2_g, m_w_ff1, m_w_ff2, m_ln_f_g, v_ln1_g, v_w_in, v_g_q_a, v_w_uq, v_g_kv_a, v_w_ukv, v_g_qn_b, v_g_kn_b, v_g_out, v_w_out, v_ln2_g, v_w_ff1, v_w_ff2, v_ln_f_g):
    depth = w_in.shape[0]
    S = x.shape[1]
    big_w = {"w_in": w_in, "w_uq": w_uq, "w_ukv": w_ukv, "w_out": w_out, "w_ff1": w_ff1, "w_ff2": w_ff2}
    big_m = {"w_in": m_w_in, "w_uq": m_w_uq, "w_ukv": m_w_ukv, "w_out": m_w_out, "w_ff1": m_w_ff1, "w_ff2": m_w_ff2}
    big_v = {"w_in": v_w_in, "w_uq": v_w_uq, "w_ukv": v_w_ukv, "w_out": v_w_out, "w_ff1": v_w_ff1, "w_ff2": v_w_ff2}
    small_w = {"ln1_g": ln1_g, "g_q_a": g_q_a, "g_kv_a": g_kv_a, "g_qn_b": g_qn_b, "g_kn_b": g_kn_b, "g_out": g_out,
               "ln2_g": ln2_g}
    small_m = {"ln1_g": m_ln1_g, "g_q_a": m_g_q_a, "g_kv_a": m_g_kv_a, "g_qn_b": m_g_qn_b, "g_kn_b": m_g_kn_b,
               "g_out": m_g_out, "ln2_g": m_ln2_g}
    small_v = {"ln1_g": v_ln1_g, "g_q_a": v_g_q_a, "g_kv_a": v_g_kv_a, "g_qn_b": v_g_qn_b, "g_kn_b": v_g_kn_b,
               "g_out": v_g_out, "ln2_g": v_ln2_g}

    weights = []
    for l in range(depth):
        shards = _comm_shards(*[big_w[n][l].astype(CDT) for n in BIG])
        weights.append(_full_weights(_all_gather(shards)))
    gains = [_layer_gains(l, ln1_g, g_q_a, g_kv_a, g_qn_b, g_kn_b, g_out, ln2_g) for l in range(depth)]

    loss_part, dx, dws, dgs, dlnf = _local_step(x.reshape(S, D_MODEL), loss_target.reshape(S, D_MODEL), weights,
                                                 gains, ln_f_g)
    loss = lax.psum(loss_part[0, 0], ("x", "y", "c"))
    grad_x = dx.reshape(1, S, D_MODEL)

    shard_grads = []
    for l in range(depth):
        gs = _grads_by_destination(dws[l])
        got = _rs_sibling_exchange(gs)
        chip = [_rs_chip_sum(g, r, name="rs_chip_sum_" + n) for g, r, n in zip(gs, got, COMM)]
        landed = _rs_chip_exchange(chip)
        shard_grads.append(_from_comm_shards([_rs_final_sum(r, name="rs_final_sum_" + n) for r, n in zip(landed, COMM)]))
    big_g = {n: jnp.stack([shard_grads[l][n] for l in range(depth)]) for n in BIG}

    named = [_gain_grads(dgs[l]) for l in range(depth)]
    per_layer = {n: jnp.stack([named[l][n] for l in range(depth)]) for n, _ in SMALL_SIZES}
    small_g = _unpack_small(_all_reduce_small(_pack_small(per_layer, dlnf.reshape(-1))), depth)

    upd = {}
    for n in BIG:
        shp = big_w[n].shape
        two_d = (shp[0] * shp[1], shp[2])
        d, nm, nv = _adamw(big_w[n].reshape(two_d), big_g[n].reshape(two_d), big_m[n].reshape(two_d),
                           big_v[n].reshape(two_d), name="adamw_" + n)
        upd[n] = (d.reshape(shp), nm.reshape(shp), nv.reshape(shp))
    small_w["ln_f_g"], small_m["ln_f_g"], small_v["ln_f_g"] = ln_f_g, m_ln_f_g, v_ln_f_g
    names_small = [n for n, _ in SMALL_SIZES]
    pw = _pack_small({n: small_w[n] for n in names_small}, small_w["ln_f_g"])
    pg = _pack_small({n: small_g[n] for n in names_small}, small_g["ln_f_g"])
    pm = _pack_small({n: small_m[n] for n in names_small}, small_m["ln_f_g"])
    pv = _pack_small({n: small_v[n] for n in names_small}, small_v["ln_f_g"])
    d, nm, nv = _adamw(pw, pg, pm, pv, name="adamw_small")
    sd, snm, snv = _unpack_small(d, depth), _unpack_small(nm, depth), _unpack_small(nv, depth)
    for n in names_small + ["ln_f_g"]:
        upd[n] = (sd[n], snm[n], snv[n])

    order = ["ln1_g", "w_in", "g_q_a", "w_uq", "g_kv_a", "w_ukv", "g_qn_b", "g_kn_b", "g_out", "w_out", "ln2_g", "w_ff1",
             "w_ff2", "ln_f_g"]
    grads = {**big_g, **small_g}
    return (loss, grad_x, *[grads[n] for n in order], *[upd[n][0] for n in order], *[upd[n][1] for n in order],
            *[upd[n][2] for n in order])
```

```python
import functools
import math

import jax
import jax.numpy as jnp
from jax import lax
from jax.experimental import pallas as pl
from jax.experimental.pallas import tpu as pltpu

D_MODEL = 2048
D_FF = 8192
EPS = 1e-6
NEG_INF = -1e30
Q_LORA = 448
ROPE_THETA = 10000.0
GRID_W = 64
DILATIONS = (1, 4, 16)
BAND_HALF = 64
SCALE_A = 1.0 / math.sqrt(192.0)
SCALE_H = 1.0 / math.sqrt(128.0)
ADAM_LR, ADAM_B1, ADAM_B2, ADAM_EPS, ADAM_WD, ADAM_STEP = 0.001, 0.9, 0.999, 1e-08, 0.01, 10

CDT = jnp.bfloat16
F32 = jnp.float32
LANES = 128
VMEM_LIMIT = 56 * 1024 * 1024

PROJ_W = 4608
LAT_W = 1024
KV_LO, KV_HI = 448, 960
OFF_BQ, OFF_BK, OFF_BV, OFF_CQ, OFF_CK, OFF_CV = 1024, 1792, 2048, 2304, 3072, 3840

NN = ((1,), (0,))
NT = ((1,), (1,))
TN = ((0,), (0,))

BIG = ("w_in", "w_uq", "w_ukv", "w_out", "w_ff1", "w_ff2")
COMM = ("w_in_t", "w_ff1_t", "w_out", "w_ff2", "lat")
COMM_SHAPE = {"w_in_t": (576, 2048), "w_ff1_t": (1024, 2048), "w_out": (256, 2048), "w_ff2": (1024, 2048),
              "lat": (848, 128)}
UQ_ROWS = 448 * 96 // LANES


def _dot(a, b, dims):
    return lax.dot_general(a, b, (dims, ((), ())), preferred_element_type=F32)


def _cparams(dims=None):
    return pltpu.CompilerParams(dimension_semantics=dims, vmem_limit_bytes=VMEM_LIMIT)


def _sds(shape, dtype):
    return jax.ShapeDtypeStruct(shape, dtype)


MESH = pl.DeviceIdType.MESH
ANY = pl.BlockSpec(memory_space=pl.ANY)


class _Rider:
    def __init__(self, name, arrays, out_shape, scratch, aliases, start, finish):
        self.name, self.arrays, self.out_shape, self.scratch = name, list(arrays), list(out_shape), list(scratch)
        self.aliases, self.start, self.finish = dict(aliases), start, finish


def _join(a, b):
    if a is None or b is None:
        return a if b is None else b
    na, oa, sa = len(a.arrays), len(a.out_shape), len(a.scratch)
    aliases = dict(a.aliases)
    aliases.update({na + i: oa + o for i, o in b.aliases.items()})

    def start(ins, outs, sems):
        a.start(ins[:na], outs[:oa], sems[:sa])
        b.start(ins[na:], outs[oa:], sems[sa:])

    def finish(ins, outs, sems):
        a.finish(ins[:na], outs[:oa], sems[:sa])
        b.finish(ins[na:], outs[oa:], sems[sa:])

    return _Rider(a.name + "_" + b.name, a.arrays + b.arrays, a.out_shape + b.out_shape, a.scratch + b.scratch,
                  aliases, start, finish)


def _carried_call(body, *, name, grid, in_specs, out_specs, out_shape, scratch_shapes, dims, args, rider):
    in_specs, out_specs, out_shape = list(in_specs), list(out_specs), list(out_shape)
    scratch_shapes = list(scratch_shapes)
    if rider is None:
        res = pl.pallas_call(body, name=name, grid=grid, in_specs=in_specs, out_specs=out_specs, out_shape=out_shape,
                             scratch_shapes=scratch_shapes, compiler_params=_cparams(dims))(*args)
        return list(res), []
    n_in, n_out, n_scr = len(in_specs), len(out_specs), len(scratch_shapes)
    r_in, r_out = len(rider.arrays), len(rider.out_shape)

    def wrapped(*refs):
        o0 = n_in + r_in
        s0 = o0 + n_out + r_out
        ins, outs, sems = refs[n_in:o0], refs[o0 + n_out:s0], refs[s0 + n_scr:]
        ids = [pl.program_id(a) for a in range(len(grid))]
        first = functools.reduce(jnp.logical_and, [i == 0 for i in ids])
        last = functools.reduce(jnp.logical_and, [i == g - 1 for i, g in zip(ids, grid)])

        @pl.when(first)
        def _():
            rider.start(ins, outs, sems)

        body(*refs[:n_in], *refs[o0:o0 + n_out], *refs[s0:s0 + n_scr])

        @pl.when(last)
        def _():
            rider.finish(ins, outs, sems)

    res = pl.pallas_call(
        wrapped, name=name + "_" + rider.name, grid=grid, in_specs=in_specs + [ANY] * r_in,
        out_specs=out_specs + [ANY] * r_out, out_shape=out_shape + rider.out_shape,
        scratch_shapes=scratch_shapes + rider.scratch,
        input_output_aliases={n_in + i: n_out + o for i, o in rider.aliases.items()},
        compiler_params=_cparams(("arbitrary",) * len(grid)),
    )(*args, *rider.arrays)
    return list(res[:n_out]), list(res[n_out:])


def _run_rider(rider):
    def body(*refs):
        r_in, r_out = len(rider.arrays), len(rider.out_shape)
        ins, outs, sems = refs[:r_in], refs[r_in:r_in + r_out], refs[r_in + r_out:]
        rider.start(ins, outs, sems)
        rider.finish(ins, outs, sems)

    res = pl.pallas_call(
        body, name=rider.name, in_specs=[ANY] * len(rider.arrays), out_specs=[ANY] * len(rider.out_shape),
        out_shape=rider.out_shape, scratch_shapes=rider.scratch, input_output_aliases=rider.aliases,
    )(*rider.arrays)
    return list(res)


def _mesh_place():
    x, y, c = lax.axis_index("x"), lax.axis_index("y"), lax.axis_index("c")
    return x, y, c, [(1 - x, y), (x, 1 - y), (1 - x, 1 - y)]


def _remote(src, dst, send, recv, dev):
    return pltpu.make_async_remote_copy(src_ref=src, dst_ref=dst, send_sem=send, recv_sem=recv, device_id=dev,
                                        device_id_type=MESH)


def _ag_first_rider(shards, tag):
    n = len(shards)

    def copies(ins, outs, sems):
        send, recv, _ = sems
        x, y, c, chips = _mesh_place()
        me = 4 * x + 2 * y + c
        peers = [(x, y, 1 - c)] + [(cx, cy, c) for cx, cy in chips]
        out = []
        for t in range(n):
            for k, dev in enumerate(peers):
                theirs = 4 * dev[0] + 2 * dev[1] + dev[2]
                out.append((_remote(ins[t], outs[t].at[me], send.at[t, k], recv.at[t, k], dev),
                            _remote(ins[t], outs[t].at[theirs], send.at[t, k], recv.at[t, k], dev)))
        mine = [pltpu.make_async_copy(ins[t], outs[t].at[me], sems[2].at[t]) for t in range(n)]
        return out, mine

    def start(ins, outs, sems):
        pairs, mine = copies(ins, outs, sems)
        for cp in mine:
            cp.start()
        for snd, _ in pairs:
            snd.start()

    def finish(ins, outs, sems):
        pairs, mine = copies(ins, outs, sems)
        for _, rcv in pairs:
            rcv.wait_recv()
        for snd, _ in pairs:
            snd.wait_send()
        for cp in mine:
            cp.wait()

    return _Rider("ag1" + tag, shards, [_sds((8,) + s.shape, s.dtype) for s in shards],
                  [pltpu.SemaphoreType.DMA((n, 4)), pltpu.SemaphoreType.DMA((n, 4)), pltpu.SemaphoreType.DMA((n,))],
                  {}, start, finish)


def _ag_second_rider(gathered, tag):
    n = len(gathered)

    def copies(ins, outs, sems):
        send, recv = sems
        x, y, c, chips = _mesh_place()
        out = []
        for t in range(n):
            for j, (cx, cy) in enumerate(chips):
                here, there = 4 * cx + 2 * cy + c, 4 * cx + 2 * cy + (1 - c)
                out.append((_remote(ins[t].at[here], outs[t].at[here], send.at[t, j], recv.at[t, j], (x, y, 1 - c)),
                            _remote(ins[t].at[here], outs[t].at[there], send.at[t, j], recv.at[t, j], (x, y, 1 - c))))
        return out

    def start(ins, outs, sems):
        for snd, _ in copies(ins, outs, sems):
            snd.start()

    def finish(ins, outs, sems):
        pairs = copies(ins, outs, sems)
        for _, rcv in pairs:
            rcv.wait_recv()
        for snd, _ in pairs:
            snd.wait_send()

    return _Rider("ag2" + tag, gathered, [_sds(g.shape, g.dtype) for g in gathered],
                  [pltpu.SemaphoreType.DMA((n, 3)), pltpu.SemaphoreType.DMA((n, 3))],
                  {t: t for t in range(n)}, start, finish)


def _rs_sibling_rider(gs, tag):
    n = len(gs)

    def copies(ins, outs, sems):
        send, recv = sems
        x, y, c, _ = _mesh_place()
        return [_remote(ins[t].at[k, 1 - c], outs[t].at[k], send.at[t, k], recv.at[t, k], (x, y, 1 - c))
                for t in range(n) for k in range(4)]

    def start(ins, outs, sems):
        for cp in copies(ins, outs, sems):
            cp.start()

    def finish(ins, outs, sems):
        for cp in copies(ins, outs, sems):
            cp.wait()

    return _Rider("rs1" + tag, gs, [_sds((4,) + g.shape[2:], g.dtype) for g in gs],
                  [pltpu.SemaphoreType.DMA((n, 4)), pltpu.SemaphoreType.DMA((n, 4))], {}, start, finish)


def _rs_chip_rider(ps, tag):
    n = len(ps)

    def copies(ins, outs, sems):
        send, recv, local = sems
        x, y, c, chips = _mesh_place()
        my_chip = 2 * x + y
        out = []
        for t in range(n):
            for j, (cx, cy) in enumerate(chips):
                dev = (cx, cy, c)
                out.append((_remote(ins[t].at[2 * cx + cy], outs[t].at[my_chip], send.at[t, j], recv.at[t, j], dev),
                            _remote(ins[t].at[my_chip], outs[t].at[2 * cx + cy], send.at[t, j], recv.at[t, j], dev)))
        mine = [pltpu.make_async_copy(ins[t].at[my_chip], outs[t].at[my_chip], local.at[t]) for t in range(n)]
        return out, mine

    def start(ins, outs, sems):
        pairs, mine = copies(ins, outs, sems)
        for cp in mine:
            cp.start()
        for snd, _ in pairs:
            snd.start()

    def finish(ins, outs, sems):
        pairs, mine = copies(ins, outs, sems)
        for _, rcv in pairs:
            rcv.wait_recv()
        for snd, _ in pairs:
            snd.wait_send()
        for cp in mine:
            cp.wait()

    return _Rider("rs2" + tag, ps, [_sds(p.shape, p.dtype) for p in ps],
                  [pltpu.SemaphoreType.DMA((n, 3)), pltpu.SemaphoreType.DMA((n, 3)), pltpu.SemaphoreType.DMA((n,))],
                  {}, start, finish)


def _matmul(a, b, *, mode, tm, tn, tk, out_dtype, name, epi=None, extra=None, rider=None):
    if mode == "nn":
        (M, K), (K2, N) = a.shape, b.shape
    elif mode == "nt":
        (M, K), (N, K2) = a.shape, b.shape
    else:
        (K, M), (K2, N) = a.shape, b.shape
    tm, tn, tk = min(tm, M), min(tn, N), min(tk, K)
    assert K == K2 and M % tm == 0 and N % tn == 0 and K % tk == 0, (name, a.shape, b.shape)
    nk = K // tk
    dims = {"nn": NN, "nt": NT, "tn": TN}[mode]
    if mode == "tn":
        a_spec = pl.BlockSpec((tk, tm), lambda i, j, k: (k, i))
    else:
        a_spec = pl.BlockSpec((tm, tk), lambda i, j, k: (i, k))
    if mode == "nt":
        b_spec = pl.BlockSpec((tn, tk), lambda i, j, k: (j, k))
    else:
        b_spec = pl.BlockSpec((tk, tn), lambda i, j, k: (k, j))
    tile = pl.BlockSpec((tm, tn), lambda i, j, k: (i, j))
    n_extra = 1 if epi in ("residual", "drelu2") else 0
    n_out = 2 if epi == "relu2" else 1

    def body(*refs):
        a_ref, b_ref = refs[0], refs[1]
        extra_refs = refs[2:2 + n_extra]
        out_refs = refs[2 + n_extra:2 + n_extra + n_out]

        def finish(acc):
            if epi is None:
                out_refs[0][...] = acc.astype(out_dtype)
            elif epi == "residual":
                out_refs[0][...] = (extra_refs[0][...] + acc).astype(out_dtype)
            elif epi == "relu2":
                out_refs[0][...] = acc.astype(out_dtype)
                r = jnp.maximum(acc, 0.0)
                out_refs[1][...] = (r * r).astype(out_dtype)
            else:
                z = extra_refs[0][...].astype(F32)
                out_refs[0][...] = (acc * (2.0 * jnp.maximum(z, 0.0))).astype(out_dtype)

        part = _dot(a_ref[...], b_ref[...], dims)
        if nk == 1:
            finish(part)
        else:
            acc_ref = refs[-1]
            k = pl.program_id(2)

            @pl.when(k == 0)
            def _():
                acc_ref[...] = part

            @pl.when(k > 0)
            def _():
                acc_ref[...] += part

            @pl.when(k == nk - 1)
            def _():
                finish(acc_ref[...])

    res, carried = _carried_call(
        body, name=name, grid=(M // tm, N // tn, nk), in_specs=[a_spec, b_spec] + [tile] * n_extra,
        out_specs=[tile] * n_out, out_shape=[_sds((M, N), out_dtype)] * n_out,
        scratch_shapes=[pltpu.VMEM((tm, tn), F32)] if nk > 1 else [],
        dims=("parallel", "parallel", "arbitrary"), args=[a, b] + ([extra] if n_extra else []), rider=rider)
    res = res if n_out > 1 else res[0]
    return res if rider is None else (res, carried)


def _rms_val(x, g, n):
    r = lax.rsqrt(jnp.sum(x * x, axis=-1, keepdims=True) * (1.0 / n) + EPS)
    y = x * r
    return (y if g is None else y * g), r


def _rms_bwd_val(x, g, dy, n):
    r = lax.rsqrt(jnp.sum(x * x, axis=-1, keepdims=True) * (1.0 / n) + EPS)
    xhat = x * r
    dyg = dy if g is None else dy * g
    dx = r * (dyg - xhat * (jnp.sum(dyg * xhat, axis=-1, keepdims=True) * (1.0 / n)))
    return dx, dy * xhat


def _rope_val(x, c, sa, sb, shift):
    return x * c + pltpu.roll(x, LANES - shift, 1) * sa + pltpu.roll(x, shift, 1) * sb


def _rope_t_val(dy, c, sa, sb, shift):
    return dy * c + pltpu.roll(dy * sa, shift, 1) + pltpu.roll(dy * sb, LANES - shift, 1)


def _colsum(x):
    return jnp.sum(x, axis=0, keepdims=True)


def _rope_tables(S):
    pos = lax.broadcasted_iota(jnp.int32, (S, LANES), 0)
    lane = lax.broadcasted_iota(jnp.int32, (S, LANES), 1)

    def tables(p, dim, active):
        half = dim // 2
        inv = jnp.power(ROPE_THETA, -(2 * (lane % half)).astype(F32) / dim)
        a = p.astype(F32) * inv
        first = (lane % dim) < half
        zero = jnp.zeros((S, LANES), F32)
        return (jnp.where(active, jnp.cos(a), zero), jnp.where(active & first, -jnp.sin(a), zero),
                jnp.where(active & ~first, jnp.sin(a), zero))

    tab_a = tables(pos, 64, lane >= 64)
    tab_b = tables(jnp.where(lane < 64, pos // GRID_W, pos % GRID_W), 64, lane >= 0)
    tab_c = tables(pos, 128, lane >= 0)
    return tab_a, tab_b, tab_c


ROPE_SHIFT_AB = 32
ROPE_SHIFT_C = 64


def _rms_fwd(x, g, *, name, tr=512):
    S, W = x.shape
    tr = min(tr, S)

    def body(x_ref, g_ref, o_ref):
        y, _ = _rms_val(x_ref[...], g_ref[...], W)
        o_ref[...] = y.astype(CDT)

    return pl.pallas_call(
        body, name=name, grid=(S // tr,),
        in_specs=[pl.BlockSpec((tr, W), lambda i: (i, 0)), pl.BlockSpec((1, W), lambda i: (0, 0))],
        out_specs=pl.BlockSpec((tr, W), lambda i: (i, 0)), out_shape=_sds((S, W), CDT),
        compiler_params=_cparams(("parallel",)),
    )(x, g.reshape(1, W))


def _rms_bwd(x, g, dy, res, *, name, tr=256):
    S, W = x.shape
    tr = min(tr, S)

    def body(x_ref, g_ref, dy_ref, res_ref, dx_ref, dxb_ref, dg_ref):
        dx, dgt = _rms_bwd_val(x_ref[...], g_ref[...], dy_ref[...], W)
        dx = res_ref[...] + dx
        dx_ref[...] = dx
        dxb_ref[...] = dx.astype(CDT)

        @pl.when(pl.program_id(0) == 0)
        def _():
            dg_ref[...] = jnp.zeros_like(dg_ref)

        dg_ref[...] += _colsum(dgt)

    row = pl.BlockSpec((tr, W), lambda i: (i, 0))
    vec = pl.BlockSpec((1, W), lambda i: (0, 0))
    return pl.pallas_call(
        body, name=name, grid=(S // tr,),
        in_specs=[row, vec, row, row], out_specs=[row, row, vec],
        out_shape=[_sds((S, W), F32), _sds((S, W), CDT), _sds((1, W), F32)],
        compiler_params=_cparams(("arbitrary",)),
    )(x, g.reshape(1, W), dy, res)


def _loss_head(x, g, tgt, *, tr=256):
    S, W = x.shape
    tr = min(tr, S)

    def body(x_ref, g_ref, t_ref, loss_ref, dx_ref, dxb_ref, dg_ref):
        xv, gv = x_ref[...], g_ref[...]
        y, _ = _rms_val(xv, gv, W)
        err = y - t_ref[...]
        part = 0.5 * jnp.sum(jnp.sum(err * err, axis=-1, keepdims=True) * (1.0 / W), axis=0, keepdims=True)
        dx, dgt = _rms_bwd_val(xv, gv, err * (1.0 / W), W)
        dx_ref[...] = dx
        dxb_ref[...] = dx.astype(CDT)

        @pl.when(pl.program_id(0) == 0)
        def _():
            dg_ref[...] = jnp.zeros_like(dg_ref)
            loss_ref[...] = jnp.zeros_like(loss_ref)

        dg_ref[...] += _colsum(dgt)
        loss_ref[...] += jnp.broadcast_to(part, (1, LANES))

    row = pl.BlockSpec((tr, W), lambda i: (i, 0))
    vec = pl.BlockSpec((1, W), lambda i: (0, 0))
    return pl.pallas_call(
        body, name="loss_head", grid=(S // tr,),
        in_specs=[row, vec, row], out_specs=[pl.BlockSpec((1, LANES), lambda i: (0, 0)), row, row, vec],
        out_shape=[_sds((1, LANES), F32), _sds((S, W), F32), _sds((S, W), CDT), _sds((1, W), F32)],
        compiler_params=_cparams(("arbitrary",)),
    )(x, g.reshape(1, W), tgt)


def _tab_specs(tr):
    return [pl.BlockSpec((tr, LANES), lambda i: (i, 0))] * 9


def _lat_masks(shape):
    lane = lax.broadcasted_iota(jnp.int32, shape, 1)
    return lane < KV_LO, (lane >= KV_LO) & (lane < KV_HI)


def _prep_fwd(proj, glat, gqn, gkn, tabs, *, tr=128):
    S = proj.shape[0]
    tr = min(tr, S)

    def body(p_ref, glat_ref, gqn_ref, gkn_ref, ac, aa, ab, bc, ba, bb, cc, ca, cb,
             lat_ref, kpe_ref, qb_ref, kb_ref, vb_ref, qc_ref, kc_ref, vc_ref):
        x = p_ref[:, 0:LAT_W]
        is_q, is_kv = _lat_masks(x.shape)
        yq, _ = _rms_val(jnp.where(is_q, x, 0.0), None, Q_LORA)
        ykv, _ = _rms_val(jnp.where(is_kv, x, 0.0), None, KV_HI - KV_LO)
        lat_ref[...] = ((yq + ykv) * glat_ref[...]).astype(CDT)
        kpe_ref[...] = _rope_val(x[:, LAT_W - LANES:LAT_W], ac[...], aa[...], ab[...], ROPE_SHIFT_AB).astype(CDT)
        for h in range(6):
            xh = p_ref[:, OFF_BQ + LANES * h:OFF_BQ + LANES * (h + 1)]
            y = _rope_val(_rms_val(xh, gqn_ref[...], LANES)[0], bc[...], ba[...], bb[...], ROPE_SHIFT_AB)
            qb_ref[:, LANES * h:LANES * (h + 1)] = y.astype(CDT)
        for h in range(2):
            xh = p_ref[:, OFF_BK + LANES * h:OFF_BK + LANES * (h + 1)]
            y = _rope_val(_rms_val(xh, gkn_ref[...], LANES)[0], bc[...], ba[...], bb[...], ROPE_SHIFT_AB)
            kb_ref[:, LANES * h:LANES * (h + 1)] = y.astype(CDT)
        vb_ref[...] = p_ref[:, OFF_BV:OFF_BV + 256].astype(CDT)
        for h in range(6):
            sl = slice(LANES * h, LANES * (h + 1))
            qc_ref[:, sl] = _rope_val(p_ref[:, OFF_CQ + LANES * h:OFF_CQ + LANES * (h + 1)], cc[...], ca[...], cb[...],
                                      ROPE_SHIFT_C)
            kc_ref[:, sl] = _rope_val(p_ref[:, OFF_CK + LANES * h:OFF_CK + LANES * (h + 1)], cc[...], ca[...], cb[...],
                                      ROPE_SHIFT_C)
        vc_ref[...] = p_ref[:, OFF_CV:OFF_CV + 768]

    vec = lambda w: pl.BlockSpec((1, w), lambda i: (0, 0))
    row = lambda w: pl.BlockSpec((tr, w), lambda i: (i, 0))
    return pl.pallas_call(
        body, name="prep_fwd", grid=(S // tr,),
        in_specs=[row(PROJ_W), vec(LAT_W), vec(128), vec(128)] + _tab_specs(tr),
        out_specs=[row(LAT_W), row(128), row(768), row(256), row(256), row(768), row(768), row(768)],
        out_shape=[_sds((S, LAT_W), CDT), _sds((S, 128), CDT), _sds((S, 768), CDT), _sds((S, 256), CDT),
                   _sds((S, 256), CDT), _sds((S, 768), F32), _sds((S, 768), F32), _sds((S, 768), F32)],
        compiler_params=_cparams(("parallel",)),
    )(proj, glat, gqn, gkn, *tabs[0], *tabs[1], *tabs[2])


def _prep_a2_fwd(qkva, kpe, tab_a, *, tr=512):
    S = qkva.shape[0]
    tr = min(tr, S)

    def body(x_ref, kpe_ref, ac, aa, ab, qa_ref, ka_ref, va_ref):
        for h in range(4):
            lo, hi = 2 * LANES * h, 2 * LANES * h + LANES
            qa_ref[:, lo:hi] = x_ref[:, lo:hi].astype(CDT)
            qa_ref[:, hi:hi + LANES] = _rope_val(x_ref[:, hi:hi + LANES], ac[...], aa[...], ab[...],
                                                 ROPE_SHIFT_AB).astype(CDT)
            ka_ref[:, lo:hi] = x_ref[:, 1024 + LANES * h:1024 + LANES * (h + 1)].astype(CDT)
            ka_ref[:, hi:hi + LANES] = kpe_ref[...]
        va_ref[...] = x_ref[:, 1536:2048].astype(CDT)

    row = lambda w: pl.BlockSpec((tr, w), lambda i: (i, 0))
    return pl.pallas_call(
        body, name="prep_a2_fwd", grid=(S // tr,),
        in_specs=[row(2048), row(128)] + _tab_specs(tr)[:3],
        out_specs=[row(1024), row(1024), row(512)],
        out_shape=[_sds((S, 1024), CDT), _sds((S, 1024), CDT), _sds((S, 512), CDT)],
        compiler_params=_cparams(("parallel",)),
    )(qkva, kpe, *tab_a)


def _prep_a2_bwd(dqa, dka, dva, tab_a, *, tr=512):
    S = dqa.shape[0]
    tr = min(tr, S)

    def body(dq_ref, dk_ref, dv_ref, ac, aa, ab, dx_ref, dkr_ref):
        dkpe = jnp.zeros((tr, LANES), F32)
        for h in range(4):
            lo, hi = 2 * LANES * h, 2 * LANES * h + LANES
            dx_ref[:, lo:hi] = dq_ref[:, lo:hi].astype(CDT)
            dx_ref[:, hi:hi + LANES] = _rope_t_val(dq_ref[:, hi:hi + LANES], ac[...], aa[...], ab[...],
                                                   ROPE_SHIFT_AB).astype(CDT)
            dx_ref[:, 1024 + LANES * h:1024 + LANES * (h + 1)] = dk_ref[:, lo:hi].astype(CDT)
            dkpe = dkpe + dk_ref[:, hi:hi + LANES]
        dx_ref[:, 1536:2048] = dv_ref[...].astype(CDT)
        dkr_ref[...] = _rope_t_val(dkpe, ac[...], aa[...], ab[...], ROPE_SHIFT_AB)

    row = lambda w: pl.BlockSpec((tr, w), lambda i: (i, 0))
    return pl.pallas_call(
        body, name="prep_a2_bwd", grid=(S // tr,),
        in_specs=[row(1024), row(1024), row(512)] + _tab_specs(tr)[:3],
        out_specs=[row(2048), row(128)],
        out_shape=[_sds((S, 2048), CDT), _sds((S, 128), F32)],
        compiler_params=_cparams(("parallel",)),
    )(dqa, dka, dva, *tab_a)


def _prep_bwd(proj, glat, gqn, gkn, tabs, dlat, dkr, dqb, dkb, dvb, dqc, dkc, dvc, *, tr=128):
    S = proj.shape[0]
    tr = min(tr, S)

    def body(p_ref, glat_ref, gqn_ref, gkn_ref, ac, aa, ab, bc, ba, bb, cc, ca, cb,
             dlat_ref, dkr_ref, dqb_ref, dkb_ref, dvb_ref, dqc_ref, dkc_ref, dvc_ref,
             dp_ref, dglat_ref, dgqn_ref, dgkn_ref):
        @pl.when(pl.program_id(0) == 0)
        def _():
            dglat_ref[...] = jnp.zeros_like(dglat_ref)
            dgqn_ref[...] = jnp.zeros_like(dgqn_ref)
            dgkn_ref[...] = jnp.zeros_like(dgkn_ref)

        x = p_ref[:, 0:LAT_W]
        is_q, is_kv = _lat_masks(x.shape)
        dy, g = dlat_ref[...], glat_ref[...]
        dxq, dgq = _rms_bwd_val(jnp.where(is_q, x, 0.0), g, jnp.where(is_q, dy, 0.0), Q_LORA)
        dxkv, dgkv = _rms_bwd_val(jnp.where(is_kv, x, 0.0), g, jnp.where(is_kv, dy, 0.0), KV_HI - KV_LO)
        dglat_ref[...] += _colsum(dgq + dgkv)
        dx = dxq + dxkv
        dp_ref[:, 0:LAT_W - LANES] = dx[:, 0:LAT_W - LANES].astype(CDT)
        dp_ref[:, LAT_W - LANES:LAT_W] = (dx[:, LAT_W - LANES:LAT_W] + dkr_ref[...]).astype(CDT)
        dgqn = jnp.zeros((1, LANES), F32)
        for h in range(6):
            sl = slice(LANES * h, LANES * (h + 1))
            po = slice(OFF_BQ + LANES * h, OFF_BQ + LANES * (h + 1))
            dyh = _rope_t_val(dqb_ref[:, sl], bc[...], ba[...], bb[...], ROPE_SHIFT_AB)
            dxh, dgt = _rms_bwd_val(p_ref[:, po], gqn_ref[...], dyh, LANES)
            dp_ref[:, po] = dxh.astype(CDT)
            dgqn = dgqn + _colsum(dgt)
        dgqn_ref[...] += dgqn
        dgkn = jnp.zeros((1, LANES), F32)
        for h in range(2):
            sl = slice(LANES * h, LANES * (h + 1))
            po = slice(OFF_BK + LANES * h, OFF_BK + LANES * (h + 1))
            dyh = _rope_t_val(dkb_ref[:, sl], bc[...], ba[...], bb[...], ROPE_SHIFT_AB)
            dxh, dgt = _rms_bwd_val(p_ref[:, po], gkn_ref[...], dyh, LANES)
            dp_ref[:, po] = dxh.astype(CDT)
            dgkn = dgkn + _colsum(dgt)
        dgkn_ref[...] += dgkn
        dp_ref[:, OFF_BV:OFF_BV + 256] = dvb_ref[...].astype(CDT)
        for h in range(6):
            sl = slice(LANES * h, LANES * (h + 1))
            dp_ref[:, OFF_CQ + LANES * h:OFF_CQ + LANES * (h + 1)] = _rope_t_val(
                dqc_ref[:, sl], cc[...], ca[...], cb[...], ROPE_SHIFT_C).astype(CDT)
            dp_ref[:, OFF_CK + LANES * h:OFF_CK + LANES * (h + 1)] = _rope_t_val(
                dkc_ref[:, sl], cc[...], ca[...], cb[...], ROPE_SHIFT_C).astype(CDT)
        dp_ref[:, OFF_CV:OFF_CV + 768] = dvc_ref[...].astype(CDT)

    vec = lambda w: pl.BlockSpec((1, w), lambda i: (0, 0))
    row = lambda w: pl.BlockSpec((tr, w), lambda i: (i, 0))
    return pl.pallas_call(
        body, name="prep_bwd", grid=(S // tr,),
        in_specs=[row(PROJ_W), vec(LAT_W), vec(128), vec(128)] + _tab_specs(tr)
        + [row(LAT_W), row(128), row(768), row(256), row(256), row(768), row(768), row(768)],
        out_specs=[row(PROJ_W), vec(LAT_W), vec(128), vec(128)],
        out_shape=[_sds((S, PROJ_W), CDT), _sds((1, LAT_W), F32), _sds((1, 128), F32), _sds((1, 128), F32)],
        compiler_params=_cparams(("arbitrary",)),
    )(proj, glat, gqn, gkn, *tabs[0], *tabs[1], *tabs[2], dlat, dkr, dqb, dkb, dvb, dqc, dkc, dvc)


ATTN_TK = 512
LOG2E = 1.4426950408889634


def _attn_fwd(q, k, v, *, H, G, dk, dv, scale, name, tq=512, rider=None):
    S = q.shape[0]
    tq = min(tq, S)
    tk = min(ATTN_TK, S)
    c2 = scale * LOG2E

    def body(q_ref, k_ref, v_ref, o_ref, l_ref):
        qv = q_ref[...]
        chunks = [pl.ds(c * tk, tk) for c in range(S // tk)]
        s = [_dot(qv, k_ref[rows, :], NT) for rows in chunks]
        m = functools.reduce(jnp.maximum, [jnp.max(sc, axis=-1, keepdims=True) for sc in s])
        den = jnp.zeros((tq, 1), F32)
        acc = jnp.zeros((tq, dv), F32)
        for sc, rows in zip(s, chunks):
            e = jnp.exp2((sc - m) * c2)
            den = den + jnp.sum(e, axis=-1, keepdims=True)
            acc = acc + _dot(e.astype(CDT), v_ref[rows, :], NN)
        o_ref[...] = acc * (1.0 / den)
        l_ref[...] = jnp.broadcast_to(m * scale + jnp.log(den), (tq, LANES))

    return _carried_call(
        body, name=name, grid=(H, S // tq),
        in_specs=[pl.BlockSpec((tq, dk), lambda h, i: (i, h)), pl.BlockSpec((S, dk), lambda h, i: (0, h // G)),
                  pl.BlockSpec((S, dv), lambda h, i: (0, h // G))],
        out_specs=[pl.BlockSpec((tq, dv), lambda h, i: (i, h)), pl.BlockSpec((tq, LANES), lambda h, i: (i, h))],
        out_shape=[_sds((S, H * dv), F32), _sds((S, H * LANES), F32)], scratch_shapes=[],
        dims=("parallel", "parallel"), args=[q, k, v], rider=rider)


def _attn_bwd(q, k, v, do, lse, delta, *, H, G, dk, dv, scale, name, tq=256, rider=None):
    S = q.shape[0]
    tq = min(tq, S)
    Hkv = H // G
    c2 = scale * LOG2E

    def body(q_ref, k_ref, v_ref, do_ref, l_ref, d_ref, dq_ref, dk_ref, dv_ref):
        @pl.when((pl.program_id(1) == 0) & (pl.program_id(2) == 0))
        def _():
            dk_ref[...] = jnp.zeros_like(dk_ref)
            dv_ref[...] = jnp.zeros_like(dv_ref)

        qv, kv, dov = q_ref[...], k_ref[...], do_ref[...]
        p = jnp.exp2(_dot(qv, kv, NT) * c2 - l_ref[:, 0:1] * LOG2E)
        dp = _dot(dov, v_ref[...], NT)
        ds = (p * (dp - d_ref[:, 0:1]) * scale).astype(CDT)
        dq_ref[...] = _dot(ds, kv, NN)
        dk_ref[...] += _dot(ds, qv, TN)
        dv_ref[...] += _dot(p.astype(CDT), dov, TN)

    qi = lambda hk, g, i: (i, hk * G + g)
    return _carried_call(
        body, name=name, grid=(Hkv, G, S // tq),
        in_specs=[pl.BlockSpec((tq, dk), qi), pl.BlockSpec((S, dk), lambda hk, g, i: (0, hk)),
                  pl.BlockSpec((S, dv), lambda hk, g, i: (0, hk)), pl.BlockSpec((tq, dv), qi),
                  pl.BlockSpec((tq, LANES), qi), pl.BlockSpec((tq, LANES), qi)],
        out_specs=[pl.BlockSpec((tq, dk), qi), pl.BlockSpec((S, dk), lambda hk, g, i: (0, hk)),
                   pl.BlockSpec((S, dv), lambda hk, g, i: (0, hk))],
        out_shape=[_sds((S, H * dk), F32), _sds((S, Hkv * dk), F32), _sds((S, Hkv * dv), F32)], scratch_shapes=[],
        dims=("parallel", "arbitrary", "arbitrary"), args=[q, k, v, do, lse, delta], rider=rider)


BAND_SUB = 128
BAND_WIN = 384
BAND_UNROLL = 4


def _band_blocks(S, d):
    L = S // d
    assert L % BAND_SUB == 0 and S % (BAND_SUB * BAND_UNROLL) == 0
    return L, L // BAND_SUB, min(BAND_WIN, L)


def _band_index(blk, d, nb, L, win):
    r, jb = blk // nb, blk % nb
    l0 = jb * BAND_SUB
    w0 = jnp.clip(l0 - BAND_SUB, 0, L - win)
    return r + d * l0, r + d * w0, l0, w0


def _band_rows(start, size, d):
    return pl.ds(pl.multiple_of(start, BAND_SUB), size) if d == 1 else pl.ds(start, size, stride=d)


def _band_mask(l0, w0, win):
    rpos = l0 + lax.broadcasted_iota(jnp.int32, (BAND_SUB, win), 0)
    cpos = w0 + lax.broadcasted_iota(jnp.int32, (BAND_SUB, win), 1)
    return jnp.abs(rpos - cpos) <= BAND_HALF


def _mixc_fwd(q, k, v, rider=None):
    S, W = q.shape

    def body(q_ref, k_ref, v_ref, o_ref, l_ref, *scratch):
        ob_refs, lb_refs = scratch[0:3], scratch[3:6]
        for b, d in enumerate(DILATIONS):
            L, nb, win = _band_blocks(S, d)

            def step(it, carry, b=b, d=d, L=L, nb=nb, win=win):
                for u in range(BAND_UNROLL):
                    qs, ks, l0, w0 = _band_index(it * BAND_UNROLL + u, d, nb, L, win)
                    qrows, krows = _band_rows(qs, BAND_SUB, d), _band_rows(ks, win, d)
                    kw, vw = k_ref[krows, :].astype(CDT), v_ref[krows, :].astype(CDT)
                    s = jnp.where(_band_mask(l0, w0, win), _dot(q_ref[qrows, :].astype(CDT), kw, NT) * SCALE_H, NEG_INF)
                    m = jnp.max(s, axis=-1, keepdims=True)
                    e = jnp.exp(s - m)
                    den = jnp.sum(e, axis=-1, keepdims=True)
                    ob_refs[b][qrows, :] = _dot((e * (1.0 / den)).astype(CDT), vw, NN)
                    lb_refs[b][qrows, :] = jnp.broadcast_to(m + jnp.log(den), (BAND_SUB, LANES))
                return carry

            lax.fori_loop(0, S // (BAND_SUB * BAND_UNROLL), step, 0)

        def combine(c, carry):
            rows = pl.ds(pl.multiple_of(c * 256, 256), 256)
            l0, l1, l2 = lb_refs[0][rows, :], lb_refs[1][rows, :], lb_refs[2][rows, :]
            m = jnp.maximum(jnp.maximum(l0, l1), l2)
            e0, e1, e2 = jnp.exp(l0 - m), jnp.exp(l1 - m), jnp.exp(l2 - m)
            den = e0 + e1 + e2
            inv = 1.0 / den
            o_ref[rows, :] = ((e0 * inv) * ob_refs[0][rows, :] + (e1 * inv) * ob_refs[1][rows, :]
                              + (e2 * inv) * ob_refs[2][rows, :])
            l_ref[rows, :] = m + jnp.log(den)
            return carry

        lax.fori_loop(0, S // 256, combine, 0)

    head = pl.BlockSpec((S, LANES), lambda h: (0, h))
    return _carried_call(
        body, name="mixc_fwd", grid=(W // LANES,), in_specs=[head] * 3, out_specs=[head] * 2,
        out_shape=[_sds((S, W), F32)] * 2, scratch_shapes=[pltpu.VMEM((S, LANES), F32)] * 6,
        dims=("parallel",), args=[q, k, v], rider=rider)


def _mixc_bwd(q, k, v, do, lse, dd, rider=None):
    S, W = q.shape

    def body(q_ref, k_ref, v_ref, do_ref, l_ref, d_ref, dq_ref, dk_ref, dv_ref):
        dq_ref[...] = jnp.zeros_like(dq_ref)
        dk_ref[...] = jnp.zeros_like(dk_ref)
        dv_ref[...] = jnp.zeros_like(dv_ref)
        for d in DILATIONS:
            L, nb, win = _band_blocks(S, d)

            def step(it, carry, d=d, L=L, nb=nb, win=win):
                for u in range(BAND_UNROLL):
                    qs, ks, l0, w0 = _band_index(it * BAND_UNROLL + u, d, nb, L, win)
                    qrows, krows = _band_rows(qs, BAND_SUB, d), _band_rows(ks, win, d)
                    qv, dov = q_ref[qrows, :].astype(CDT), do_ref[qrows, :].astype(CDT)
                    kw, vw = k_ref[krows, :].astype(CDT), v_ref[krows, :].astype(CDT)
                    s = jnp.where(_band_mask(l0, w0, win), _dot(qv, kw, NT) * SCALE_H, NEG_INF)
                    p = jnp.exp(s - l_ref[qrows, :][:, 0:1])
                    dp = _dot(dov, vw, NT)
                    ds = (p * (dp - d_ref[qrows, :][:, 0:1]) * SCALE_H).astype(CDT)
                    dq_ref[qrows, :] += _dot(ds, kw, NN)
                    dk_ref[krows, :] += _dot(ds, qv, TN)
                    dv_ref[krows, :] += _dot(p.astype(CDT), dov, TN)
                return carry

            lax.fori_loop(0, S // (BAND_SUB * BAND_UNROLL), step, 0)

    head = pl.BlockSpec((S, LANES), lambda h: (0, h))
    return _carried_call(
        body, name="mixc_bwd", grid=(W // LANES,), in_specs=[head] * 6, out_specs=[head] * 3,
        out_shape=[_sds((S, W), F32)] * 3, scratch_shapes=[], dims=("parallel",), args=[q, k, v, do, lse, dd],
        rider=rider)


def _outnorm_fwd(oa, ob, oc, g, *, tr=256):
    S = oa.shape[0]
    tr = min(tr, S)

    def body(a_ref, b_ref, c_ref, g_ref, m_ref):
        m_ref[:, 0:512] = (_rms_val(a_ref[...], None, 512)[0] * g_ref[:, 0:512]).astype(CDT)
        m_ref[:, 512:1280] = (_rms_val(b_ref[...], None, 768)[0] * g_ref[:, 512:1280]).astype(CDT)
        m_ref[:, 1280:2048] = (_rms_val(c_ref[...], None, 768)[0] * g_ref[:, 1280:2048]).astype(CDT)

    row = lambda w: pl.BlockSpec((tr, w), lambda i: (i, 0))
    return pl.pallas_call(
        body, name="outnorm_fwd", grid=(S // tr,),
        in_specs=[row(512), row(768), row(768), pl.BlockSpec((1, 2048), lambda i: (0, 0))],
        out_specs=row(2048), out_shape=_sds((S, 2048), CDT), compiler_params=_cparams(("parallel",)),
    )(oa, ob, oc, g)


def _outnorm_bwd(oa, ob, oc, g, dm, *, tr=256):
    S = oa.shape[0]
    tr = min(tr, S)

    def body(a_ref, b_ref, c_ref, g_ref, dm_ref, doa_ref, dob_ref, doc_ref, da_ref, db_ref, dc_ref, dg_ref):
        @pl.when(pl.program_id(0) == 0)
        def _():
            dg_ref[...] = jnp.zeros_like(dg_ref)

        for o_ref, do_ref, d_ref, lo, w in ((a_ref, doa_ref, da_ref, 0, 512), (b_ref, dob_ref, db_ref, 512, 768),
                                            (c_ref, doc_ref, dc_ref, 1280, 768)):
            o = o_ref[...]
            dmv = dm_ref[:, lo:lo + w]
            do, _ = _rms_bwd_val(o, None, dmv * g_ref[:, lo:lo + w], w)
            r = lax.rsqrt(jnp.sum(o * o, axis=-1, keepdims=True) * (1.0 / w) + EPS)
            dg_ref[:, lo:lo + w] += _colsum(dmv * (o * r))
            do_ref[...] = do.astype(do_ref.dtype)
            for h in range(w // LANES):
                sl = slice(LANES * h, LANES * (h + 1))
                d_ref[:, sl] = jnp.broadcast_to(jnp.sum(do[:, sl] * o[:, sl], axis=-1, keepdims=True), (tr, LANES))

    row = lambda w: pl.BlockSpec((tr, w), lambda i: (i, 0))
    vec = pl.BlockSpec((1, 2048), lambda i: (0, 0))
    return pl.pallas_call(
        body, name="outnorm_bwd", grid=(S // tr,),
        in_specs=[row(512), row(768), row(768), vec, row(2048)],
        out_specs=[row(512), row(768), row(768), row(512), row(768), row(768), vec],
        out_shape=[_sds((S, 512), CDT), _sds((S, 768), CDT), _sds((S, 768), F32), _sds((S, 512), F32),
                   _sds((S, 768), F32), _sds((S, 768), F32), _sds((1, 2048), F32)],
        compiler_params=_cparams(("arbitrary",)),
    )(oa, ob, oc, g, dm)


def _row_tile(r, c, itemsize, limit=1 << 20):
    best = 16
    for t in range(16, r + 1, 16):
        if r % t == 0 and t * c * itemsize <= limit:
            best = t
    return best


def _rs_chip_sum(g, got, *, name):
    _, _, r, c = g.shape
    tr = _row_tile(r, c, 2)
    core = lax.axis_index("c").astype(jnp.int32).reshape(1)

    def body(c_ref, a_ref, b_ref, o_ref):
        o_ref[...] = (a_ref[...].astype(F32) + b_ref[...].astype(F32)).astype(o_ref.dtype)

    spec = pltpu.PrefetchScalarGridSpec(
        num_scalar_prefetch=1, grid=(4, r // tr),
        in_specs=[pl.BlockSpec((None, None, tr, c), lambda k, i, cr: (k, cr[0], i, 0)),
                  pl.BlockSpec((None, tr, c), lambda k, i, cr: (k, i, 0))],
        out_specs=pl.BlockSpec((None, tr, c), lambda k, i, cr: (k, i, 0)))
    return pl.pallas_call(body, name=name, grid_spec=spec, out_shape=_sds((4, r, c), g.dtype),
                          compiler_params=_cparams(("parallel", "parallel")))(core, g, got)


def _rs_final_sum(r4, *, name):
    _, r, c = r4.shape
    tr = _row_tile(r, c, 4)

    def body(r_ref, o_ref):
        o_ref[...] = ((r_ref[0].astype(F32) + r_ref[1].astype(F32)) + r_ref[2].astype(F32)) + r_ref[3].astype(F32)

    return pl.pallas_call(
        body, name=name, grid=(r // tr,), in_specs=[pl.BlockSpec((4, tr, c), lambda i: (0, i, 0))],
        out_specs=pl.BlockSpec((tr, c), lambda i: (i, 0)), out_shape=_sds((r, c), F32),
        compiler_params=_cparams(("parallel",)))(r4)


def _all_reduce_small(v):
    R = v.shape[0]

    def body(v_ref, out_ref, buf_ref, send_sems, recv_sems):
        x, y, c = lax.axis_index("x"), lax.axis_index("y"), lax.axis_index("c")
        me = 4 * x + 2 * y + c
        buf_ref[me] = v_ref[...]
        peers = []
        for r in range(1, 8):
            px, py, pc = x ^ (r >> 2), y ^ ((r >> 1) & 1), c ^ (r & 1)
            peers.append((r, (px, py, pc), 4 * px + 2 * py + pc))
        sends = [pltpu.make_async_remote_copy(
            src_ref=v_ref, dst_ref=buf_ref.at[me], send_sem=send_sems.at[r - 1], recv_sem=recv_sems.at[r - 1],
            device_id=dev, device_id_type=MESH) for r, dev, _ in peers]
        for cp in sends:
            cp.start()
        for r, dev, idx in peers:
            pltpu.make_async_remote_copy(
                src_ref=v_ref, dst_ref=buf_ref.at[idx], send_sem=send_sems.at[r - 1], recv_sem=recv_sems.at[r - 1],
                device_id=dev, device_id_type=MESH).wait_recv()
        for cp in sends:
            cp.wait_send()
        acc = buf_ref[0]
        for k in range(1, 8):
            acc = acc + buf_ref[k]
        out_ref[...] = acc

    vm = pl.BlockSpec(memory_space=pltpu.VMEM)
    return pl.pallas_call(
        body, name="all_reduce_small", out_shape=_sds((R, LANES), F32), in_specs=[vm], out_specs=vm,
        scratch_shapes=[pltpu.VMEM((8, R, LANES), F32), pltpu.SemaphoreType.DMA((7,)), pltpu.SemaphoreType.DMA((7,))],
    )(v)


def _adamw(w, g, m, v, *, name):
    R, C = w.shape
    tr = R
    for cand in (1024, 512, 256, 128, 64, 32, 16, 8):
        if R % cand == 0 and cand * C * 4 <= 2 * 1024 * 1024:
            tr = cand
            break

    def body(w_ref, g_ref, m_ref, v_ref, d_ref, nm_ref, nv_ref):
        gv = g_ref[...]
        mn = ADAM_B1 * m_ref[...] + (1.0 - ADAM_B1) * gv
        vn = ADAM_B2 * v_ref[...] + (1.0 - ADAM_B2) * (gv * gv)
        m_hat = mn / (1.0 - ADAM_B1 ** ADAM_STEP)
        v_hat = vn / (1.0 - ADAM_B2 ** ADAM_STEP)
        d_ref[...] = -ADAM_LR * (m_hat / (jnp.sqrt(v_hat) + ADAM_EPS) + ADAM_WD * w_ref[...])
        nm_ref[...] = mn
        nv_ref[...] = vn

    blk = pl.BlockSpec((tr, C), lambda i: (i, 0))
    return pl.pallas_call(
        body, name=name, grid=(R // tr,), in_specs=[blk] * 4, out_specs=[blk] * 3,
        out_shape=[_sds((R, C), F32)] * 3, compiler_params=_cparams(("parallel",)))(w, g, m, v)


def _wuq_pad(w):
    w = w.reshape(448, 4, 192)
    z = jnp.zeros((448, 4, 64), w.dtype)
    return jnp.concatenate([w[:, :, 0:128], z, w[:, :, 128:192]], axis=2).reshape(448, 1024)


def _wuq_unpad(w):
    w = w.reshape(448, 4, 256)
    return jnp.concatenate([w[:, :, 0:128], w[:, :, 192:256]], axis=2).reshape(448, 768)


def _wukv_perm(w):
    return w.reshape(512, 4, 2, 128).transpose(0, 2, 1, 3).reshape(512, 1024)


def _wukv_unperm(w):
    return w.reshape(512, 2, 4, 128).transpose(0, 2, 1, 3).reshape(512, 1024)


def _lat_weight(w_uq, w_ukv):
    z = lambda r, c: jnp.zeros((r, c), w_uq.dtype)
    top = jnp.concatenate([_wuq_pad(w_uq), z(448, 1024)], axis=1)
    mid = jnp.concatenate([z(512, 1024), _wukv_perm(w_ukv)], axis=1)
    return jnp.concatenate([top, mid, z(64, 2048)], axis=0)


def _lat_weight_grads(dw):
    return _wuq_unpad(dw[0:KV_LO, 0:1024]), _wukv_unperm(dw[KV_LO:KV_HI, 1024:2048])


def _comm_shards(w_in, w_uq, w_ukv, w_out, w_ff1, w_ff2):
    lat = jnp.concatenate([w_uq.reshape(UQ_ROWS, LANES), w_ukv.reshape(512, LANES)], axis=0)
    return [w_in.T, w_ff1.T, w_out, w_ff2, lat]


def _from_comm_shards(parts):
    w_in_t, w_ff1_t, w_out, w_ff2, lat = parts
    return {"w_in": w_in_t.T, "w_ff1": w_ff1_t.T, "w_out": w_out, "w_ff2": w_ff2,
            "w_uq": lat[0:UQ_ROWS].reshape(448, 96), "w_ukv": lat[UQ_ROWS:].reshape(512, 128)}


def _early_weights(g_in_t, g_lat):
    w_uq = g_lat[:, 0:UQ_ROWS].reshape(8, 448, 96).transpose(1, 0, 2).reshape(448, 768)
    w_ukv = g_lat[:, UQ_ROWS:].reshape(8, 512, 128).transpose(1, 0, 2).reshape(512, 1024)
    return g_in_t.reshape(PROJ_W, D_MODEL), _lat_weight(w_uq, w_ukv)


def _layer_fwd(x, W, G, tabs, plan=None):
    W = dict(W)
    shards, nxt = plan if plan is not None else (None, None)
    rider = lambda make: None if plan is None else make()
    s = {"x0": x}
    s["h1"] = _rms_fwd(x, G["ln1_g"], name="rms1_fwd")
    s["proj"] = _matmul(s["h1"], W["w_in_t"], mode="nt", tm=1024, tn=768, tk=2048, out_dtype=F32, name="mm_in")
    s["lat"], kpe, s["qb"], s["kb"], s["vb"], s["qc"], s["kc"], s["vc"] = _prep_fwd(
        s["proj"], G["glat"], G["gqn"], G["gkn"], tabs)
    qkva = _matmul(s["lat"], W["w_lat"], mode="nn", tm=1024, tn=1024, tk=1024, out_dtype=F32, name="mm_lat")
    s["qa"], s["ka"], s["va"] = _prep_a2_fwd(qkva, kpe, tabs[0])
    (s["oa"], s["lse_a"]), got_a = _attn_fwd(
        s["qa"], s["ka"], s["va"], H=4, G=1, dk=256, dv=128, scale=SCALE_A, name="attn_a_fwd",
        rider=rider(lambda: _ag_first_rider([shards["w_ff1_t"]], "_ff1")))
    (s["ob"], s["lse_b"]), got_b = _attn_fwd(
        s["qb"], s["kb"], s["vb"], H=6, G=3, dk=128, dv=128, scale=SCALE_H, name="attn_b_fwd",
        rider=rider(lambda: _ag_first_rider([shards["w_ff2"], shards["w_out"]], "_ff2_out")))
    (s["oc"], s["lse_c"]), got_c = _mixc_fwd(
        s["qc"], s["kc"], s["vc"],
        rider=rider(lambda: _join(_ag_second_rider(got_a + got_b, "_ff_out"),
                                  None if nxt is None else _ag_first_rider(nxt, "_early"))))
    if plan is not None:
        W["w_ff1_t"], W["w_ff2"] = got_c[0].reshape(D_FF, D_MODEL), got_c[1].reshape(D_FF, D_MODEL)
        W["w_out"] = got_c[2].reshape(D_MODEL, D_MODEL)
    s["mixed"] = _outnorm_fwd(s["oa"], s["ob"], s["oc"], G["g_out"])
    s["x1"] = _matmul(s["mixed"], W["w_out"], mode="nn", tm=1024, tn=1024, tk=2048, out_dtype=F32, name="mm_out",
                      epi="residual", extra=x)
    s["h2"] = _rms_fwd(s["x1"], G["ln2_g"], name="rms2_fwd")
    ff1 = functools.partial(_matmul, s["h2"], W["w_ff1_t"], mode="nt", tm=1024, tn=1024, tk=2048, out_dtype=CDT,
                            name="mm_ff1", epi="relu2")
    if nxt is None:
        (s["z"], s["u"]), early = ff1(), None
    else:
        (s["z"], s["u"]), early = ff1(rider=_ag_second_rider(got_c[3:5], "_early"))
    x2 = _matmul(s["u"], W["w_ff2"], mode="nn", tm=1024, tn=1024, tk=2048, out_dtype=F32, name="mm_ff2",
                 epi="residual", extra=s["x1"])
    return x2, s, W, early


def _by_destination(dw, name):
    return dw.reshape((4, 2) + COMM_SHAPE[name])


def _early_by_destination(dw_in_t, dw_lat):
    dw_uq, dw_ukv = _lat_weight_grads(dw_lat)
    lat = jnp.concatenate([dw_uq.reshape(448, 8, 96).transpose(1, 0, 2).reshape(8, UQ_ROWS, LANES),
                           dw_ukv.reshape(512, 8, 128).transpose(1, 0, 2)], axis=1)
    return [_by_destination(dw_in_t, "w_in_t"), _by_destination(lat, "lat")]


def _layer_bwd(dx2, dx2b, s, W, G, tabs, scatter=False, pending=None):
    dw, dg, landed = {}, {}, {}
    ff2_dx = functools.partial(_matmul, dx2b, W["w_ff2"], mode="nt", tm=1024, tn=1024, tk=2048, out_dtype=CDT,
                               name="mm_ff2_dx", epi="drelu2", extra=s["z"])
    if pending is None:
        dz = ff2_dx()
    else:
        dz, got = ff2_dx(rider=_rs_sibling_rider(pending, "_early"))
        chip = [_rs_chip_sum(g, r, name="rs_chip_sum_" + n) for g, r, n in zip(pending, got, ("w_in_t", "lat"))]
    dw["w_ff2"] = _matmul(s["u"], dx2b, mode="tn", tm=2048, tn=2048, tk=512, out_dtype=CDT, name="mm_ff2_dw")
    ff1_dx = functools.partial(_matmul, dz, W["w_ff1_t"], mode="nn", tm=1024, tn=1024, tk=2048, out_dtype=F32,
                               name="mm_ff1_dx")
    if pending is None:
        dh2 = ff1_dx()
    else:
        dh2, got = ff1_dx(rider=_rs_chip_rider(chip, "_early"))
        landed["above_w_in_t"], landed["above_lat"] = got
    dw["w_ff1_t"] = _matmul(dz, s["h2"], mode="tn", tm=2048, tn=2048, tk=512, out_dtype=CDT, name="mm_ff1_dw")
    dx1, dx1b, dg["ln2_g"] = _rms_bwd(s["x1"], G["ln2_g"], dh2, dx2, name="rms2_bwd")
    dmixed = _matmul(dx1b, W["w_out"], mode="nt", tm=1024, tn=1024, tk=2048, out_dtype=F32, name="mm_out_dx")
    out_dw = functools.partial(_matmul, s["mixed"], dx1b, mode="tn", tm=2048, tn=2048, tk=512, out_dtype=CDT,
                               name="mm_out_dw")
    if not scatter:
        dw["w_out"] = out_dw()
        riders = [None, None, None]
    else:
        g_ff = [_by_destination(dw["w_ff2"], "w_ff2"), _by_destination(dw["w_ff1_t"], "w_ff1_t")]
        dw["w_out"], got = out_dw(rider=_rs_sibling_rider(g_ff, "_ff"))
        chip_ff2 = _rs_chip_sum(g_ff[0], got[0], name="rs_chip_sum_w_ff2")
        chip_ff1 = _rs_chip_sum(g_ff[1], got[1], name="rs_chip_sum_w_ff1_t")
        g_out = [_by_destination(dw["w_out"], "w_out")]
        riders = [_rs_chip_rider([chip_ff2], "_ff2"),
                  _join(_rs_chip_rider([chip_ff1], "_ff1"), _rs_sibling_rider(g_out, "_out")), None]
    doa, dob, doc, dla, dlb, dlc, dg["g_out"] = _outnorm_bwd(s["oa"], s["ob"], s["oc"], G["g_out"], dmixed)
    (dqa, dka, dva), got = _attn_bwd(s["qa"], s["ka"], s["va"], doa, s["lse_a"], dla, H=4, G=1, dk=256, dv=128,
                                     scale=SCALE_A, name="attn_a_bwd", rider=riders[0])
    if scatter:
        landed["w_ff2"] = got[0]
    (dqb, dkb, dvb), got = _attn_bwd(s["qb"], s["kb"], s["vb"], dob, s["lse_b"], dlb, H=6, G=3, dk=128, dv=128,
                                     scale=SCALE_H, name="attn_b_bwd", rider=riders[1])
    if scatter:
        landed["w_ff1_t"] = got[0]
        riders[2] = _rs_chip_rider([_rs_chip_sum(g_out[0], got[1], name="rs_chip_sum_w_out")], "_out")
    (dqc, dkc, dvc), got = _mixc_bwd(s["qc"], s["kc"], s["vc"], doc, s["lse_c"], dlc, rider=riders[2])
    if scatter:
        landed["w_out"] = got[0]
    dqkva, dkr = _prep_a2_bwd(dqa, dka, dva, tabs[0])
    dlat = _matmul(dqkva, W["w_lat"], mode="nt", tm=1024, tn=1024, tk=2048, out_dtype=F32, name="mm_lat_dx")
    dw["w_lat"] = _matmul(s["lat"], dqkva, mode="tn", tm=1024, tn=2048, tk=512, out_dtype=CDT, name="mm_lat_dw")
    dproj, dg["glat"], dg["gqn"], dg["gkn"] = _prep_bwd(
        s["proj"], G["glat"], G["gqn"], G["gkn"], tabs, dlat, dkr, dqb, dkb, dvb, dqc, dkc, dvc)
    dh1 = _matmul(dproj, W["w_in_t"], mode="nn", tm=1024, tn=1024, tk=1536, out_dtype=F32, name="mm_in_dx")
    dw["w_in_t"] = _matmul(dproj, s["h1"], mode="tn", tm=1536, tn=2048, tk=512, out_dtype=CDT, name="mm_in_dw")
    dx0, dx0b, dg["ln1_g"] = _rms_bwd(s["x0"], G["ln1_g"], dh1, dx1, name="rms1_bwd")
    return dx0, dx0b, dw, dg, landed


def _layer_gains(l, ln1_g, g_q_a, g_kv_a, g_qn_b, g_kn_b, g_out, ln2_g):
    return {"ln1_g": ln1_g[l], "ln2_g": ln2_g[l], "g_out": g_out[l].reshape(1, 2048),
            "glat": jnp.concatenate([g_q_a[l], g_kv_a[l], jnp.zeros((LAT_W - KV_HI,), F32)]).reshape(1, LAT_W),
            "gqn": g_qn_b[l].reshape(1, 128), "gkn": g_kn_b[l].reshape(1, 128)}


def _gain_grads(dg):
    glat = dg["glat"].reshape(-1)
    return {"ln1_g": dg["ln1_g"].reshape(-1), "g_q_a": glat[0:KV_LO], "g_kv_a": glat[KV_LO:KV_HI],
            "g_qn_b": dg["gqn"].reshape(-1), "g_kn_b": dg["gkn"].reshape(-1), "g_out": dg["g_out"].reshape(-1),
            "ln2_g": dg["ln2_g"].reshape(-1)}


def _local_step(x, tgt, weights, gains, ln_f_g):
    S = x.shape[0]
    tabs = _rope_tables(S)
    depth = len(weights)
    saved = []
    for l in range(depth):
        x, s, _, _ = _layer_fwd(x, weights[l], gains[l], tabs)
        saved.append(s)
    loss, dx, dxb, dlnf = _loss_head(x, ln_f_g, tgt)
    dws, dgs = [None] * depth, [None] * depth
    for l in reversed(range(depth)):
        dx, dxb, dws[l], dgs[l], _ = _layer_bwd(dx, dxb, saved[l], weights[l], gains[l], tabs)
    return loss, dx, dws, dgs, dlnf


SMALL_SIZES = (("ln1_g", 2048), ("g_q_a", 448), ("g_kv_a", 512), ("g_qn_b", 128), ("g_kn_b", 128), ("g_out", 2048),
               ("ln2_g", 2048))


def _pack_small(per_layer, ln_f):
    flat = jnp.concatenate([per_layer[n].reshape(-1) for n, _ in SMALL_SIZES] + [ln_f.reshape(-1)])
    rows = -(-flat.shape[0] // (8 * LANES)) * 8
    return jnp.concatenate([flat, jnp.zeros((rows * LANES - flat.shape[0],), F32)]).reshape(rows, LANES)


def _unpack_small(packed, depth):
    flat, out, lo = packed.reshape(-1), {}, 0
    for n, w in SMALL_SIZES:
        out[n] = flat[lo:lo + depth * w].reshape(depth, w)
        lo += depth * w
    out["ln_f_g"] = flat[lo:lo + 2048]
    return out


def kernel(x, ln1_g, w_in, g_q_a, w_uq, g_kv_a, w_ukv, g_qn_b, g_kn_b, g_out, w_out, ln2_g, w_ff1, w_ff2, ln_f_g, loss_target, m_ln1_g, m_w_in, m_g_q_a, m_w_uq, m_g_kv_a, m_w_ukv, m_g_qn_b, m_g_kn_b, m_g_out, m_w_out, m_ln2_g, m_w_ff1, m_w_ff2, m_ln_f_g, v_ln1_g, v_w_in, v_g_q_a, v_w_uq, v_g_kv_a, v_w_ukv, v_g_qn_b, v_g_kn_b, v_g_out, v_w_out, v_ln2_g, v_w_ff1, v_w_ff2, v_ln_f_g):
    depth = w_in.shape[0]
    S = x.shape[1]
    big_w = {"w_in": w_in, "w_uq": w_uq, "w_ukv": w_ukv, "w_out": w_out, "w_ff1": w_ff1, "w_ff2": w_ff2}
    big_m = {"w_in": m_w_in, "w_uq": m_w_uq, "w_ukv": m_w_ukv, "w_out": m_w_out, "w_ff1": m_w_ff1, "w_ff2": m_w_ff2}
    big_v = {"w_in": v_w_in, "w_uq": v_w_uq, "w_ukv": v_w_ukv, "w_out": v_w_out, "w_ff1": v_w_ff1, "w_ff2": v_w_ff2}
    small_w = {"ln1_g": ln1_g, "g_q_a": g_q_a, "g_kv_a": g_kv_a, "g_qn_b": g_qn_b, "g_kn_b": g_kn_b, "g_out": g_out,
               "ln2_g": ln2_g}
    small_m = {"ln1_g": m_ln1_g, "g_q_a": m_g_q_a, "g_kv_a": m_g_kv_a, "g_qn_b": m_g_qn_b, "g_kn_b": m_g_kn_b,
               "g_out": m_g_out, "ln2_g": m_ln2_g}
    small_v = {"ln1_g": v_ln1_g, "g_q_a": v_g_q_a, "g_kv_a": v_g_kv_a, "g_qn_b": v_g_qn_b, "g_kn_b": v_g_kn_b,
               "g_out": v_g_out, "ln2_g": v_ln2_g}

    shards = [dict(zip(COMM, _comm_shards(*[big_w[n][l].astype(CDT) for n in BIG]))) for l in range(depth)]
    early_shards = [[sh["w_in_t"], sh["lat"]] for sh in shards]
    early = _run_rider(_ag_second_rider(_run_rider(_ag_first_rider(early_shards[0], "_early")), "_early"))
    gains = [_layer_gains(l, ln1_g, g_q_a, g_kv_a, g_qn_b, g_kn_b, g_out, ln2_g) for l in range(depth)]
    tabs = _rope_tables(S)

    h = x.reshape(S, D_MODEL)
    saved, weights = [], []
    for l in range(depth):
        W = dict(zip(("w_in_t", "w_lat"), _early_weights(*early)))
        h, s, W, early = _layer_fwd(h, W, gains[l], tabs,
                                    plan=(shards[l], early_shards[l + 1] if l + 1 < depth else None))
        saved.append(s)
        weights.append(W)
    loss_part, dx, dxb, dlnf = _loss_head(h, ln_f_g, loss_target.reshape(S, D_MODEL))
    loss = lax.psum(loss_part[0, 0], ("x", "y", "c"))

    dgs, landed, pending = [None] * depth, [None] * depth, None
    for l in reversed(range(depth)):
        dx, dxb, dw, dgs[l], landed[l] = _layer_bwd(dx, dxb, saved[l], weights[l], gains[l], tabs, scatter=True,
                                                    pending=pending)
        if pending is not None:
            landed[l + 1]["w_in_t"], landed[l + 1]["lat"] = landed[l].pop("above_w_in_t"), landed[l].pop("above_lat")
        pending = _early_by_destination(dw["w_in_t"], dw["w_lat"])
    got = _run_rider(_rs_sibling_rider(pending, "_early"))
    chip = [_rs_chip_sum(g, r, name="rs_chip_sum_" + n) for g, r, n in zip(pending, got, ("w_in_t", "lat"))]
    landed[0]["w_in_t"], landed[0]["lat"] = _run_rider(_rs_chip_rider(chip, "_early"))
    grad_x = dx.reshape(1, S, D_MODEL)

    shard_grads = [_from_comm_shards([_rs_final_sum(landed[l][n], name="rs_final_sum_" + n) for n in COMM])
                   for l in range(depth)]
    big_g = {n: jnp.stack([shard_grads[l][n] for l in range(depth)]) for n in BIG}

    named = [_gain_grads(dgs[l]) for l in range(depth)]
    per_layer = {n: jnp.stack([named[l][n] for l in range(depth)]) for n, _ in SMALL_SIZES}
    small_g = _unpack_small(_all_reduce_small(_pack_small(per_layer, dlnf.reshape(-1))), depth)

    upd = {}
    for n in BIG:
        shp = big_w[n].shape
        two_d = (shp[0] * shp[1], shp[2])
        d, nm, nv = _adamw(big_w[n].reshape(two_d), big_g[n].reshape(two_d), big_m[n].reshape(two_d),
                           big_v[n].reshape(two_d), name="adamw_" + n)
        upd[n] = (d.reshape(shp), nm.reshape(shp), nv.reshape(shp))
    small_w["ln_f_g"], small_m["ln_f_g"], small_v["ln_f_g"] = ln_f_g, m_ln_f_g, v_ln_f_g
    names_small = [n for n, _ in SMALL_SIZES]
    pw = _pack_small({n: small_w[n] for n in names_small}, small_w["ln_f_g"])
    pg = _pack_small({n: small_g[n] for n in names_small}, small_g["ln_f_g"])
    pm = _pack_small({n: small_m[n] for n in names_small}, small_m["ln_f_g"])
    pv = _pack_small({n: small_v[n] for n in names_small}, small_v["ln_f_g"])
    d, nm, nv = _adamw(pw, pg, pm, pv, name="adamw_small")
    sd, snm, snv = _unpack_small(d, depth), _unpack_small(nm, depth), _unpack_small(nv, depth)
    for n in names_small + ["ln_f_g"]:
        upd[n] = (sd[n], snm[n], snv[n])

    order = ["ln1_g", "w_in", "g_q_a", "w_uq", "g_kv_a", "w_ukv", "g_qn_b", "g_kn_b", "g_out", "w_out", "ln2_g", "w_ff1",
             "w_ff2", "ln_f_g"]
    grads = {**big_g, **small_g}
    return (loss, grad_x, *[grads[n] for n in order], *[upd[n][0] for n in order], *[upd[n][1] for n in order],
            *[upd[n][2] for n in order])
```

```python
import functools
import math

import jax
import jax.numpy as jnp
from jax import lax
from jax.experimental import pallas as pl
from jax.experimental.pallas import tpu as pltpu

D_MODEL = 2048
D_FF = 8192
EPS = 1e-6
NEG_INF = -1e30
Q_LORA = 448
ROPE_THETA = 10000.0
GRID_W = 64
DILATIONS = (1, 4, 16)
BAND_HALF = 64
SCALE_A = 1.0 / math.sqrt(192.0)
SCALE_H = 1.0 / math.sqrt(128.0)
ADAM_LR, ADAM_B1, ADAM_B2, ADAM_EPS, ADAM_WD, ADAM_STEP = 0.001, 0.9, 0.999, 1e-08, 0.01, 10

CDT = jnp.bfloat16
F32 = jnp.float32
LANES = 128
VMEM_LIMIT = 56 * 1024 * 1024

PROJ_W = 4608
LAT_W = 1024
KV_LO, KV_HI = 448, 960
OFF_BQ, OFF_BK, OFF_BV, OFF_CQ, OFF_CK, OFF_CV = 1024, 1792, 2048, 2304, 3072, 3840

NN = ((1,), (0,))
NT = ((1,), (1,))
TN = ((0,), (0,))

BIG = ("w_in", "w_uq", "w_ukv", "w_out", "w_ff1", "w_ff2")
COMM = ("w_in_t", "w_ff1_t", "w_out", "w_ff2", "lat")
COMM_SHAPE = {"w_in_t": (576, 2048), "w_ff1_t": (1024, 2048), "w_out": (256, 2048), "w_ff2": (1024, 2048),
              "lat": (848, 128)}
UQ_ROWS = 448 * 96 // LANES


def _dot(a, b, dims):
    return lax.dot_general(a, b, (dims, ((), ())), preferred_element_type=F32)


def _cparams(dims=None):
    return pltpu.CompilerParams(dimension_semantics=dims, vmem_limit_bytes=VMEM_LIMIT)


def _sds(shape, dtype):
    return jax.ShapeDtypeStruct(shape, dtype)


MESH = pl.DeviceIdType.MESH
ANY = pl.BlockSpec(memory_space=pl.ANY)


class _Rider:
    def __init__(self, name, arrays, out_shape, scratch, aliases, start, finish):
        self.name, self.arrays, self.out_shape, self.scratch = name, list(arrays), list(out_shape), list(scratch)
        self.aliases, self.start, self.finish = dict(aliases), start, finish


def _join(a, b):
    if a is None or b is None:
        return a if b is None else b
    na, oa, sa = len(a.arrays), len(a.out_shape), len(a.scratch)
    aliases = dict(a.aliases)
    aliases.update({na + i: oa + o for i, o in b.aliases.items()})

    def start(ins, outs, sems):
        a.start(ins[:na], outs[:oa], sems[:sa])
        b.start(ins[na:], outs[oa:], sems[sa:])

    def finish(ins, outs, sems):
        a.finish(ins[:na], outs[:oa], sems[:sa])
        b.finish(ins[na:], outs[oa:], sems[sa:])

    return _Rider(a.name + "_" + b.name, a.arrays + b.arrays, a.out_shape + b.out_shape, a.scratch + b.scratch,
                  aliases, start, finish)


def _carried_call(body, *, name, grid, in_specs, out_specs, out_shape, scratch_shapes, dims, args, rider):
    in_specs, out_specs, out_shape = list(in_specs), list(out_specs), list(out_shape)
    scratch_shapes = list(scratch_shapes)
    if rider is None:
        res = pl.pallas_call(body, name=name, grid=grid, in_specs=in_specs, out_specs=out_specs, out_shape=out_shape,
                             scratch_shapes=scratch_shapes, compiler_params=_cparams(dims))(*args)
        return list(res), []
    n_in, n_out, n_scr = len(in_specs), len(out_specs), len(scratch_shapes)
    r_in, r_out = len(rider.arrays), len(rider.out_shape)

    def wrapped(*refs):
        o0 = n_in + r_in
        s0 = o0 + n_out + r_out
        ins, outs, sems = refs[n_in:o0], refs[o0 + n_out:s0], refs[s0 + n_scr:]
        ids = [pl.program_id(a) for a in range(len(grid))]
        first = functools.reduce(jnp.logical_and, [i == 0 for i in ids])
        last = functools.reduce(jnp.logical_and, [i == g - 1 for i, g in zip(ids, grid)])

        @pl.when(first)
        def _():
            rider.start(ins, outs, sems)

        body(*refs[:n_in], *refs[o0:o0 + n_out], *refs[s0:s0 + n_scr])

        @pl.when(last)
        def _():
            rider.finish(ins, outs, sems)

    res = pl.pallas_call(
        wrapped, name=name + "_" + rider.name, grid=grid, in_specs=in_specs + [ANY] * r_in,
        out_specs=out_specs + [ANY] * r_out, out_shape=out_shape + rider.out_shape,
        scratch_shapes=scratch_shapes + rider.scratch,
        input_output_aliases={n_in + i: n_out + o for i, o in rider.aliases.items()},
        compiler_params=_cparams(("arbitrary",) * len(grid)),
    )(*args, *rider.arrays)
    return list(res[:n_out]), list(res[n_out:])


def _run_rider(rider):
    def body(*refs):
        r_in, r_out = len(rider.arrays), len(rider.out_shape)
        ins, outs, sems = refs[:r_in], refs[r_in:r_in + r_out], refs[r_in + r_out:]
        rider.start(ins, outs, sems)
        rider.finish(ins, outs, sems)

    res = pl.pallas_call(
        body, name=rider.name, in_specs=[ANY] * len(rider.arrays), out_specs=[ANY] * len(rider.out_shape),
        out_shape=rider.out_shape, scratch_shapes=rider.scratch, input_output_aliases=rider.aliases,
    )(*rider.arrays)
    return list(res)


def _mesh_place():
    x, y, c = lax.axis_index("x"), lax.axis_index("y"), lax.axis_index("c")
    return x, y, c, [(1 - x, y), (x, 1 - y), (1 - x, 1 - y)]


def _remote(src, dst, send, recv, dev):
    return pltpu.make_async_remote_copy(src_ref=src, dst_ref=dst, send_sem=send, recv_sem=recv, device_id=dev,
                                        device_id_type=MESH)


def _ag_first_rider(shards, tag):
    n = len(shards)

    def copies(ins, outs, sems):
        send, recv, _ = sems
        x, y, c, chips = _mesh_place()
        me = 4 * x + 2 * y + c
        peers = [(x, y, 1 - c)] + [(cx, cy, c) for cx, cy in chips]
        out = []
        for t in range(n):
            for k, dev in enumerate(peers):
                theirs = 4 * dev[0] + 2 * dev[1] + dev[2]
                out.append((_remote(ins[t], outs[t].at[me], send.at[t, k], recv.at[t, k], dev),
                            _remote(ins[t], outs[t].at[theirs], send.at[t, k], recv.at[t, k], dev)))
        mine = [pltpu.make_async_copy(ins[t], outs[t].at[me], sems[2].at[t]) for t in range(n)]
        return out, mine

    def start(ins, outs, sems):
        pairs, mine = copies(ins, outs, sems)
        for cp in mine:
            cp.start()
        for snd, _ in pairs:
            snd.start()

    def finish(ins, outs, sems):
        pairs, mine = copies(ins, outs, sems)
        for _, rcv in pairs:
            rcv.wait_recv()
        for snd, _ in pairs:
            snd.wait_send()
        for cp in mine:
            cp.wait()

    return _Rider("ag1" + tag, shards, [_sds((8,) + s.shape, s.dtype) for s in shards],
                  [pltpu.SemaphoreType.DMA((n, 4)), pltpu.SemaphoreType.DMA((n, 4)), pltpu.SemaphoreType.DMA((n,))],
                  {}, start, finish)


def _ag_second_rider(gathered, tag):
    n = len(gathered)

    def copies(ins, outs, sems):
        send, recv = sems
        x, y, c, chips = _mesh_place()
        out = []
        for t in range(n):
            for j, (cx, cy) in enumerate(chips):
                here, there = 4 * cx + 2 * cy + c, 4 * cx + 2 * cy + (1 - c)
                out.append((_remote(ins[t].at[here], outs[t].at[here], send.at[t, j], recv.at[t, j], (x, y, 1 - c)),
                            _remote(ins[t].at[here], outs[t].at[there], send.at[t, j], recv.at[t, j], (x, y, 1 - c))))
        return out

    def start(ins, outs, sems):
        for snd, _ in copies(ins, outs, sems):
            snd.start()

    def finish(ins, outs, sems):
        pairs = copies(ins, outs, sems)
        for _, rcv in pairs:
            rcv.wait_recv()
        for snd, _ in pairs:
            snd.wait_send()

    return _Rider("ag2" + tag, gathered, [_sds(g.shape, g.dtype) for g in gathered],
                  [pltpu.SemaphoreType.DMA((n, 3)), pltpu.SemaphoreType.DMA((n, 3))],
                  {t: t for t in range(n)}, start, finish)


def _rs_sibling_rider(gs, tag):
    n = len(gs)

    def copies(ins, outs, sems):
        send, recv = sems
        x, y, c, _ = _mesh_place()
        return [_remote(ins[t].at[k, 1 - c], outs[t].at[k], send.at[t, k], recv.at[t, k], (x, y, 1 - c))
                for t in range(n) for k in range(4)]

    def start(ins, outs, sems):
        for cp in copies(ins, outs, sems):
            cp.start()

    def finish(ins, outs, sems):
        for cp in copies(ins, outs, sems):
            cp.wait()

    return _Rider("rs1" + tag, gs, [_sds((4,) + g.shape[2:], g.dtype) for g in gs],
                  [pltpu.SemaphoreType.DMA((n, 4)), pltpu.SemaphoreType.DMA((n, 4))], {}, start, finish)


def _rs_chip_rider(ps, tag):
    n = len(ps)

    def copies(ins, outs, sems):
        send, recv, local = sems
        x, y, c, chips = _mesh_place()
        my_chip = 2 * x + y
        out = []
        for t in range(n):
            for j, (cx, cy) in enumerate(chips):
                dev = (cx, cy, c)
                out.append((_remote(ins[t].at[2 * cx + cy], outs[t].at[my_chip], send.at[t, j], recv.at[t, j], dev),
                            _remote(ins[t].at[my_chip], outs[t].at[2 * cx + cy], send.at[t, j], recv.at[t, j], dev)))
        mine = [pltpu.make_async_copy(ins[t].at[my_chip], outs[t].at[my_chip], local.at[t]) for t in range(n)]
        return out, mine

    def start(ins, outs, sems):
        pairs, mine = copies(ins, outs, sems)
        for cp in mine:
            cp.start()
        for snd, _ in pairs:
            snd.start()

    def finish(ins, outs, sems):
        pairs, mine = copies(ins, outs, sems)
        for _, rcv in pairs:
            rcv.wait_recv()
        for snd, _ in pairs:
            snd.wait_send()
        for cp in mine:
            cp.wait()

    return _Rider("rs2" + tag, ps, [_sds(p.shape, p.dtype) for p in ps],
                  [pltpu.SemaphoreType.DMA((n, 3)), pltpu.SemaphoreType.DMA((n, 3)), pltpu.SemaphoreType.DMA((n,))],
                  {}, start, finish)


def _matmul(a, b, *, mode, tm, tn, tk, out_dtype, name, epi=None, extra=None, rider=None):
    if mode == "nn":
        (M, K), (K2, N) = a.shape, b.shape
    elif mode == "nt":
        (M, K), (N, K2) = a.shape, b.shape
    else:
        (K, M), (K2, N) = a.shape, b.shape
    tm, tn, tk = min(tm, M), min(tn, N), min(tk, K)
    assert K == K2 and M % tm == 0 and N % tn == 0 and K % tk == 0, (name, a.shape, b.shape)
    nk = K // tk
    dims = {"nn": NN, "nt": NT, "tn": TN}[mode]
    if mode == "tn":
        a_spec = pl.BlockSpec((tk, tm), lambda i, j, k: (k, i))
    else:
        a_spec = pl.BlockSpec((tm, tk), lambda i, j, k: (i, k))
    if mode == "nt":
        b_spec = pl.BlockSpec((tn, tk), lambda i, j, k: (j, k))
    else:
        b_spec = pl.BlockSpec((tk, tn), lambda i, j, k: (k, j))
    tile = pl.BlockSpec((tm, tn), lambda i, j, k: (i, j))
    n_extra = 1 if epi in ("residual", "drelu2") else 0
    n_out = 2 if epi == "relu2" else 1

    def body(*refs):
        a_ref, b_ref = refs[0], refs[1]
        extra_refs = refs[2:2 + n_extra]
        out_refs = refs[2 + n_extra:2 + n_extra + n_out]

        def finish(acc):
            if epi is None:
                out_refs[0][...] = acc.astype(out_dtype)
            elif epi == "residual":
                out_refs[0][...] = (extra_refs[0][...] + acc).astype(out_dtype)
            elif epi == "relu2":
                out_refs[0][...] = acc.astype(out_dtype)
                r = jnp.maximum(acc, 0.0)
                out_refs[1][...] = (r * r).astype(out_dtype)
            else:
                z = extra_refs[0][...].astype(F32)
                out_refs[0][...] = (acc * (2.0 * jnp.maximum(z, 0.0))).astype(out_dtype)

        part = _dot(a_ref[...], b_ref[...], dims)
        if nk == 1:
            finish(part)
        else:
            acc_ref = refs[-1]
            k = pl.program_id(2)

            @pl.when(k == 0)
            def _():
                acc_ref[...] = part

            @pl.when(k > 0)
            def _():
                acc_ref[...] += part

            @pl.when(k == nk - 1)
            def _():
                finish(acc_ref[...])

    res, carried = _carried_call(
        body, name=name, grid=(M // tm, N // tn, nk), in_specs=[a_spec, b_spec] + [tile] * n_extra,
        out_specs=[tile] * n_out, out_shape=[_sds((M, N), out_dtype)] * n_out,
        scratch_shapes=[pltpu.VMEM((tm, tn), F32)] if nk > 1 else [],
        dims=("parallel", "parallel", "arbitrary"), args=[a, b] + ([extra] if n_extra else []), rider=rider)
    res = res if n_out > 1 else res[0]
    return res if rider is None else (res, carried)


def _rms_val(x, g, n):
    r = lax.rsqrt(jnp.sum(x * x, axis=-1, keepdims=True) * (1.0 / n) + EPS)
    y = x * r
    return (y if g is None else y * g), r


def _rms_bwd_val(x, g, dy, n):
    r = lax.rsqrt(jnp.sum(x * x, axis=-1, keepdims=True) * (1.0 / n) + EPS)
    xhat = x * r
    dyg = dy if g is None else dy * g
    dx = r * (dyg - xhat * (jnp.sum(dyg * xhat, axis=-1, keepdims=True) * (1.0 / n)))
    return dx, dy * xhat


def _rope_val(x, c, sa, sb, shift):
    return x * c + pltpu.roll(x, LANES - shift, 1) * sa + pltpu.roll(x, shift, 1) * sb


def _rope_t_val(dy, c, sa, sb, shift):
    return dy * c + pltpu.roll(dy * sa, shift, 1) + pltpu.roll(dy * sb, LANES - shift, 1)


def _colsum(x):
    return jnp.sum(x, axis=0, keepdims=True)


def _rope_tables(S):
    pos = lax.broadcasted_iota(jnp.int32, (S, LANES), 0)
    lane = lax.broadcasted_iota(jnp.int32, (S, LANES), 1)

    def tables(p, dim, active):
        half = dim // 2
        inv = jnp.power(ROPE_THETA, -(2 * (lane % half)).astype(F32) / dim)
        a = p.astype(F32) * inv
        first = (lane % dim) < half
        zero = jnp.zeros((S, LANES), F32)
        return (jnp.where(active, jnp.cos(a), zero), jnp.where(active & first, -jnp.sin(a), zero),
                jnp.where(active & ~first, jnp.sin(a), zero))

    tab_a = tables(pos, 64, lane >= 64)
    tab_b = tables(jnp.where(lane < 64, pos // GRID_W, pos % GRID_W), 64, lane >= 0)
    tab_c = tables(pos, 128, lane >= 0)
    return tab_a, tab_b, tab_c


ROPE_SHIFT_AB = 32
ROPE_SHIFT_C = 64


def _rms_fwd(x, g, *, name, tr=512):
    S, W = x.shape
    tr = min(tr, S)

    def body(x_ref, g_ref, o_ref):
        y, _ = _rms_val(x_ref[...], g_ref[...], W)
        o_ref[...] = y.astype(CDT)

    return pl.pallas_call(
        body, name=name, grid=(S // tr,),
        in_specs=[pl.BlockSpec((tr, W), lambda i: (i, 0)), pl.BlockSpec((1, W), lambda i: (0, 0))],
        out_specs=pl.BlockSpec((tr, W), lambda i: (i, 0)), out_shape=_sds((S, W), CDT),
        compiler_params=_cparams(("parallel",)),
    )(x, g.reshape(1, W))


def _rms_bwd(x, g, dy, res, *, name, tr=256):
    S, W = x.shape
    tr = min(tr, S)

    def body(x_ref, g_ref, dy_ref, res_ref, dx_ref, dxb_ref, dg_ref):
        dx, dgt = _rms_bwd_val(x_ref[...], g_ref[...], dy_ref[...], W)
        dx = res_ref[...] + dx
        dx_ref[...] = dx
        dxb_ref[...] = dx.astype(CDT)

        @pl.when(pl.program_id(0) == 0)
        def _():
            dg_ref[...] = jnp.zeros_like(dg_ref)

        dg_ref[...] += _colsum(dgt)

    row = pl.BlockSpec((tr, W), lambda i: (i, 0))
    vec = pl.BlockSpec((1, W), lambda i: (0, 0))
    return pl.pallas_call(
        body, name=name, grid=(S // tr,),
        in_specs=[row, vec, row, row], out_specs=[row, row, vec],
        out_shape=[_sds((S, W), F32), _sds((S, W), CDT), _sds((1, W), F32)],
        compiler_params=_cparams(("arbitrary",)),
    )(x, g.reshape(1, W), dy, res)


def _loss_head(x, g, tgt, *, tr=256):
    S, W = x.shape
    tr = min(tr, S)

    def body(x_ref, g_ref, t_ref, loss_ref, dx_ref, dxb_ref, dg_ref):
        xv, gv = x_ref[...], g_ref[...]
        y, _ = _rms_val(xv, gv, W)
        err = y - t_ref[...]
        part = 0.5 * jnp.sum(jnp.sum(err * err, axis=-1, keepdims=True) * (1.0 / W), axis=0, keepdims=True)
        dx, dgt = _rms_bwd_val(xv, gv, err * (1.0 / W), W)
        dx_ref[...] = dx
        dxb_ref[...] = dx.astype(CDT)

        @pl.when(pl.program_id(0) == 0)
        def _():
            dg_ref[...] = jnp.zeros_like(dg_ref)
            loss_ref[...] = jnp.zeros_like(loss_ref)

        dg_ref[...] += _colsum(dgt)
        loss_ref[...] += jnp.broadcast_to(part, (1, LANES))

    row = pl.BlockSpec((tr, W), lambda i: (i, 0))
    vec = pl.BlockSpec((1, W), lambda i: (0, 0))
    return pl.pallas_call(
        body, name="loss_head", grid=(S // tr,),
        in_specs=[row, vec, row], out_specs=[pl.BlockSpec((1, LANES), lambda i: (0, 0)), row, row, vec],
        out_shape=[_sds((1, LANES), F32), _sds((S, W), F32), _sds((S, W), CDT), _sds((1, W), F32)],
        compiler_params=_cparams(("arbitrary",)),
    )(x, g.reshape(1, W), tgt)


def _tab_specs(tr):
    return [pl.BlockSpec((tr, LANES), lambda i: (i, 0))] * 9


def _lat_masks(shape):
    lane = lax.broadcasted_iota(jnp.int32, shape, 1)
    return lane < KV_LO, (lane >= KV_LO) & (lane < KV_HI)


def _prep_fwd(proj, glat, gqn, gkn, tabs, *, tr=128):
    S = proj.shape[0]
    tr = min(tr, S)

    def body(p_ref, glat_ref, gqn_ref, gkn_ref, ac, aa, ab, bc, ba, bb, cc, ca, cb,
             lat_ref, kpe_ref, qb_ref, kb_ref, vb_ref, qc_ref, kc_ref, vc_ref):
        x = p_ref[:, 0:LAT_W]
        is_q, is_kv = _lat_masks(x.shape)
        yq, _ = _rms_val(jnp.where(is_q, x, 0.0), None, Q_LORA)
        ykv, _ = _rms_val(jnp.where(is_kv, x, 0.0), None, KV_HI - KV_LO)
        lat_ref[...] = ((yq + ykv) * glat_ref[...]).astype(CDT)
        kpe_ref[...] = _rope_val(x[:, LAT_W - LANES:LAT_W], ac[...], aa[...], ab[...], ROPE_SHIFT_AB).astype(CDT)
        for h in range(6):
            xh = p_ref[:, OFF_BQ + LANES * h:OFF_BQ + LANES * (h + 1)]
            y = _rope_val(_rms_val(xh, gqn_ref[...], LANES)[0], bc[...], ba[...], bb[...], ROPE_SHIFT_AB)
            qb_ref[:, LANES * h:LANES * (h + 1)] = y.astype(CDT)
        for h in range(2):
            xh = p_ref[:, OFF_BK + LANES * h:OFF_BK + LANES * (h + 1)]
            y = _rope_val(_rms_val(xh, gkn_ref[...], LANES)[0], bc[...], ba[...], bb[...], ROPE_SHIFT_AB)
            kb_ref[:, LANES * h:LANES * (h + 1)] = y.astype(CDT)
        vb_ref[...] = p_ref[:, OFF_BV:OFF_BV + 256].astype(CDT)
        for h in range(6):
            sl = slice(LANES * h, LANES * (h + 1))
            qc_ref[:, sl] = _rope_val(p_ref[:, OFF_CQ + LANES * h:OFF_CQ + LANES * (h + 1)], cc[...], ca[...], cb[...],
                                      ROPE_SHIFT_C)
            kc_ref[:, sl] = _rope_val(p_ref[:, OFF_CK + LANES * h:OFF_CK + LANES * (h + 1)], cc[...], ca[...], cb[...],
                                      ROPE_SHIFT_C)
        vc_ref[...] = p_ref[:, OFF_CV:OFF_CV + 768]

    vec = lambda w: pl.BlockSpec((1, w), lambda i: (0, 0))
    row = lambda w: pl.BlockSpec((tr, w), lambda i: (i, 0))
    return pl.pallas_call(
        body, name="prep_fwd", grid=(S // tr,),
        in_specs=[row(PROJ_W), vec(LAT_W), vec(128), vec(128)] + _tab_specs(tr),
        out_specs=[row(LAT_W), row(128), row(768), row(256), row(256), row(768), row(768), row(768)],
        out_shape=[_sds((S, LAT_W), CDT), _sds((S, 128), CDT), _sds((S, 768), CDT), _sds((S, 256), CDT),
                   _sds((S, 256), CDT), _sds((S, 768), F32), _sds((S, 768), F32), _sds((S, 768), F32)],
        compiler_params=_cparams(("parallel",)),
    )(proj, glat, gqn, gkn, *tabs[0], *tabs[1], *tabs[2])


def _prep_a2_fwd(qkva, kpe, tab_a, *, tr=512):
    S = qkva.shape[0]
    tr = min(tr, S)

    def body(x_ref, kpe_ref, ac, aa, ab, qa_ref, ka_ref, va_ref):
        for h in range(4):
            lo, hi = 2 * LANES * h, 2 * LANES * h + LANES
            qa_ref[:, lo:hi] = x_ref[:, lo:hi].astype(CDT)
            qa_ref[:, hi:hi + LANES] = _rope_val(x_ref[:, hi:hi + LANES], ac[...], aa[...], ab[...],
                                                 ROPE_SHIFT_AB).astype(CDT)
            ka_ref[:, lo:hi] = x_ref[:, 1024 + LANES * h:1024 + LANES * (h + 1)].astype(CDT)
            ka_ref[:, hi:hi + LANES] = kpe_ref[...]
        va_ref[...] = x_ref[:, 1536:2048].astype(CDT)

    row = lambda w: pl.BlockSpec((tr, w), lambda i: (i, 0))
    return pl.pallas_call(
        body, name="prep_a2_fwd", grid=(S // tr,),
        in_specs=[row(2048), row(128)] + _tab_specs(tr)[:3],
        out_specs=[row(1024), row(1024), row(512)],
        out_shape=[_sds((S, 1024), CDT), _sds((S, 1024), CDT), _sds((S, 512), CDT)],
        compiler_params=_cparams(("parallel",)),
    )(qkva, kpe, *tab_a)


def _prep_a2_bwd(dqa, dka, dva, tab_a, *, tr=512):
    S = dqa.shape[0]
    tr = min(tr, S)

    def body(dq_ref, dk_ref, dv_ref, ac, aa, ab, dx_ref, dkr_ref):
        dkpe = jnp.zeros((tr, LANES), F32)
        for h in range(4):
            lo, hi = 2 * LANES * h, 2 * LANES * h + LANES
            dx_ref[:, lo:hi] = dq_ref[:, lo:hi].astype(CDT)
            dx_ref[:, hi:hi + LANES] = _rope_t_val(dq_ref[:, hi:hi + LANES], ac[...], aa[...], ab[...],
                                                   ROPE_SHIFT_AB).astype(CDT)
            dx_ref[:, 1024 + LANES * h:1024 + LANES * (h + 1)] = dk_ref[:, lo:hi].astype(CDT)
            dkpe = dkpe + dk_ref[:, hi:hi + LANES]
        dx_ref[:, 1536:2048] = dv_ref[...].astype(CDT)
        dkr_ref[...] = _rope_t_val(dkpe, ac[...], aa[...], ab[...], ROPE_SHIFT_AB)

    row = lambda w: pl.BlockSpec((tr, w), lambda i: (i, 0))
    return pl.pallas_call(
        body, name="prep_a2_bwd", grid=(S // tr,),
        in_specs=[row(1024), row(1024), row(512)] + _tab_specs(tr)[:3],
        out_specs=[row(2048), row(128)],
        out_shape=[_sds((S, 2048), CDT), _sds((S, 128), F32)],
        compiler_params=_cparams(("parallel",)),
    )(dqa, dka, dva, *tab_a)


def _prep_bwd(proj, glat, gqn, gkn, tabs, dlat, dkr, dqb, dkb, dvb, dqc, dkc, dvc, *, tr=128):
    S = proj.shape[0]
    tr = min(tr, S)

    def body(p_ref, glat_ref, gqn_ref, gkn_ref, ac, aa, ab, bc, ba, bb, cc, ca, cb,
             dlat_ref, dkr_ref, dqb_ref, dkb_ref, dvb_ref, dqc_ref, dkc_ref, dvc_ref,
             dp_ref, dglat_ref, dgqn_ref, dgkn_ref):
        @pl.when(pl.program_id(0) == 0)
        def _():
            dglat_ref[...] = jnp.zeros_like(dglat_ref)
            dgqn_ref[...] = jnp.zeros_like(dgqn_ref)
            dgkn_ref[...] = jnp.zeros_like(dgkn_ref)

        x = p_ref[:, 0:LAT_W]
        is_q, is_kv = _lat_masks(x.shape)
        dy, g = dlat_ref[...], glat_ref[...]
        dxq, dgq = _rms_bwd_val(jnp.where(is_q, x, 0.0), g, jnp.where(is_q, dy, 0.0), Q_LORA)
        dxkv, dgkv = _rms_bwd_val(jnp.where(is_kv, x, 0.0), g, jnp.where(is_kv, dy, 0.0), KV_HI - KV_LO)
        dglat_ref[...] += _colsum(dgq + dgkv)
        dx = dxq + dxkv
        dp_ref[:, 0:LAT_W - LANES] = dx[:, 0:LAT_W - LANES].astype(CDT)
        dp_ref[:, LAT_W - LANES:LAT_W] = (dx[:, LAT_W - LANES:LAT_W] + dkr_ref[...]).astype(CDT)
        dgqn = jnp.zeros((1, LANES), F32)
        for h in range(6):
            sl = slice(LANES * h, LANES * (h + 1))
            po = slice(OFF_BQ + LANES * h, OFF_BQ + LANES * (h + 1))
            dyh = _rope_t_val(dqb_ref[:, sl], bc[...], ba[...], bb[...], ROPE_SHIFT_AB)
            dxh, dgt = _rms_bwd_val(p_ref[:, po], gqn_ref[...], dyh, LANES)
            dp_ref[:, po] = dxh.astype(CDT)
            dgqn = dgqn + _colsum(dgt)
        dgqn_ref[...] += dgqn
        dgkn = jnp.zeros((1, LANES), F32)
        for h in range(2):
            sl = slice(LANES * h, LANES * (h + 1))
            po = slice(OFF_BK + LANES * h, OFF_BK + LANES * (h + 1))
            dyh = _rope_t_val(dkb_ref[:, sl], bc[...], ba[...], bb[...], ROPE_SHIFT_AB)
            dxh, dgt = _rms_bwd_val(p_ref[:, po], gkn_ref[...], dyh, LANES)
            dp_ref[:, po] = dxh.astype(CDT)
            dgkn = dgkn + _colsum(dgt)
        dgkn_ref[...] += dgkn
        dp_ref[:, OFF_BV:OFF_BV + 256] = dvb_ref[...].astype(CDT)
        for h in range(6):
            sl = slice(LANES * h, LANES * (h + 1))
            dp_ref[:, OFF_CQ + LANES * h:OFF_CQ + LANES * (h + 1)] = _rope_t_val(
                dqc_ref[:, sl], cc[...], ca[...], cb[...], ROPE_SHIFT_C).astype(CDT)
            dp_ref[:, OFF_CK + LANES * h:OFF_CK + LANES * (h + 1)] = _rope_t_val(
                dkc_ref[:, sl], cc[...], ca[...], cb[...], ROPE_SHIFT_C).astype(CDT)
        dp_ref[:, OFF_CV:OFF_CV + 768] = dvc_ref[...].astype(CDT)

    vec = lambda w: pl.BlockSpec((1, w), lambda i: (0, 0))
    row = lambda w: pl.BlockSpec((tr, w), lambda i: (i, 0))
    return pl.pallas_call(
        body, name="prep_bwd", grid=(S // tr,),
        in_specs=[row(PROJ_W), vec(LAT_W), vec(128), vec(128)] + _tab_specs(tr)
        + [row(LAT_W), row(128), row(768), row(256), row(256), row(768), row(768), row(768)],
        out_specs=[row(PROJ_W), vec(LAT_W), vec(128), vec(128)],
        out_shape=[_sds((S, PROJ_W), CDT), _sds((1, LAT_W), F32), _sds((1, 128), F32), _sds((1, 128), F32)],
        compiler_params=_cparams(("arbitrary",)),
    )(proj, glat, gqn, gkn, *tabs[0], *tabs[1], *tabs[2], dlat, dkr, dqb, dkb, dvb, dqc, dkc, dvc)


ATTN_TK = 512
LOG2E = 1.4426950408889634
C2_H = SCALE_H * LOG2E


def _attn_fwd(q, k, v, *, H, G, dk, dv, scale, name, tq=512, rider=None):
    S = q.shape[0]
    tq = min(tq, S)
    tk = min(ATTN_TK, S)
    c2 = scale * LOG2E

    def body(q_ref, k_ref, v_ref, o_ref, l_ref):
        qv = q_ref[...]
        chunks = [pl.ds(c * tk, tk) for c in range(S // tk)]
        s = [_dot(qv, k_ref[rows, :], NT) for rows in chunks]
        m = functools.reduce(jnp.maximum, [jnp.max(sc, axis=-1, keepdims=True) for sc in s])
        den = jnp.zeros((tq, 1), F32)
        acc = jnp.zeros((tq, dv), F32)
        for sc, rows in zip(s, chunks):
            e = jnp.exp2((sc - m) * c2)
            den = den + jnp.sum(e, axis=-1, keepdims=True)
            acc = acc + _dot(e.astype(CDT), v_ref[rows, :], NN)
        o_ref[...] = acc * (1.0 / den)
        l_ref[...] = jnp.broadcast_to(m * scale + jnp.log(den), (tq, LANES))

    return _carried_call(
        body, name=name, grid=(H, S // tq),
        in_specs=[pl.BlockSpec((tq, dk), lambda h, i: (i, h)), pl.BlockSpec((S, dk), lambda h, i: (0, h // G)),
                  pl.BlockSpec((S, dv), lambda h, i: (0, h // G))],
        out_specs=[pl.BlockSpec((tq, dv), lambda h, i: (i, h)), pl.BlockSpec((tq, LANES), lambda h, i: (i, h))],
        out_shape=[_sds((S, H * dv), F32), _sds((S, H * LANES), F32)], scratch_shapes=[],
        dims=("parallel", "parallel"), args=[q, k, v], rider=rider)


def _attn_bwd(q, k, v, do, lse, delta, *, H, G, dk, dv, scale, name, tq=256, rider=None):
    S = q.shape[0]
    tq = min(tq, S)
    Hkv = H // G
    c2 = scale * LOG2E

    def body(q_ref, k_ref, v_ref, do_ref, l_ref, d_ref, dq_ref, dk_ref, dv_ref):
        @pl.when((pl.program_id(1) == 0) & (pl.program_id(2) == 0))
        def _():
            dk_ref[...] = jnp.zeros_like(dk_ref)
            dv_ref[...] = jnp.zeros_like(dv_ref)

        qv, kv, dov = q_ref[...], k_ref[...], do_ref[...]
        p = jnp.exp2(_dot(qv, kv, NT) * c2 - l_ref[:, 0:1] * LOG2E)
        dp = _dot(dov, v_ref[...], NT)
        ds = (p * (dp - d_ref[:, 0:1]) * scale).astype(CDT)
        dq_ref[...] = _dot(ds, kv, NN)
        dk_ref[...] += _dot(ds, qv, TN)
        dv_ref[...] += _dot(p.astype(CDT), dov, TN)

    qi = lambda hk, g, i: (i, hk * G + g)
    return _carried_call(
        body, name=name, grid=(Hkv, G, S // tq),
        in_specs=[pl.BlockSpec((tq, dk), qi), pl.BlockSpec((S, dk), lambda hk, g, i: (0, hk)),
                  pl.BlockSpec((S, dv), lambda hk, g, i: (0, hk)), pl.BlockSpec((tq, dv), qi),
                  pl.BlockSpec((tq, LANES), qi), pl.BlockSpec((tq, LANES), qi)],
        out_specs=[pl.BlockSpec((tq, dk), qi), pl.BlockSpec((S, dk), lambda hk, g, i: (0, hk)),
                   pl.BlockSpec((S, dv), lambda hk, g, i: (0, hk))],
        out_shape=[_sds((S, H * dk), F32), _sds((S, Hkv * dk), F32), _sds((S, Hkv * dv), F32)], scratch_shapes=[],
        dims=("parallel", "arbitrary", "arbitrary"), args=[q, k, v, do, lse, delta], rider=rider)


BAND_SUB = 128
BAND_WIN = 384
BAND_UNROLL = 8


def _band_blocks(S, d):
    L = S // d
    assert L % BAND_SUB == 0 and S % (BAND_SUB * BAND_UNROLL) == 0
    return L, L // BAND_SUB, min(BAND_WIN, L)


def _band_index(blk, d, nb, L, win):
    r, jb = blk // nb, blk % nb
    l0 = jb * BAND_SUB
    w0 = jnp.clip(l0 - BAND_SUB, 0, L - win)
    return r + d * l0, r + d * w0, l0, w0


def _band_rows(start, size, d):
    return pl.ds(pl.multiple_of(start, BAND_SUB), size) if d == 1 else pl.ds(start, size, stride=d)


def _band_mask(l0, w0, win):
    rpos = l0 + lax.broadcasted_iota(jnp.int32, (BAND_SUB, win), 0)
    cpos = w0 + lax.broadcasted_iota(jnp.int32, (BAND_SUB, win), 1)
    return jnp.abs(rpos - cpos) <= BAND_HALF


def _mixc_fwd(q, k, v, rider=None):
    S, W = q.shape

    def body(q_ref, k_ref, v_ref, o_ref, l_ref, *scratch):
        ob_refs, lb_refs = scratch[0:3], scratch[3:6]
        for b, d in enumerate(DILATIONS):
            L, nb, win = _band_blocks(S, d)

            def step(it, carry, b=b, d=d, L=L, nb=nb, win=win):
                idx = [_band_index(it * BAND_UNROLL + u, d, nb, L, win) for u in range(BAND_UNROLL)]
                qrows = [_band_rows(i[0], BAND_SUB, d) for i in idx]
                krows = [_band_rows(i[1], win, d) for i in idx]
                qv = [q_ref[r, :].astype(CDT) for r in qrows]
                kw = [k_ref[r, :].astype(CDT) for r in krows]
                vw = [v_ref[r, :].astype(CDT) for r in krows]
                s = [jnp.where(_band_mask(i[2], i[3], win), _dot(a, kk, NT), NEG_INF) for i, a, kk in zip(idx, qv, kw)]
                m = [jnp.max(x, axis=-1, keepdims=True) for x in s]
                e = [jnp.exp2((x - mm) * C2_H) for x, mm in zip(s, m)]
                den = [jnp.sum(x, axis=-1, keepdims=True) for x in e]
                o = [_dot((x * (1.0 / dd)).astype(CDT), vv, NN) for x, dd, vv in zip(e, den, vw)]
                for r, ou, mm, dd in zip(qrows, o, m, den):
                    ob_refs[b][r, :] = ou
                    lb_refs[b][r, :] = jnp.broadcast_to(mm * SCALE_H + jnp.log(dd), (BAND_SUB, LANES))
                return carry

            lax.fori_loop(0, S // (BAND_SUB * BAND_UNROLL), step, 0)

        def combine(c, carry):
            rows = pl.ds(pl.multiple_of(c * 256, 256), 256)
            l0, l1, l2 = lb_refs[0][rows, :], lb_refs[1][rows, :], lb_refs[2][rows, :]
            m = jnp.maximum(jnp.maximum(l0, l1), l2)
            e0, e1, e2 = jnp.exp(l0 - m), jnp.exp(l1 - m), jnp.exp(l2 - m)
            den = e0 + e1 + e2
            inv = 1.0 / den
            o_ref[rows, :] = ((e0 * inv) * ob_refs[0][rows, :] + (e1 * inv) * ob_refs[1][rows, :]
                              + (e2 * inv) * ob_refs[2][rows, :])
            l_ref[rows, :] = m + jnp.log(den)
            return carry

        lax.fori_loop(0, S // 256, combine, 0)

    head = pl.BlockSpec((S, LANES), lambda h: (0, h))
    return _carried_call(
        body, name="mixc_fwd", grid=(W // LANES,), in_specs=[head] * 3, out_specs=[head] * 2,
        out_shape=[_sds((S, W), F32)] * 2, scratch_shapes=[pltpu.VMEM((S, LANES), F32)] * 6,
        dims=("parallel",), args=[q, k, v], rider=rider)


def _mixc_bwd(q, k, v, do, lse, dd, rider=None):
    S, W = q.shape

    def body(q_ref, k_ref, v_ref, do_ref, l_ref, d_ref, dq_ref, dk_ref, dv_ref):
        dq_ref[...] = jnp.zeros_like(dq_ref)
        dk_ref[...] = jnp.zeros_like(dk_ref)
        dv_ref[...] = jnp.zeros_like(dv_ref)
        for d in DILATIONS:
            L, nb, win = _band_blocks(S, d)

            def step(it, carry, d=d, L=L, nb=nb, win=win):
                idx = [_band_index(it * BAND_UNROLL + u, d, nb, L, win) for u in range(BAND_UNROLL)]
                qrows = [_band_rows(i[0], BAND_SUB, d) for i in idx]
                krows = [_band_rows(i[1], win, d) for i in idx]
                qv = [q_ref[r, :].astype(CDT) for r in qrows]
                dov = [do_ref[r, :].astype(CDT) for r in qrows]
                kw = [k_ref[r, :].astype(CDT) for r in krows]
                vw = [v_ref[r, :].astype(CDT) for r in krows]
                lse2 = [l_ref[r, :][:, 0:1] * LOG2E for r in qrows]
                dd = [d_ref[r, :][:, 0:1] for r in qrows]
                s = [jnp.where(_band_mask(i[2], i[3], win), _dot(a, kk, NT), NEG_INF) for i, a, kk in zip(idx, qv, kw)]
                p = [jnp.exp2(x * C2_H - ll) for x, ll in zip(s, lse2)]
                dp = [_dot(a, vv, NT) for a, vv in zip(dov, vw)]
                ds = [(pp * (x - y) * SCALE_H).astype(CDT) for pp, x, y in zip(p, dp, dd)]
                dq = [_dot(x, kk, NN) for x, kk in zip(ds, kw)]
                dk = [_dot(x, a, TN) for x, a in zip(ds, qv)]
                dv = [_dot(pp.astype(CDT), a, TN) for pp, a in zip(p, dov)]
                for u in range(BAND_UNROLL):
                    dq_ref[qrows[u], :] += dq[u]
                    dk_ref[krows[u], :] += dk[u]
                    dv_ref[krows[u], :] += dv[u]
                return carry

            lax.fori_loop(0, S // (BAND_SUB * BAND_UNROLL), step, 0)

    head = pl.BlockSpec((S, LANES), lambda h: (0, h))
    return _carried_call(
        body, name="mixc_bwd", grid=(W // LANES,), in_specs=[head] * 6, out_specs=[head] * 3,
        out_shape=[_sds((S, W), F32)] * 3, scratch_shapes=[], dims=("parallel",), args=[q, k, v, do, lse, dd],
        rider=rider)


def _outnorm_fwd(oa, ob, oc, g, *, tr=256):
    S = oa.shape[0]
    tr = min(tr, S)

    def body(a_ref, b_ref, c_ref, g_ref, m_ref):
        m_ref[:, 0:512] = (_rms_val(a_ref[...], None, 512)[0] * g_ref[:, 0:512]).astype(CDT)
        m_ref[:, 512:1280] = (_rms_val(b_ref[...], None, 768)[0] * g_ref[:, 512:1280]).astype(CDT)
        m_ref[:, 1280:2048] = (_rms_val(c_ref[...], None, 768)[0] * g_ref[:, 1280:2048]).astype(CDT)

    row = lambda w: pl.BlockSpec((tr, w), lambda i: (i, 0))
    return pl.pallas_call(
        body, name="outnorm_fwd", grid=(S // tr,),
        in_specs=[row(512), row(768), row(768), pl.BlockSpec((1, 2048), lambda i: (0, 0))],
        out_specs=row(2048), out_shape=_sds((S, 2048), CDT), compiler_params=_cparams(("parallel",)),
    )(oa, ob, oc, g)


def _outnorm_bwd(oa, ob, oc, g, dm, *, tr=256):
    S = oa.shape[0]
    tr = min(tr, S)

    def body(a_ref, b_ref, c_ref, g_ref, dm_ref, doa_ref, dob_ref, doc_ref, da_ref, db_ref, dc_ref, dg_ref):
        @pl.when(pl.program_id(0) == 0)
        def _():
            dg_ref[...] = jnp.zeros_like(dg_ref)

        for o_ref, do_ref, d_ref, lo, w in ((a_ref, doa_ref, da_ref, 0, 512), (b_ref, dob_ref, db_ref, 512, 768),
                                            (c_ref, doc_ref, dc_ref, 1280, 768)):
            o = o_ref[...]
            dmv = dm_ref[:, lo:lo + w]
            do, _ = _rms_bwd_val(o, None, dmv * g_ref[:, lo:lo + w], w)
            r = lax.rsqrt(jnp.sum(o * o, axis=-1, keepdims=True) * (1.0 / w) + EPS)
            dg_ref[:, lo:lo + w] += _colsum(dmv * (o * r))
            do_ref[...] = do.astype(do_ref.dtype)
            for h in range(w // LANES):
                sl = slice(LANES * h, LANES * (h + 1))
                d_ref[:, sl] = jnp.broadcast_to(jnp.sum(do[:, sl] * o[:, sl], axis=-1, keepdims=True), (tr, LANES))

    row = lambda w: pl.BlockSpec((tr, w), lambda i: (i, 0))
    vec = pl.BlockSpec((1, 2048), lambda i: (0, 0))
    return pl.pallas_call(
        body, name="outnorm_bwd", grid=(S // tr,),
        in_specs=[row(512), row(768), row(768), vec, row(2048)],
        out_specs=[row(512), row(768), row(768), row(512), row(768), row(768), vec],
        out_shape=[_sds((S, 512), CDT), _sds((S, 768), CDT), _sds((S, 768), F32), _sds((S, 512), F32),
                   _sds((S, 768), F32), _sds((S, 768), F32), _sds((1, 2048), F32)],
        compiler_params=_cparams(("arbitrary",)),
    )(oa, ob, oc, g, dm)


def _row_tile(r, c, itemsize, limit=1 << 20):
    best = 16
    for t in range(16, r + 1, 16):
        if r % t == 0 and t * c * itemsize <= limit:
            best = t
    return best


def _rs_chip_sum(g, got, *, name):
    _, _, r, c = g.shape
    tr = _row_tile(r, c, 2)
    core = lax.axis_index("c").astype(jnp.int32).reshape(1)

    def body(c_ref, a_ref, b_ref, o_ref):
        o_ref[...] = (a_ref[...].astype(F32) + b_ref[...].astype(F32)).astype(o_ref.dtype)

    spec = pltpu.PrefetchScalarGridSpec(
        num_scalar_prefetch=1, grid=(4, r // tr),
        in_specs=[pl.BlockSpec((None, None, tr, c), lambda k, i, cr: (k, cr[0], i, 0)),
                  pl.BlockSpec((None, tr, c), lambda k, i, cr: (k, i, 0))],
        out_specs=pl.BlockSpec((None, tr, c), lambda k, i, cr: (k, i, 0)))
    return pl.pallas_call(body, name=name, grid_spec=spec, out_shape=_sds((4, r, c), g.dtype),
                          compiler_params=_cparams(("parallel", "parallel")))(core, g, got)


def _rs_final_sum(r4, *, name):
    _, r, c = r4.shape
    tr = _row_tile(r, c, 4)

    def body(r_ref, o_ref):
        o_ref[...] = ((r_ref[0].astype(F32) + r_ref[1].astype(F32)) + r_ref[2].astype(F32)) + r_ref[3].astype(F32)

    return pl.pallas_call(
        body, name=name, grid=(r // tr,), in_specs=[pl.BlockSpec((4, tr, c), lambda i: (0, i, 0))],
        out_specs=pl.BlockSpec((tr, c), lambda i: (i, 0)), out_shape=_sds((r, c), F32),
        compiler_params=_cparams(("parallel",)))(r4)


def _all_reduce_small(v):
    R = v.shape[0]

    def body(v_ref, out_ref, buf_ref, send_sems, recv_sems):
        x, y, c = lax.axis_index("x"), lax.axis_index("y"), lax.axis_index("c")
        me = 4 * x + 2 * y + c
        buf_ref[me] = v_ref[...]
        peers = []
        for r in range(1, 8):
            px, py, pc = x ^ (r >> 2), y ^ ((r >> 1) & 1), c ^ (r & 1)
            peers.append((r, (px, py, pc), 4 * px + 2 * py + pc))
        sends = [pltpu.make_async_remote_copy(
            src_ref=v_ref, dst_ref=buf_ref.at[me], send_sem=send_sems.at[r - 1], recv_sem=recv_sems.at[r - 1],
            device_id=dev, device_id_type=MESH) for r, dev, _ in peers]
        for cp in sends:
            cp.start()
        for r, dev, idx in peers:
            pltpu.make_async_remote_copy(
                src_ref=v_ref, dst_ref=buf_ref.at[idx], send_sem=send_sems.at[r - 1], recv_sem=recv_sems.at[r - 1],
                device_id=dev, device_id_type=MESH).wait_recv()
        for cp in sends:
            cp.wait_send()
        acc = buf_ref[0]
        for k in range(1, 8):
            acc = acc + buf_ref[k]
        out_ref[...] = acc

    vm = pl.BlockSpec(memory_space=pltpu.VMEM)
    return pl.pallas_call(
        body, name="all_reduce_small", out_shape=_sds((R, LANES), F32), in_specs=[vm], out_specs=vm,
        scratch_shapes=[pltpu.VMEM((8, R, LANES), F32), pltpu.SemaphoreType.DMA((7,)), pltpu.SemaphoreType.DMA((7,))],
    )(v)


def _adamw(w, g, m, v, *, name):
    R, C = w.shape
    tr = R
    for cand in (1024, 512, 256, 128, 64, 32, 16, 8):
        if R % cand == 0 and cand * C * 4 <= 2 * 1024 * 1024:
            tr = cand
            break

    def body(w_ref, g_ref, m_ref, v_ref, d_ref, nm_ref, nv_ref):
        gv = g_ref[...]
        mn = ADAM_B1 * m_ref[...] + (1.0 - ADAM_B1) * gv
        vn = ADAM_B2 * v_ref[...] + (1.0 - ADAM_B2) * (gv * gv)
        m_hat = mn / (1.0 - ADAM_B1 ** ADAM_STEP)
        v_hat = vn / (1.0 - ADAM_B2 ** ADAM_STEP)
        d_ref[...] = -ADAM_LR * (m_hat / (jnp.sqrt(v_hat) + ADAM_EPS) + ADAM_WD * w_ref[...])
        nm_ref[...] = mn
        nv_ref[...] = vn

    blk = pl.BlockSpec((tr, C), lambda i: (i, 0))
    return pl.pallas_call(
        body, name=name, grid=(R // tr,), in_specs=[blk] * 4, out_specs=[blk] * 3,
        out_shape=[_sds((R, C), F32)] * 3, compiler_params=_cparams(("parallel",)))(w, g, m, v)


def _wuq_pad(w):
    w = w.reshape(448, 4, 192)
    z = jnp.zeros((448, 4, 64), w.dtype)
    return jnp.concatenate([w[:, :, 0:128], z, w[:, :, 128:192]], axis=2).reshape(448, 1024)


def _wuq_unpad(w):
    w = w.reshape(448, 4, 256)
    return jnp.concatenate([w[:, :, 0:128], w[:, :, 192:256]], axis=2).reshape(448, 768)


def _wukv_perm(w):
    return w.reshape(512, 4, 2, 128).transpose(0, 2, 1, 3).reshape(512, 1024)


def _wukv_unperm(w):
    return w.reshape(512, 2, 4, 128).transpose(0, 2, 1, 3).reshape(512, 1024)


def _lat_weight(w_uq, w_ukv):
    z = lambda r, c: jnp.zeros((r, c), w_uq.dtype)
    top = jnp.concatenate([_wuq_pad(w_uq), z(448, 1024)], axis=1)
    mid = jnp.concatenate([z(512, 1024), _wukv_perm(w_ukv)], axis=1)
    return jnp.concatenate([top, mid, z(64, 2048)], axis=0)


def _lat_weight_grads(dw):
    return _wuq_unpad(dw[0:KV_LO, 0:1024]), _wukv_unperm(dw[KV_LO:KV_HI, 1024:2048])


def _comm_shards(w_in, w_uq, w_ukv, w_out, w_ff1, w_ff2):
    lat = jnp.concatenate([w_uq.reshape(UQ_ROWS, LANES), w_ukv.reshape(512, LANES)], axis=0)
    return [w_in.T, w_ff1.T, w_out, w_ff2, lat]


def _from_comm_shards(parts):
    w_in_t, w_ff1_t, w_out, w_ff2, lat = parts
    return {"w_in": w_in_t.T, "w_ff1": w_ff1_t.T, "w_out": w_out, "w_ff2": w_ff2,
            "w_uq": lat[0:UQ_ROWS].reshape(448, 96), "w_ukv": lat[UQ_ROWS:].reshape(512, 128)}


def _early_weights(g_in_t, g_lat):
    w_uq = g_lat[:, 0:UQ_ROWS].reshape(8, 448, 96).transpose(1, 0, 2).reshape(448, 768)
    w_ukv = g_lat[:, UQ_ROWS:].reshape(8, 512, 128).transpose(1, 0, 2).reshape(512, 1024)
    return g_in_t.reshape(PROJ_W, D_MODEL), _lat_weight(w_uq, w_ukv)


def _layer_fwd(x, W, G, tabs, plan=None):
    W = dict(W)
    shards, nxt = plan if plan is not None else (None, None)
    rider = lambda make: None if plan is None else make()
    s = {"x0": x}
    s["h1"] = _rms_fwd(x, G["ln1_g"], name="rms1_fwd")
    s["proj"] = _matmul(s["h1"], W["w_in_t"], mode="nt", tm=1024, tn=768, tk=2048, out_dtype=F32, name="mm_in")
    s["lat"], kpe, s["qb"], s["kb"], s["vb"], s["qc"], s["kc"], s["vc"] = _prep_fwd(
        s["proj"], G["glat"], G["gqn"], G["gkn"], tabs)
    qkva = _matmul(s["lat"], W["w_lat"], mode="nn", tm=1024, tn=1024, tk=1024, out_dtype=F32, name="mm_lat")
    s["qa"], s["ka"], s["va"] = _prep_a2_fwd(qkva, kpe, tabs[0])
    (s["oa"], s["lse_a"]), got_a = _attn_fwd(
        s["qa"], s["ka"], s["va"], H=4, G=1, dk=256, dv=128, scale=SCALE_A, name="attn_a_fwd",
        rider=rider(lambda: _ag_first_rider([shards["w_ff1_t"]], "_ff1")))
    (s["ob"], s["lse_b"]), got_b = _attn_fwd(
        s["qb"], s["kb"], s["vb"], H=6, G=3, dk=128, dv=128, scale=SCALE_H, name="attn_b_fwd",
        rider=rider(lambda: _ag_first_rider([shards["w_ff2"], shards["w_out"]], "_ff2_out")))
    (s["oc"], s["lse_c"]), got_c = _mixc_fwd(
        s["qc"], s["kc"], s["vc"],
        rider=rider(lambda: _join(_ag_second_rider(got_a + got_b, "_ff_out"),
                                  None if nxt is None else _ag_first_rider(nxt, "_early"))))
    if plan is not None:
        W["w_ff1_t"], W["w_ff2"] = got_c[0].reshape(D_FF, D_MODEL), got_c[1].reshape(D_FF, D_MODEL)
        W["w_out"] = got_c[2].reshape(D_MODEL, D_MODEL)
    s["mixed"] = _outnorm_fwd(s["oa"], s["ob"], s["oc"], G["g_out"])
    s["x1"] = _matmul(s["mixed"], W["w_out"], mode="nn", tm=1024, tn=1024, tk=2048, out_dtype=F32, name="mm_out",
                      epi="residual", extra=x)
    s["h2"] = _rms_fwd(s["x1"], G["ln2_g"], name="rms2_fwd")
    ff1 = functools.partial(_matmul, s["h2"], W["w_ff1_t"], mode="nt", tm=1024, tn=1024, tk=2048, out_dtype=CDT,
                            name="mm_ff1", epi="relu2")
    if nxt is None:
        (s["z"], s["u"]), early = ff1(), None
    else:
        (s["z"], s["u"]), early = ff1(rider=_ag_second_rider(got_c[3:5], "_early"))
    x2 = _matmul(s["u"], W["w_ff2"], mode="nn", tm=1024, tn=1024, tk=2048, out_dtype=F32, name="mm_ff2",
                 epi="residual", extra=s["x1"])
    return x2, s, W, early


def _by_destination(dw, name):
    return dw.reshape((4, 2) + COMM_SHAPE[name])


def _early_by_destination(dw_in_t, dw_lat):
    dw_uq, dw_ukv = _lat_weight_grads(dw_lat)
    lat = jnp.concatenate([dw_uq.reshape(448, 8, 96).transpose(1, 0, 2).reshape(8, UQ_ROWS, LANES),
                           dw_ukv.reshape(512, 8, 128).transpose(1, 0, 2)], axis=1)
    return [_by_destination(dw_in_t, "w_in_t"), _by_destination(lat, "lat")]


def _layer_bwd(dx2, dx2b, s, W, G, tabs, scatter=False, pending=None):
    dw, dg, landed = {}, {}, {}
    ff2_dx = functools.partial(_matmul, dx2b, W["w_ff2"], mode="nt", tm=1024, tn=1024, tk=2048, out_dtype=CDT,
                               name="mm_ff2_dx", epi="drelu2", extra=s["z"])
    if pending is None:
        dz = ff2_dx()
    else:
        dz, got = ff2_dx(rider=_rs_sibling_rider(pending, "_early"))
        chip = [_rs_chip_sum(g, r, name="rs_chip_sum_" + n) for g, r, n in zip(pending, got, ("w_in_t", "lat"))]
    dw["w_ff2"] = _matmul(s["u"], dx2b, mode="tn", tm=2048, tn=1024, tk=2048, out_dtype=CDT, name="mm_ff2_dw")
    ff1_dx = functools.partial(_matmul, dz, W["w_ff1_t"], mode="nn", tm=1024, tn=1024, tk=2048, out_dtype=F32,
                               name="mm_ff1_dx")
    if pending is None:
        dh2 = ff1_dx()
    else:
        dh2, got = ff1_dx(rider=_rs_chip_rider(chip, "_early"))
        landed["above_w_in_t"], landed["above_lat"] = got
    dw["w_ff1_t"] = _matmul(dz, s["h2"], mode="tn", tm=2048, tn=1024, tk=2048, out_dtype=CDT, name="mm_ff1_dw")
    dx1, dx1b, dg["ln2_g"] = _rms_bwd(s["x1"], G["ln2_g"], dh2, dx2, name="rms2_bwd")
    dmixed = _matmul(dx1b, W["w_out"], mode="nt", tm=1024, tn=1024, tk=2048, out_dtype=F32, name="mm_out_dx")
    out_dw = functools.partial(_matmul, s["mixed"], dx1b, mode="tn", tm=2048, tn=1024, tk=2048, out_dtype=CDT,
                               name="mm_out_dw")
    if not scatter:
        dw["w_out"] = out_dw()
        riders = [None, None, None]
    else:
        g_ff = [_by_destination(dw["w_ff2"], "w_ff2"), _by_destination(dw["w_ff1_t"], "w_ff1_t")]
        dw["w_out"], got = out_dw(rider=_rs_sibling_rider(g_ff, "_ff"))
        chip_ff2 = _rs_chip_sum(g_ff[0], got[0], name="rs_chip_sum_w_ff2")
        chip_ff1 = _rs_chip_sum(g_ff[1], got[1], name="rs_chip_sum_w_ff1_t")
        g_out = [_by_destination(dw["w_out"], "w_out")]
        riders = [_rs_chip_rider([chip_ff2], "_ff2"),
                  _join(_rs_chip_rider([chip_ff1], "_ff1"), _rs_sibling_rider(g_out, "_out")), None]
    doa, dob, doc, dla, dlb, dlc, dg["g_out"] = _outnorm_bwd(s["oa"], s["ob"], s["oc"], G["g_out"], dmixed)
    (dqa, dka, dva), got = _attn_bwd(s["qa"], s["ka"], s["va"], doa, s["lse_a"], dla, H=4, G=1, dk=256, dv=128,
                                     scale=SCALE_A, name="attn_a_bwd", rider=riders[0])
    if scatter:
        landed["w_ff2"] = got[0]
    (dqb, dkb, dvb), got = _attn_bwd(s["qb"], s["kb"], s["vb"], dob, s["lse_b"], dlb, H=6, G=3, dk=128, dv=128,
                                     scale=SCALE_H, name="attn_b_bwd", rider=riders[1])
    if scatter:
        landed["w_ff1_t"] = got[0]
        riders[2] = _rs_chip_rider([_rs_chip_sum(g_out[0], got[1], name="rs_chip_sum_w_out")], "_out")
    (dqc, dkc, dvc), got = _mixc_bwd(s["qc"], s["kc"], s["vc"], doc, s["lse_c"], dlc, rider=riders[2])
    if scatter:
        landed["w_out"] = got[0]
    dqkva, dkr = _prep_a2_bwd(dqa, dka, dva, tabs[0])
    dlat = _matmul(dqkva, W["w_lat"], mode="nt", tm=1024, tn=1024, tk=2048, out_dtype=F32, name="mm_lat_dx")
    dw["w_lat"] = _matmul(s["lat"], dqkva, mode="tn", tm=1024, tn=2048, tk=2048, out_dtype=CDT, name="mm_lat_dw")
    dproj, dg["glat"], dg["gqn"], dg["gkn"] = _prep_bwd(
        s["proj"], G["glat"], G["gqn"], G["gkn"], tabs, dlat, dkr, dqb, dkb, dvb, dqc, dkc, dvc)
    dh1 = _matmul(dproj, W["w_in_t"], mode="nn", tm=1024, tn=1024, tk=2304, out_dtype=F32, name="mm_in_dx")
    dw["w_in_t"] = _matmul(dproj, s["h1"], mode="tn", tm=1536, tn=1024, tk=2048, out_dtype=CDT, name="mm_in_dw")
    dx0, dx0b, dg["ln1_g"] = _rms_bwd(s["x0"], G["ln1_g"], dh1, dx1, name="rms1_bwd")
    return dx0, dx0b, dw, dg, landed


def _layer_gains(l, ln1_g, g_q_a, g_kv_a, g_qn_b, g_kn_b, g_out, ln2_g):
    return {"ln1_g": ln1_g[l], "ln2_g": ln2_g[l], "g_out": g_out[l].reshape(1, 2048),
            "glat": jnp.concatenate([g_q_a[l], g_kv_a[l], jnp.zeros((LAT_W - KV_HI,), F32)]).reshape(1, LAT_W),
            "gqn": g_qn_b[l].reshape(1, 128), "gkn": g_kn_b[l].reshape(1, 128)}


def _gain_grads(dg):
    glat = dg["glat"].reshape(-1)
    return {"ln1_g": dg["ln1_g"].reshape(-1), "g_q_a": glat[0:KV_LO], "g_kv_a": glat[KV_LO:KV_HI],
            "g_qn_b": dg["gqn"].reshape(-1), "g_kn_b": dg["gkn"].reshape(-1), "g_out": dg["g_out"].reshape(-1),
            "ln2_g": dg["ln2_g"].reshape(-1)}


def _local_step(x, tgt, weights, gains, ln_f_g):
    S = x.shape[0]
    tabs = _rope_tables(S)
    depth = len(weights)
    saved = []
    for l in range(depth):
        x, s, _, _ = _layer_fwd(x, weights[l], gains[l], tabs)
        saved.append(s)
    loss, dx, dxb, dlnf = _loss_head(x, ln_f_g, tgt)
    dws, dgs = [None] * depth, [None] * depth
    for l in reversed(range(depth)):
        dx, dxb, dws[l], dgs[l], _ = _layer_bwd(dx, dxb, saved[l], weights[l], gains[l], tabs)
    return loss, dx, dws, dgs, dlnf


SMALL_SIZES = (("ln1_g", 2048), ("g_q_a", 448), ("g_kv_a", 512), ("g_qn_b", 128), ("g_kn_b", 128), ("g_out", 2048),
               ("ln2_g", 2048))


def _pack_small(per_layer, ln_f):
    flat = jnp.concatenate([per_layer[n].reshape(-1) for n, _ in SMALL_SIZES] + [ln_f.reshape(-1)])
    rows = -(-flat.shape[0] // (8 * LANES)) * 8
    return jnp.concatenate([flat, jnp.zeros((rows * LANES - flat.shape[0],), F32)]).reshape(rows, LANES)


def _unpack_small(packed, depth):
    flat, out, lo = packed.reshape(-1), {}, 0
    for n, w in SMALL_SIZES:
        out[n] = flat[lo:lo + depth * w].reshape(depth, w)
        lo += depth * w
    out["ln_f_g"] = flat[lo:lo + 2048]
    return out


def kernel(x, ln1_g, w_in, g_q_a, w_uq, g_kv_a, w_ukv, g_qn_b, g_kn_b, g_out, w_out, ln2_g, w_ff1, w_ff2, ln_f_g, loss_target, m_ln1_g, m_w_in, m_g_q_a, m_w_uq, m_g_kv_a, m_w_ukv, m_g_qn_b, m_g_kn_b, m_g_out, m_w_out, m_ln2_g, m_w_ff1, m_w_ff2, m_ln_f_g, v_ln1_g, v_w_in, v_g_q_a, v_w_uq, v_g_kv_a, v_w_ukv, v_g_qn_b, v_g_kn_b, v_g_out, v_w_out, v_ln2_g, v_w_ff1, v_w_ff2, v_ln_f_g):
    depth = w_in.shape[0]
    S = x.shape[1]
    big_w = {"w_in": w_in, "w_uq": w_uq, "w_ukv": w_ukv, "w_out": w_out, "w_ff1": w_ff1, "w_ff2": w_ff2}
    big_m = {"w_in": m_w_in, "w_uq": m_w_uq, "w_ukv": m_w_ukv, "w_out": m_w_out, "w_ff1": m_w_ff1, "w_ff2": m_w_ff2}
    big_v = {"w_in": v_w_in, "w_uq": v_w_uq, "w_ukv": v_w_ukv, "w_out": v_w_out, "w_ff1": v_w_ff1, "w_ff2": v_w_ff2}
    small_w = {"ln1_g": ln1_g, "g_q_a": g_q_a, "g_kv_a": g_kv_a, "g_qn_b": g_qn_b, "g_kn_b": g_kn_b, "g_out": g_out,
               "ln2_g": ln2_g}
    small_m = {"ln1_g": m_ln1_g, "g_q_a": m_g_q_a, "g_kv_a": m_g_kv_a, "g_qn_b": m_g_qn_b, "g_kn_b": m_g_kn_b,
               "g_out": m_g_out, "ln2_g": m_ln2_g}
    small_v = {"ln1_g": v_ln1_g, "g_q_a": v_g_q_a, "g_kv_a": v_g_kv_a, "g_qn_b": v_g_qn_b, "g_kn_b": v_g_kn_b,
               "g_out": v_g_out, "ln2_g": v_ln2_g}

    shards = [dict(zip(COMM, _comm_shards(*[big_w[n][l].astype(CDT) for n in BIG]))) for l in range(depth)]
    early_shards = [[sh["w_in_t"], sh["lat"]] for sh in shards]
    early = _run_rider(_ag_second_rider(_run_rider(_ag_first_rider(early_shards[0], "_early")), "_early"))
    gains = [_layer_gains(l, ln1_g, g_q_a, g_kv_a, g_qn_b, g_kn_b, g_out, ln2_g) for l in range(depth)]
    tabs = _rope_tables(S)

    h = x.reshape(S, D_MODEL)
    saved, weights = [], []
    for l in range(depth):
        W = dict(zip(("w_in_t", "w_lat"), _early_weights(*early)))
        h, s, W, early = _layer_fwd(h, W, gains[l], tabs,
                                    plan=(shards[l], early_shards[l + 1] if l + 1 < depth else None))
        saved.append(s)
        weights.append(W)
    loss_part, dx, dxb, dlnf = _loss_head(h, ln_f_g, loss_target.reshape(S, D_MODEL))
    loss = lax.psum(loss_part[0, 0], ("x", "y", "c"))

    dgs, landed, pending = [None] * depth, [None] * depth, None
    for l in reversed(range(depth)):
        dx, dxb, dw, dgs[l], landed[l] = _layer_bwd(dx, dxb, saved[l], weights[l], gains[l], tabs, scatter=True,
                                                    pending=pending)
        if pending is not None:
            landed[l + 1]["w_in_t"], landed[l + 1]["lat"] = landed[l].pop("above_w_in_t"), landed[l].pop("above_lat")
        pending = _early_by_destination(dw["w_in_t"], dw["w_lat"])
    got = _run_rider(_rs_sibling_rider(pending, "_early"))
    chip = [_rs_chip_sum(g, r, name="rs_chip_sum_" + n) for g, r, n in zip(pending, got, ("w_in_t", "lat"))]
    landed[0]["w_in_t"], landed[0]["lat"] = _run_rider(_rs_chip_rider(chip, "_early"))
    grad_x = dx.reshape(1, S, D_MODEL)

    shard_grads = [_from_comm_shards([_rs_final_sum(landed[l][n], name="rs_final_sum_" + n) for n in COMM])
                   for l in range(depth)]
    big_g = {n: jnp.stack([shard_grads[l][n] for l in range(depth)]) for n in BIG}

    named = [_gain_grads(dgs[l]) for l in range(depth)]
    per_layer = {n: jnp.stack([named[l][n] for l in range(depth)]) for n, _ in SMALL_SIZES}
    small_g = _unpack_small(_all_reduce_small(_pack_small(per_layer, dlnf.reshape(-1))), depth)

    upd = {}
    for n in BIG:
        shp = big_w[n].shape
        two_d = (shp[0] * shp[1], shp[2])
        d, nm, nv = _adamw(big_w[n].reshape(two_d), big_g[n].reshape(two_d), big_m[n].reshape(two_d),
                           big_v[n].reshape(two_d), name="adamw_" + n)
        upd[n] = (d.reshape(shp), nm.reshape(shp), nv.reshape(shp))
    small_w["ln_f_g"], small_m["ln_f_g"], small_v["ln_f_g"] = ln_f_g, m_ln_f_g, v_ln_f_g
    names_small = [n for n, _ in SMALL_SIZES]
    pw = _pack_small({n: small_w[n] for n in names_small}, small_w["ln_f_g"])
    pg = _pack_small({n: small_g[n] for n in names_small}, small_g["ln_f_g"])
    pm = _pack_small({n: small_m[n] for n in names_small}, small_m["ln_f_g"])
    pv = _pack_small({n: small_v[n] for n in names_small}, small_v["ln_f_g"])
    d, nm, nv = _adamw(pw, pg, pm, pv, name="adamw_small")
    sd, snm, snv = _unpack_small(d, depth), _unpack_small(nm, depth), _unpack_small(nv, depth)
    for n in names_small + ["ln_f_g"]:
        upd[n] = (sd[n], snm[n], snv[n])

    order = ["ln1_g", "w_in", "g_q_a", "w_uq", "g_kv_a", "w_ukv", "g_qn_b", "g_kn_b", "g_out", "w_out", "ln2_g", "w_ff1",
             "w_ff2", "ln_f_g"]
    grads = {**big_g, **small_g}
    return (loss, grad_x, *[grads[n] for n in order], *[upd[n][0] for n in order], *[upd[n][1] for n in order],
            *[upd[n][2] for n in order])
```

```python
import functools
import math

import jax
import jax.numpy as jnp
from jax import lax
from jax.experimental import pallas as pl
from jax.experimental.pallas import tpu as pltpu

D_MODEL = 2048
D_FF = 8192
EPS = 1e-6
NEG_INF = -1e30
Q_LORA = 448
ROPE_THETA = 10000.0
GRID_W = 64
DILATIONS = (1, 4, 16)
BAND_HALF = 64
SCALE_A = 1.0 / math.sqrt(192.0)
SCALE_H = 1.0 / math.sqrt(128.0)
ADAM_LR, ADAM_B1, ADAM_B2, ADAM_EPS, ADAM_WD, ADAM_STEP = 0.001, 0.9, 0.999, 1e-08, 0.01, 10

CDT = jnp.bfloat16
F32 = jnp.float32
LANES = 128
VMEM_LIMIT = 56 * 1024 * 1024

PROJ_W = 4608
LAT_W = 1024
KV_LO, KV_HI = 448, 960
OFF_BQ, OFF_BK, OFF_BV, OFF_CQ, OFF_CK, OFF_CV = 1024, 1792, 2048, 2304, 3072, 3840

NN = ((1,), (0,))
NT = ((1,), (1,))
TN = ((0,), (0,))

BIG = ("w_in", "w_uq", "w_ukv", "w_out", "w_ff1", "w_ff2")
COMM = ("w_in_t", "w_ff1_t", "w_out", "w_ff2", "lat")
COMM_SHAPE = {"w_in_t": (576, 2048), "w_ff1_t": (1024, 2048), "w_out": (256, 2048), "w_ff2": (1024, 2048),
              "lat": (848, 128)}
UQ_ROWS = 448 * 96 // LANES


def _dot(a, b, dims):
    return lax.dot_general(a, b, (dims, ((), ())), preferred_element_type=F32)


def _cparams(dims=None):
    return pltpu.CompilerParams(dimension_semantics=dims, vmem_limit_bytes=VMEM_LIMIT)


def _sds(shape, dtype):
    return jax.ShapeDtypeStruct(shape, dtype)


MESH = pl.DeviceIdType.MESH
ANY = pl.BlockSpec(memory_space=pl.ANY)


class _Rider:
    def __init__(self, name, arrays, out_shape, scratch, aliases, start, finish):
        self.name, self.arrays, self.out_shape, self.scratch = name, list(arrays), list(out_shape), list(scratch)
        self.aliases, self.start, self.finish = dict(aliases), start, finish


def _join(a, b):
    if a is None or b is None:
        return a if b is None else b
    na, oa, sa = len(a.arrays), len(a.out_shape), len(a.scratch)
    aliases = dict(a.aliases)
    aliases.update({na + i: oa + o for i, o in b.aliases.items()})

    def start(ins, outs, sems):
        a.start(ins[:na], outs[:oa], sems[:sa])
        b.start(ins[na:], outs[oa:], sems[sa:])

    def finish(ins, outs, sems):
        a.finish(ins[:na], outs[:oa], sems[:sa])
        b.finish(ins[na:], outs[oa:], sems[sa:])

    return _Rider(a.name + "_" + b.name, a.arrays + b.arrays, a.out_shape + b.out_shape, a.scratch + b.scratch,
                  aliases, start, finish)


def _carried_call(body, *, name, grid, in_specs, out_specs, out_shape, scratch_shapes, dims, args, rider):
    in_specs, out_specs, out_shape = list(in_specs), list(out_specs), list(out_shape)
    scratch_shapes = list(scratch_shapes)
    if rider is None:
        res = pl.pallas_call(body, name=name, grid=grid, in_specs=in_specs, out_specs=out_specs, out_shape=out_shape,
                             scratch_shapes=scratch_shapes, compiler_params=_cparams(dims))(*args)
        return list(res), []
    n_in, n_out, n_scr = len(in_specs), len(out_specs), len(scratch_shapes)
    r_in, r_out = len(rider.arrays), len(rider.out_shape)

    def wrapped(*refs):
        o0 = n_in + r_in
        s0 = o0 + n_out + r_out
        ins, outs, sems = refs[n_in:o0], refs[o0 + n_out:s0], refs[s0 + n_scr:]
        ids = [pl.program_id(a) for a in range(len(grid))]
        first = functools.reduce(jnp.logical_and, [i == 0 for i in ids])
        last = functools.reduce(jnp.logical_and, [i == g - 1 for i, g in zip(ids, grid)])

        @pl.when(first)
        def _():
            rider.start(ins, outs, sems)

        body(*refs[:n_in], *refs[o0:o0 + n_out], *refs[s0:s0 + n_scr])

        @pl.when(last)
        def _():
            rider.finish(ins, outs, sems)

    res = pl.pallas_call(
        wrapped, name=name + "_" + rider.name, grid=grid, in_specs=in_specs + [ANY] * r_in,
        out_specs=out_specs + [ANY] * r_out, out_shape=out_shape + rider.out_shape,
        scratch_shapes=scratch_shapes + rider.scratch,
        input_output_aliases={n_in + i: n_out + o for i, o in rider.aliases.items()},
        compiler_params=_cparams(("arbitrary",) * len(grid)),
    )(*args, *rider.arrays)
    return list(res[:n_out]), list(res[n_out:])


def _run_rider(rider):
    def body(*refs):
        r_in, r_out = len(rider.arrays), len(rider.out_shape)
        ins, outs, sems = refs[:r_in], refs[r_in:r_in + r_out], refs[r_in + r_out:]
        rider.start(ins, outs, sems)
        rider.finish(ins, outs, sems)

    res = pl.pallas_call(
        body, name=rider.name, in_specs=[ANY] * len(rider.arrays), out_specs=[ANY] * len(rider.out_shape),
        out_shape=rider.out_shape, scratch_shapes=rider.scratch, input_output_aliases=rider.aliases,
    )(*rider.arrays)
    return list(res)


def _mesh_place():
    x, y, c = lax.axis_index("x"), lax.axis_index("y"), lax.axis_index("c")
    return x, y, c, [(1 - x, y), (x, 1 - y), (1 - x, 1 - y)]


def _remote(src, dst, send, recv, dev):
    return pltpu.make_async_remote_copy(src_ref=src, dst_ref=dst, send_sem=send, recv_sem=recv, device_id=dev,
                                        device_id_type=MESH)


def _ag_first_rider(shards, tag):
    n = len(shards)

    def copies(ins, outs, sems):
        send, recv, _ = sems
        x, y, c, chips = _mesh_place()
        me = 4 * x + 2 * y + c
        peers = [(x, y, 1 - c)] + [(cx, cy, c) for cx, cy in chips]
        out = []
        for t in range(n):
            for k, dev in enumerate(peers):
                theirs = 4 * dev[0] + 2 * dev[1] + dev[2]
                out.append((_remote(ins[t], outs[t].at[me], send.at[t, k], recv.at[t, k], dev),
                            _remote(ins[t], outs[t].at[theirs], send.at[t, k], recv.at[t, k], dev)))
        mine = [pltpu.make_async_copy(ins[t], outs[t].at[me], sems[2].at[t]) for t in range(n)]
        return out, mine

    def start(ins, outs, sems):
        pairs, mine = copies(ins, outs, sems)
        for cp in mine:
            cp.start()
        for snd, _ in pairs:
            snd.start()

    def finish(ins, outs, sems):
        pairs, mine = copies(ins, outs, sems)
        for _, rcv in pairs:
            rcv.wait_recv()
        for snd, _ in pairs:
            snd.wait_send()
        for cp in mine:
            cp.wait()

    return _Rider("ag1" + tag, shards, [_sds((8,) + s.shape, s.dtype) for s in shards],
                  [pltpu.SemaphoreType.DMA((n, 4)), pltpu.SemaphoreType.DMA((n, 4)), pltpu.SemaphoreType.DMA((n,))],
                  {}, start, finish)


def _ag_second_rider(gathered, tag):
    n = len(gathered)

    def copies(ins, outs, sems):
        send, recv = sems
        x, y, c, chips = _mesh_place()
        out = []
        for t in range(n):
            for j, (cx, cy) in enumerate(chips):
                here, there = 4 * cx + 2 * cy + c, 4 * cx + 2 * cy + (1 - c)
                out.append((_remote(ins[t].at[here], outs[t].at[here], send.at[t, j], recv.at[t, j], (x, y, 1 - c)),
                            _remote(ins[t].at[here], outs[t].at[there], send.at[t, j], recv.at[t, j], (x, y, 1 - c))))
        return out

    def start(ins, outs, sems):
        for snd, _ in copies(ins, outs, sems):
            snd.start()

    def finish(ins, outs, sems):
        pairs = copies(ins, outs, sems)
        for _, rcv in pairs:
            rcv.wait_recv()
        for snd, _ in pairs:
            snd.wait_send()

    return _Rider("ag2" + tag, gathered, [_sds(g.shape, g.dtype) for g in gathered],
                  [pltpu.SemaphoreType.DMA((n, 3)), pltpu.SemaphoreType.DMA((n, 3))],
                  {t: t for t in range(n)}, start, finish)


def _rs_sibling_rider(gs, tag):
    n = len(gs)

    def copies(ins, outs, sems):
        send, recv = sems
        x, y, c, _ = _mesh_place()
        return [_remote(ins[t].at[k, 1 - c], outs[t].at[k], send.at[t, k], recv.at[t, k], (x, y, 1 - c))
                for t in range(n) for k in range(4)]

    def start(ins, outs, sems):
        for cp in copies(ins, outs, sems):
            cp.start()

    def finish(ins, outs, sems):
        for cp in copies(ins, outs, sems):
            cp.wait()

    return _Rider("rs1" + tag, gs, [_sds((4,) + g.shape[2:], g.dtype) for g in gs],
                  [pltpu.SemaphoreType.DMA((n, 4)), pltpu.SemaphoreType.DMA((n, 4))], {}, start, finish)


def _rs_chip_rider(ps, tag):
    n = len(ps)

    def copies(ins, outs, sems):
        send, recv, local = sems
        x, y, c, chips = _mesh_place()
        my_chip = 2 * x + y
        out = []
        for t in range(n):
            for j, (cx, cy) in enumerate(chips):
                dev = (cx, cy, c)
                out.append((_remote(ins[t].at[2 * cx + cy], outs[t].at[my_chip], send.at[t, j], recv.at[t, j], dev),
                            _remote(ins[t].at[my_chip], outs[t].at[2 * cx + cy], send.at[t, j], recv.at[t, j], dev)))
        mine = [pltpu.make_async_copy(ins[t].at[my_chip], outs[t].at[my_chip], local.at[t]) for t in range(n)]
        return out, mine

    def start(ins, outs, sems):
        pairs, mine = copies(ins, outs, sems)
        for cp in mine:
            cp.start()
        for snd, _ in pairs:
            snd.start()

    def finish(ins, outs, sems):
        pairs, mine = copies(ins, outs, sems)
        for _, rcv in pairs:
            rcv.wait_recv()
        for snd, _ in pairs:
            snd.wait_send()
        for cp in mine:
            cp.wait()

    return _Rider("rs2" + tag, ps, [_sds(p.shape, p.dtype) for p in ps],
                  [pltpu.SemaphoreType.DMA((n, 3)), pltpu.SemaphoreType.DMA((n, 3)), pltpu.SemaphoreType.DMA((n,))],
                  {}, start, finish)


def _matmul(a, b, *, mode, tm, tn, tk, out_dtype, name, epi=None, extra=None, rider=None):
    if mode == "nn":
        (M, K), (K2, N) = a.shape, b.shape
    elif mode == "nt":
        (M, K), (N, K2) = a.shape, b.shape
    else:
        (K, M), (K2, N) = a.shape, b.shape
    tm, tn, tk = min(tm, M), min(tn, N), min(tk, K)
    assert K == K2 and M % tm == 0 and N % tn == 0 and K % tk == 0, (name, a.shape, b.shape)
    nk = K // tk
    dims = {"nn": NN, "nt": NT, "tn": TN}[mode]
    if mode == "tn":
        a_spec = pl.BlockSpec((tk, tm), lambda i, j, k: (k, i))
    else:
        a_spec = pl.BlockSpec((tm, tk), lambda i, j, k: (i, k))
    if mode == "nt":
        b_spec = pl.BlockSpec((tn, tk), lambda i, j, k: (j, k))
    else:
        b_spec = pl.BlockSpec((tk, tn), lambda i, j, k: (k, j))
    tile = pl.BlockSpec((tm, tn), lambda i, j, k: (i, j))
    n_extra = 1 if epi in ("residual", "drelu2") else 0
    n_out = 2 if epi == "relu2" else 1

    def body(*refs):
        a_ref, b_ref = refs[0], refs[1]
        extra_refs = refs[2:2 + n_extra]
        out_refs = refs[2 + n_extra:2 + n_extra + n_out]

        def finish(acc):
            if epi is None:
                out_refs[0][...] = acc.astype(out_dtype)
            elif epi == "residual":
                out_refs[0][...] = (extra_refs[0][...] + acc).astype(out_dtype)
            elif epi == "relu2":
                out_refs[0][...] = acc.astype(out_dtype)
                r = jnp.maximum(acc, 0.0)
                out_refs[1][...] = (r * r).astype(out_dtype)
            else:
                z = extra_refs[0][...].astype(F32)
                out_refs[0][...] = (acc * (2.0 * jnp.maximum(z, 0.0))).astype(out_dtype)

        part = _dot(a_ref[...], b_ref[...], dims)
        if nk == 1:
            finish(part)
        else:
            acc_ref = refs[-1]
            k = pl.program_id(2)

            @pl.when(k == 0)
            def _():
                acc_ref[...] = part

            @pl.when(k > 0)
            def _():
                acc_ref[...] += part

            @pl.when(k == nk - 1)
            def _():
                finish(acc_ref[...])

    res, carried = _carried_call(
        body, name=name, grid=(M // tm, N // tn, nk), in_specs=[a_spec, b_spec] + [tile] * n_extra,
        out_specs=[tile] * n_out, out_shape=[_sds((M, N), out_dtype)] * n_out,
        scratch_shapes=[pltpu.VMEM((tm, tn), F32)] if nk > 1 else [],
        dims=("parallel", "parallel", "arbitrary"), args=[a, b] + ([extra] if n_extra else []), rider=rider)
    res = res if n_out > 1 else res[0]
    return res if rider is None else (res, carried)


def _rms_val(x, g, n):
    r = lax.rsqrt(jnp.sum(x * x, axis=-1, keepdims=True) * (1.0 / n) + EPS)
    y = x * r
    return (y if g is None else y * g), r


def _rms_bwd_val(x, g, dy, n):
    r = lax.rsqrt(jnp.sum(x * x, axis=-1, keepdims=True) * (1.0 / n) + EPS)
    xhat = x * r
    dyg = dy if g is None else dy * g
    dx = r * (dyg - xhat * (jnp.sum(dyg * xhat, axis=-1, keepdims=True) * (1.0 / n)))
    return dx, dy * xhat


def _rope_val(x, c, sa, sb, shift):
    return x * c + pltpu.roll(x, LANES - shift, 1) * sa + pltpu.roll(x, shift, 1) * sb


def _rope_t_val(dy, c, sa, sb, shift):
    return dy * c + pltpu.roll(dy * sa, shift, 1) + pltpu.roll(dy * sb, LANES - shift, 1)


def _colsum(x):
    return jnp.sum(x, axis=0, keepdims=True)


def _rope_tables(S):
    pos = lax.broadcasted_iota(jnp.int32, (S, LANES), 0)
    lane = lax.broadcasted_iota(jnp.int32, (S, LANES), 1)

    def tables(p, dim, active):
        half = dim // 2
        inv = jnp.power(ROPE_THETA, -(2 * (lane % half)).astype(F32) / dim)
        a = p.astype(F32) * inv
        first = (lane % dim) < half
        zero = jnp.zeros((S, LANES), F32)
        return (jnp.where(active, jnp.cos(a), zero), jnp.where(active & first, -jnp.sin(a), zero),
                jnp.where(active & ~first, jnp.sin(a), zero))

    tab_a = tables(pos, 64, lane >= 64)
    tab_b = tables(jnp.where(lane < 64, pos // GRID_W, pos % GRID_W), 64, lane >= 0)
    tab_c = tables(pos, 128, lane >= 0)
    return tab_a, tab_b, tab_c


ROPE_SHIFT_AB = 32
ROPE_SHIFT_C = 64


def _rms_fwd(x, g, *, name, tr=512):
    S, W = x.shape
    tr = min(tr, S)

    def body(x_ref, g_ref, o_ref):
        y, _ = _rms_val(x_ref[...], g_ref[...], W)
        o_ref[...] = y.astype(CDT)

    return pl.pallas_call(
        body, name=name, grid=(S // tr,),
        in_specs=[pl.BlockSpec((tr, W), lambda i: (i, 0)), pl.BlockSpec((1, W), lambda i: (0, 0))],
        out_specs=pl.BlockSpec((tr, W), lambda i: (i, 0)), out_shape=_sds((S, W), CDT),
        compiler_params=_cparams(("parallel",)),
    )(x, g.reshape(1, W))


def _rms_bwd(x, g, dy, res, *, name, tr=256):
    S, W = x.shape
    tr = min(tr, S)

    def body(x_ref, g_ref, dy_ref, res_ref, dx_ref, dxb_ref, dg_ref):
        dx, dgt = _rms_bwd_val(x_ref[...], g_ref[...], dy_ref[...], W)
        dx = res_ref[...] + dx
        dx_ref[...] = dx
        dxb_ref[...] = dx.astype(CDT)

        @pl.when(pl.program_id(0) == 0)
        def _():
            dg_ref[...] = jnp.zeros_like(dg_ref)

        dg_ref[...] += _colsum(dgt)

    row = pl.BlockSpec((tr, W), lambda i: (i, 0))
    vec = pl.BlockSpec((1, W), lambda i: (0, 0))
    return pl.pallas_call(
        body, name=name, grid=(S // tr,),
        in_specs=[row, vec, row, row], out_specs=[row, row, vec],
        out_shape=[_sds((S, W), F32), _sds((S, W), CDT), _sds((1, W), F32)],
        compiler_params=_cparams(("arbitrary",)),
    )(x, g.reshape(1, W), dy, res)


def _loss_head(x, g, tgt, *, tr=256):
    S, W = x.shape
    tr = min(tr, S)

    def body(x_ref, g_ref, t_ref, loss_ref, dx_ref, dxb_ref, dg_ref):
        xv, gv = x_ref[...], g_ref[...]
        y, _ = _rms_val(xv, gv, W)
        err = y - t_ref[...]
        part = 0.5 * jnp.sum(jnp.sum(err * err, axis=-1, keepdims=True) * (1.0 / W), axis=0, keepdims=True)
        dx, dgt = _rms_bwd_val(xv, gv, err * (1.0 / W), W)
        dx_ref[...] = dx
        dxb_ref[...] = dx.astype(CDT)

        @pl.when(pl.program_id(0) == 0)
        def _():
            dg_ref[...] = jnp.zeros_like(dg_ref)
            loss_ref[...] = jnp.zeros_like(loss_ref)

        dg_ref[...] += _colsum(dgt)
        loss_ref[...] += jnp.broadcast_to(part, (1, LANES))

    row = pl.BlockSpec((tr, W), lambda i: (i, 0))
    vec = pl.BlockSpec((1, W), lambda i: (0, 0))
    return pl.pallas_call(
        body, name="loss_head", grid=(S // tr,),
        in_specs=[row, vec, row], out_specs=[pl.BlockSpec((1, LANES), lambda i: (0, 0)), row, row, vec],
        out_shape=[_sds((1, LANES), F32), _sds((S, W), F32), _sds((S, W), CDT), _sds((1, W), F32)],
        compiler_params=_cparams(("arbitrary",)),
    )(x, g.reshape(1, W), tgt)


def _tab_specs(tr):
    return [pl.BlockSpec((tr, LANES), lambda i: (i, 0))] * 9


def _lat_masks(shape):
    lane = lax.broadcasted_iota(jnp.int32, shape, 1)
    return lane < KV_LO, (lane >= KV_LO) & (lane < KV_HI)


def _prep_fwd(proj, glat, gqn, gkn, tabs, *, tr=256):
    S = proj.shape[0]
    tr = min(tr, S)

    def body(p_ref, glat_ref, gqn_ref, gkn_ref, ac, aa, ab, bc, ba, bb, cc, ca, cb,
             lat_ref, kpe_ref, qb_ref, kb_ref, vb_ref, qc_ref, kc_ref, vc_ref):
        x = p_ref[:, 0:LAT_W]
        is_q, is_kv = _lat_masks(x.shape)
        yq, _ = _rms_val(jnp.where(is_q, x, 0.0), None, Q_LORA)
        ykv, _ = _rms_val(jnp.where(is_kv, x, 0.0), None, KV_HI - KV_LO)
        lat_ref[...] = ((yq + ykv) * glat_ref[...]).astype(CDT)
        kpe_ref[...] = _rope_val(x[:, LAT_W - LANES:LAT_W], ac[...], aa[...], ab[...], ROPE_SHIFT_AB).astype(CDT)
        for h in range(6):
            xh = p_ref[:, OFF_BQ + LANES * h:OFF_BQ + LANES * (h + 1)]
            y = _rope_val(_rms_val(xh, gqn_ref[...], LANES)[0], bc[...], ba[...], bb[...], ROPE_SHIFT_AB)
            qb_ref[:, LANES * h:LANES * (h + 1)] = y.astype(CDT)
        for h in range(2):
            xh = p_ref[:, OFF_BK + LANES * h:OFF_BK + LANES * (h + 1)]
            y = _rope_val(_rms_val(xh, gkn_ref[...], LANES)[0], bc[...], ba[...], bb[...], ROPE_SHIFT_AB)
            kb_ref[:, LANES * h:LANES * (h + 1)] = y.astype(CDT)
        vb_ref[...] = p_ref[:, OFF_BV:OFF_BV + 256].astype(CDT)
        for h in range(6):
            sl = slice(LANES * h, LANES * (h + 1))
            qc_ref[:, sl] = _rope_val(p_ref[:, OFF_CQ + LANES * h:OFF_CQ + LANES * (h + 1)], cc[...], ca[...], cb[...],
                                      ROPE_SHIFT_C)
            kc_ref[:, sl] = _rope_val(p_ref[:, OFF_CK + LANES * h:OFF_CK + LANES * (h + 1)], cc[...], ca[...], cb[...],
                                      ROPE_SHIFT_C)
        vc_ref[...] = p_ref[:, OFF_CV:OFF_CV + 768]

    vec = lambda w: pl.BlockSpec((1, w), lambda i: (0, 0))
    row = lambda w: pl.BlockSpec((tr, w), lambda i: (i, 0))
    return pl.pallas_call(
        body, name="prep_fwd", grid=(S // tr,),
        in_specs=[row(PROJ_W), vec(LAT_W), vec(128), vec(128)] + _tab_specs(tr),
        out_specs=[row(LAT_W), row(128), row(768), row(256), row(256), row(768), row(768), row(768)],
        out_shape=[_sds((S, LAT_W), CDT), _sds((S, 128), CDT), _sds((S, 768), CDT), _sds((S, 256), CDT),
                   _sds((S, 256), CDT), _sds((S, 768), F32), _sds((S, 768), F32), _sds((S, 768), F32)],
        compiler_params=_cparams(("parallel",)),
    )(proj, glat, gqn, gkn, *tabs[0], *tabs[1], *tabs[2])


def _prep_a2_fwd(qkva, kpe, tab_a, *, tr=512):
    S = qkva.shape[0]
    tr = min(tr, S)

    def body(x_ref, kpe_ref, ac, aa, ab, qa_ref, ka_ref, va_ref):
        for h in range(4):
            lo, hi = 2 * LANES * h, 2 * LANES * h + LANES
            qa_ref[:, lo:hi] = x_ref[:, lo:hi].astype(CDT)
            qa_ref[:, hi:hi + LANES] = _rope_val(x_ref[:, hi:hi + LANES], ac[...], aa[...], ab[...],
                                                 ROPE_SHIFT_AB).astype(CDT)
            ka_ref[:, lo:hi] = x_ref[:, 1024 + LANES * h:1024 + LANES * (h + 1)].astype(CDT)
            ka_ref[:, hi:hi + LANES] = kpe_ref[...]
        va_ref[...] = x_ref[:, 1536:2048].astype(CDT)

    row = lambda w: pl.BlockSpec((tr, w), lambda i: (i, 0))
    return pl.pallas_call(
        body, name="prep_a2_fwd", grid=(S // tr,),
        in_specs=[row(2048), row(128)] + _tab_specs(tr)[:3],
        out_specs=[row(1024), row(1024), row(512)],
        out_shape=[_sds((S, 1024), CDT), _sds((S, 1024), CDT), _sds((S, 512), CDT)],
        compiler_params=_cparams(("parallel",)),
    )(qkva, kpe, *tab_a)


def _prep_a2_bwd(dqa, dka, dva, tab_a, *, tr=512):
    S = dqa.shape[0]
    tr = min(tr, S)

    def body(dq_ref, dk_ref, dv_ref, ac, aa, ab, dx_ref, dkr_ref):
        dkpe = jnp.zeros((tr, LANES), F32)
        for h in range(4):
            lo, hi = 2 * LANES * h, 2 * LANES * h + LANES
            dx_ref[:, lo:hi] = dq_ref[:, lo:hi].astype(CDT)
            dx_ref[:, hi:hi + LANES] = _rope_t_val(dq_ref[:, hi:hi + LANES], ac[...], aa[...], ab[...],
                                                   ROPE_SHIFT_AB).astype(CDT)
            dx_ref[:, 1024 + LANES * h:1024 + LANES * (h + 1)] = dk_ref[:, lo:hi].astype(CDT)
            dkpe = dkpe + dk_ref[:, hi:hi + LANES]
        dx_ref[:, 1536:2048] = dv_ref[...].astype(CDT)
        dkr_ref[...] = _rope_t_val(dkpe, ac[...], aa[...], ab[...], ROPE_SHIFT_AB)

    row = lambda w: pl.BlockSpec((tr, w), lambda i: (i, 0))
    return pl.pallas_call(
        body, name="prep_a2_bwd", grid=(S // tr,),
        in_specs=[row(1024), row(1024), row(512)] + _tab_specs(tr)[:3],
        out_specs=[row(2048), row(128)],
        out_shape=[_sds((S, 2048), CDT), _sds((S, 128), F32)],
        compiler_params=_cparams(("parallel",)),
    )(dqa, dka, dva, *tab_a)


def _prep_bwd(proj, glat, gqn, gkn, tabs, dlat, dkr, dqb, dkb, dvb, dqc, dkc, dvc, *, tr=256):
    S = proj.shape[0]
    tr = min(tr, S)

    def body(p_ref, glat_ref, gqn_ref, gkn_ref, ac, aa, ab, bc, ba, bb, cc, ca, cb,
             dlat_ref, dkr_ref, dqb_ref, dkb_ref, dvb_ref, dqc_ref, dkc_ref, dvc_ref,
             dp_ref, dglat_ref, dgqn_ref, dgkn_ref):
        @pl.when(pl.program_id(0) == 0)
        def _():
            dglat_ref[...] = jnp.zeros_like(dglat_ref)
            dgqn_ref[...] = jnp.zeros_like(dgqn_ref)
            dgkn_ref[...] = jnp.zeros_like(dgkn_ref)

        x = p_ref[:, 0:LAT_W]
        is_q, is_kv = _lat_masks(x.shape)
        dy, g = dlat_ref[...], glat_ref[...]
        dxq, dgq = _rms_bwd_val(jnp.where(is_q, x, 0.0), g, jnp.where(is_q, dy, 0.0), Q_LORA)
        dxkv, dgkv = _rms_bwd_val(jnp.where(is_kv, x, 0.0), g, jnp.where(is_kv, dy, 0.0), KV_HI - KV_LO)
        dglat_ref[...] += _colsum(dgq + dgkv)
        dx = dxq + dxkv
        dp_ref[:, 0:LAT_W - LANES] = dx[:, 0:LAT_W - LANES].astype(CDT)
        dp_ref[:, LAT_W - LANES:LAT_W] = (dx[:, LAT_W - LANES:LAT_W] + dkr_ref[...]).astype(CDT)
        dgqn = jnp.zeros((1, LANES), F32)
        for h in range(6):
            sl = slice(LANES * h, LANES * (h + 1))
            po = slice(OFF_BQ + LANES * h, OFF_BQ + LANES * (h + 1))
            dyh = _rope_t_val(dqb_ref[:, sl], bc[...], ba[...], bb[...], ROPE_SHIFT_AB)
            dxh, dgt = _rms_bwd_val(p_ref[:, po], gqn_ref[...], dyh, LANES)
            dp_ref[:, po] = dxh.astype(CDT)
            dgqn = dgqn + _colsum(dgt)
        dgqn_ref[...] += dgqn
        dgkn = jnp.zeros((1, LANES), F32)
        for h in range(2):
            sl = slice(LANES * h, LANES * (h + 1))
            po = slice(OFF_BK + LANES * h, OFF_BK + LANES * (h + 1))
            dyh = _rope_t_val(dkb_ref[:, sl], bc[...], ba[...], bb[...], ROPE_SHIFT_AB)
            dxh, dgt = _rms_bwd_val(p_ref[:, po], gkn_ref[...], dyh, LANES)
            dp_ref[:, po] = dxh.astype(CDT)
            dgkn = dgkn + _colsum(dgt)
        dgkn_ref[...] += dgkn
        dp_ref[:, OFF_BV:OFF_BV + 256] = dvb_ref[...].astype(CDT)
        for h in range(6):
            sl = slice(LANES * h, LANES * (h + 1))
            dp_ref[:, OFF_CQ + LANES * h:OFF_CQ + LANES * (h + 1)] = _rope_t_val(
                dqc_ref[:, sl], cc[...], ca[...], cb[...], ROPE_SHIFT_C).astype(CDT)
            dp_ref[:, OFF_CK + LANES * h:OFF_CK + LANES * (h + 1)] = _rope_t_val(
                dkc_ref[:, sl], cc[...], ca[...], cb[...], ROPE_SHIFT_C).astype(CDT)
        dp_ref[:, OFF_CV:OFF_CV + 768] = dvc_ref[...].astype(CDT)

    vec = lambda w: pl.BlockSpec((1, w), lambda i: (0, 0))
    row = lambda w: pl.BlockSpec((tr, w), lambda i: (i, 0))
    return pl.pallas_call(
        body, name="prep_bwd", grid=(S // tr,),
        in_specs=[row(PROJ_W), vec(LAT_W), vec(128), vec(128)] + _tab_specs(tr)
        + [row(LAT_W), row(128), row(768), row(256), row(256), row(768), row(768), row(768)],
        out_specs=[row(PROJ_W), vec(LAT_W), vec(128), vec(128)],
        out_shape=[_sds((S, PROJ_W), CDT), _sds((1, LAT_W), F32), _sds((1, 128), F32), _sds((1, 128), F32)],
        compiler_params=_cparams(("arbitrary",)),
    )(proj, glat, gqn, gkn, *tabs[0], *tabs[1], *tabs[2], dlat, dkr, dqb, dkb, dvb, dqc, dkc, dvc)


ATTN_TK = 512
LOG2E = 1.4426950408889634
C2_H = SCALE_H * LOG2E


def _attn_fwd(q, k, v, *, H, G, dk, dv, scale, name, tq=512, rider=None):
    S = q.shape[0]
    tq = min(tq, S)
    tk = min(ATTN_TK, S)
    c2 = scale * LOG2E

    half = tq // 2

    def body(q_ref, k_ref, v_ref, o_ref, l_ref):
        chunks = [pl.ds(c * tk, tk) for c in range(S // tk)]
        qa, qb = q_ref[0:half, :], q_ref[half:tq, :]
        rowmax = lambda s: functools.reduce(jnp.maximum, [jnp.max(sc, axis=-1, keepdims=True) for sc in s])
        s_a = [_dot(qa, k_ref[rows, :], NT) for rows in chunks]
        m_a = rowmax(s_a)
        e_a, s_b = [], []
        for c, rows in enumerate(chunks):
            e_a.append(jnp.exp2((s_a[c] - m_a) * c2))
            s_b.append(_dot(qb, k_ref[rows, :], NT))
        m_b = rowmax(s_b)
        den_a, den_b = jnp.zeros((half, 1), F32), jnp.zeros((half, 1), F32)
        acc_a, acc_b = jnp.zeros((half, dv), F32), jnp.zeros((half, dv), F32)
        e_b = []
        for c, rows in enumerate(chunks):
            acc_a = acc_a + _dot(e_a[c].astype(CDT), v_ref[rows, :], NN)
            den_a = den_a + jnp.sum(e_a[c], axis=-1, keepdims=True)
            e_b.append(jnp.exp2((s_b[c] - m_b) * c2))
        for c, rows in enumerate(chunks):
            acc_b = acc_b + _dot(e_b[c].astype(CDT), v_ref[rows, :], NN)
            den_b = den_b + jnp.sum(e_b[c], axis=-1, keepdims=True)
        o_ref[0:half, :] = acc_a * (1.0 / den_a)
        o_ref[half:tq, :] = acc_b * (1.0 / den_b)
        l_ref[0:half, :] = jnp.broadcast_to(m_a * scale + jnp.log(den_a), (half, LANES))
        l_ref[half:tq, :] = jnp.broadcast_to(m_b * scale + jnp.log(den_b), (half, LANES))

    return _carried_call(
        body, name=name, grid=(H, S // tq),
        in_specs=[pl.BlockSpec((tq, dk), lambda h, i: (i, h)), pl.BlockSpec((S, dk), lambda h, i: (0, h // G)),
                  pl.BlockSpec((S, dv), lambda h, i: (0, h // G))],
        out_specs=[pl.BlockSpec((tq, dv), lambda h, i: (i, h)), pl.BlockSpec((tq, LANES), lambda h, i: (i, h))],
        out_shape=[_sds((S, H * dv), F32), _sds((S, H * LANES), F32)], scratch_shapes=[],
        dims=("parallel", "parallel"), args=[q, k, v], rider=rider)


def _attn_bwd(q, k, v, do, lse, delta, *, H, G, dk, dv, scale, name, tq=256, rider=None):
    S = q.shape[0]
    tq = min(tq, S)
    Hkv = H // G
    c2 = scale * LOG2E

    def body(q_ref, k_ref, v_ref, do_ref, l_ref, d_ref, dq_ref, dk_ref, dv_ref):
        @pl.when((pl.program_id(1) == 0) & (pl.program_id(2) == 0))
        def _():
            dk_ref[...] = jnp.zeros_like(dk_ref)
            dv_ref[...] = jnp.zeros_like(dv_ref)

        qv, kv, dov = q_ref[...], k_ref[...], do_ref[...]
        p = jnp.exp2(_dot(qv, kv, NT) * c2 - l_ref[:, 0:1] * LOG2E)
        dp = _dot(dov, v_ref[...], NT)
        ds = (p * (dp - d_ref[:, 0:1]) * scale).astype(CDT)
        dq_ref[...] = _dot(ds, kv, NN)
        dk_ref[...] += _dot(ds, qv, TN)
        dv_ref[...] += _dot(p.astype(CDT), dov, TN)

    qi = lambda hk, g, i: (i, hk * G + g)
    return _carried_call(
        body, name=name, grid=(Hkv, G, S // tq),
        in_specs=[pl.BlockSpec((tq, dk), qi), pl.BlockSpec((S, dk), lambda hk, g, i: (0, hk)),
                  pl.BlockSpec((S, dv), lambda hk, g, i: (0, hk)), pl.BlockSpec((tq, dv), qi),
                  pl.BlockSpec((tq, LANES), qi), pl.BlockSpec((tq, LANES), qi)],
        out_specs=[pl.BlockSpec((tq, dk), qi), pl.BlockSpec((S, dk), lambda hk, g, i: (0, hk)),
                   pl.BlockSpec((S, dv), lambda hk, g, i: (0, hk))],
        out_shape=[_sds((S, H * dk), F32), _sds((S, Hkv * dk), F32), _sds((S, Hkv * dv), F32)], scratch_shapes=[],
        dims=("parallel", "arbitrary", "arbitrary"), args=[q, k, v, do, lse, delta], rider=rider)


BAND_SUB = 128
BAND_WIN = 384
BAND_UNROLL = 8


def _band_blocks(S, d):
    L = S // d
    assert L % BAND_SUB == 0 and S % (BAND_SUB * BAND_UNROLL) == 0
    return L, L // BAND_SUB, min(BAND_WIN, L)


def _band_index(blk, d, nb, L, win):
    r, jb = blk // nb, blk % nb
    l0 = jb * BAND_SUB
    w0 = jnp.clip(l0 - BAND_SUB, 0, L - win)
    return r + d * l0, r + d * w0, l0, w0


def _band_rows(start, size, d):
    return pl.ds(pl.multiple_of(start, BAND_SUB), size) if d == 1 else pl.ds(start, size, stride=d)


def _band_mask(l0, w0, win):
    rpos = l0 + lax.broadcasted_iota(jnp.int32, (BAND_SUB, win), 0)
    cpos = w0 + lax.broadcasted_iota(jnp.int32, (BAND_SUB, win), 1)
    return jnp.abs(rpos - cpos) <= BAND_HALF


def _mixc_fwd(q, k, v, rider=None):
    S, W = q.shape

    def body(q_ref, k_ref, v_ref, o_ref, l_ref, *scratch):
        ob_refs, lb_refs = scratch[0:3], scratch[3:6]
        for b, d in enumerate(DILATIONS):
            L, nb, win = _band_blocks(S, d)

            def step(it, carry, b=b, d=d, L=L, nb=nb, win=win):
                idx = [_band_index(it * BAND_UNROLL + u, d, nb, L, win) for u in range(BAND_UNROLL)]
                qrows = [_band_rows(i[0], BAND_SUB, d) for i in idx]
                krows = [_band_rows(i[1], win, d) for i in idx]
                qv = [q_ref[r, :].astype(CDT) for r in qrows]
                kw = [k_ref[r, :].astype(CDT) for r in krows]
                vw = [v_ref[r, :].astype(CDT) for r in krows]
                s = [jnp.where(_band_mask(i[2], i[3], win), _dot(a, kk, NT), NEG_INF) for i, a, kk in zip(idx, qv, kw)]
                m = [jnp.max(x, axis=-1, keepdims=True) for x in s]
                e = [jnp.exp2((x - mm) * C2_H) for x, mm in zip(s, m)]
                den = [jnp.sum(x, axis=-1, keepdims=True) for x in e]
                o = [_dot((x * (1.0 / dd)).astype(CDT), vv, NN) for x, dd, vv in zip(e, den, vw)]
                for r, ou, mm, dd in zip(qrows, o, m, den):
                    ob_refs[b][r, :] = ou
                    lb_refs[b][r, :] = jnp.broadcast_to(mm * SCALE_H + jnp.log(dd), (BAND_SUB, LANES))
                return carry

            lax.fori_loop(0, S // (BAND_SUB * BAND_UNROLL), step, 0)

        def combine(c, carry):
            rows = pl.ds(pl.multiple_of(c * 256, 256), 256)
            l0, l1, l2 = lb_refs[0][rows, :], lb_refs[1][rows, :], lb_refs[2][rows, :]
            m = jnp.maximum(jnp.maximum(l0, l1), l2)
            e0, e1, e2 = jnp.exp(l0 - m), jnp.exp(l1 - m), jnp.exp(l2 - m)
            den = e0 + e1 + e2
            inv = 1.0 / den
            o_ref[rows, :] = ((e0 * inv) * ob_refs[0][rows, :] + (e1 * inv) * ob_refs[1][rows, :]
                              + (e2 * inv) * ob_refs[2][rows, :])
            l_ref[rows, :] = m + jnp.log(den)
            return carry

        lax.fori_loop(0, S // 256, combine, 0)

    head = pl.BlockSpec((S, LANES), lambda h: (0, h))
    return _carried_call(
        body, name="mixc_fwd", grid=(W // LANES,), in_specs=[head] * 3, out_specs=[head] * 2,
        out_shape=[_sds((S, W), F32)] * 2, scratch_shapes=[pltpu.VMEM((S, LANES), F32)] * 6,
        dims=("parallel",), args=[q, k, v], rider=rider)


def _mixc_bwd(q, k, v, do, lse, dd, rider=None):
    S, W = q.shape

    def body(q_ref, k_ref, v_ref, do_ref, l_ref, d_ref, dq_ref, dk_ref, dv_ref):
        dq_ref[...] = jnp.zeros_like(dq_ref)
        dk_ref[...] = jnp.zeros_like(dk_ref)
        dv_ref[...] = jnp.zeros_like(dv_ref)
        for d in DILATIONS:
            L, nb, win = _band_blocks(S, d)

            def step(it, carry, d=d, L=L, nb=nb, win=win):
                idx = [_band_index(it * BAND_UNROLL + u, d, nb, L, win) for u in range(BAND_UNROLL)]
                qrows = [_band_rows(i[0], BAND_SUB, d) for i in idx]
                krows = [_band_rows(i[1], win, d) for i in idx]
                qv = [q_ref[r, :].astype(CDT) for r in qrows]
                dov = [do_ref[r, :].astype(CDT) for r in qrows]
                kw = [k_ref[r, :].astype(CDT) for r in krows]
                vw = [v_ref[r, :].astype(CDT) for r in krows]
                lse2 = [l_ref[r, :][:, 0:1] * LOG2E for r in qrows]
                dd = [d_ref[r, :][:, 0:1] for r in qrows]
                s = [jnp.where(_band_mask(i[2], i[3], win), _dot(a, kk, NT), NEG_INF) for i, a, kk in zip(idx, qv, kw)]
                p = [jnp.exp2(x * C2_H - ll) for x, ll in zip(s, lse2)]
                dp = [_dot(a, vv, NT) for a, vv in zip(dov, vw)]
                ds = [(pp * (x - y) * SCALE_H).astype(CDT) for pp, x, y in zip(p, dp, dd)]
                dq = [_dot(x, kk, NN) for x, kk in zip(ds, kw)]
                dk = [_dot(x, a, TN) for x, a in zip(ds, qv)]
                dv = [_dot(pp.astype(CDT), a, TN) for pp, a in zip(p, dov)]
                for u in range(BAND_UNROLL):
                    dq_ref[qrows[u], :] += dq[u]
                    dk_ref[krows[u], :] += dk[u]
                    dv_ref[krows[u], :] += dv[u]
                return carry

            lax.fori_loop(0, S // (BAND_SUB * BAND_UNROLL), step, 0)

    head = pl.BlockSpec((S, LANES), lambda h: (0, h))
    return _carried_call(
        body, name="mixc_bwd", grid=(W // LANES,), in_specs=[head] * 6, out_specs=[head] * 3,
        out_shape=[_sds((S, W), F32)] * 3, scratch_shapes=[], dims=("parallel",), args=[q, k, v, do, lse, dd],
        rider=rider)


def _outnorm_fwd(oa, ob, oc, g, *, tr=256):
    S = oa.shape[0]
    tr = min(tr, S)

    def body(a_ref, b_ref, c_ref, g_ref, m_ref):
        m_ref[:, 0:512] = (_rms_val(a_ref[...], None, 512)[0] * g_ref[:, 0:512]).astype(CDT)
        m_ref[:, 512:1280] = (_rms_val(b_ref[...], None, 768)[0] * g_ref[:, 512:1280]).astype(CDT)
        m_ref[:, 1280:2048] = (_rms_val(c_ref[...], None, 768)[0] * g_ref[:, 1280:2048]).astype(CDT)

    row = lambda w: pl.BlockSpec((tr, w), lambda i: (i, 0))
    return pl.pallas_call(
        body, name="outnorm_fwd", grid=(S // tr,),
        in_specs=[row(512), row(768), row(768), pl.BlockSpec((1, 2048), lambda i: (0, 0))],
        out_specs=row(2048), out_shape=_sds((S, 2048), CDT), compiler_params=_cparams(("parallel",)),
    )(oa, ob, oc, g)


def _outnorm_bwd(oa, ob, oc, g, dm, *, tr=256):
    S = oa.shape[0]
    tr = min(tr, S)

    def body(a_ref, b_ref, c_ref, g_ref, dm_ref, doa_ref, dob_ref, doc_ref, da_ref, db_ref, dc_ref, dg_ref):
        @pl.when(pl.program_id(0) == 0)
        def _():
            dg_ref[...] = jnp.zeros_like(dg_ref)

        for o_ref, do_ref, d_ref, lo, w in ((a_ref, doa_ref, da_ref, 0, 512), (b_ref, dob_ref, db_ref, 512, 768),
                                            (c_ref, doc_ref, dc_ref, 1280, 768)):
            o = o_ref[...]
            dmv = dm_ref[:, lo:lo + w]
            do, _ = _rms_bwd_val(o, None, dmv * g_ref[:, lo:lo + w], w)
            r = lax.rsqrt(jnp.sum(o * o, axis=-1, keepdims=True) * (1.0 / w) + EPS)
            dg_ref[:, lo:lo + w] += _colsum(dmv * (o * r))
            do_ref[...] = do.astype(do_ref.dtype)
            for h in range(w // LANES):
                sl = slice(LANES * h, LANES * (h + 1))
                d_ref[:, sl] = jnp.broadcast_to(jnp.sum(do[:, sl] * o[:, sl], axis=-1, keepdims=True), (tr, LANES))

    row = lambda w: pl.BlockSpec((tr, w), lambda i: (i, 0))
    vec = pl.BlockSpec((1, 2048), lambda i: (0, 0))
    return pl.pallas_call(
        body, name="outnorm_bwd", grid=(S // tr,),
        in_specs=[row(512), row(768), row(768), vec, row(2048)],
        out_specs=[row(512), row(768), row(768), row(512), row(768), row(768), vec],
        out_shape=[_sds((S, 512), CDT), _sds((S, 768), CDT), _sds((S, 768), F32), _sds((S, 512), F32),
                   _sds((S, 768), F32), _sds((S, 768), F32), _sds((1, 2048), F32)],
        compiler_params=_cparams(("arbitrary",)),
    )(oa, ob, oc, g, dm)


def _row_tile(r, c, itemsize, limit=1 << 20):
    best = 16
    for t in range(16, r + 1, 16):
        if r % t == 0 and t * c * itemsize <= limit:
            best = t
    return best


def _rs_chip_sum(g, got, *, name):
    _, _, r, c = g.shape
    tr = _row_tile(r, c, 2)
    core = lax.axis_index("c").astype(jnp.int32).reshape(1)

    def body(c_ref, a_ref, b_ref, o_ref):
        o_ref[...] = (a_ref[...].astype(F32) + b_ref[...].astype(F32)).astype(o_ref.dtype)

    spec = pltpu.PrefetchScalarGridSpec(
        num_scalar_prefetch=1, grid=(4, r // tr),
        in_specs=[pl.BlockSpec((None, None, tr, c), lambda k, i, cr: (k, cr[0], i, 0)),
                  pl.BlockSpec((None, tr, c), lambda k, i, cr: (k, i, 0))],
        out_specs=pl.BlockSpec((None, tr, c), lambda k, i, cr: (k, i, 0)))
    return pl.pallas_call(body, name=name, grid_spec=spec, out_shape=_sds((4, r, c), g.dtype),
                          compiler_params=_cparams(("parallel", "parallel")))(core, g, got)


def _rs_final_sum(r4, *, name):
    _, r, c = r4.shape
    tr = _row_tile(r, c, 4)

    def body(r_ref, o_ref):
        o_ref[...] = ((r_ref[0].astype(F32) + r_ref[1].astype(F32)) + r_ref[2].astype(F32)) + r_ref[3].astype(F32)

    return pl.pallas_call(
        body, name=name, grid=(r // tr,), in_specs=[pl.BlockSpec((4, tr, c), lambda i: (0, i, 0))],
        out_specs=pl.BlockSpec((tr, c), lambda i: (i, 0)), out_shape=_sds((r, c), F32),
        compiler_params=_cparams(("parallel",)))(r4)


def _all_reduce_small(v):
    R = v.shape[0]

    def body(v_ref, out_ref, buf_ref, send_sems, recv_sems):
        x, y, c = lax.axis_index("x"), lax.axis_index("y"), lax.axis_index("c")
        me = 4 * x + 2 * y + c
        buf_ref[me] = v_ref[...]
        peers = []
        for r in range(1, 8):
            px, py, pc = x ^ (r >> 2), y ^ ((r >> 1) & 1), c ^ (r & 1)
            peers.append((r, (px, py, pc), 4 * px + 2 * py + pc))
        sends = [pltpu.make_async_remote_copy(
            src_ref=v_ref, dst_ref=buf_ref.at[me], send_sem=send_sems.at[r - 1], recv_sem=recv_sems.at[r - 1],
            device_id=dev, device_id_type=MESH) for r, dev, _ in peers]
        for cp in sends:
            cp.start()
        for r, dev, idx in peers:
            pltpu.make_async_remote_copy(
                src_ref=v_ref, dst_ref=buf_ref.at[idx], send_sem=send_sems.at[r - 1], recv_sem=recv_sems.at[r - 1],
                device_id=dev, device_id_type=MESH).wait_recv()
        for cp in sends:
            cp.wait_send()
        acc = buf_ref[0]
        for k in range(1, 8):
            acc = acc + buf_ref[k]
        out_ref[...] = acc

    vm = pl.BlockSpec(memory_space=pltpu.VMEM)
    return pl.pallas_call(
        body, name="all_reduce_small", out_shape=_sds((R, LANES), F32), in_specs=[vm], out_specs=vm,
        scratch_shapes=[pltpu.VMEM((8, R, LANES), F32), pltpu.SemaphoreType.DMA((7,)), pltpu.SemaphoreType.DMA((7,))],
    )(v)


def _adamw(w, g, m, v, *, name):
    R, C = w.shape
    tr = R
    for cand in (1024, 512, 256, 128, 64, 32, 16, 8):
        if R % cand == 0 and cand * C * 4 <= 2 * 1024 * 1024:
            tr = cand
            break

    def body(w_ref, g_ref, m_ref, v_ref, d_ref, nm_ref, nv_ref):
        gv = g_ref[...]
        mn = ADAM_B1 * m_ref[...] + (1.0 - ADAM_B1) * gv
        vn = ADAM_B2 * v_ref[...] + (1.0 - ADAM_B2) * (gv * gv)
        m_hat = mn / (1.0 - ADAM_B1 ** ADAM_STEP)
        v_hat = vn / (1.0 - ADAM_B2 ** ADAM_STEP)
        d_ref[...] = -ADAM_LR * (m_hat / (jnp.sqrt(v_hat) + ADAM_EPS) + ADAM_WD * w_ref[...])
        nm_ref[...] = mn
        nv_ref[...] = vn

    blk = pl.BlockSpec((tr, C), lambda i: (i, 0))
    return pl.pallas_call(
        body, name=name, grid=(R // tr,), in_specs=[blk] * 4, out_specs=[blk] * 3,
        out_shape=[_sds((R, C), F32)] * 3, compiler_params=_cparams(("parallel",)))(w, g, m, v)


def _wuq_pad(w):
    w = w.reshape(448, 4, 192)
    z = jnp.zeros((448, 4, 64), w.dtype)
    return jnp.concatenate([w[:, :, 0:128], z, w[:, :, 128:192]], axis=2).reshape(448, 1024)


def _wuq_unpad(w):
    w = w.reshape(448, 4, 256)
    return jnp.concatenate([w[:, :, 0:128], w[:, :, 192:256]], axis=2).reshape(448, 768)


def _wukv_perm(w):
    return w.reshape(512, 4, 2, 128).transpose(0, 2, 1, 3).reshape(512, 1024)


def _wukv_unperm(w):
    return w.reshape(512, 2, 4, 128).transpose(0, 2, 1, 3).reshape(512, 1024)


def _lat_weight(w_uq, w_ukv):
    z = lambda r, c: jnp.zeros((r, c), w_uq.dtype)
    top = jnp.concatenate([_wuq_pad(w_uq), z(448, 1024)], axis=1)
    mid = jnp.concatenate([z(512, 1024), _wukv_perm(w_ukv)], axis=1)
    return jnp.concatenate([top, mid, z(64, 2048)], axis=0)


def _lat_weight_grads(dw):
    return _wuq_unpad(dw[0:KV_LO, 0:1024]), _wukv_unperm(dw[KV_LO:KV_HI, 1024:2048])


def _comm_shards(w_in, w_uq, w_ukv, w_out, w_ff1, w_ff2):
    lat = jnp.concatenate([w_uq.reshape(UQ_ROWS, LANES), w_ukv.reshape(512, LANES)], axis=0)
    return [w_in.T, w_ff1.T, w_out, w_ff2, lat]


def _from_comm_shards(parts):
    w_in_t, w_ff1_t, w_out, w_ff2, lat = parts
    return {"w_in": w_in_t.T, "w_ff1": w_ff1_t.T, "w_out": w_out, "w_ff2": w_ff2,
            "w_uq": lat[0:UQ_ROWS].reshape(448, 96), "w_ukv": lat[UQ_ROWS:].reshape(512, 128)}


def _early_weights(g_in_t, g_lat):
    w_uq = g_lat[:, 0:UQ_ROWS].reshape(8, 448, 96).transpose(1, 0, 2).reshape(448, 768)
    w_ukv = g_lat[:, UQ_ROWS:].reshape(8, 512, 128).transpose(1, 0, 2).reshape(512, 1024)
    return g_in_t.reshape(PROJ_W, D_MODEL), _lat_weight(w_uq, w_ukv)


def _layer_fwd(x, W, G, tabs, plan=None):
    W = dict(W)
    shards, nxt = plan if plan is not None else (None, None)
    rider = lambda make: None if plan is None else make()
    s = {"x0": x}
    s["h1"] = _rms_fwd(x, G["ln1_g"], name="rms1_fwd")
    s["proj"] = _matmul(s["h1"], W["w_in_t"], mode="nt", tm=1024, tn=768, tk=2048, out_dtype=F32, name="mm_in")
    s["lat"], kpe, s["qb"], s["kb"], s["vb"], s["qc"], s["kc"], s["vc"] = _prep_fwd(
        s["proj"], G["glat"], G["gqn"], G["gkn"], tabs)
    qkva = _matmul(s["lat"], W["w_lat"], mode="nn", tm=1024, tn=1024, tk=1024, out_dtype=F32, name="mm_lat")
    s["qa"], s["ka"], s["va"] = _prep_a2_fwd(qkva, kpe, tabs[0])
    (s["oa"], s["lse_a"]), got_a = _attn_fwd(
        s["qa"], s["ka"], s["va"], H=4, G=1, dk=256, dv=128, scale=SCALE_A, name="attn_a_fwd",
        rider=rider(lambda: _ag_first_rider([shards["w_ff1_t"]], "_ff1")))
    (s["ob"], s["lse_b"]), got_b = _attn_fwd(
        s["qb"], s["kb"], s["vb"], H=6, G=3, dk=128, dv=128, scale=SCALE_H, name="attn_b_fwd",
        rider=rider(lambda: _ag_first_rider([shards["w_ff2"], shards["w_out"]], "_ff2_out")))
    (s["oc"], s["lse_c"]), got_c = _mixc_fwd(
        s["qc"], s["kc"], s["vc"],
        rider=rider(lambda: _join(_ag_second_rider(got_a + got_b, "_ff_out"),
                                  None if nxt is None else _ag_first_rider(nxt, "_early"))))
    if plan is not None:
        W["w_ff1_t"], W["w_ff2"] = got_c[0].reshape(D_FF, D_MODEL), got_c[1].reshape(D_FF, D_MODEL)
        W["w_out"] = got_c[2].reshape(D_MODEL, D_MODEL)
    s["mixed"] = _outnorm_fwd(s["oa"], s["ob"], s["oc"], G["g_out"])
    s["x1"] = _matmul(s["mixed"], W["w_out"], mode="nn", tm=1024, tn=1024, tk=2048, out_dtype=F32, name="mm_out",
                      epi="residual", extra=x)
    s["h2"] = _rms_fwd(s["x1"], G["ln2_g"], name="rms2_fwd")
    ff1 = functools.partial(_matmul, s["h2"], W["w_ff1_t"], mode="nt", tm=1024, tn=1024, tk=2048, out_dtype=CDT,
                            name="mm_ff1", epi="relu2")
    if nxt is None:
        (s["z"], s["u"]), early = ff1(), None
    else:
        (s["z"], s["u"]), early = ff1(rider=_ag_second_rider(got_c[3:5], "_early"))
    x2 = _matmul(s["u"], W["w_ff2"], mode="nn", tm=1024, tn=1024, tk=2048, out_dtype=F32, name="mm_ff2",
                 epi="residual", extra=s["x1"])
    return x2, s, W, early


def _by_destination(dw, name):
    return dw.reshape((4, 2) + COMM_SHAPE[name])


def _early_by_destination(dw_in_t, dw_lat):
    dw_uq, dw_ukv = _lat_weight_grads(dw_lat)
    lat = jnp.concatenate([dw_uq.reshape(448, 8, 96).transpose(1, 0, 2).reshape(8, UQ_ROWS, LANES),
                           dw_ukv.reshape(512, 8, 128).transpose(1, 0, 2)], axis=1)
    return [_by_destination(dw_in_t, "w_in_t"), _by_destination(lat, "lat")]


def _layer_bwd(dx2, dx2b, s, W, G, tabs, scatter=False, pending=None):
    dw, dg, landed = {}, {}, {}
    ff2_dx = functools.partial(_matmul, dx2b, W["w_ff2"], mode="nt", tm=1024, tn=1024, tk=2048, out_dtype=CDT,
                               name="mm_ff2_dx", epi="drelu2", extra=s["z"])
    if pending is None:
        dz = ff2_dx()
    else:
        dz, got = ff2_dx(rider=_rs_sibling_rider(pending, "_early"))
        chip = [_rs_chip_sum(g, r, name="rs_chip_sum_" + n) for g, r, n in zip(pending, got, ("w_in_t", "lat"))]
    dw["w_ff2"] = _matmul(s["u"], dx2b, mode="tn", tm=2048, tn=1024, tk=2048, out_dtype=CDT, name="mm_ff2_dw")
    ff1_dx = functools.partial(_matmul, dz, W["w_ff1_t"], mode="nn", tm=1024, tn=1024, tk=2048, out_dtype=F32,
                               name="mm_ff1_dx")
    if pending is None:
        dh2 = ff1_dx()
    else:
        dh2, got = ff1_dx(rider=_rs_chip_rider(chip, "_early"))
        landed["above_w_in_t"], landed["above_lat"] = got
    dw["w_ff1_t"] = _matmul(dz, s["h2"], mode="tn", tm=2048, tn=1024, tk=2048, out_dtype=CDT, name="mm_ff1_dw")
    dx1, dx1b, dg["ln2_g"] = _rms_bwd(s["x1"], G["ln2_g"], dh2, dx2, name="rms2_bwd")
    dmixed = _matmul(dx1b, W["w_out"], mode="nt", tm=1024, tn=1024, tk=2048, out_dtype=F32, name="mm_out_dx")
    out_dw = functools.partial(_matmul, s["mixed"], dx1b, mode="tn", tm=1024, tn=1024, tk=2048, out_dtype=CDT,
                               name="mm_out_dw")
    if not scatter:
        dw["w_out"] = out_dw()
        riders = [None, None, None]
    else:
        g_ff = [_by_destination(dw["w_ff2"], "w_ff2"), _by_destination(dw["w_ff1_t"], "w_ff1_t")]
        dw["w_out"], got = out_dw(rider=_rs_sibling_rider(g_ff, "_ff"))
        chip_ff2 = _rs_chip_sum(g_ff[0], got[0], name="rs_chip_sum_w_ff2")
        chip_ff1 = _rs_chip_sum(g_ff[1], got[1], name="rs_chip_sum_w_ff1_t")
        g_out = [_by_destination(dw["w_out"], "w_out")]
        riders = [_rs_chip_rider([chip_ff2], "_ff2"),
                  _join(_rs_chip_rider([chip_ff1], "_ff1"), _rs_sibling_rider(g_out, "_out")), None]
    doa, dob, doc, dla, dlb, dlc, dg["g_out"] = _outnorm_bwd(s["oa"], s["ob"], s["oc"], G["g_out"], dmixed)
    (dqa, dka, dva), got = _attn_bwd(s["qa"], s["ka"], s["va"], doa, s["lse_a"], dla, H=4, G=1, dk=256, dv=128,
                                     scale=SCALE_A, name="attn_a_bwd", rider=riders[0])
    if scatter:
        landed["w_ff2"] = got[0]
    (dqb, dkb, dvb), got = _attn_bwd(s["qb"], s["kb"], s["vb"], dob, s["lse_b"], dlb, H=6, G=3, dk=128, dv=128,
                                     scale=SCALE_H, name="attn_b_bwd", rider=riders[1])
    if scatter:
        landed["w_ff1_t"] = got[0]
        riders[2] = _rs_chip_rider([_rs_chip_sum(g_out[0], got[1], name="rs_chip_sum_w_out")], "_out")
    (dqc, dkc, dvc), got = _mixc_bwd(s["qc"], s["kc"], s["vc"], doc, s["lse_c"], dlc, rider=riders[2])
    if scatter:
        landed["w_out"] = got[0]
    dqkva, dkr = _prep_a2_bwd(dqa, dka, dva, tabs[0])
    dlat = _matmul(dqkva, W["w_lat"], mode="nt", tm=1024, tn=1024, tk=2048, out_dtype=F32, name="mm_lat_dx")
    dw["w_lat"] = _matmul(s["lat"], dqkva, mode="tn", tm=1024, tn=2048, tk=2048, out_dtype=CDT, name="mm_lat_dw")
    dproj, dg["glat"], dg["gqn"], dg["gkn"] = _prep_bwd(
        s["proj"], G["glat"], G["gqn"], G["gkn"], tabs, dlat, dkr, dqb, dkb, dvb, dqc, dkc, dvc)
    dh1 = _matmul(dproj, W["w_in_t"], mode="nn", tm=1024, tn=1024, tk=2304, out_dtype=F32, name="mm_in_dx")
    dw["w_in_t"] = _matmul(dproj, s["h1"], mode="tn", tm=1536, tn=1024, tk=2048, out_dtype=CDT, name="mm_in_dw")
    dx0, dx0b, dg["ln1_g"] = _rms_bwd(s["x0"], G["ln1_g"], dh1, dx1, name="rms1_bwd")
    return dx0, dx0b, dw, dg, landed


def _layer_gains(l, ln1_g, g_q_a, g_kv_a, g_qn_b, g_kn_b, g_out, ln2_g):
    return {"ln1_g": ln1_g[l], "ln2_g": ln2_g[l], "g_out": g_out[l].reshape(1, 2048),
            "glat": jnp.concatenate([g_q_a[l], g_kv_a[l], jnp.zeros((LAT_W - KV_HI,), F32)]).reshape(1, LAT_W),
            "gqn": g_qn_b[l].reshape(1, 128), "gkn": g_kn_b[l].reshape(1, 128)}


def _gain_grads(dg):
    glat = dg["glat"].reshape(-1)
    return {"ln1_g": dg["ln1_g"].reshape(-1), "g_q_a": glat[0:KV_LO], "g_kv_a": glat[KV_LO:KV_HI],
            "g_qn_b": dg["gqn"].reshape(-1), "g_kn_b": dg["gkn"].reshape(-1), "g_out": dg["g_out"].reshape(-1),
            "ln2_g": dg["ln2_g"].reshape(-1)}


def _local_step(x, tgt, weights, gains, ln_f_g):
    S = x.shape[0]
    tabs = _rope_tables(S)
    depth = len(weights)
    saved = []
    for l in range(depth):
        x, s, _, _ = _layer_fwd(x, weights[l], gains[l], tabs)
        saved.append(s)
    loss, dx, dxb, dlnf = _loss_head(x, ln_f_g, tgt)
    dws, dgs = [None] * depth, [None] * depth
    for l in reversed(range(depth)):
        dx, dxb, dws[l], dgs[l], _ = _layer_bwd(dx, dxb, saved[l], weights[l], gains[l], tabs)
    return loss, dx, dws, dgs, dlnf


SMALL_SIZES = (("ln1_g", 2048), ("g_q_a", 448), ("g_kv_a", 512), ("g_qn_b", 128), ("g_kn_b", 128), ("g_out", 2048),
               ("ln2_g", 2048))


def _pack_small(per_layer, ln_f):
    flat = jnp.concatenate([per_layer[n].reshape(-1) for n, _ in SMALL_SIZES] + [ln_f.reshape(-1)])
    rows = -(-flat.shape[0] // (8 * LANES)) * 8
    return jnp.concatenate([flat, jnp.zeros((rows * LANES - flat.shape[0],), F32)]).reshape(rows, LANES)


def _unpack_small(packed, depth):
    flat, out, lo = packed.reshape(-1), {}, 0
    for n, w in SMALL_SIZES:
        out[n] = flat[lo:lo + depth * w].reshape(depth, w)
        lo += depth * w
    out["ln_f_g"] = flat[lo:lo + 2048]
    return out


def kernel(x, ln1_g, w_in, g_q_a, w_uq, g_kv_a, w_ukv, g_qn_b, g_kn_b, g_out, w_out, ln2_g, w_ff1, w_ff2, ln_f_g, loss_target, m_ln1_g, m_w_in, m_g_q_a, m_w_uq, m_g_kv_a, m_w_ukv, m_g_qn_b, m_g_kn_b, m_g_out, m_w_out, m_ln2_g, m_w_ff1, m_w_ff2, m_ln_f_g, v_ln1_g, v_w_in, v_g_q_a, v_w_uq, v_g_kv_a, v_w_ukv, v_g_qn_b, v_g_kn_b, v_g_out, v_w_out, v_ln2_g, v_w_ff1, v_w_ff2, v_ln_f_g):
    depth = w_in.shape[0]
    S = x.shape[1]
    big_w = {"w_in": w_in, "w_uq": w_uq, "w_ukv": w_ukv, "w_out": w_out, "w_ff1": w_ff1, "w_ff2": w_ff2}
    big_m = {"w_in": m_w_in, "w_uq": m_w_uq, "w_ukv": m_w_ukv, "w_out": m_w_out, "w_ff1": m_w_ff1, "w_ff2": m_w_ff2}
    big_v = {"w_in": v_w_in, "w_uq": v_w_uq, "w_ukv": v_w_ukv, "w_out": v_w_out, "w_ff1": v_w_ff1, "w_ff2": v_w_ff2}
    small_w = {"ln1_g": ln1_g, "g_q_a": g_q_a, "g_kv_a": g_kv_a, "g_qn_b": g_qn_b, "g_kn_b": g_kn_b, "g_out": g_out,
               "ln2_g": ln2_g}
    small_m = {"ln1_g": m_ln1_g, "g_q_a": m_g_q_a, "g_kv_a": m_g_kv_a, "g_qn_b": m_g_qn_b, "g_kn_b": m_g_kn_b,
               "g_out": m_g_out, "ln2_g": m_ln2_g}
    small_v = {"ln1_g": v_ln1_g, "g_q_a": v_g_q_a, "g_kv_a": v_g_kv_a, "g_qn_b": v_g_qn_b, "g_kn_b": v_g_kn_b,
               "g_out": v_g_out, "ln2_g": v_ln2_g}

    shards = [dict(zip(COMM, _comm_shards(*[big_w[n][l].astype(CDT) for n in BIG]))) for l in range(depth)]
    early_shards = [[sh["w_in_t"], sh["lat"]] for sh in shards]
    early = _run_rider(_ag_second_rider(_run_rider(_ag_first_rider(early_shards[0], "_early")), "_early"))
    gains = [_layer_gains(l, ln1_g, g_q_a, g_kv_a, g_qn_b, g_kn_b, g_out, ln2_g) for l in range(depth)]
    tabs = _rope_tables(S)

    h = x.reshape(S, D_MODEL)
    saved, weights = [], []
    for l in range(depth):
        W = dict(zip(("w_in_t", "w_lat"), _early_weights(*early)))
        h, s, W, early = _layer_fwd(h, W, gains[l], tabs,
                                    plan=(shards[l], early_shards[l + 1] if l + 1 < depth else None))
        saved.append(s)
        weights.append(W)
    loss_part, dx, dxb, dlnf = _loss_head(h, ln_f_g, loss_target.reshape(S, D_MODEL))
    loss = lax.psum(loss_part[0, 0], ("x", "y", "c"))

    dgs, landed, pending = [None] * depth, [None] * depth, None
    for l in reversed(range(depth)):
        dx, dxb, dw, dgs[l], landed[l] = _layer_bwd(dx, dxb, saved[l], weights[l], gains[l], tabs, scatter=True,
                                                    pending=pending)
        if pending is not None:
            landed[l + 1]["w_in_t"], landed[l + 1]["lat"] = landed[l].pop("above_w_in_t"), landed[l].pop("above_lat")
        pending = _early_by_destination(dw["w_in_t"], dw["w_lat"])
    got = _run_rider(_rs_sibling_rider(pending, "_early"))
    chip = [_rs_chip_sum(g, r, name="rs_chip_sum_" + n) for g, r, n in zip(pending, got, ("w_in_t", "lat"))]
    landed[0]["w_in_t"], landed[0]["lat"] = _run_rider(_rs_chip_rider(chip, "_early"))
    grad_x = dx.reshape(1, S, D_MODEL)

    shard_grads = [_from_comm_shards([_rs_final_sum(landed[l][n], name="rs_final_sum_" + n) for n in COMM])
                   for l in range(depth)]
    big_g = {n: jnp.stack([shard_grads[l][n] for l in range(depth)]) for n in BIG}

    named = [_gain_grads(dgs[l]) for l in range(depth)]
    per_layer = {n: jnp.stack([named[l][n] for l in range(depth)]) for n, _ in SMALL_SIZES}
    small_g = _unpack_small(_all_reduce_small(_pack_small(per_layer, dlnf.reshape(-1))), depth)

    upd = {}
    for n in BIG:
        shp = big_w[n].shape
        two_d = (shp[0] * shp[1], shp[2])
        d, nm, nv = _adamw(big_w[n].reshape(two_d), big_g[n].reshape(two_d), big_m[n].reshape(two_d),
                           big_v[n].reshape(two_d), name="adamw_" + n)
        upd[n] = (d.reshape(shp), nm.reshape(shp), nv.reshape(shp))
    small_w["ln_f_g"], small_m["ln_f_g"], small_v["ln_f_g"] = ln_f_g, m_ln_f_g, v_ln_f_g
    names_small = [n for n, _ in SMALL_SIZES]
    pw = _pack_small({n: small_w[n] for n in names_small}, small_w["ln_f_g"])
    pg = _pack_small({n: small_g[n] for n in names_small}, small_g["ln_f_g"])
    pm = _pack_small({n: small_m[n] for n in names_small}, small_m["ln_f_g"])
    pv = _pack_small({n: small_v[n] for n in names_small}, small_v["ln_f_g"])
    d, nm, nv = _adamw(pw, pg, pm, pv, name="adamw_small")
    sd, snm, snv = _unpack_small(d, depth), _unpack_small(nm, depth), _unpack_small(nv, depth)
    for n in names_small + ["ln_f_g"]:
        upd[n] = (sd[n], snm[n], snv[n])

    order = ["ln1_g", "w_in", "g_q_a", "w_uq", "g_kv_a", "w_ukv", "g_qn_b", "g_kn_b", "g_out", "w_out", "ln2_g", "w_ff1",
             "w_ff2", "ln_f_g"]
    grads = {**big_g, **small_g}
    return (loss, grad_x, *[grads[n] for n in order], *[upd[n][0] for n in order], *[upd[n][1] for n in order],
            *[upd[n][2] for n in order])
```

```python
import functools
import math

import jax
import jax.numpy as jnp
from jax import lax
from jax.experimental import pallas as pl
from jax.experimental.pallas import tpu as pltpu

D_MODEL = 2048
D_FF = 8192
EPS = 1e-6
NEG_INF = -1e30
Q_LORA = 448
ROPE_THETA = 10000.0
GRID_W = 64
DILATIONS = (1, 4, 16)
BAND_HALF = 64
SCALE_A = 1.0 / math.sqrt(192.0)
SCALE_H = 1.0 / math.sqrt(128.0)
ADAM_LR, ADAM_B1, ADAM_B2, ADAM_EPS, ADAM_WD, ADAM_STEP = 0.001, 0.9, 0.999, 1e-08, 0.01, 10

CDT = jnp.bfloat16
F32 = jnp.float32
LANES = 128
VMEM_LIMIT = 56 * 1024 * 1024

PROJ_W = 4608
LAT_W = 1024
KV_LO, KV_HI = 448, 960
OFF_BQ, OFF_BK, OFF_BV, OFF_CQ, OFF_CK, OFF_CV = 1024, 1792, 2048, 2304, 3072, 3840

NN = ((1,), (0,))
NT = ((1,), (1,))
TN = ((0,), (0,))

BIG = ("w_in", "w_uq", "w_ukv", "w_out", "w_ff1", "w_ff2")
COMM = ("w_in_t", "w_ff1_t", "w_out", "w_ff2", "lat")
COMM_SHAPE = {"w_in_t": (576, 2048), "w_ff1_t": (1024, 2048), "w_out": (256, 2048), "w_ff2": (1024, 2048),
              "lat": (848, 128)}
UQ_ROWS = 448 * 96 // LANES


def _dot(a, b, dims):
    return lax.dot_general(a, b, (dims, ((), ())), preferred_element_type=F32)


def _cparams(dims=None):
    return pltpu.CompilerParams(dimension_semantics=dims, vmem_limit_bytes=VMEM_LIMIT)


def _sds(shape, dtype):
    return jax.ShapeDtypeStruct(shape, dtype)


MESH = pl.DeviceIdType.MESH
ANY = pl.BlockSpec(memory_space=pl.ANY)


class _Rider:
    def __init__(self, name, arrays, out_shape, scratch, aliases, start, finish):
        self.name, self.arrays, self.out_shape, self.scratch = name, list(arrays), list(out_shape), list(scratch)
        self.aliases, self.start, self.finish = dict(aliases), start, finish


def _join(a, b):
    if a is None or b is None:
        return a if b is None else b
    na, oa, sa = len(a.arrays), len(a.out_shape), len(a.scratch)
    aliases = dict(a.aliases)
    aliases.update({na + i: oa + o for i, o in b.aliases.items()})

    def start(ins, outs, sems):
        a.start(ins[:na], outs[:oa], sems[:sa])
        b.start(ins[na:], outs[oa:], sems[sa:])

    def finish(ins, outs, sems):
        a.finish(ins[:na], outs[:oa], sems[:sa])
        b.finish(ins[na:], outs[oa:], sems[sa:])

    return _Rider(a.name + "_" + b.name, a.arrays + b.arrays, a.out_shape + b.out_shape, a.scratch + b.scratch,
                  aliases, start, finish)


def _carried_call(body, *, name, grid, in_specs, out_specs, out_shape, scratch_shapes, dims, args, rider):
    in_specs, out_specs, out_shape = list(in_specs), list(out_specs), list(out_shape)
    scratch_shapes = list(scratch_shapes)
    if rider is None:
        res = pl.pallas_call(body, name=name, grid=grid, in_specs=in_specs, out_specs=out_specs, out_shape=out_shape,
                             scratch_shapes=scratch_shapes, compiler_params=_cparams(dims))(*args)
        return list(res), []
    n_in, n_out, n_scr = len(in_specs), len(out_specs), len(scratch_shapes)
    r_in, r_out = len(rider.arrays), len(rider.out_shape)

    def wrapped(*refs):
        o0 = n_in + r_in
        s0 = o0 + n_out + r_out
        ins, outs, sems = refs[n_in:o0], refs[o0 + n_out:s0], refs[s0 + n_scr:]
        ids = [pl.program_id(a) for a in range(len(grid))]
        first = functools.reduce(jnp.logical_and, [i == 0 for i in ids])
        last = functools.reduce(jnp.logical_and, [i == g - 1 for i, g in zip(ids, grid)])

        @pl.when(first)
        def _():
            rider.start(ins, outs, sems)

        body(*refs[:n_in], *refs[o0:o0 + n_out], *refs[s0:s0 + n_scr])

        @pl.when(last)
        def _():
            rider.finish(ins, outs, sems)

    res = pl.pallas_call(
        wrapped, name=name + "_" + rider.name, grid=grid, in_specs=in_specs + [ANY] * r_in,
        out_specs=out_specs + [ANY] * r_out, out_shape=out_shape + rider.out_shape,
        scratch_shapes=scratch_shapes + rider.scratch,
        input_output_aliases={n_in + i: n_out + o for i, o in rider.aliases.items()},
        compiler_params=_cparams(("arbitrary",) * len(grid)),
    )(*args, *rider.arrays)
    return list(res[:n_out]), list(res[n_out:])


def _run_rider(rider):
    def body(*refs):
        r_in, r_out = len(rider.arrays), len(rider.out_shape)
        ins, outs, sems = refs[:r_in], refs[r_in:r_in + r_out], refs[r_in + r_out:]
        rider.start(ins, outs, sems)
        rider.finish(ins, outs, sems)

    res = pl.pallas_call(
        body, name=rider.name, in_specs=[ANY] * len(rider.arrays), out_specs=[ANY] * len(rider.out_shape),
        out_shape=rider.out_shape, scratch_shapes=rider.scratch, input_output_aliases=rider.aliases,
    )(*rider.arrays)
    return list(res)


def _mesh_place():
    x, y, c = lax.axis_index("x"), lax.axis_index("y"), lax.axis_index("c")
    return x, y, c, [(1 - x, y), (x, 1 - y), (1 - x, 1 - y)]


def _remote(src, dst, send, recv, dev):
    return pltpu.make_async_remote_copy(src_ref=src, dst_ref=dst, send_sem=send, recv_sem=recv, device_id=dev,
                                        device_id_type=MESH)


def _ag_first_rider(shards, tag):
    n = len(shards)

    def copies(ins, outs, sems):
        send, recv, _ = sems
        x, y, c, chips = _mesh_place()
        me = 4 * x + 2 * y + c
        peers = [(x, y, 1 - c)] + [(cx, cy, c) for cx, cy in chips]
        out = []
        for t in range(n):
            for k, dev in enumerate(peers):
                theirs = 4 * dev[0] + 2 * dev[1] + dev[2]
                out.append((_remote(ins[t], outs[t].at[me], send.at[t, k], recv.at[t, k], dev),
                            _remote(ins[t], outs[t].at[theirs], send.at[t, k], recv.at[t, k], dev)))
        mine = [pltpu.make_async_copy(ins[t], outs[t].at[me], sems[2].at[t]) for t in range(n)]
        return out, mine

    def start(ins, outs, sems):
        pairs, mine = copies(ins, outs, sems)
        for cp in mine:
            cp.start()
        for snd, _ in pairs:
            snd.start()

    def finish(ins, outs, sems):
        pairs, mine = copies(ins, outs, sems)
        for _, rcv in pairs:
            rcv.wait_recv()
        for snd, _ in pairs:
            snd.wait_send()
        for cp in mine:
            cp.wait()

    return _Rider("ag1" + tag, shards, [_sds((8,) + s.shape, s.dtype) for s in shards],
                  [pltpu.SemaphoreType.DMA((n, 4)), pltpu.SemaphoreType.DMA((n, 4)), pltpu.SemaphoreType.DMA((n,))],
                  {}, start, finish)


def _ag_second_rider(gathered, tag):
    n = len(gathered)

    def copies(ins, outs, sems):
        send, recv = sems
        x, y, c, chips = _mesh_place()
        out = []
        for t in range(n):
            for j, (cx, cy) in enumerate(chips):
                here, there = 4 * cx + 2 * cy + c, 4 * cx + 2 * cy + (1 - c)
                out.append((_remote(ins[t].at[here], outs[t].at[here], send.at[t, j], recv.at[t, j], (x, y, 1 - c)),
                            _remote(ins[t].at[here], outs[t].at[there], send.at[t, j], recv.at[t, j], (x, y, 1 - c))))
        return out

    def start(ins, outs, sems):
        for snd, _ in copies(ins, outs, sems):
            snd.start()

    def finish(ins, outs, sems):
        pairs = copies(ins, outs, sems)
        for _, rcv in pairs:
            rcv.wait_recv()
        for snd, _ in pairs:
            snd.wait_send()

    return _Rider("ag2" + tag, gathered, [_sds(g.shape, g.dtype) for g in gathered],
                  [pltpu.SemaphoreType.DMA((n, 3)), pltpu.SemaphoreType.DMA((n, 3))],
                  {t: t for t in range(n)}, start, finish)


def _rs_sibling_rider(gs, tag):
    n = len(gs)

    def copies(ins, outs, sems):
        send, recv = sems
        x, y, c, _ = _mesh_place()
        return [_remote(ins[t].at[k, 1 - c], outs[t].at[k], send.at[t, k], recv.at[t, k], (x, y, 1 - c))
                for t in range(n) for k in range(4)]

    def start(ins, outs, sems):
        for cp in copies(ins, outs, sems):
            cp.start()

    def finish(ins, outs, sems):
        for cp in copies(ins, outs, sems):
            cp.wait()

    return _Rider("rs1" + tag, gs, [_sds((4,) + g.shape[2:], g.dtype) for g in gs],
                  [pltpu.SemaphoreType.DMA((n, 4)), pltpu.SemaphoreType.DMA((n, 4))], {}, start, finish)


def _rs_chip_rider(ps, tag):
    n = len(ps)

    def copies(ins, outs, sems):
        send, recv, local = sems
        x, y, c, chips = _mesh_place()
        my_chip = 2 * x + y
        out = []
        for t in range(n):
            for j, (cx, cy) in enumerate(chips):
                dev = (cx, cy, c)
                out.append((_remote(ins[t].at[2 * cx + cy], outs[t].at[my_chip], send.at[t, j], recv.at[t, j], dev),
                            _remote(ins[t].at[my_chip], outs[t].at[2 * cx + cy], send.at[t, j], recv.at[t, j], dev)))
        mine = [pltpu.make_async_copy(ins[t].at[my_chip], outs[t].at[my_chip], local.at[t]) for t in range(n)]
        return out, mine

    def start(ins, outs, sems):
        pairs, mine = copies(ins, outs, sems)
        for cp in mine:
            cp.start()
        for snd, _ in pairs:
            snd.start()

    def finish(ins, outs, sems):
        pairs, mine = copies(ins, outs, sems)
        for _, rcv in pairs:
            rcv.wait_recv()
        for snd, _ in pairs:
            snd.wait_send()
        for cp in mine:
            cp.wait()

    return _Rider("rs2" + tag, ps, [_sds(p.shape, p.dtype) for p in ps],
                  [pltpu.SemaphoreType.DMA((n, 3)), pltpu.SemaphoreType.DMA((n, 3)), pltpu.SemaphoreType.DMA((n,))],
                  {}, start, finish)


def _matmul(a, b, *, mode, tm, tn, tk, out_dtype, name, epi=None, extra=None, rider=None):
    if mode == "nn":
        (M, K), (K2, N) = a.shape, b.shape
    elif mode == "nt":
        (M, K), (N, K2) = a.shape, b.shape
    else:
        (K, M), (K2, N) = a.shape, b.shape
    tm, tn, tk = min(tm, M), min(tn, N), min(tk, K)
    assert K == K2 and M % tm == 0 and N % tn == 0 and K % tk == 0, (name, a.shape, b.shape)
    nk = K // tk
    dims = {"nn": NN, "nt": NT, "tn": TN}[mode]
    if mode == "tn":
        a_spec = pl.BlockSpec((tk, tm), lambda i, j, k: (k, i))
    else:
        a_spec = pl.BlockSpec((tm, tk), lambda i, j, k: (i, k))
    if mode == "nt":
        b_spec = pl.BlockSpec((tn, tk), lambda i, j, k: (j, k))
    else:
        b_spec = pl.BlockSpec((tk, tn), lambda i, j, k: (k, j))
    tile = pl.BlockSpec((tm, tn), lambda i, j, k: (i, j))
    n_extra = 1 if epi in ("residual", "drelu2") else 0
    n_out = 2 if epi == "relu2" else 1

    def body(*refs):
        a_ref, b_ref = refs[0], refs[1]
        extra_refs = refs[2:2 + n_extra]
        out_refs = refs[2 + n_extra:2 + n_extra + n_out]

        def finish(acc):
            if epi is None:
                out_refs[0][...] = acc.astype(out_dtype)
            elif epi == "residual":
                out_refs[0][...] = (extra_refs[0][...] + acc).astype(out_dtype)
            elif epi == "relu2":
                out_refs[0][...] = acc.astype(out_dtype)
                r = jnp.maximum(acc, 0.0)
                out_refs[1][...] = (r * r).astype(out_dtype)
            else:
                z = extra_refs[0][...].astype(F32)
                out_refs[0][...] = (acc * (2.0 * jnp.maximum(z, 0.0))).astype(out_dtype)

        part = _dot(a_ref[...], b_ref[...], dims)
        if nk == 1:
            finish(part)
        else:
            acc_ref = refs[-1]
            k = pl.program_id(2)

            @pl.when(k == 0)
            def _():
                acc_ref[...] = part

            @pl.when(k > 0)
            def _():
                acc_ref[...] += part

            @pl.when(k == nk - 1)
            def _():
                finish(acc_ref[...])

    res, carried = _carried_call(
        body, name=name, grid=(M // tm, N // tn, nk), in_specs=[a_spec, b_spec] + [tile] * n_extra,
        out_specs=[tile] * n_out, out_shape=[_sds((M, N), out_dtype)] * n_out,
        scratch_shapes=[pltpu.VMEM((tm, tn), F32)] if nk > 1 else [],
        dims=("parallel", "parallel", "arbitrary"), args=[a, b] + ([extra] if n_extra else []), rider=rider)
    res = res if n_out > 1 else res[0]
    return res if rider is None else (res, carried)


def _rms_val(x, g, n):
    r = lax.rsqrt(jnp.sum(x * x, axis=-1, keepdims=True) * (1.0 / n) + EPS)
    y = x * r
    return (y if g is None else y * g), r


def _rms_bwd_val(x, g, dy, n):
    r = lax.rsqrt(jnp.sum(x * x, axis=-1, keepdims=True) * (1.0 / n) + EPS)
    xhat = x * r
    dyg = dy if g is None else dy * g
    dx = r * (dyg - xhat * (jnp.sum(dyg * xhat, axis=-1, keepdims=True) * (1.0 / n)))
    return dx, dy * xhat


def _rope_val(x, c, sa, sb, shift):
    return x * c + pltpu.roll(x, LANES - shift, 1) * sa + pltpu.roll(x, shift, 1) * sb


def _rope_t_val(dy, c, sa, sb, shift):
    return dy * c + pltpu.roll(dy * sa, shift, 1) + pltpu.roll(dy * sb, LANES - shift, 1)


def _colsum(x):
    return jnp.sum(x, axis=0, keepdims=True)


def _rope_tables(S):
    pos = lax.broadcasted_iota(jnp.int32, (S, LANES), 0)
    lane = lax.broadcasted_iota(jnp.int32, (S, LANES), 1)

    def tables(p, dim, active):
        half = dim // 2
        inv = jnp.power(ROPE_THETA, -(2 * (lane % half)).astype(F32) / dim)
        a = p.astype(F32) * inv
        first = (lane % dim) < half
        zero = jnp.zeros((S, LANES), F32)
        return (jnp.where(active, jnp.cos(a), zero), jnp.where(active & first, -jnp.sin(a), zero),
                jnp.where(active & ~first, jnp.sin(a), zero))

    tab_a = tables(pos, 64, lane >= 64)
    tab_b = tables(jnp.where(lane < 64, pos // GRID_W, pos % GRID_W), 64, lane >= 0)
    tab_c = tables(pos, 128, lane >= 0)
    return tab_a, tab_b, tab_c


ROPE_SHIFT_AB = 32
ROPE_SHIFT_C = 64


def _rms_fwd(x, g, *, name, tr=512):
    S, W = x.shape
    tr = min(tr, S)

    def body(x_ref, g_ref, o_ref):
        y, _ = _rms_val(x_ref[...], g_ref[...], W)
        o_ref[...] = y.astype(CDT)

    return pl.pallas_call(
        body, name=name, grid=(S // tr,),
        in_specs=[pl.BlockSpec((tr, W), lambda i: (i, 0)), pl.BlockSpec((1, W), lambda i: (0, 0))],
        out_specs=pl.BlockSpec((tr, W), lambda i: (i, 0)), out_shape=_sds((S, W), CDT),
        compiler_params=_cparams(("parallel",)),
    )(x, g.reshape(1, W))


def _rms_bwd(x, g, dy, res, *, name, tr=256):
    S, W = x.shape
    tr = min(tr, S)

    def body(x_ref, g_ref, dy_ref, res_ref, dx_ref, dxb_ref, dg_ref):
        dx, dgt = _rms_bwd_val(x_ref[...], g_ref[...], dy_ref[...], W)
        dx = res_ref[...] + dx
        dx_ref[...] = dx
        dxb_ref[...] = dx.astype(CDT)

        @pl.when(pl.program_id(0) == 0)
        def _():
            dg_ref[...] = jnp.zeros_like(dg_ref)

        dg_ref[...] += _colsum(dgt)

    row = pl.BlockSpec((tr, W), lambda i: (i, 0))
    vec = pl.BlockSpec((1, W), lambda i: (0, 0))
    return pl.pallas_call(
        body, name=name, grid=(S // tr,),
        in_specs=[row, vec, row, row], out_specs=[row, row, vec],
        out_shape=[_sds((S, W), F32), _sds((S, W), CDT), _sds((1, W), F32)],
        compiler_params=_cparams(("arbitrary",)),
    )(x, g.reshape(1, W), dy, res)


def _loss_head(x, g, tgt, *, tr=256):
    S, W = x.shape
    tr = min(tr, S)

    def body(x_ref, g_ref, t_ref, loss_ref, dx_ref, dxb_ref, dg_ref):
        xv, gv = x_ref[...], g_ref[...]
        y, _ = _rms_val(xv, gv, W)
        err = y - t_ref[...]
        part = 0.5 * jnp.sum(jnp.sum(err * err, axis=-1, keepdims=True) * (1.0 / W), axis=0, keepdims=True)
        dx, dgt = _rms_bwd_val(xv, gv, err * (1.0 / W), W)
        dx_ref[...] = dx
        dxb_ref[...] = dx.astype(CDT)

        @pl.when(pl.program_id(0) == 0)
        def _():
            dg_ref[...] = jnp.zeros_like(dg_ref)
            loss_ref[...] = jnp.zeros_like(loss_ref)

        dg_ref[...] += _colsum(dgt)
        loss_ref[...] += jnp.broadcast_to(part, (1, LANES))

    row = pl.BlockSpec((tr, W), lambda i: (i, 0))
    vec = pl.BlockSpec((1, W), lambda i: (0, 0))
    return pl.pallas_call(
        body, name="loss_head", grid=(S // tr,),
        in_specs=[row, vec, row], out_specs=[pl.BlockSpec((1, LANES), lambda i: (0, 0)), row, row, vec],
        out_shape=[_sds((1, LANES), F32), _sds((S, W), F32), _sds((S, W), CDT), _sds((1, W), F32)],
        compiler_params=_cparams(("arbitrary",)),
    )(x, g.reshape(1, W), tgt)


def _tab_specs(tr):
    return [pl.BlockSpec((tr, LANES), lambda i: (i, 0))] * 9


def _lat_masks(shape):
    lane = lax.broadcasted_iota(jnp.int32, shape, 1)
    return lane < KV_LO, (lane >= KV_LO) & (lane < KV_HI)


def _prep_fwd(proj, glat, gqn, gkn, tabs, *, tr=256):
    S = proj.shape[0]
    tr = min(tr, S)

    def body(p_ref, glat_ref, gqn_ref, gkn_ref, ac, aa, ab, bc, ba, bb, cc, ca, cb,
             lat_ref, kpe_ref, qb_ref, kb_ref, vb_ref, qc_ref, kc_ref, vc_ref):
        x = p_ref[:, 0:LAT_W]
        is_q, is_kv = _lat_masks(x.shape)
        yq, _ = _rms_val(jnp.where(is_q, x, 0.0), None, Q_LORA)
        ykv, _ = _rms_val(jnp.where(is_kv, x, 0.0), None, KV_HI - KV_LO)
        lat_ref[...] = ((yq + ykv) * glat_ref[...]).astype(CDT)
        kpe_ref[...] = _rope_val(x[:, LAT_W - LANES:LAT_W], ac[...], aa[...], ab[...], ROPE_SHIFT_AB).astype(CDT)
        for h in range(6):
            xh = p_ref[:, OFF_BQ + LANES * h:OFF_BQ + LANES * (h + 1)]
            y = _rope_val(_rms_val(xh, gqn_ref[...], LANES)[0], bc[...], ba[...], bb[...], ROPE_SHIFT_AB)
            qb_ref[:, LANES * h:LANES * (h + 1)] = y.astype(CDT)
        for h in range(2):
            xh = p_ref[:, OFF_BK + LANES * h:OFF_BK + LANES * (h + 1)]
            y = _rope_val(_rms_val(xh, gkn_ref[...], LANES)[0], bc[...], ba[...], bb[...], ROPE_SHIFT_AB)
            kb_ref[:, LANES * h:LANES * (h + 1)] = y.astype(CDT)
        vb_ref[...] = p_ref[:, OFF_BV:OFF_BV + 256].astype(CDT)
        for h in range(6):
            sl = slice(LANES * h, LANES * (h + 1))
            qc_ref[:, sl] = _rope_val(p_ref[:, OFF_CQ + LANES * h:OFF_CQ + LANES * (h + 1)], cc[...], ca[...], cb[...],
                                      ROPE_SHIFT_C)
            kc_ref[:, sl] = _rope_val(p_ref[:, OFF_CK + LANES * h:OFF_CK + LANES * (h + 1)], cc[...], ca[...], cb[...],
                                      ROPE_SHIFT_C)
        vc_ref[...] = p_ref[:, OFF_CV:OFF_CV + 768]

    vec = lambda w: pl.BlockSpec((1, w), lambda i: (0, 0))
    row = lambda w: pl.BlockSpec((tr, w), lambda i: (i, 0))
    return pl.pallas_call(
        body, name="prep_fwd", grid=(S // tr,),
        in_specs=[row(PROJ_W), vec(LAT_W), vec(128), vec(128)] + _tab_specs(tr),
        out_specs=[row(LAT_W), row(128), row(768), row(256), row(256), row(768), row(768), row(768)],
        out_shape=[_sds((S, LAT_W), CDT), _sds((S, 128), CDT), _sds((S, 768), CDT), _sds((S, 256), CDT),
                   _sds((S, 256), CDT), _sds((S, 768), F32), _sds((S, 768), F32), _sds((S, 768), F32)],
        compiler_params=_cparams(("parallel",)),
    )(proj, glat, gqn, gkn, *tabs[0], *tabs[1], *tabs[2])


def _prep_a2_fwd(qkva, kpe, tab_a, *, tr=512):
    S = qkva.shape[0]
    tr = min(tr, S)

    def body(x_ref, kpe_ref, ac, aa, ab, qa_ref, ka_ref, va_ref):
        for h in range(4):
            lo, hi = 2 * LANES * h, 2 * LANES * h + LANES
            qa_ref[:, lo:hi] = x_ref[:, lo:hi].astype(CDT)
            qa_ref[:, hi:hi + LANES] = _rope_val(x_ref[:, hi:hi + LANES], ac[...], aa[...], ab[...],
                                                 ROPE_SHIFT_AB).astype(CDT)
            ka_ref[:, lo:hi] = x_ref[:, 1024 + LANES * h:1024 + LANES * (h + 1)].astype(CDT)
            ka_ref[:, hi:hi + LANES] = kpe_ref[...]
        va_ref[...] = x_ref[:, 1536:2048].astype(CDT)

    row = lambda w: pl.BlockSpec((tr, w), lambda i: (i, 0))
    return pl.pallas_call(
        body, name="prep_a2_fwd", grid=(S // tr,),
        in_specs=[row(2048), row(128)] + _tab_specs(tr)[:3],
        out_specs=[row(1024), row(1024), row(512)],
        out_shape=[_sds((S, 1024), CDT), _sds((S, 1024), CDT), _sds((S, 512), CDT)],
        compiler_params=_cparams(("parallel",)),
    )(qkva, kpe, *tab_a)


def _prep_a2_bwd(dqa, dka, dva, tab_a, *, tr=512):
    S = dqa.shape[0]
    tr = min(tr, S)

    def body(dq_ref, dk_ref, dv_ref, ac, aa, ab, dx_ref, dkr_ref):
        dkpe = jnp.zeros((tr, LANES), F32)
        for h in range(4):
            lo, hi = 2 * LANES * h, 2 * LANES * h + LANES
            dx_ref[:, lo:hi] = dq_ref[:, lo:hi].astype(CDT)
            dx_ref[:, hi:hi + LANES] = _rope_t_val(dq_ref[:, hi:hi + LANES], ac[...], aa[...], ab[...],
                                                   ROPE_SHIFT_AB).astype(CDT)
            dx_ref[:, 1024 + LANES * h:1024 + LANES * (h + 1)] = dk_ref[:, lo:hi].astype(CDT)
            dkpe = dkpe + dk_ref[:, hi:hi + LANES]
        dx_ref[:, 1536:2048] = dv_ref[...].astype(CDT)
        dkr_ref[...] = _rope_t_val(dkpe, ac[...], aa[...], ab[...], ROPE_SHIFT_AB)

    row = lambda w: pl.BlockSpec((tr, w), lambda i: (i, 0))
    return pl.pallas_call(
        body, name="prep_a2_bwd", grid=(S // tr,),
        in_specs=[row(1024), row(1024), row(512)] + _tab_specs(tr)[:3],
        out_specs=[row(2048), row(128)],
        out_shape=[_sds((S, 2048), CDT), _sds((S, 128), F32)],
        compiler_params=_cparams(("parallel",)),
    )(dqa, dka, dva, *tab_a)


def _prep_bwd(proj, glat, gqn, gkn, tabs, dlat, dkr, dqb, dkb, dvb, dqc, dkc, dvc, *, tr=256):
    S = proj.shape[0]
    tr = min(tr, S)

    def body(p_ref, glat_ref, gqn_ref, gkn_ref, ac, aa, ab, bc, ba, bb, cc, ca, cb,
             dlat_ref, dkr_ref, dqb_ref, dkb_ref, dvb_ref, dqc_ref, dkc_ref, dvc_ref,
             dp_ref, dglat_ref, dgqn_ref, dgkn_ref):
        @pl.when(pl.program_id(0) == 0)
        def _():
            dglat_ref[...] = jnp.zeros_like(dglat_ref)
            dgqn_ref[...] = jnp.zeros_like(dgqn_ref)
            dgkn_ref[...] = jnp.zeros_like(dgkn_ref)

        x = p_ref[:, 0:LAT_W]
        is_q, is_kv = _lat_masks(x.shape)
        dy, g = dlat_ref[...], glat_ref[...]
        dxq, dgq = _rms_bwd_val(jnp.where(is_q, x, 0.0), g, jnp.where(is_q, dy, 0.0), Q_LORA)
        dxkv, dgkv = _rms_bwd_val(jnp.where(is_kv, x, 0.0), g, jnp.where(is_kv, dy, 0.0), KV_HI - KV_LO)
        dglat_ref[...] += _colsum(dgq + dgkv)
        dx = dxq + dxkv
        dp_ref[:, 0:LAT_W - LANES] = dx[:, 0:LAT_W - LANES].astype(CDT)
        dp_ref[:, LAT_W - LANES:LAT_W] = (dx[:, LAT_W - LANES:LAT_W] + dkr_ref[...]).astype(CDT)
        dgqn = jnp.zeros((1, LANES), F32)
        for h in range(6):
            sl = slice(LANES * h, LANES * (h + 1))
            po = slice(OFF_BQ + LANES * h, OFF_BQ + LANES * (h + 1))
            dyh = _rope_t_val(dqb_ref[:, sl], bc[...], ba[...], bb[...], ROPE_SHIFT_AB)
            dxh, dgt = _rms_bwd_val(p_ref[:, po], gqn_ref[...], dyh, LANES)
            dp_ref[:, po] = dxh.astype(CDT)
            dgqn = dgqn + _colsum(dgt)
        dgqn_ref[...] += dgqn
        dgkn = jnp.zeros((1, LANES), F32)
        for h in range(2):
            sl = slice(LANES * h, LANES * (h + 1))
            po = slice(OFF_BK + LANES * h, OFF_BK + LANES * (h + 1))
            dyh = _rope_t_val(dkb_ref[:, sl], bc[...], ba[...], bb[...], ROPE_SHIFT_AB)
            dxh, dgt = _rms_bwd_val(p_ref[:, po], gkn_ref[...], dyh, LANES)
            dp_ref[:, po] = dxh.astype(CDT)
            dgkn = dgkn + _colsum(dgt)
        dgkn_ref[...] += dgkn
        dp_ref[:, OFF_BV:OFF_BV + 256] = dvb_ref[...].astype(CDT)
        for h in range(6):
            sl = slice(LANES * h, LANES * (h + 1))
            dp_ref[:, OFF_CQ + LANES * h:OFF_CQ + LANES * (h + 1)] = _rope_t_val(
                dqc_ref[:, sl], cc[...], ca[...], cb[...], ROPE_SHIFT_C).astype(CDT)
            dp_ref[:, OFF_CK + LANES * h:OFF_CK + LANES * (h + 1)] = _rope_t_val(
                dkc_ref[:, sl], cc[...], ca[...], cb[...], ROPE_SHIFT_C).astype(CDT)
        dp_ref[:, OFF_CV:OFF_CV + 768] = dvc_ref[...].astype(CDT)

    vec = lambda w: pl.BlockSpec((1, w), lambda i: (0, 0))
    row = lambda w: pl.BlockSpec((tr, w), lambda i: (i, 0))
    return pl.pallas_call(
        body, name="prep_bwd", grid=(S // tr,),
        in_specs=[row(PROJ_W), vec(LAT_W), vec(128), vec(128)] + _tab_specs(tr)
        + [row(LAT_W), row(128), row(768), row(256), row(256), row(768), row(768), row(768)],
        out_specs=[row(PROJ_W), vec(LAT_W), vec(128), vec(128)],
        out_shape=[_sds((S, PROJ_W), CDT), _sds((1, LAT_W), F32), _sds((1, 128), F32), _sds((1, 128), F32)],
        compiler_params=_cparams(("arbitrary",)),
    )(proj, glat, gqn, gkn, *tabs[0], *tabs[1], *tabs[2], dlat, dkr, dqb, dkb, dvb, dqc, dkc, dvc)


ATTN_TK = 512
LOG2E = 1.4426950408889634
C2_H = SCALE_H * LOG2E


def _attn_fwd(q, k, v, *, H, G, dk, dv, scale, name, tq=512, rider=None):
    S = q.shape[0]
    tq = min(tq, S)
    tk = min(ATTN_TK, S)
    c2 = scale * LOG2E

    half = tq // 2

    def body(q_ref, k_ref, v_ref, o_ref, l_ref):
        chunks = [pl.ds(c * tk, tk) for c in range(S // tk)]
        qa, qb = q_ref[0:half, :], q_ref[half:tq, :]
        rowmax = lambda s: functools.reduce(jnp.maximum, [jnp.max(sc, axis=-1, keepdims=True) for sc in s])
        s_a = [_dot(qa, k_ref[rows, :], NT) for rows in chunks]
        m_a = rowmax(s_a)
        e_a, s_b = [], []
        for c, rows in enumerate(chunks):
            e_a.append(jnp.exp2((s_a[c] - m_a) * c2))
            s_b.append(_dot(qb, k_ref[rows, :], NT))
        m_b = rowmax(s_b)
        den_a, den_b = jnp.zeros((half, 1), F32), jnp.zeros((half, 1), F32)
        acc_a, acc_b = jnp.zeros((half, dv), F32), jnp.zeros((half, dv), F32)
        e_b = []
        for c, rows in enumerate(chunks):
            acc_a = acc_a + _dot(e_a[c].astype(CDT), v_ref[rows, :], NN)
            den_a = den_a + jnp.sum(e_a[c], axis=-1, keepdims=True)
            e_b.append(jnp.exp2((s_b[c] - m_b) * c2))
        for c, rows in enumerate(chunks):
            acc_b = acc_b + _dot(e_b[c].astype(CDT), v_ref[rows, :], NN)
            den_b = den_b + jnp.sum(e_b[c], axis=-1, keepdims=True)
        o_ref[0:half, :] = acc_a * (1.0 / den_a)
        o_ref[half:tq, :] = acc_b * (1.0 / den_b)
        l_ref[0:half, :] = jnp.broadcast_to(m_a * scale + jnp.log(den_a), (half, LANES))
        l_ref[half:tq, :] = jnp.broadcast_to(m_b * scale + jnp.log(den_b), (half, LANES))

    return _carried_call(
        body, name=name, grid=(H, S // tq),
        in_specs=[pl.BlockSpec((tq, dk), lambda h, i: (i, h)), pl.BlockSpec((S, dk), lambda h, i: (0, h // G)),
                  pl.BlockSpec((S, dv), lambda h, i: (0, h // G))],
        out_specs=[pl.BlockSpec((tq, dv), lambda h, i: (i, h)), pl.BlockSpec((tq, LANES), lambda h, i: (i, h))],
        out_shape=[_sds((S, H * dv), F32), _sds((S, H * LANES), F32)], scratch_shapes=[],
        dims=("parallel", "parallel"), args=[q, k, v], rider=rider)


def _attn_bwd(q, k, v, do, lse, delta, *, H, G, dk, dv, scale, name, tq=256, rider=None):
    S = q.shape[0]
    tq = min(tq, S)
    Hkv = H // G
    c2 = scale * LOG2E

    def body(q_ref, k_ref, v_ref, do_ref, l_ref, d_ref, dq_ref, dk_ref, dv_ref):
        @pl.when((pl.program_id(1) == 0) & (pl.program_id(2) == 0))
        def _():
            dk_ref[...] = jnp.zeros_like(dk_ref)
            dv_ref[...] = jnp.zeros_like(dv_ref)

        qv, kv, dov = q_ref[...], k_ref[...], do_ref[...]
        p = jnp.exp2(_dot(qv, kv, NT) * c2 - l_ref[:, 0:1] * LOG2E)
        dp = _dot(dov, v_ref[...], NT)
        ds = (p * (dp - d_ref[:, 0:1]) * scale).astype(CDT)
        dq_ref[...] = _dot(ds, kv, NN)
        dk_ref[...] += _dot(ds, qv, TN)
        dv_ref[...] += _dot(p.astype(CDT), dov, TN)

    qi = lambda hk, g, i: (i, hk * G + g)
    return _carried_call(
        body, name=name, grid=(Hkv, G, S // tq),
        in_specs=[pl.BlockSpec((tq, dk), qi), pl.BlockSpec((S, dk), lambda hk, g, i: (0, hk)),
                  pl.BlockSpec((S, dv), lambda hk, g, i: (0, hk)), pl.BlockSpec((tq, dv), qi),
                  pl.BlockSpec((tq, LANES), qi), pl.BlockSpec((tq, LANES), qi)],
        out_specs=[pl.BlockSpec((tq, dk), qi), pl.BlockSpec((S, dk), lambda hk, g, i: (0, hk)),
                   pl.BlockSpec((S, dv), lambda hk, g, i: (0, hk))],
        out_shape=[_sds((S, H * dk), F32), _sds((S, Hkv * dk), F32), _sds((S, Hkv * dv), F32)], scratch_shapes=[],
        dims=("parallel", "arbitrary", "arbitrary"), args=[q, k, v, do, lse, delta], rider=rider)


BAND_SUB = 128
BAND_WIN = 384
BAND_UNROLL = 8


def _band_blocks(S, d):
    L = S // d
    assert L % BAND_SUB == 0 and S % (BAND_SUB * BAND_UNROLL) == 0
    return L, L // BAND_SUB, min(BAND_WIN, L)


def _band_index(blk, d, nb, L, win):
    r, jb = blk // nb, blk % nb
    l0 = jb * BAND_SUB
    w0 = jnp.clip(l0 - BAND_SUB, 0, L - win)
    return r + d * l0, r + d * w0, l0, w0


def _band_rows(start, size, d):
    return pl.ds(pl.multiple_of(start, BAND_SUB), size) if d == 1 else pl.ds(start, size, stride=d)


def _band_mask(l0, w0, win):
    rpos = l0 + lax.broadcasted_iota(jnp.int32, (BAND_SUB, win), 0)
    cpos = w0 + lax.broadcasted_iota(jnp.int32, (BAND_SUB, win), 1)
    return jnp.abs(rpos - cpos) <= BAND_HALF


def _mixc_fwd(q, k, v, rider=None):
    S, W = q.shape

    def body(q_ref, k_ref, v_ref, o_ref, l_ref, *scratch):
        ob_refs, lb_refs = scratch[0:3], scratch[3:6]
        for b, d in enumerate(DILATIONS):
            L, nb, win = _band_blocks(S, d)

            def step(it, carry, b=b, d=d, L=L, nb=nb, win=win):
                idx = [_band_index(it * BAND_UNROLL + u, d, nb, L, win) for u in range(BAND_UNROLL)]
                qrows = [_band_rows(i[0], BAND_SUB, d) for i in idx]
                krows = [_band_rows(i[1], win, d) for i in idx]
                qv = [q_ref[r, :].astype(CDT) for r in qrows]
                kw = [k_ref[r, :].astype(CDT) for r in krows]
                vw = [v_ref[r, :].astype(CDT) for r in krows]
                s = [jnp.where(_band_mask(i[2], i[3], win), _dot(a, kk, NT), NEG_INF) for i, a, kk in zip(idx, qv, kw)]
                m = [jnp.max(x, axis=-1, keepdims=True) for x in s]
                e = [jnp.exp2((x - mm) * C2_H) for x, mm in zip(s, m)]
                den = [jnp.sum(x, axis=-1, keepdims=True) for x in e]
                o = [_dot((x * (1.0 / dd)).astype(CDT), vv, NN) for x, dd, vv in zip(e, den, vw)]
                for r, ou, mm, dd in zip(qrows, o, m, den):
                    ob_refs[b][r, :] = ou
                    lb_refs[b][r, :] = jnp.broadcast_to(mm * SCALE_H + jnp.log(dd), (BAND_SUB, LANES))
                return carry

            lax.fori_loop(0, S // (BAND_SUB * BAND_UNROLL), step, 0)

        def combine(c, carry):
            rows = pl.ds(pl.multiple_of(c * 256, 256), 256)
            l0, l1, l2 = lb_refs[0][rows, :], lb_refs[1][rows, :], lb_refs[2][rows, :]
            m = jnp.maximum(jnp.maximum(l0, l1), l2)
            e0, e1, e2 = jnp.exp(l0 - m), jnp.exp(l1 - m), jnp.exp(l2 - m)
            den = e0 + e1 + e2
            inv = 1.0 / den
            o_ref[rows, :] = ((e0 * inv) * ob_refs[0][rows, :] + (e1 * inv) * ob_refs[1][rows, :]
                              + (e2 * inv) * ob_refs[2][rows, :])
            l_ref[rows, :] = m + jnp.log(den)
            return carry

        lax.fori_loop(0, S // 256, combine, 0)

    head = pl.BlockSpec((S, LANES), lambda h: (0, h))
    return _carried_call(
        body, name="mixc_fwd", grid=(W // LANES,), in_specs=[head] * 3, out_specs=[head] * 2,
        out_shape=[_sds((S, W), F32)] * 2, scratch_shapes=[pltpu.VMEM((S, LANES), F32)] * 6,
        dims=("parallel",), args=[q, k, v], rider=rider)


def _mixc_bwd(q, k, v, do, lse, dd, rider=None):
    S, W = q.shape

    def body(q_ref, k_ref, v_ref, do_ref, l_ref, d_ref, dq_ref, dk_ref, dv_ref):
        dq_ref[...] = jnp.zeros_like(dq_ref)
        dk_ref[...] = jnp.zeros_like(dk_ref)
        dv_ref[...] = jnp.zeros_like(dv_ref)
        for d in DILATIONS:
            L, nb, win = _band_blocks(S, d)

            def step(it, carry, d=d, L=L, nb=nb, win=win):
                idx = [_band_index(it * BAND_UNROLL + u, d, nb, L, win) for u in range(BAND_UNROLL)]
                qrows = [_band_rows(i[0], BAND_SUB, d) for i in idx]
                krows = [_band_rows(i[1], win, d) for i in idx]
                qv = [q_ref[r, :].astype(CDT) for r in qrows]
                dov = [do_ref[r, :].astype(CDT) for r in qrows]
                kw = [k_ref[r, :].astype(CDT) for r in krows]
                vw = [v_ref[r, :].astype(CDT) for r in krows]
                lse2 = [l_ref[r, :][:, 0:1] * LOG2E for r in qrows]
                dd = [d_ref[r, :][:, 0:1] for r in qrows]
                s = [jnp.where(_band_mask(i[2], i[3], win), _dot(a, kk, NT), NEG_INF) for i, a, kk in zip(idx, qv, kw)]
                p = [jnp.exp2(x * C2_H - ll) for x, ll in zip(s, lse2)]
                dp = [_dot(a, vv, NT) for a, vv in zip(dov, vw)]
                ds = [(pp * (x - y) * SCALE_H).astype(CDT) for pp, x, y in zip(p, dp, dd)]
                dq = [_dot(x, kk, NN) for x, kk in zip(ds, kw)]
                dk = [_dot(x, a, TN) for x, a in zip(ds, qv)]
                dv = [_dot(pp.astype(CDT), a, TN) for pp, a in zip(p, dov)]
                for u in range(BAND_UNROLL):
                    dq_ref[qrows[u], :] += dq[u]
                    dk_ref[krows[u], :] += dk[u]
                    dv_ref[krows[u], :] += dv[u]
                return carry

            lax.fori_loop(0, S // (BAND_SUB * BAND_UNROLL), step, 0)

    head = pl.BlockSpec((S, LANES), lambda h: (0, h))
    return _carried_call(
        body, name="mixc_bwd", grid=(W // LANES,), in_specs=[head] * 6, out_specs=[head] * 3,
        out_shape=[_sds((S, W), F32)] * 3, scratch_shapes=[], dims=("parallel",), args=[q, k, v, do, lse, dd],
        rider=rider)


def _outnorm_fwd(oa, ob, oc, g, *, tr=256):
    S = oa.shape[0]
    tr = min(tr, S)

    def body(a_ref, b_ref, c_ref, g_ref, m_ref):
        m_ref[:, 0:512] = (_rms_val(a_ref[...], None, 512)[0] * g_ref[:, 0:512]).astype(CDT)
        m_ref[:, 512:1280] = (_rms_val(b_ref[...], None, 768)[0] * g_ref[:, 512:1280]).astype(CDT)
        m_ref[:, 1280:2048] = (_rms_val(c_ref[...], None, 768)[0] * g_ref[:, 1280:2048]).astype(CDT)

    row = lambda w: pl.BlockSpec((tr, w), lambda i: (i, 0))
    return pl.pallas_call(
        body, name="outnorm_fwd", grid=(S // tr,),
        in_specs=[row(512), row(768), row(768), pl.BlockSpec((1, 2048), lambda i: (0, 0))],
        out_specs=row(2048), out_shape=_sds((S, 2048), CDT), compiler_params=_cparams(("parallel",)),
    )(oa, ob, oc, g)


def _outnorm_bwd(oa, ob, oc, g, dm, *, tr=256):
    S = oa.shape[0]
    tr = min(tr, S)

    def body(a_ref, b_ref, c_ref, g_ref, dm_ref, doa_ref, dob_ref, doc_ref, da_ref, db_ref, dc_ref, dg_ref):
        @pl.when(pl.program_id(0) == 0)
        def _():
            dg_ref[...] = jnp.zeros_like(dg_ref)

        for o_ref, do_ref, d_ref, lo, w in ((a_ref, doa_ref, da_ref, 0, 512), (b_ref, dob_ref, db_ref, 512, 768),
                                            (c_ref, doc_ref, dc_ref, 1280, 768)):
            o = o_ref[...]
            dmv = dm_ref[:, lo:lo + w]
            do, _ = _rms_bwd_val(o, None, dmv * g_ref[:, lo:lo + w], w)
            r = lax.rsqrt(jnp.sum(o * o, axis=-1, keepdims=True) * (1.0 / w) + EPS)
            dg_ref[:, lo:lo + w] += _colsum(dmv * (o * r))
            do_ref[...] = do.astype(do_ref.dtype)
            for h in range(w // LANES):
                sl = slice(LANES * h, LANES * (h + 1))
                d_ref[:, sl] = jnp.broadcast_to(jnp.sum(do[:, sl] * o[:, sl], axis=-1, keepdims=True), (tr, LANES))

    row = lambda w: pl.BlockSpec((tr, w), lambda i: (i, 0))
    vec = pl.BlockSpec((1, 2048), lambda i: (0, 0))
    return pl.pallas_call(
        body, name="outnorm_bwd", grid=(S // tr,),
        in_specs=[row(512), row(768), row(768), vec, row(2048)],
        out_specs=[row(512), row(768), row(768), row(512), row(768), row(768), vec],
        out_shape=[_sds((S, 512), CDT), _sds((S, 768), CDT), _sds((S, 768), F32), _sds((S, 512), F32),
                   _sds((S, 768), F32), _sds((S, 768), F32), _sds((1, 2048), F32)],
        compiler_params=_cparams(("arbitrary",)),
    )(oa, ob, oc, g, dm)


def _row_tile(r, c, itemsize, limit=1 << 20):
    best = 16
    for t in range(16, r + 1, 16):
        if r % t == 0 and t * c * itemsize <= limit:
            best = t
    return best


def _rs_chip_sum(g, got, *, name):
    _, _, r, c = g.shape
    tr = _row_tile(r, c, 2)
    core = lax.axis_index("c").astype(jnp.int32).reshape(1)

    def body(c_ref, a_ref, b_ref, o_ref):
        o_ref[...] = (a_ref[...].astype(F32) + b_ref[...].astype(F32)).astype(o_ref.dtype)

    spec = pltpu.PrefetchScalarGridSpec(
        num_scalar_prefetch=1, grid=(4, r // tr),
        in_specs=[pl.BlockSpec((None, None, tr, c), lambda k, i, cr: (k, cr[0], i, 0)),
                  pl.BlockSpec((None, tr, c), lambda k, i, cr: (k, i, 0))],
        out_specs=pl.BlockSpec((None, tr, c), lambda k, i, cr: (k, i, 0)))
    return pl.pallas_call(body, name=name, grid_spec=spec, out_shape=_sds((4, r, c), g.dtype),
                          compiler_params=_cparams(("parallel", "parallel")))(core, g, got)


def _rs_final_sum(r4, *, name):
    _, r, c = r4.shape
    tr = _row_tile(r, c, 4)

    def body(r_ref, o_ref):
        o_ref[...] = ((r_ref[0].astype(F32) + r_ref[1].astype(F32)) + r_ref[2].astype(F32)) + r_ref[3].astype(F32)

    return pl.pallas_call(
        body, name=name, grid=(r // tr,), in_specs=[pl.BlockSpec((4, tr, c), lambda i: (0, i, 0))],
        out_specs=pl.BlockSpec((tr, c), lambda i: (i, 0)), out_shape=_sds((r, c), F32),
        compiler_params=_cparams(("parallel",)))(r4)


def _all_reduce_small(v):
    R = v.shape[0]

    def body(v_ref, out_ref, buf_ref, send_sems, recv_sems):
        x, y, c = lax.axis_index("x"), lax.axis_index("y"), lax.axis_index("c")
        me = 4 * x + 2 * y + c
        buf_ref[me] = v_ref[...]
        peers = []
        for r in range(1, 8):
            px, py, pc = x ^ (r >> 2), y ^ ((r >> 1) & 1), c ^ (r & 1)
            peers.append((r, (px, py, pc), 4 * px + 2 * py + pc))
        sends = [pltpu.make_async_remote_copy(
            src_ref=v_ref, dst_ref=buf_ref.at[me], send_sem=send_sems.at[r - 1], recv_sem=recv_sems.at[r - 1],
            device_id=dev, device_id_type=MESH) for r, dev, _ in peers]
        for cp in sends:
            cp.start()
        for r, dev, idx in peers:
            pltpu.make_async_remote_copy(
                src_ref=v_ref, dst_ref=buf_ref.at[idx], send_sem=send_sems.at[r - 1], recv_sem=recv_sems.at[r - 1],
                device_id=dev, device_id_type=MESH).wait_recv()
        for cp in sends:
            cp.wait_send()
        acc = buf_ref[0]
        for k in range(1, 8):
            acc = acc + buf_ref[k]
        out_ref[...] = acc

    vm = pl.BlockSpec(memory_space=pltpu.VMEM)
    return pl.pallas_call(
        body, name="all_reduce_small", out_shape=_sds((R, LANES), F32), in_specs=[vm], out_specs=vm,
        scratch_shapes=[pltpu.VMEM((8, R, LANES), F32), pltpu.SemaphoreType.DMA((7,)), pltpu.SemaphoreType.DMA((7,))],
    )(v)


def _adamw(w, g, m, v, *, name):
    R, C = w.shape
    tr = R
    for cand in (1024, 512, 256, 128, 64, 32, 16, 8):
        if R % cand == 0 and cand * C * 4 <= 2 * 1024 * 1024:
            tr = cand
            break

    def body(w_ref, g_ref, m_ref, v_ref, d_ref, nm_ref, nv_ref):
        gv = g_ref[...]
        mn = ADAM_B1 * m_ref[...] + (1.0 - ADAM_B1) * gv
        vn = ADAM_B2 * v_ref[...] + (1.0 - ADAM_B2) * (gv * gv)
        m_hat = mn / (1.0 - ADAM_B1 ** ADAM_STEP)
        v_hat = vn / (1.0 - ADAM_B2 ** ADAM_STEP)
        d_ref[...] = -ADAM_LR * (m_hat / (jnp.sqrt(v_hat) + ADAM_EPS) + ADAM_WD * w_ref[...])
        nm_ref[...] = mn
        nv_ref[...] = vn

    blk = pl.BlockSpec((tr, C), lambda i: (i, 0))
    return pl.pallas_call(
        body, name=name, grid=(R // tr,), in_specs=[blk] * 4, out_specs=[blk] * 3,
        out_shape=[_sds((R, C), F32)] * 3, compiler_params=_cparams(("parallel",)))(w, g, m, v)


def _wuq_pad(w):
    w = w.reshape(448, 4, 192)
    z = jnp.zeros((448, 4, 64), w.dtype)
    return jnp.concatenate([w[:, :, 0:128], z, w[:, :, 128:192]], axis=2).reshape(448, 1024)


def _wuq_unpad(w):
    w = w.reshape(448, 4, 256)
    return jnp.concatenate([w[:, :, 0:128], w[:, :, 192:256]], axis=2).reshape(448, 768)


def _wukv_perm(w):
    return w.reshape(512, 4, 2, 128).transpose(0, 2, 1, 3).reshape(512, 1024)


def _wukv_unperm(w):
    return w.reshape(512, 2, 4, 128).transpose(0, 2, 1, 3).reshape(512, 1024)


def _lat_weight(w_uq, w_ukv):
    z = lambda r, c: jnp.zeros((r, c), w_uq.dtype)
    top = jnp.concatenate([_wuq_pad(w_uq), z(448, 1024)], axis=1)
    mid = jnp.concatenate([z(512, 1024), _wukv_perm(w_ukv)], axis=1)
    return jnp.concatenate([top, mid, z(64, 2048)], axis=0)


def _lat_weight_grads(dw):
    return _wuq_unpad(dw[0:KV_LO, 0:1024]), _wukv_unperm(dw[KV_LO:KV_HI, 1024:2048])


def _comm_shards(w_in, w_uq, w_ukv, w_out, w_ff1, w_ff2):
    lat = jnp.concatenate([w_uq.reshape(UQ_ROWS, LANES), w_ukv.reshape(512, LANES)], axis=0)
    return [w_in.T, w_ff1.T, w_out, w_ff2, lat]


def _from_comm_shards(parts):
    w_in_t, w_ff1_t, w_out, w_ff2, lat = parts
    return {"w_in": w_in_t.T, "w_ff1": w_ff1_t.T, "w_out": w_out, "w_ff2": w_ff2,
            "w_uq": lat[0:UQ_ROWS].reshape(448, 96), "w_ukv": lat[UQ_ROWS:].reshape(512, 128)}


def _early_weights(g_in_t, g_lat):
    w_uq = g_lat[:, 0:UQ_ROWS].reshape(8, 448, 96).transpose(1, 0, 2).reshape(448, 768)
    w_ukv = g_lat[:, UQ_ROWS:].reshape(8, 512, 128).transpose(1, 0, 2).reshape(512, 1024)
    return g_in_t.reshape(PROJ_W, D_MODEL), _lat_weight(w_uq, w_ukv)


def _layer_fwd(x, W, G, tabs, plan=None):
    W = dict(W)
    sh, nxt, f_half = plan if plan is not None else (None, None, None)
    first = plan is not None and f_half is None
    early_next = None if nxt is None else [nxt["w_in_t"], nxt["lat"]]
    ag1 = lambda arrays, tag: None if (plan is None or arrays is None) else _ag_first_rider(arrays, tag)
    ag2 = lambda arrays, tag: None if (plan is None or arrays is None) else _ag_second_rider(arrays, tag)
    s = {"x0": x}
    s["h1"] = _rms_fwd(x, G["ln1_g"], name="rms1_fwd")
    s["proj"] = _matmul(s["h1"], W["w_in_t"], mode="nt", tm=1024, tn=768, tk=2048, out_dtype=F32, name="mm_in")
    s["lat"], kpe, s["qb"], s["kb"], s["vb"], s["qc"], s["kc"], s["vc"] = _prep_fwd(
        s["proj"], G["glat"], G["gqn"], G["gkn"], tabs)
    qkva = _matmul(s["lat"], W["w_lat"], mode="nn", tm=1024, tn=1024, tk=1024, out_dtype=F32, name="mm_lat")
    s["qa"], s["ka"], s["va"] = _prep_a2_fwd(qkva, kpe, tabs[0])
    early, f_half_next = None, None
    attn_a = functools.partial(_attn_fwd, s["qa"], s["ka"], s["va"], H=4, G=1, dk=256, dv=128, scale=SCALE_A,
                               name="attn_a_fwd")
    attn_b = functools.partial(_attn_fwd, s["qb"], s["kb"], s["vb"], H=6, G=3, dk=128, dv=128, scale=SCALE_H,
                               name="attn_b_fwd")
    if plan is None:
        (s["oa"], s["lse_a"]), _ = attn_a()
        (s["ob"], s["lse_b"]), _ = attn_b()
        (s["oc"], s["lse_c"]), _ = _mixc_fwd(s["qc"], s["kc"], s["vc"])
    elif first:
        (s["oa"], s["lse_a"]), got_a = attn_a(rider=ag1([sh["w_ff1_t"]], "_ff1"))
        (s["ob"], s["lse_b"]), got_b = attn_b(rider=ag1([sh["w_ff2"], sh["w_out"]], "_ff2_out"))
        (s["oc"], s["lse_c"]), got_c = _mixc_fwd(s["qc"], s["kc"], s["vc"],
                                                 rider=_join(ag2(got_a + got_b, "_ff_out"), ag1(early_next, "_early")))
        full, early_half = got_c[0:3], got_c[3:5]
    else:
        (s["oa"], s["lse_a"]), early_half = attn_a(rider=ag1(early_next, "_early"))
        (s["ob"], s["lse_b"]), got_b = attn_b(rider=_join(ag1([sh["w_out"]], "_out"), ag2(f_half, "_ff")))
        (s["oc"], s["lse_c"]), got_c = _mixc_fwd(
            s["qc"], s["kc"], s["vc"], rider=_join(ag2(got_b[0:1], "_out"), ag2(early_half or None, "_early")))
        full, early = [got_b[1], got_b[2], got_c[0]], (got_c[1:3] or None)
    if plan is not None:
        W["w_ff1_t"], W["w_ff2"] = full[0].reshape(D_FF, D_MODEL), full[1].reshape(D_FF, D_MODEL)
        W["w_out"] = full[2].reshape(D_MODEL, D_MODEL)
    s["mixed"] = _outnorm_fwd(s["oa"], s["ob"], s["oc"], G["g_out"])
    s["x1"] = _matmul(s["mixed"], W["w_out"], mode="nn", tm=1024, tn=1024, tk=2048, out_dtype=F32, name="mm_out",
                      epi="residual", extra=x)
    s["h2"] = _rms_fwd(s["x1"], G["ln2_g"], name="rms2_fwd")
    ff1 = functools.partial(_matmul, s["h2"], W["w_ff1_t"], mode="nt", tm=1024, tn=1024, tk=2048, out_dtype=CDT,
                            name="mm_ff1", epi="relu2")
    if nxt is None:
        s["z"], s["u"] = ff1()
        x2 = _matmul(s["u"], W["w_ff2"], mode="nn", tm=1024, tn=1024, tk=2048, out_dtype=F32, name="mm_ff2",
                     epi="residual", extra=s["x1"])
    else:
        (s["z"], s["u"]), got = ff1(rider=_join(ag2(early_half, "_early") if first else None,
                                                ag1([nxt["w_ff1_t"]], "_ff1")))
        if first:
            early, got = got[0:2], got[2:]
        x2, got2 = _matmul(s["u"], W["w_ff2"], mode="nn", tm=1024, tn=1024, tk=2048, out_dtype=F32, name="mm_ff2",
                           epi="residual", extra=s["x1"], rider=ag1([nxt["w_ff2"]], "_ff2"))
        f_half_next = [got[0], got2[0]]
    return x2, s, W, early, f_half_next


def _by_destination(dw, name):
    return dw.reshape((4, 2) + COMM_SHAPE[name])


def _early_by_destination(dw_in_t, dw_lat):
    dw_uq, dw_ukv = _lat_weight_grads(dw_lat)
    lat = jnp.concatenate([dw_uq.reshape(448, 8, 96).transpose(1, 0, 2).reshape(8, UQ_ROWS, LANES),
                           dw_ukv.reshape(512, 8, 128).transpose(1, 0, 2)], axis=1)
    return [_by_destination(dw_in_t, "w_in_t"), _by_destination(lat, "lat")]


def _layer_bwd(dx2, dx2b, s, W, G, tabs, scatter=False, pending=None):
    dw, dg, landed = {}, {}, {}
    ff2_dx = functools.partial(_matmul, dx2b, W["w_ff2"], mode="nt", tm=1024, tn=1024, tk=2048, out_dtype=CDT,
                               name="mm_ff2_dx", epi="drelu2", extra=s["z"])
    if pending is None:
        dz = ff2_dx()
    else:
        dz, got = ff2_dx(rider=_rs_sibling_rider(pending, "_early"))
        chip = [_rs_chip_sum(g, r, name="rs_chip_sum_" + n) for g, r, n in zip(pending, got, ("w_in_t", "lat"))]
    dw["w_ff2"] = _matmul(s["u"], dx2b, mode="tn", tm=2048, tn=1024, tk=2048, out_dtype=CDT, name="mm_ff2_dw")
    ff1_dx = functools.partial(_matmul, dz, W["w_ff1_t"], mode="nn", tm=1024, tn=1024, tk=2048, out_dtype=F32,
                               name="mm_ff1_dx")
    if pending is None:
        dh2 = ff1_dx()
    else:
        dh2, got = ff1_dx(rider=_rs_chip_rider(chip, "_early"))
        landed["above_w_in_t"], landed["above_lat"] = got
    dw["w_ff1_t"] = _matmul(dz, s["h2"], mode="tn", tm=2048, tn=1024, tk=2048, out_dtype=CDT, name="mm_ff1_dw")
    dx1, dx1b, dg["ln2_g"] = _rms_bwd(s["x1"], G["ln2_g"], dh2, dx2, name="rms2_bwd")
    dmixed = _matmul(dx1b, W["w_out"], mode="nt", tm=1024, tn=1024, tk=2048, out_dtype=F32, name="mm_out_dx")
    out_dw = functools.partial(_matmul, s["mixed"], dx1b, mode="tn", tm=1024, tn=1024, tk=2048, out_dtype=CDT,
                               name="mm_out_dw")
    if not scatter:
        dw["w_out"] = out_dw()
        riders = [None, None, None]
    else:
        g_ff = [_by_destination(dw["w_ff2"], "w_ff2"), _by_destination(dw["w_ff1_t"], "w_ff1_t")]
        dw["w_out"], got = out_dw(rider=_rs_sibling_rider(g_ff, "_ff"))
        chip_ff2 = _rs_chip_sum(g_ff[0], got[0], name="rs_chip_sum_w_ff2")
        chip_ff1 = _rs_chip_sum(g_ff[1], got[1], name="rs_chip_sum_w_ff1_t")
        g_out = [_by_destination(dw["w_out"], "w_out")]
        riders = [_rs_chip_rider([chip_ff2], "_ff2"),
                  _join(_rs_chip_rider([chip_ff1], "_ff1"), _rs_sibling_rider(g_out, "_out")), None]
    doa, dob, doc, dla, dlb, dlc, dg["g_out"] = _outnorm_bwd(s["oa"], s["ob"], s["oc"], G["g_out"], dmixed)
    (dqa, dka, dva), got = _attn_bwd(s["qa"], s["ka"], s["va"], doa, s["lse_a"], dla, H=4, G=1, dk=256, dv=128,
                                     scale=SCALE_A, name="attn_a_bwd", rider=riders[0])
    if scatter:
        landed["w_ff2"] = got[0]
    (dqb, dkb, dvb), got = _attn_bwd(s["qb"], s["kb"], s["vb"], dob, s["lse_b"], dlb, H=6, G=3, dk=128, dv=128,
                                     scale=SCALE_H, name="attn_b_bwd", rider=riders[1])
    if scatter:
        landed["w_ff1_t"] = got[0]
        riders[2] = _rs_chip_rider([_rs_chip_sum(g_out[0], got[1], name="rs_chip_sum_w_out")], "_out")
    (dqc, dkc, dvc), got = _mixc_bwd(s["qc"], s["kc"], s["vc"], doc, s["lse_c"], dlc, rider=riders[2])
    if scatter:
        landed["w_out"] = got[0]
    dqkva, dkr = _prep_a2_bwd(dqa, dka, dva, tabs[0])
    dlat = _matmul(dqkva, W["w_lat"], mode="nt", tm=1024, tn=1024, tk=2048, out_dtype=F32, name="mm_lat_dx")
    dw["w_lat"] = _matmul(s["lat"], dqkva, mode="tn", tm=1024, tn=2048, tk=2048, out_dtype=CDT, name="mm_lat_dw")
    dproj, dg["glat"], dg["gqn"], dg["gkn"] = _prep_bwd(
        s["proj"], G["glat"], G["gqn"], G["gkn"], tabs, dlat, dkr, dqb, dkb, dvb, dqc, dkc, dvc)
    dh1 = _matmul(dproj, W["w_in_t"], mode="nn", tm=1024, tn=1024, tk=2304, out_dtype=F32, name="mm_in_dx")
    dw["w_in_t"] = _matmul(dproj, s["h1"], mode="tn", tm=1536, tn=1024, tk=2048, out_dtype=CDT, name="mm_in_dw")
    dx0, dx0b, dg["ln1_g"] = _rms_bwd(s["x0"], G["ln1_g"], dh1, dx1, name="rms1_bwd")
    return dx0, dx0b, dw, dg, landed


def _layer_gains(l, ln1_g, g_q_a, g_kv_a, g_qn_b, g_kn_b, g_out, ln2_g):
    return {"ln1_g": ln1_g[l], "ln2_g": ln2_g[l], "g_out": g_out[l].reshape(1, 2048),
            "glat": jnp.concatenate([g_q_a[l], g_kv_a[l], jnp.zeros((LAT_W - KV_HI,), F32)]).reshape(1, LAT_W),
            "gqn": g_qn_b[l].reshape(1, 128), "gkn": g_kn_b[l].reshape(1, 128)}


def _gain_grads(dg):
    glat = dg["glat"].reshape(-1)
    return {"ln1_g": dg["ln1_g"].reshape(-1), "g_q_a": glat[0:KV_LO], "g_kv_a": glat[KV_LO:KV_HI],
            "g_qn_b": dg["gqn"].reshape(-1), "g_kn_b": dg["gkn"].reshape(-1), "g_out": dg["g_out"].reshape(-1),
            "ln2_g": dg["ln2_g"].reshape(-1)}


def _local_step(x, tgt, weights, gains, ln_f_g):
    S = x.shape[0]
    tabs = _rope_tables(S)
    depth = len(weights)
    saved = []
    for l in range(depth):
        x, s, _, _, _ = _layer_fwd(x, weights[l], gains[l], tabs)
        saved.append(s)
    loss, dx, dxb, dlnf = _loss_head(x, ln_f_g, tgt)
    dws, dgs = [None] * depth, [None] * depth
    for l in reversed(range(depth)):
        dx, dxb, dws[l], dgs[l], _ = _layer_bwd(dx, dxb, saved[l], weights[l], gains[l], tabs)
    return loss, dx, dws, dgs, dlnf


SMALL_SIZES = (("ln1_g", 2048), ("g_q_a", 448), ("g_kv_a", 512), ("g_qn_b", 128), ("g_kn_b", 128), ("g_out", 2048),
               ("ln2_g", 2048))


def _pack_small(per_layer, ln_f):
    flat = jnp.concatenate([per_layer[n].reshape(-1) for n, _ in SMALL_SIZES] + [ln_f.reshape(-1)])
    rows = -(-flat.shape[0] // (8 * LANES)) * 8
    return jnp.concatenate([flat, jnp.zeros((rows * LANES - flat.shape[0],), F32)]).reshape(rows, LANES)


def _unpack_small(packed, depth):
    flat, out, lo = packed.reshape(-1), {}, 0
    for n, w in SMALL_SIZES:
        out[n] = flat[lo:lo + depth * w].reshape(depth, w)
        lo += depth * w
    out["ln_f_g"] = flat[lo:lo + 2048]
    return out


def kernel(x, ln1_g, w_in, g_q_a, w_uq, g_kv_a, w_ukv, g_qn_b, g_kn_b, g_out, w_out, ln2_g, w_ff1, w_ff2, ln_f_g, loss_target, m_ln1_g, m_w_in, m_g_q_a, m_w_uq, m_g_kv_a, m_w_ukv, m_g_qn_b, m_g_kn_b, m_g_out, m_w_out, m_ln2_g, m_w_ff1, m_w_ff2, m_ln_f_g, v_ln1_g, v_w_in, v_g_q_a, v_w_uq, v_g_kv_a, v_w_ukv, v_g_qn_b, v_g_kn_b, v_g_out, v_w_out, v_ln2_g, v_w_ff1, v_w_ff2, v_ln_f_g):
    depth = w_in.shape[0]
    S = x.shape[1]
    big_w = {"w_in": w_in, "w_uq": w_uq, "w_ukv": w_ukv, "w_out": w_out, "w_ff1": w_ff1, "w_ff2": w_ff2}
    big_m = {"w_in": m_w_in, "w_uq": m_w_uq, "w_ukv": m_w_ukv, "w_out": m_w_out, "w_ff1": m_w_ff1, "w_ff2": m_w_ff2}
    big_v = {"w_in": v_w_in, "w_uq": v_w_uq, "w_ukv": v_w_ukv, "w_out": v_w_out, "w_ff1": v_w_ff1, "w_ff2": v_w_ff2}
    small_w = {"ln1_g": ln1_g, "g_q_a": g_q_a, "g_kv_a": g_kv_a, "g_qn_b": g_qn_b, "g_kn_b": g_kn_b, "g_out": g_out,
               "ln2_g": ln2_g}
    small_m = {"ln1_g": m_ln1_g, "g_q_a": m_g_q_a, "g_kv_a": m_g_kv_a, "g_qn_b": m_g_qn_b, "g_kn_b": m_g_kn_b,
               "g_out": m_g_out, "ln2_g": m_ln2_g}
    small_v = {"ln1_g": v_ln1_g, "g_q_a": v_g_q_a, "g_kv_a": v_g_kv_a, "g_qn_b": v_g_qn_b, "g_kn_b": v_g_kn_b,
               "g_out": v_g_out, "ln2_g": v_ln2_g}

    shards = [dict(zip(COMM, _comm_shards(*[big_w[n][l].astype(CDT) for n in BIG]))) for l in range(depth)]
    early_shards = [[sh["w_in_t"], sh["lat"]] for sh in shards]
    early = _run_rider(_ag_second_rider(_run_rider(_ag_first_rider(early_shards[0], "_early")), "_early"))
    gains = [_layer_gains(l, ln1_g, g_q_a, g_kv_a, g_qn_b, g_kn_b, g_out, ln2_g) for l in range(depth)]
    tabs = _rope_tables(S)

    h = x.reshape(S, D_MODEL)
    saved, weights, f_half = [], [], None
    for l in range(depth):
        W = dict(zip(("w_in_t", "w_lat"), _early_weights(*early)))
        h, s, W, early, f_half = _layer_fwd(h, W, gains[l], tabs,
                                            plan=(shards[l], shards[l + 1] if l + 1 < depth else None, f_half))
        saved.append(s)
        weights.append(W)
    loss_part, dx, dxb, dlnf = _loss_head(h, ln_f_g, loss_target.reshape(S, D_MODEL))
    loss = lax.psum(loss_part[0, 0], ("x", "y", "c"))

    dgs, landed, pending = [None] * depth, [None] * depth, None
    for l in reversed(range(depth)):
        dx, dxb, dw, dgs[l], landed[l] = _layer_bwd(dx, dxb, saved[l], weights[l], gains[l], tabs, scatter=True,
                                                    pending=pending)
        if pending is not None:
            landed[l + 1]["w_in_t"], landed[l + 1]["lat"] = landed[l].pop("above_w_in_t"), landed[l].pop("above_lat")
        pending = _early_by_destination(dw["w_in_t"], dw["w_lat"])
    got = _run_rider(_rs_sibling_rider(pending, "_early"))
    chip = [_rs_chip_sum(g, r, name="rs_chip_sum_" + n) for g, r, n in zip(pending, got, ("w_in_t", "lat"))]
    landed[0]["w_in_t"], landed[0]["lat"] = _run_rider(_rs_chip_rider(chip, "_early"))
    grad_x = dx.reshape(1, S, D_MODEL)

    shard_grads = [_from_comm_shards([_rs_final_sum(landed[l][n], name="rs_final_sum_" + n) for n in COMM])
                   for l in range(depth)]
    big_g = {n: jnp.stack([shard_grads[l][n] for l in range(depth)]) for n in BIG}

    named = [_gain_grads(dgs[l]) for l in range(depth)]
    per_layer = {n: jnp.stack([named[l][n] for l in range(depth)]) for n, _ in SMALL_SIZES}
    small_g = _unpack_small(_all_reduce_small(_pack_small(per_layer, dlnf.reshape(-1))), depth)

    upd = {}
    for n in BIG:
        shp = big_w[n].shape
        two_d = (shp[0] * shp[1], shp[2])
        d, nm, nv = _adamw(big_w[n].reshape(two_d), big_g[n].reshape(two_d), big_m[n].reshape(two_d),
                           big_v[n].reshape(two_d), name="adamw_" + n)
        upd[n] = (d.reshape(shp), nm.reshape(shp), nv.reshape(shp))
    small_w["ln_f_g"], small_m["ln_f_g"], small_v["ln_f_g"] = ln_f_g, m_ln_f_g, v_ln_f_g
    names_small = [n for n, _ in SMALL_SIZES]
    pw = _pack_small({n: small_w[n] for n in names_small}, small_w["ln_f_g"])
    pg = _pack_small({n: small_g[n] for n in names_small}, small_g["ln_f_g"])
    pm = _pack_small({n: small_m[n] for n in names_small}, small_m["ln_f_g"])
    pv = _pack_small({n: small_v[n] for n in names_small}, small_v["ln_f_g"])
    d, nm, nv = _adamw(pw, pg, pm, pv, name="adamw_small")
    sd, snm, snv = _unpack_small(d, depth), _unpack_small(nm, depth), _unpack_small(nv, depth)
    for n in names_small + ["ln_f_g"]:
        upd[n] = (sd[n], snm[n], snv[n])

    order = ["ln1_g", "w_in", "g_q_a", "w_uq", "g_kv_a", "w_ukv", "g_qn_b", "g_kn_b", "g_out", "w_out", "ln2_g", "w_ff1",
             "w_ff2", "ln_f_g"]
    grads = {**big_g, **small_g}
    return (loss, grad_x, *[grads[n] for n in order], *[upd[n][0] for n in order], *[upd[n][1] for n in order],
            *[upd[n][2] for n in order])
```

```python
import functools
import math

import jax
import jax.numpy as jnp
from jax import lax
from jax.experimental import pallas as pl
from jax.experimental.pallas import tpu as pltpu

D_MODEL = 2048
D_FF = 8192
EPS = 1e-6
NEG_INF = -1e30
Q_LORA = 448
ROPE_THETA = 10000.0
GRID_W = 64
DILATIONS = (1, 4, 16)
BAND_HALF = 64
SCALE_A = 1.0 / math.sqrt(192.0)
SCALE_H = 1.0 / math.sqrt(128.0)
ADAM_LR, ADAM_B1, ADAM_B2, ADAM_EPS, ADAM_WD, ADAM_STEP = 0.001, 0.9, 0.999, 1e-08, 0.01, 10

CDT = jnp.bfloat16
F32 = jnp.float32
LANES = 128
VMEM_LIMIT = 56 * 1024 * 1024

PROJ_W = 4608
LAT_W = 1024
KV_LO, KV_HI = 448, 960
OFF_BQ, OFF_BK, OFF_BV, OFF_CQ, OFF_CK, OFF_CV = 1024, 1792, 2048, 2304, 3072, 3840

NN = ((1,), (0,))
NT = ((1,), (1,))
TN = ((0,), (0,))

BIG = ("w_in", "w_uq", "w_ukv", "w_out", "w_ff1", "w_ff2")
COMM = ("w_in_t", "w_ff1_t", "w_out", "w_ff2", "lat")
COMM_SHAPE = {"w_in_t": (576, 2048), "w_ff1_t": (1024, 2048), "w_out": (256, 2048), "w_ff2": (1024, 2048),
              "lat": (848, 128)}
UQ_ROWS = 448 * 96 // LANES


def _dot(a, b, dims):
    return lax.dot_general(a, b, (dims, ((), ())), preferred_element_type=F32)


def _cparams(dims=None):
    return pltpu.CompilerParams(dimension_semantics=dims, vmem_limit_bytes=VMEM_LIMIT)


def _sds(shape, dtype):
    return jax.ShapeDtypeStruct(shape, dtype)


MESH = pl.DeviceIdType.MESH
ANY = pl.BlockSpec(memory_space=pl.ANY)


class _Rider:
    def __init__(self, name, arrays, out_shape, scratch, aliases, start, finish):
        self.name, self.arrays, self.out_shape, self.scratch = name, list(arrays), list(out_shape), list(scratch)
        self.aliases, self.start, self.finish = dict(aliases), start, finish


def _join(a, b):
    if a is None or b is None:
        return a if b is None else b
    na, oa, sa = len(a.arrays), len(a.out_shape), len(a.scratch)
    aliases = dict(a.aliases)
    aliases.update({na + i: oa + o for i, o in b.aliases.items()})

    def start(ins, outs, sems):
        a.start(ins[:na], outs[:oa], sems[:sa])
        b.start(ins[na:], outs[oa:], sems[sa:])

    def finish(ins, outs, sems):
        a.finish(ins[:na], outs[:oa], sems[:sa])
        b.finish(ins[na:], outs[oa:], sems[sa:])

    return _Rider(a.name + "_" + b.name, a.arrays + b.arrays, a.out_shape + b.out_shape, a.scratch + b.scratch,
                  aliases, start, finish)


def _carried_call(body, *, name, grid, in_specs, out_specs, out_shape, scratch_shapes, dims, args, rider):
    in_specs, out_specs, out_shape = list(in_specs), list(out_specs), list(out_shape)
    scratch_shapes = list(scratch_shapes)
    if rider is None:
        res = pl.pallas_call(body, name=name, grid=grid, in_specs=in_specs, out_specs=out_specs, out_shape=out_shape,
                             scratch_shapes=scratch_shapes, compiler_params=_cparams(dims))(*args)
        return list(res), []
    n_in, n_out, n_scr = len(in_specs), len(out_specs), len(scratch_shapes)
    r_in, r_out = len(rider.arrays), len(rider.out_shape)

    def wrapped(*refs):
        o0 = n_in + r_in
        s0 = o0 + n_out + r_out
        ins, outs, sems = refs[n_in:o0], refs[o0 + n_out:s0], refs[s0 + n_scr:]
        ids = [pl.program_id(a) for a in range(len(grid))]
        first = functools.reduce(jnp.logical_and, [i == 0 for i in ids])
        last = functools.reduce(jnp.logical_and, [i == g - 1 for i, g in zip(ids, grid)])

        @pl.when(first)
        def _():
            rider.start(ins, outs, sems)

        body(*refs[:n_in], *refs[o0:o0 + n_out], *refs[s0:s0 + n_scr])

        @pl.when(last)
        def _():
            rider.finish(ins, outs, sems)

    res = pl.pallas_call(
        wrapped, name=name + "_" + rider.name, grid=grid, in_specs=in_specs + [ANY] * r_in,
        out_specs=out_specs + [ANY] * r_out, out_shape=out_shape + rider.out_shape,
        scratch_shapes=scratch_shapes + rider.scratch,
        input_output_aliases={n_in + i: n_out + o for i, o in rider.aliases.items()},
        compiler_params=_cparams(("arbitrary",) * len(grid)),
    )(*args, *rider.arrays)
    return list(res[:n_out]), list(res[n_out:])


def _run_rider(rider):
    def body(*refs):
        r_in, r_out = len(rider.arrays), len(rider.out_shape)
        ins, outs, sems = refs[:r_in], refs[r_in:r_in + r_out], refs[r_in + r_out:]
        rider.start(ins, outs, sems)
        rider.finish(ins, outs, sems)

    res = pl.pallas_call(
        body, name=rider.name, in_specs=[ANY] * len(rider.arrays), out_specs=[ANY] * len(rider.out_shape),
        out_shape=rider.out_shape, scratch_shapes=rider.scratch, input_output_aliases=rider.aliases,
    )(*rider.arrays)
    return list(res)


def _mesh_place():
    x, y, c = lax.axis_index("x"), lax.axis_index("y"), lax.axis_index("c")
    return x, y, c, [(1 - x, y), (x, 1 - y), (1 - x, 1 - y)]


def _remote(src, dst, send, recv, dev):
    return pltpu.make_async_remote_copy(src_ref=src, dst_ref=dst, send_sem=send, recv_sem=recv, device_id=dev,
                                        device_id_type=MESH)


def _ag_first_rider(shards, tag, part=(0, 1), into=None):
    n = len(shards)

    def copies(ins, outs, sems):
        send, recv, _ = sems
        x, y, c, chips = _mesh_place()
        me = 4 * x + 2 * y + c
        peers = [(x, y, 1 - c)] + [(cx, cy, c) for cx, cy in chips]
        out, mine = [], []
        for t in range(n):
            size = shards[t].shape[0] // part[1]
            rows = pl.ds(part[0] * size, size)
            for k, dev in enumerate(peers):
                theirs = 4 * dev[0] + 2 * dev[1] + dev[2]
                out.append((_remote(ins[t].at[rows], outs[t].at[me, rows], send.at[t, k], recv.at[t, k], dev),
                            _remote(ins[t].at[rows], outs[t].at[theirs, rows], send.at[t, k], recv.at[t, k], dev)))
            mine.append(pltpu.make_async_copy(ins[t].at[rows], outs[t].at[me, rows], sems[2].at[t]))
        return out, mine

    def start(ins, outs, sems):
        pairs, mine = copies(ins, outs, sems)
        for cp in mine:
            cp.start()
        for snd, _ in pairs:
            snd.start()

    def finish(ins, outs, sems):
        pairs, mine = copies(ins, outs, sems)
        for _, rcv in pairs:
            rcv.wait_recv()
        for snd, _ in pairs:
            snd.wait_send()
        for cp in mine:
            cp.wait()

    return _Rider("ag1" + tag, list(shards) + list(into or []), [_sds((8,) + s.shape, s.dtype) for s in shards],
                  [pltpu.SemaphoreType.DMA((n, 4)), pltpu.SemaphoreType.DMA((n, 4)), pltpu.SemaphoreType.DMA((n,))],
                  {} if into is None else {n + t: t for t in range(n)}, start, finish)


def _ag_second_rider(gathered, tag):
    n = len(gathered)

    def copies(ins, outs, sems):
        send, recv = sems
        x, y, c, chips = _mesh_place()
        out = []
        for t in range(n):
            for j, (cx, cy) in enumerate(chips):
                here, there = 4 * cx + 2 * cy + c, 4 * cx + 2 * cy + (1 - c)
                out.append((_remote(ins[t].at[here], outs[t].at[here], send.at[t, j], recv.at[t, j], (x, y, 1 - c)),
                            _remote(ins[t].at[here], outs[t].at[there], send.at[t, j], recv.at[t, j], (x, y, 1 - c))))
        return out

    def start(ins, outs, sems):
        for snd, _ in copies(ins, outs, sems):
            snd.start()

    def finish(ins, outs, sems):
        pairs = copies(ins, outs, sems)
        for _, rcv in pairs:
            rcv.wait_recv()
        for snd, _ in pairs:
            snd.wait_send()

    return _Rider("ag2" + tag, gathered, [_sds(g.shape, g.dtype) for g in gathered],
                  [pltpu.SemaphoreType.DMA((n, 3)), pltpu.SemaphoreType.DMA((n, 3))],
                  {t: t for t in range(n)}, start, finish)


def _rs_sibling_rider(gs, tag):
    n = len(gs)

    def copies(ins, outs, sems):
        send, recv = sems
        x, y, c, _ = _mesh_place()
        return [_remote(ins[t].at[k, 1 - c], outs[t].at[k], send.at[t, k], recv.at[t, k], (x, y, 1 - c))
                for t in range(n) for k in range(4)]

    def start(ins, outs, sems):
        for cp in copies(ins, outs, sems):
            cp.start()

    def finish(ins, outs, sems):
        for cp in copies(ins, outs, sems):
            cp.wait()

    return _Rider("rs1" + tag, gs, [_sds((4,) + g.shape[2:], g.dtype) for g in gs],
                  [pltpu.SemaphoreType.DMA((n, 4)), pltpu.SemaphoreType.DMA((n, 4))], {}, start, finish)


def _rs_chip_rider(ps, tag):
    n = len(ps)

    def copies(ins, outs, sems):
        send, recv, local = sems
        x, y, c, chips = _mesh_place()
        my_chip = 2 * x + y
        out = []
        for t in range(n):
            for j, (cx, cy) in enumerate(chips):
                dev = (cx, cy, c)
                out.append((_remote(ins[t].at[2 * cx + cy], outs[t].at[my_chip], send.at[t, j], recv.at[t, j], dev),
                            _remote(ins[t].at[my_chip], outs[t].at[2 * cx + cy], send.at[t, j], recv.at[t, j], dev)))
        mine = [pltpu.make_async_copy(ins[t].at[my_chip], outs[t].at[my_chip], local.at[t]) for t in range(n)]
        return out, mine

    def start(ins, outs, sems):
        pairs, mine = copies(ins, outs, sems)
        for cp in mine:
            cp.start()
        for snd, _ in pairs:
            snd.start()

    def finish(ins, outs, sems):
        pairs, mine = copies(ins, outs, sems)
        for _, rcv in pairs:
            rcv.wait_recv()
        for snd, _ in pairs:
            snd.wait_send()
        for cp in mine:
            cp.wait()

    return _Rider("rs2" + tag, ps, [_sds(p.shape, p.dtype) for p in ps],
                  [pltpu.SemaphoreType.DMA((n, 3)), pltpu.SemaphoreType.DMA((n, 3)), pltpu.SemaphoreType.DMA((n,))],
                  {}, start, finish)


def _matmul(a, b, *, mode, tm, tn, tk, out_dtype, name, epi=None, extra=None, rider=None):
    if mode == "nn":
        (M, K), (K2, N) = a.shape, b.shape
    elif mode == "nt":
        (M, K), (N, K2) = a.shape, b.shape
    else:
        (K, M), (K2, N) = a.shape, b.shape
    tm, tn, tk = min(tm, M), min(tn, N), min(tk, K)
    assert K == K2 and M % tm == 0 and N % tn == 0 and K % tk == 0, (name, a.shape, b.shape)
    nk = K // tk
    dims = {"nn": NN, "nt": NT, "tn": TN}[mode]
    if mode == "tn":
        a_spec = pl.BlockSpec((tk, tm), lambda i, j, k: (k, i))
    else:
        a_spec = pl.BlockSpec((tm, tk), lambda i, j, k: (i, k))
    if mode == "nt":
        b_spec = pl.BlockSpec((tn, tk), lambda i, j, k: (j, k))
    else:
        b_spec = pl.BlockSpec((tk, tn), lambda i, j, k: (k, j))
    tile = pl.BlockSpec((tm, tn), lambda i, j, k: (i, j))
    n_extra = 1 if epi in ("residual", "drelu2") else 0
    n_out = 2 if epi == "relu2" else 1

    def body(*refs):
        a_ref, b_ref = refs[0], refs[1]
        extra_refs = refs[2:2 + n_extra]
        out_refs = refs[2 + n_extra:2 + n_extra + n_out]

        def finish(acc):
            if epi is None:
                out_refs[0][...] = acc.astype(out_dtype)
            elif epi == "residual":
                out_refs[0][...] = (extra_refs[0][...] + acc).astype(out_dtype)
            elif epi == "relu2":
                out_refs[0][...] = acc.astype(out_dtype)
                r = jnp.maximum(acc, 0.0)
                out_refs[1][...] = (r * r).astype(out_dtype)
            else:
                z = extra_refs[0][...].astype(F32)
                out_refs[0][...] = (acc * (2.0 * jnp.maximum(z, 0.0))).astype(out_dtype)

        part = _dot(a_ref[...], b_ref[...], dims)
        if nk == 1:
            finish(part)
        else:
            acc_ref = refs[-1]
            k = pl.program_id(2)

            @pl.when(k == 0)
            def _():
                acc_ref[...] = part

            @pl.when(k > 0)
            def _():
                acc_ref[...] += part

            @pl.when(k == nk - 1)
            def _():
                finish(acc_ref[...])

    res, carried = _carried_call(
        body, name=name, grid=(M // tm, N // tn, nk), in_specs=[a_spec, b_spec] + [tile] * n_extra,
        out_specs=[tile] * n_out, out_shape=[_sds((M, N), out_dtype)] * n_out,
        scratch_shapes=[pltpu.VMEM((tm, tn), F32)] if nk > 1 else [],
        dims=("parallel", "parallel", "arbitrary"), args=[a, b] + ([extra] if n_extra else []), rider=rider)
    res = res if n_out > 1 else res[0]
    return res if rider is None else (res, carried)


def _rms_val(x, g, n):
    r = lax.rsqrt(jnp.sum(x * x, axis=-1, keepdims=True) * (1.0 / n) + EPS)
    y = x * r
    return (y if g is None else y * g), r


def _rms_bwd_val(x, g, dy, n):
    r = lax.rsqrt(jnp.sum(x * x, axis=-1, keepdims=True) * (1.0 / n) + EPS)
    xhat = x * r
    dyg = dy if g is None else dy * g
    dx = r * (dyg - xhat * (jnp.sum(dyg * xhat, axis=-1, keepdims=True) * (1.0 / n)))
    return dx, dy * xhat


def _rope_val(x, c, sa, sb, shift):
    return x * c + pltpu.roll(x, LANES - shift, 1) * sa + pltpu.roll(x, shift, 1) * sb


def _rope_t_val(dy, c, sa, sb, shift):
    return dy * c + pltpu.roll(dy * sa, shift, 1) + pltpu.roll(dy * sb, LANES - shift, 1)


def _colsum(x):
    return jnp.sum(x, axis=0, keepdims=True)


def _rope_tables(S):
    pos = lax.broadcasted_iota(jnp.int32, (S, LANES), 0)
    lane = lax.broadcasted_iota(jnp.int32, (S, LANES), 1)

    def tables(p, dim, active):
        half = dim // 2
        inv = jnp.power(ROPE_THETA, -(2 * (lane % half)).astype(F32) / dim)
        a = p.astype(F32) * inv
        first = (lane % dim) < half
        zero = jnp.zeros((S, LANES), F32)
        return (jnp.where(active, jnp.cos(a), zero), jnp.where(active & first, -jnp.sin(a), zero),
                jnp.where(active & ~first, jnp.sin(a), zero))

    tab_a = tables(pos, 64, lane >= 64)
    tab_b = tables(jnp.where(lane < 64, pos // GRID_W, pos % GRID_W), 64, lane >= 0)
    tab_c = tables(pos, 128, lane >= 0)
    return tab_a, tab_b, tab_c


ROPE_SHIFT_AB = 32
ROPE_SHIFT_C = 64


def _rms_fwd(x, g, *, name, tr=512):
    S, W = x.shape
    tr = min(tr, S)

    def body(x_ref, g_ref, o_ref):
        y, _ = _rms_val(x_ref[...], g_ref[...], W)
        o_ref[...] = y.astype(CDT)

    return pl.pallas_call(
        body, name=name, grid=(S // tr,),
        in_specs=[pl.BlockSpec((tr, W), lambda i: (i, 0)), pl.BlockSpec((1, W), lambda i: (0, 0))],
        out_specs=pl.BlockSpec((tr, W), lambda i: (i, 0)), out_shape=_sds((S, W), CDT),
        compiler_params=_cparams(("parallel",)),
    )(x, g.reshape(1, W))


def _rms_bwd(x, g, dy, res, *, name, tr=256):
    S, W = x.shape
    tr = min(tr, S)

    def body(x_ref, g_ref, dy_ref, res_ref, dx_ref, dxb_ref, dg_ref):
        dx, dgt = _rms_bwd_val(x_ref[...], g_ref[...], dy_ref[...], W)
        dx = res_ref[...] + dx
        dx_ref[...] = dx
        dxb_ref[...] = dx.astype(CDT)

        @pl.when(pl.program_id(0) == 0)
        def _():
            dg_ref[...] = jnp.zeros_like(dg_ref)

        dg_ref[...] += _colsum(dgt)

    row = pl.BlockSpec((tr, W), lambda i: (i, 0))
    vec = pl.BlockSpec((1, W), lambda i: (0, 0))
    return pl.pallas_call(
        body, name=name, grid=(S // tr,),
        in_specs=[row, vec, row, row], out_specs=[row, row, vec],
        out_shape=[_sds((S, W), F32), _sds((S, W), CDT), _sds((1, W), F32)],
        compiler_params=_cparams(("arbitrary",)),
    )(x, g.reshape(1, W), dy, res)


def _loss_head(x, g, tgt, *, tr=256):
    S, W = x.shape
    tr = min(tr, S)

    def body(x_ref, g_ref, t_ref, loss_ref, dx_ref, dxb_ref, dg_ref):
        xv, gv = x_ref[...], g_ref[...]
        y, _ = _rms_val(xv, gv, W)
        err = y - t_ref[...]
        part = 0.5 * jnp.sum(jnp.sum(err * err, axis=-1, keepdims=True) * (1.0 / W), axis=0, keepdims=True)
        dx, dgt = _rms_bwd_val(xv, gv, err * (1.0 / W), W)
        dx_ref[...] = dx
        dxb_ref[...] = dx.astype(CDT)

        @pl.when(pl.program_id(0) == 0)
        def _():
            dg_ref[...] = jnp.zeros_like(dg_ref)
            loss_ref[...] = jnp.zeros_like(loss_ref)

        dg_ref[...] += _colsum(dgt)
        loss_ref[...] += jnp.broadcast_to(part, (1, LANES))

    row = pl.BlockSpec((tr, W), lambda i: (i, 0))
    vec = pl.BlockSpec((1, W), lambda i: (0, 0))
    return pl.pallas_call(
        body, name="loss_head", grid=(S // tr,),
        in_specs=[row, vec, row], out_specs=[pl.BlockSpec((1, LANES), lambda i: (0, 0)), row, row, vec],
        out_shape=[_sds((1, LANES), F32), _sds((S, W), F32), _sds((S, W), CDT), _sds((1, W), F32)],
        compiler_params=_cparams(("arbitrary",)),
    )(x, g.reshape(1, W), tgt)


def _tab_specs(tr):
    return [pl.BlockSpec((tr, LANES), lambda i: (i, 0))] * 9


def _lat_masks(shape):
    lane = lax.broadcasted_iota(jnp.int32, shape, 1)
    return lane < KV_LO, (lane >= KV_LO) & (lane < KV_HI)


def _prep_fwd(proj, glat, gqn, gkn, tabs, *, tr=256):
    S = proj.shape[0]
    tr = min(tr, S)

    def body(p_ref, glat_ref, gqn_ref, gkn_ref, ac, aa, ab, bc, ba, bb, cc, ca, cb,
             lat_ref, kpe_ref, qb_ref, kb_ref, vb_ref, qc_ref, kc_ref, vc_ref):
        x = p_ref[:, 0:LAT_W]
        is_q, is_kv = _lat_masks(x.shape)
        yq, _ = _rms_val(jnp.where(is_q, x, 0.0), None, Q_LORA)
        ykv, _ = _rms_val(jnp.where(is_kv, x, 0.0), None, KV_HI - KV_LO)
        lat_ref[...] = ((yq + ykv) * glat_ref[...]).astype(CDT)
        kpe_ref[...] = _rope_val(x[:, LAT_W - LANES:LAT_W], ac[...], aa[...], ab[...], ROPE_SHIFT_AB).astype(CDT)
        for h in range(6):
            xh = p_ref[:, OFF_BQ + LANES * h:OFF_BQ + LANES * (h + 1)]
            y = _rope_val(_rms_val(xh, gqn_ref[...], LANES)[0], bc[...], ba[...], bb[...], ROPE_SHIFT_AB)
            qb_ref[:, LANES * h:LANES * (h + 1)] = y.astype(CDT)
        for h in range(2):
            xh = p_ref[:, OFF_BK + LANES * h:OFF_BK + LANES * (h + 1)]
            y = _rope_val(_rms_val(xh, gkn_ref[...], LANES)[0], bc[...], ba[...], bb[...], ROPE_SHIFT_AB)
            kb_ref[:, LANES * h:LANES * (h + 1)] = y.astype(CDT)
        vb_ref[...] = p_ref[:, OFF_BV:OFF_BV + 256].astype(CDT)
        for h in range(6):
            sl = slice(LANES * h, LANES * (h + 1))
            qc_ref[:, sl] = _rope_val(p_ref[:, OFF_CQ + LANES * h:OFF_CQ + LANES * (h + 1)], cc[...], ca[...], cb[...],
                                      ROPE_SHIFT_C)
            kc_ref[:, sl] = _rope_val(p_ref[:, OFF_CK + LANES * h:OFF_CK + LANES * (h + 1)], cc[...], ca[...], cb[...],
                                      ROPE_SHIFT_C)
        vc_ref[...] = p_ref[:, OFF_CV:OFF_CV + 768]

    vec = lambda w: pl.BlockSpec((1, w), lambda i: (0, 0))
    row = lambda w: pl.BlockSpec((tr, w), lambda i: (i, 0))
    return pl.pallas_call(
        body, name="prep_fwd", grid=(S // tr,),
        in_specs=[row(PROJ_W), vec(LAT_W), vec(128), vec(128)] + _tab_specs(tr),
        out_specs=[row(LAT_W), row(128), row(768), row(256), row(256), row(768), row(768), row(768)],
        out_shape=[_sds((S, LAT_W), CDT), _sds((S, 128), CDT), _sds((S, 768), CDT), _sds((S, 256), CDT),
                   _sds((S, 256), CDT), _sds((S, 768), F32), _sds((S, 768), F32), _sds((S, 768), F32)],
        compiler_params=_cparams(("parallel",)),
    )(proj, glat, gqn, gkn, *tabs[0], *tabs[1], *tabs[2])


def _prep_a2_fwd(qkva, kpe, tab_a, *, tr=512):
    S = qkva.shape[0]
    tr = min(tr, S)

    def body(x_ref, kpe_ref, ac, aa, ab, qa_ref, ka_ref, va_ref):
        for h in range(4):
            lo, hi = 2 * LANES * h, 2 * LANES * h + LANES
            qa_ref[:, lo:hi] = x_ref[:, lo:hi].astype(CDT)
            qa_ref[:, hi:hi + LANES] = _rope_val(x_ref[:, hi:hi + LANES], ac[...], aa[...], ab[...],
                                                 ROPE_SHIFT_AB).astype(CDT)
            ka_ref[:, lo:hi] = x_ref[:, 1024 + LANES * h:1024 + LANES * (h + 1)].astype(CDT)
            ka_ref[:, hi:hi + LANES] = kpe_ref[...]
        va_ref[...] = x_ref[:, 1536:2048].astype(CDT)

    row = lambda w: pl.BlockSpec((tr, w), lambda i: (i, 0))
    return pl.pallas_call(
        body, name="prep_a2_fwd", grid=(S // tr,),
        in_specs=[row(2048), row(128)] + _tab_specs(tr)[:3],
        out_specs=[row(1024), row(1024), row(512)],
        out_shape=[_sds((S, 1024), CDT), _sds((S, 1024), CDT), _sds((S, 512), CDT)],
        compiler_params=_cparams(("parallel",)),
    )(qkva, kpe, *tab_a)


def _prep_a2_bwd(dqa, dka, dva, tab_a, *, tr=512):
    S = dqa.shape[0]
    tr = min(tr, S)

    def body(dq_ref, dk_ref, dv_ref, ac, aa, ab, dx_ref, dkr_ref):
        dkpe = jnp.zeros((tr, LANES), F32)
        for h in range(4):
            lo, hi = 2 * LANES * h, 2 * LANES * h + LANES
            dx_ref[:, lo:hi] = dq_ref[:, lo:hi].astype(CDT)
            dx_ref[:, hi:hi + LANES] = _rope_t_val(dq_ref[:, hi:hi + LANES], ac[...], aa[...], ab[...],
                                                   ROPE_SHIFT_AB).astype(CDT)
            dx_ref[:, 1024 + LANES * h:1024 + LANES * (h + 1)] = dk_ref[:, lo:hi].astype(CDT)
            dkpe = dkpe + dk_ref[:, hi:hi + LANES]
        dx_ref[:, 1536:2048] = dv_ref[...].astype(CDT)
        dkr_ref[...] = _rope_t_val(dkpe, ac[...], aa[...], ab[...], ROPE_SHIFT_AB)

    row = lambda w: pl.BlockSpec((tr, w), lambda i: (i, 0))
    return pl.pallas_call(
        body, name="prep_a2_bwd", grid=(S // tr,),
        in_specs=[row(1024), row(1024), row(512)] + _tab_specs(tr)[:3],
        out_specs=[row(2048), row(128)],
        out_shape=[_sds((S, 2048), CDT), _sds((S, 128), F32)],
        compiler_params=_cparams(("parallel",)),
    )(dqa, dka, dva, *tab_a)


def _prep_bwd(proj, glat, gqn, gkn, tabs, dlat, dkr, dqb, dkb, dvb, dqc, dkc, dvc, *, tr=256):
    S = proj.shape[0]
    tr = min(tr, S)

    def body(p_ref, glat_ref, gqn_ref, gkn_ref, ac, aa, ab, bc, ba, bb, cc, ca, cb,
             dlat_ref, dkr_ref, dqb_ref, dkb_ref, dvb_ref, dqc_ref, dkc_ref, dvc_ref,
             dp_ref, dglat_ref, dgqn_ref, dgkn_ref):
        @pl.when(pl.program_id(0) == 0)
        def _():
            dglat_ref[...] = jnp.zeros_like(dglat_ref)
            dgqn_ref[...] = jnp.zeros_like(dgqn_ref)
            dgkn_ref[...] = jnp.zeros_like(dgkn_ref)

        x = p_ref[:, 0:LAT_W]
        is_q, is_kv = _lat_masks(x.shape)
        dy, g = dlat_ref[...], glat_ref[...]
        dxq, dgq = _rms_bwd_val(jnp.where(is_q, x, 0.0), g, jnp.where(is_q, dy, 0.0), Q_LORA)
        dxkv, dgkv = _rms_bwd_val(jnp.where(is_kv, x, 0.0), g, jnp.where(is_kv, dy, 0.0), KV_HI - KV_LO)
        dglat_ref[...] += _colsum(dgq + dgkv)
        dx = dxq + dxkv
        dp_ref[:, 0:LAT_W - LANES] = dx[:, 0:LAT_W - LANES].astype(CDT)
        dp_ref[:, LAT_W - LANES:LAT_W] = (dx[:, LAT_W - LANES:LAT_W] + dkr_ref[...]).astype(CDT)
        dgqn = jnp.zeros((1, LANES), F32)
        for h in range(6):
            sl = slice(LANES * h, LANES * (h + 1))
            po = slice(OFF_BQ + LANES * h, OFF_BQ + LANES * (h + 1))
            dyh = _rope_t_val(dqb_ref[:, sl], bc[...], ba[...], bb[...], ROPE_SHIFT_AB)
            dxh, dgt = _rms_bwd_val(p_ref[:, po], gqn_ref[...], dyh, LANES)
            dp_ref[:, po] = dxh.astype(CDT)
            dgqn = dgqn + _colsum(dgt)
        dgqn_ref[...] += dgqn
        dgkn = jnp.zeros((1, LANES), F32)
        for h in range(2):
            sl = slice(LANES * h, LANES * (h + 1))
            po = slice(OFF_BK + LANES * h, OFF_BK + LANES * (h + 1))
            dyh = _rope_t_val(dkb_ref[:, sl], bc[...], ba[...], bb[...], ROPE_SHIFT_AB)
            dxh, dgt = _rms_bwd_val(p_ref[:, po], gkn_ref[...], dyh, LANES)
            dp_ref[:, po] = dxh.astype(CDT)
            dgkn = dgkn + _colsum(dgt)
        dgkn_ref[...] += dgkn
        dp_ref[:, OFF_BV:OFF_BV + 256] = dvb_ref[...].astype(CDT)
        for h in range(6):
            sl = slice(LANES * h, LANES * (h + 1))
            dp_ref[:, OFF_CQ + LANES * h:OFF_CQ + LANES * (h + 1)] = _rope_t_val(
                dqc_ref[:, sl], cc[...], ca[...], cb[...], ROPE_SHIFT_C).astype(CDT)
            dp_ref[:, OFF_CK + LANES * h:OFF_CK + LANES * (h + 1)] = _rope_t_val(
                dkc_ref[:, sl], cc[...], ca[...], cb[...], ROPE_SHIFT_C).astype(CDT)
        dp_ref[:, OFF_CV:OFF_CV + 768] = dvc_ref[...].astype(CDT)

    vec = lambda w: pl.BlockSpec((1, w), lambda i: (0, 0))
    row = lambda w: pl.BlockSpec((tr, w), lambda i: (i, 0))
    return pl.pallas_call(
        body, name="prep_bwd", grid=(S // tr,),
        in_specs=[row(PROJ_W), vec(LAT_W), vec(128), vec(128)] + _tab_specs(tr)
        + [row(LAT_W), row(128), row(768), row(256), row(256), row(768), row(768), row(768)],
        out_specs=[row(PROJ_W), vec(LAT_W), vec(128), vec(128)],
        out_shape=[_sds((S, PROJ_W), CDT), _sds((1, LAT_W), F32), _sds((1, 128), F32), _sds((1, 128), F32)],
        compiler_params=_cparams(("arbitrary",)),
    )(proj, glat, gqn, gkn, *tabs[0], *tabs[1], *tabs[2], dlat, dkr, dqb, dkb, dvb, dqc, dkc, dvc)


ATTN_TK = 512
LOG2E = 1.4426950408889634
C2_H = SCALE_H * LOG2E


def _attn_fwd(q, k, v, *, H, G, dk, dv, scale, name, tq=512, rider=None):
    S = q.shape[0]
    tq = min(tq, S)
    tk = min(ATTN_TK, S)
    c2 = scale * LOG2E

    half = tq // 2

    def body(q_ref, k_ref, v_ref, o_ref, l_ref):
        chunks = [pl.ds(c * tk, tk) for c in range(S // tk)]
        qa, qb = q_ref[0:half, :], q_ref[half:tq, :]
        rowmax = lambda s: functools.reduce(jnp.maximum, [jnp.max(sc, axis=-1, keepdims=True) for sc in s])
        s_a = [_dot(qa, k_ref[rows, :], NT) for rows in chunks]
        m_a = rowmax(s_a)
        e_a, s_b = [], []
        for c, rows in enumerate(chunks):
            e_a.append(jnp.exp2((s_a[c] - m_a) * c2))
            s_b.append(_dot(qb, k_ref[rows, :], NT))
        m_b = rowmax(s_b)
        den_a, den_b = jnp.zeros((half, 1), F32), jnp.zeros((half, 1), F32)
        acc_a, acc_b = jnp.zeros((half, dv), F32), jnp.zeros((half, dv), F32)
        e_b = []
        for c, rows in enumerate(chunks):
            acc_a = acc_a + _dot(e_a[c].astype(CDT), v_ref[rows, :], NN)
            den_a = den_a + jnp.sum(e_a[c], axis=-1, keepdims=True)
            e_b.append(jnp.exp2((s_b[c] - m_b) * c2))
        for c, rows in enumerate(chunks):
            acc_b = acc_b + _dot(e_b[c].astype(CDT), v_ref[rows, :], NN)
            den_b = den_b + jnp.sum(e_b[c], axis=-1, keepdims=True)
        o_ref[0:half, :] = acc_a * (1.0 / den_a)
        o_ref[half:tq, :] = acc_b * (1.0 / den_b)
        l_ref[0:half, :] = jnp.broadcast_to(m_a * scale + jnp.log(den_a), (half, LANES))
        l_ref[half:tq, :] = jnp.broadcast_to(m_b * scale + jnp.log(den_b), (half, LANES))

    return _carried_call(
        body, name=name, grid=(H, S // tq),
        in_specs=[pl.BlockSpec((tq, dk), lambda h, i: (i, h)), pl.BlockSpec((S, dk), lambda h, i: (0, h // G)),
                  pl.BlockSpec((S, dv), lambda h, i: (0, h // G))],
        out_specs=[pl.BlockSpec((tq, dv), lambda h, i: (i, h)), pl.BlockSpec((tq, LANES), lambda h, i: (i, h))],
        out_shape=[_sds((S, H * dv), F32), _sds((S, H * LANES), F32)], scratch_shapes=[],
        dims=("parallel", "parallel"), args=[q, k, v], rider=rider)


def _attn_bwd(q, k, v, do, lse, delta, *, H, G, dk, dv, scale, name, tq=256, rider=None):
    S = q.shape[0]
    tq = min(tq, S)
    Hkv = H // G
    c2 = scale * LOG2E

    def body(q_ref, k_ref, v_ref, do_ref, l_ref, d_ref, dq_ref, dk_ref, dv_ref):
        @pl.when((pl.program_id(1) == 0) & (pl.program_id(2) == 0))
        def _():
            dk_ref[...] = jnp.zeros_like(dk_ref)
            dv_ref[...] = jnp.zeros_like(dv_ref)

        qv, kv, dov = q_ref[...], k_ref[...], do_ref[...]
        p = jnp.exp2(_dot(qv, kv, NT) * c2 - l_ref[:, 0:1] * LOG2E)
        dp = _dot(dov, v_ref[...], NT)
        ds = (p * (dp - d_ref[:, 0:1]) * scale).astype(CDT)
        dq_ref[...] = _dot(ds, kv, NN)
        dk_ref[...] += _dot(ds, qv, TN)
        dv_ref[...] += _dot(p.astype(CDT), dov, TN)

    qi = lambda hk, g, i: (i, hk * G + g)
    return _carried_call(
        body, name=name, grid=(Hkv, G, S // tq),
        in_specs=[pl.BlockSpec((tq, dk), qi), pl.BlockSpec((S, dk), lambda hk, g, i: (0, hk)),
                  pl.BlockSpec((S, dv), lambda hk, g, i: (0, hk)), pl.BlockSpec((tq, dv), qi),
                  pl.BlockSpec((tq, LANES), qi), pl.BlockSpec((tq, LANES), qi)],
        out_specs=[pl.BlockSpec((tq, dk), qi), pl.BlockSpec((S, dk), lambda hk, g, i: (0, hk)),
                   pl.BlockSpec((S, dv), lambda hk, g, i: (0, hk))],
        out_shape=[_sds((S, H * dk), F32), _sds((S, Hkv * dk), F32), _sds((S, Hkv * dv), F32)], scratch_shapes=[],
        dims=("parallel", "arbitrary", "arbitrary"), args=[q, k, v, do, lse, delta], rider=rider)


BAND_SUB = 128
BAND_WIN = 384
BAND_UNROLL = 8


def _band_blocks(S, d):
    L = S // d
    assert L % BAND_SUB == 0 and S % (BAND_SUB * BAND_UNROLL) == 0
    return L, L // BAND_SUB, min(BAND_WIN, L)


def _band_index(blk, d, nb, L, win):
    r, jb = blk // nb, blk % nb
    l0 = jb * BAND_SUB
    w0 = jnp.clip(l0 - BAND_SUB, 0, L - win)
    return r + d * l0, r + d * w0, l0, w0


def _band_rows(start, size, d):
    return pl.ds(pl.multiple_of(start, BAND_SUB), size) if d == 1 else pl.ds(start, size, stride=d)


def _band_mask(l0, w0, win):
    rpos = l0 + lax.broadcasted_iota(jnp.int32, (BAND_SUB, win), 0)
    cpos = w0 + lax.broadcasted_iota(jnp.int32, (BAND_SUB, win), 1)
    return jnp.abs(rpos - cpos) <= BAND_HALF


def _mixc_fwd(q, k, v, rider=None):
    S, W = q.shape

    def body(q_ref, k_ref, v_ref, o_ref, l_ref, *scratch):
        ob_refs, lb_refs = scratch[0:3], scratch[3:6]
        for b, d in enumerate(DILATIONS):
            L, nb, win = _band_blocks(S, d)

            def step(it, carry, b=b, d=d, L=L, nb=nb, win=win):
                idx = [_band_index(it * BAND_UNROLL + u, d, nb, L, win) for u in range(BAND_UNROLL)]
                qrows = [_band_rows(i[0], BAND_SUB, d) for i in idx]
                krows = [_band_rows(i[1], win, d) for i in idx]
                qv = [q_ref[r, :].astype(CDT) for r in qrows]
                kw = [k_ref[r, :].astype(CDT) for r in krows]
                vw = [v_ref[r, :].astype(CDT) for r in krows]
                s = [jnp.where(_band_mask(i[2], i[3], win), _dot(a, kk, NT), NEG_INF) for i, a, kk in zip(idx, qv, kw)]
                m = [jnp.max(x, axis=-1, keepdims=True) for x in s]
                e = [jnp.exp2((x - mm) * C2_H) for x, mm in zip(s, m)]
                den = [jnp.sum(x, axis=-1, keepdims=True) for x in e]
                o = [_dot((x * (1.0 / dd)).astype(CDT), vv, NN) for x, dd, vv in zip(e, den, vw)]
                for r, ou, mm, dd in zip(qrows, o, m, den):
                    ob_refs[b][r, :] = ou
                    lb_refs[b][r, :] = jnp.broadcast_to(mm * SCALE_H + jnp.log(dd), (BAND_SUB, LANES))
                return carry

            lax.fori_loop(0, S // (BAND_SUB * BAND_UNROLL), step, 0)

        def combine(c, carry):
            rows = pl.ds(pl.multiple_of(c * 256, 256), 256)
            l0, l1, l2 = lb_refs[0][rows, :], lb_refs[1][rows, :], lb_refs[2][rows, :]
            m = jnp.maximum(jnp.maximum(l0, l1), l2)
            e0, e1, e2 = jnp.exp(l0 - m), jnp.exp(l1 - m), jnp.exp(l2 - m)
            den = e0 + e1 + e2
            inv = 1.0 / den
            o_ref[rows, :] = ((e0 * inv) * ob_refs[0][rows, :] + (e1 * inv) * ob_refs[1][rows, :]
                              + (e2 * inv) * ob_refs[2][rows, :])
            l_ref[rows, :] = m + jnp.log(den)
            return carry

        lax.fori_loop(0, S // 256, combine, 0)

    head = pl.BlockSpec((S, LANES), lambda h: (0, h))
    return _carried_call(
        body, name="mixc_fwd", grid=(W // LANES,), in_specs=[head] * 3, out_specs=[head] * 2,
        out_shape=[_sds((S, W), F32)] * 2, scratch_shapes=[pltpu.VMEM((S, LANES), F32)] * 6,
        dims=("parallel",), args=[q, k, v], rider=rider)


def _mixc_bwd(q, k, v, do, lse, dd, rider=None):
    S, W = q.shape

    def body(q_ref, k_ref, v_ref, do_ref, l_ref, d_ref, dq_ref, dk_ref, dv_ref):
        dq_ref[...] = jnp.zeros_like(dq_ref)
        dk_ref[...] = jnp.zeros_like(dk_ref)
        dv_ref[...] = jnp.zeros_like(dv_ref)
        for d in DILATIONS:
            L, nb, win = _band_blocks(S, d)

            def step(it, carry, d=d, L=L, nb=nb, win=win):
                idx = [_band_index(it * BAND_UNROLL + u, d, nb, L, win) for u in range(BAND_UNROLL)]
                qrows = [_band_rows(i[0], BAND_SUB, d) for i in idx]
                krows = [_band_rows(i[1], win, d) for i in idx]
                qv = [q_ref[r, :].astype(CDT) for r in qrows]
                dov = [do_ref[r, :].astype(CDT) for r in qrows]
                kw = [k_ref[r, :].astype(CDT) for r in krows]
                vw = [v_ref[r, :].astype(CDT) for r in krows]
                lse2 = [l_ref[r, :][:, 0:1] * LOG2E for r in qrows]
                dd = [d_ref[r, :][:, 0:1] for r in qrows]
                s = [jnp.where(_band_mask(i[2], i[3], win), _dot(a, kk, NT), NEG_INF) for i, a, kk in zip(idx, qv, kw)]
                p = [jnp.exp2(x * C2_H - ll) for x, ll in zip(s, lse2)]
                dp = [_dot(a, vv, NT) for a, vv in zip(dov, vw)]
                ds = [(pp * (x - y) * SCALE_H).astype(CDT) for pp, x, y in zip(p, dp, dd)]
                dq = [_dot(x, kk, NN) for x, kk in zip(ds, kw)]
                dk = [_dot(x, a, TN) for x, a in zip(ds, qv)]
                dv = [_dot(pp.astype(CDT), a, TN) for pp, a in zip(p, dov)]
                for u in range(BAND_UNROLL):
                    dq_ref[qrows[u], :] += dq[u]
                    dk_ref[krows[u], :] += dk[u]
                    dv_ref[krows[u], :] += dv[u]
                return carry

            lax.fori_loop(0, S // (BAND_SUB * BAND_UNROLL), step, 0)

    head = pl.BlockSpec((S, LANES), lambda h: (0, h))
    return _carried_call(
        body, name="mixc_bwd", grid=(W // LANES,), in_specs=[head] * 6, out_specs=[head] * 3,
        out_shape=[_sds((S, W), F32)] * 3, scratch_shapes=[], dims=("parallel",), args=[q, k, v, do, lse, dd],
        rider=rider)


def _outnorm_fwd(oa, ob, oc, g, *, tr=256):
    S = oa.shape[0]
    tr = min(tr, S)

    def body(a_ref, b_ref, c_ref, g_ref, m_ref):
        m_ref[:, 0:512] = (_rms_val(a_ref[...], None, 512)[0] * g_ref[:, 0:512]).astype(CDT)
        m_ref[:, 512:1280] = (_rms_val(b_ref[...], None, 768)[0] * g_ref[:, 512:1280]).astype(CDT)
        m_ref[:, 1280:2048] = (_rms_val(c_ref[...], None, 768)[0] * g_ref[:, 1280:2048]).astype(CDT)

    row = lambda w: pl.BlockSpec((tr, w), lambda i: (i, 0))
    return pl.pallas_call(
        body, name="outnorm_fwd", grid=(S // tr,),
        in_specs=[row(512), row(768), row(768), pl.BlockSpec((1, 2048), lambda i: (0, 0))],
        out_specs=row(2048), out_shape=_sds((S, 2048), CDT), compiler_params=_cparams(("parallel",)),
    )(oa, ob, oc, g)


def _outnorm_bwd(oa, ob, oc, g, dm, *, tr=256):
    S = oa.shape[0]
    tr = min(tr, S)

    def body(a_ref, b_ref, c_ref, g_ref, dm_ref, doa_ref, dob_ref, doc_ref, da_ref, db_ref, dc_ref, dg_ref):
        @pl.when(pl.program_id(0) == 0)
        def _():
            dg_ref[...] = jnp.zeros_like(dg_ref)

        for o_ref, do_ref, d_ref, lo, w in ((a_ref, doa_ref, da_ref, 0, 512), (b_ref, dob_ref, db_ref, 512, 768),
                                            (c_ref, doc_ref, dc_ref, 1280, 768)):
            o = o_ref[...]
            dmv = dm_ref[:, lo:lo + w]
            do, _ = _rms_bwd_val(o, None, dmv * g_ref[:, lo:lo + w], w)
            r = lax.rsqrt(jnp.sum(o * o, axis=-1, keepdims=True) * (1.0 / w) + EPS)
            dg_ref[:, lo:lo + w] += _colsum(dmv * (o * r))
            do_ref[...] = do.astype(do_ref.dtype)
            for h in range(w // LANES):
                sl = slice(LANES * h, LANES * (h + 1))
                d_ref[:, sl] = jnp.broadcast_to(jnp.sum(do[:, sl] * o[:, sl], axis=-1, keepdims=True), (tr, LANES))

    row = lambda w: pl.BlockSpec((tr, w), lambda i: (i, 0))
    vec = pl.BlockSpec((1, 2048), lambda i: (0, 0))
    return pl.pallas_call(
        body, name="outnorm_bwd", grid=(S // tr,),
        in_specs=[row(512), row(768), row(768), vec, row(2048)],
        out_specs=[row(512), row(768), row(768), row(512), row(768), row(768), vec],
        out_shape=[_sds((S, 512), CDT), _sds((S, 768), CDT), _sds((S, 768), F32), _sds((S, 512), F32),
                   _sds((S, 768), F32), _sds((S, 768), F32), _sds((1, 2048), F32)],
        compiler_params=_cparams(("arbitrary",)),
    )(oa, ob, oc, g, dm)


def _row_tile(r, c, itemsize, limit=1 << 20):
    best = 16
    for t in range(16, r + 1, 16):
        if r % t == 0 and t * c * itemsize <= limit:
            best = t
    return best


def _rs_chip_sum(g, got, *, name):
    _, _, r, c = g.shape
    tr = _row_tile(r, c, 2)
    core = lax.axis_index("c").astype(jnp.int32).reshape(1)

    def body(c_ref, a_ref, b_ref, o_ref):
        o_ref[...] = (a_ref[...].astype(F32) + b_ref[...].astype(F32)).astype(o_ref.dtype)

    spec = pltpu.PrefetchScalarGridSpec(
        num_scalar_prefetch=1, grid=(4, r // tr),
        in_specs=[pl.BlockSpec((None, None, tr, c), lambda k, i, cr: (k, cr[0], i, 0)),
                  pl.BlockSpec((None, tr, c), lambda k, i, cr: (k, i, 0))],
        out_specs=pl.BlockSpec((None, tr, c), lambda k, i, cr: (k, i, 0)))
    return pl.pallas_call(body, name=name, grid_spec=spec, out_shape=_sds((4, r, c), g.dtype),
                          compiler_params=_cparams(("parallel", "parallel")))(core, g, got)


def _rs_final_sum(r4, *, name):
    _, r, c = r4.shape
    tr = _row_tile(r, c, 4)

    def body(r_ref, o_ref):
        o_ref[...] = ((r_ref[0].astype(F32) + r_ref[1].astype(F32)) + r_ref[2].astype(F32)) + r_ref[3].astype(F32)

    return pl.pallas_call(
        body, name=name, grid=(r // tr,), in_specs=[pl.BlockSpec((4, tr, c), lambda i: (0, i, 0))],
        out_specs=pl.BlockSpec((tr, c), lambda i: (i, 0)), out_shape=_sds((r, c), F32),
        compiler_params=_cparams(("parallel",)))(r4)


def _all_reduce_small(v):
    R = v.shape[0]

    def body(v_ref, out_ref, buf_ref, send_sems, recv_sems):
        x, y, c = lax.axis_index("x"), lax.axis_index("y"), lax.axis_index("c")
        me = 4 * x + 2 * y + c
        buf_ref[me] = v_ref[...]
        peers = []
        for r in range(1, 8):
            px, py, pc = x ^ (r >> 2), y ^ ((r >> 1) & 1), c ^ (r & 1)
            peers.append((r, (px, py, pc), 4 * px + 2 * py + pc))
        sends = [pltpu.make_async_remote_copy(
            src_ref=v_ref, dst_ref=buf_ref.at[me], send_sem=send_sems.at[r - 1], recv_sem=recv_sems.at[r - 1],
            device_id=dev, device_id_type=MESH) for r, dev, _ in peers]
        for cp in sends:
            cp.start()
        for r, dev, idx in peers:
            pltpu.make_async_remote_copy(
                src_ref=v_ref, dst_ref=buf_ref.at[idx], send_sem=send_sems.at[r - 1], recv_sem=recv_sems.at[r - 1],
                device_id=dev, device_id_type=MESH).wait_recv()
        for cp in sends:
            cp.wait_send()
        acc = buf_ref[0]
        for k in range(1, 8):
            acc = acc + buf_ref[k]
        out_ref[...] = acc

    vm = pl.BlockSpec(memory_space=pltpu.VMEM)
    return pl.pallas_call(
        body, name="all_reduce_small", out_shape=_sds((R, LANES), F32), in_specs=[vm], out_specs=vm,
        scratch_shapes=[pltpu.VMEM((8, R, LANES), F32), pltpu.SemaphoreType.DMA((7,)), pltpu.SemaphoreType.DMA((7,))],
    )(v)


def _adamw(w, g, m, v, *, name):
    R, C = w.shape
    tr = R
    for cand in (1024, 512, 256, 128, 64, 32, 16, 8):
        if R % cand == 0 and cand * C * 4 <= 2 * 1024 * 1024:
            tr = cand
            break

    def body(w_ref, g_ref, m_ref, v_ref, d_ref, nm_ref, nv_ref):
        gv = g_ref[...]
        mn = ADAM_B1 * m_ref[...] + (1.0 - ADAM_B1) * gv
        vn = ADAM_B2 * v_ref[...] + (1.0 - ADAM_B2) * (gv * gv)
        m_hat = mn / (1.0 - ADAM_B1 ** ADAM_STEP)
        v_hat = vn / (1.0 - ADAM_B2 ** ADAM_STEP)
        d_ref[...] = -ADAM_LR * (m_hat / (jnp.sqrt(v_hat) + ADAM_EPS) + ADAM_WD * w_ref[...])
        nm_ref[...] = mn
        nv_ref[...] = vn

    blk = pl.BlockSpec((tr, C), lambda i: (i, 0))
    return pl.pallas_call(
        body, name=name, grid=(R // tr,), in_specs=[blk] * 4, out_specs=[blk] * 3,
        out_shape=[_sds((R, C), F32)] * 3, compiler_params=_cparams(("parallel",)))(w, g, m, v)


def _wuq_pad(w):
    w = w.reshape(448, 4, 192)
    z = jnp.zeros((448, 4, 64), w.dtype)
    return jnp.concatenate([w[:, :, 0:128], z, w[:, :, 128:192]], axis=2).reshape(448, 1024)


def _wuq_unpad(w):
    w = w.reshape(448, 4, 256)
    return jnp.concatenate([w[:, :, 0:128], w[:, :, 192:256]], axis=2).reshape(448, 768)


def _wukv_perm(w):
    return w.reshape(512, 4, 2, 128).transpose(0, 2, 1, 3).reshape(512, 1024)


def _wukv_unperm(w):
    return w.reshape(512, 2, 4, 128).transpose(0, 2, 1, 3).reshape(512, 1024)


def _lat_weight(w_uq, w_ukv):
    z = lambda r, c: jnp.zeros((r, c), w_uq.dtype)
    top = jnp.concatenate([_wuq_pad(w_uq), z(448, 1024)], axis=1)
    mid = jnp.concatenate([z(512, 1024), _wukv_perm(w_ukv)], axis=1)
    return jnp.concatenate([top, mid, z(64, 2048)], axis=0)


def _lat_weight_grads(dw):
    return _wuq_unpad(dw[0:KV_LO, 0:1024]), _wukv_unperm(dw[KV_LO:KV_HI, 1024:2048])


def _comm_shards(w_in, w_uq, w_ukv, w_out, w_ff1, w_ff2):
    lat = jnp.concatenate([w_uq.reshape(UQ_ROWS, LANES), w_ukv.reshape(512, LANES)], axis=0)
    return [w_in.T, w_ff1.T, w_out, w_ff2, lat]


def _from_comm_shards(parts):
    w_in_t, w_ff1_t, w_out, w_ff2, lat = parts
    return {"w_in": w_in_t.T, "w_ff1": w_ff1_t.T, "w_out": w_out, "w_ff2": w_ff2,
            "w_uq": lat[0:UQ_ROWS].reshape(448, 96), "w_ukv": lat[UQ_ROWS:].reshape(512, 128)}


def _early_weights(g_in_t, g_lat):
    w_uq = g_lat[:, 0:UQ_ROWS].reshape(8, 448, 96).transpose(1, 0, 2).reshape(448, 768)
    w_ukv = g_lat[:, UQ_ROWS:].reshape(8, 512, 128).transpose(1, 0, 2).reshape(512, 1024)
    return g_in_t.reshape(PROJ_W, D_MODEL), _lat_weight(w_uq, w_ukv)


def _layer_fwd(x, W, G, tabs, plan=None):
    W = dict(W)
    sh, nxt, ff1_half = plan if plan is not None else (None, None, None)
    first = plan is not None and ff1_half is None
    early_next = None if nxt is None else [nxt["w_in_t"], nxt["lat"]]
    ag1 = lambda arrays, tag, **kw: None if (plan is None or not arrays) else _ag_first_rider(arrays, tag, **kw)
    ag2 = lambda arrays, tag: None if (plan is None or not arrays) else _ag_second_rider(arrays, tag)
    s = {"x0": x}
    s["h1"] = _rms_fwd(x, G["ln1_g"], name="rms1_fwd")
    mm_in = functools.partial(_matmul, s["h1"], W["w_in_t"], mode="nt", tm=1024, tn=768, tk=2048, out_dtype=F32,
                              name="mm_in")
    if first:
        s["proj"], ff1_rows0 = mm_in(rider=ag1([sh["w_ff1_t"]], "_ff1a", part=(0, 2)))
    else:
        s["proj"] = mm_in()
    s["lat"], kpe, s["qb"], s["kb"], s["vb"], s["qc"], s["kc"], s["vc"] = _prep_fwd(
        s["proj"], G["glat"], G["gqn"], G["gkn"], tabs)
    qkva = _matmul(s["lat"], W["w_lat"], mode="nn", tm=1024, tn=1024, tk=1024, out_dtype=F32, name="mm_lat")
    s["qa"], s["ka"], s["va"] = _prep_a2_fwd(qkva, kpe, tabs[0])
    early, ff1_half_next = None, None
    attn_a = functools.partial(_attn_fwd, s["qa"], s["ka"], s["va"], H=4, G=1, dk=256, dv=128, scale=SCALE_A,
                               name="attn_a_fwd")
    attn_b = functools.partial(_attn_fwd, s["qb"], s["kb"], s["vb"], H=6, G=3, dk=128, dv=128, scale=SCALE_H,
                               name="attn_b_fwd")
    mixc = functools.partial(_mixc_fwd, s["qc"], s["kc"], s["vc"])
    if plan is None:
        (s["oa"], s["lse_a"]), _ = attn_a()
        (s["ob"], s["lse_b"]), _ = attn_b()
        (s["oc"], s["lse_c"]), _ = mixc()
    else:
        if first:
            (s["oa"], s["lse_a"]), got = attn_a(rider=ag1([sh["w_ff1_t"]], "_ff1b", part=(1, 2), into=ff1_rows0))
            ff1_half, early_half = got[0], []
        else:
            (s["oa"], s["lse_a"]), early_half = attn_a(rider=ag1(early_next, "_early"))
        (s["ob"], s["lse_b"]), got_b = attn_b(
            rider=_join(ag1([sh["w_out"]], "_out"), ag1([sh["w_ff2"]], "_ff2a", part=(0, 2))))
        (s["oc"], s["lse_c"]), got_c = mixc(
            rider=_join(_join(ag2(got_b[0:1], "_out"), ag1([sh["w_ff2"]], "_ff2b", part=(1, 2), into=got_b[1:2])),
                        ag2(early_half, "_early")))
        W["w_out"] = got_c[0].reshape(D_MODEL, D_MODEL)
        ff2_half, early = got_c[1], (got_c[2:4] or None)
    s["mixed"] = _outnorm_fwd(s["oa"], s["ob"], s["oc"], G["g_out"])
    mm_out = functools.partial(_matmul, s["mixed"], W["w_out"], mode="nn", tm=1024, tn=1024, tk=2048, out_dtype=F32,
                               name="mm_out", epi="residual", extra=x)
    if plan is None:
        s["x1"] = mm_out()
    else:
        s["x1"], got = mm_out(rider=ag2([ff1_half, ff2_half], "_ff"))
        W["w_ff1_t"], W["w_ff2"] = got[0].reshape(D_FF, D_MODEL), got[1].reshape(D_FF, D_MODEL)
    s["h2"] = _rms_fwd(s["x1"], G["ln2_g"], name="rms2_fwd")
    ff1 = functools.partial(_matmul, s["h2"], W["w_ff1_t"], mode="nt", tm=1024, tn=1024, tk=2048, out_dtype=CDT,
                            name="mm_ff1", epi="relu2")
    if nxt is None:
        s["z"], s["u"] = ff1()
        x2 = _matmul(s["u"], W["w_ff2"], mode="nn", tm=1024, tn=1024, tk=2048, out_dtype=F32, name="mm_ff2",
                     epi="residual", extra=s["x1"])
    else:
        (s["z"], s["u"]), got = ff1(rider=_join(ag1(early_next, "_early") if first else None,
                                                ag1([nxt["w_ff1_t"]], "_ff1a", part=(0, 2))))
        early_half, rows0 = (got[0:2], got[2:3]) if first else ([], got[0:1])
        x2, got = _matmul(s["u"], W["w_ff2"], mode="nn", tm=1024, tn=1024, tk=2048, out_dtype=F32, name="mm_ff2",
                          epi="residual", extra=s["x1"],
                          rider=_join(ag2(early_half, "_early"),
                                      ag1([nxt["w_ff1_t"]], "_ff1b", part=(1, 2), into=rows0)))
        if first:
            early, got = got[0:2], got[2:3]
        ff1_half_next = got[0]
    return x2, s, W, early, ff1_half_next


def _by_destination(dw, name):
    return dw.reshape((4, 2) + COMM_SHAPE[name])


def _early_by_destination(dw_in_t, dw_lat):
    dw_uq, dw_ukv = _lat_weight_grads(dw_lat)
    lat = jnp.concatenate([dw_uq.reshape(448, 8, 96).transpose(1, 0, 2).reshape(8, UQ_ROWS, LANES),
                           dw_ukv.reshape(512, 8, 128).transpose(1, 0, 2)], axis=1)
    return [_by_destination(dw_in_t, "w_in_t"), _by_destination(lat, "lat")]


def _layer_bwd(dx2, dx2b, s, W, G, tabs, scatter=False, pending=None):
    dw, dg, landed = {}, {}, {}
    ff2_dx = functools.partial(_matmul, dx2b, W["w_ff2"], mode="nt", tm=1024, tn=1024, tk=2048, out_dtype=CDT,
                               name="mm_ff2_dx", epi="drelu2", extra=s["z"])
    if pending is None:
        dz = ff2_dx()
    else:
        dz, got = ff2_dx(rider=_rs_sibling_rider(pending, "_early"))
        chip = [_rs_chip_sum(g, r, name="rs_chip_sum_" + n) for g, r, n in zip(pending, got, ("w_in_t", "lat"))]
    dw["w_ff2"] = _matmul(s["u"], dx2b, mode="tn", tm=2048, tn=1024, tk=2048, out_dtype=CDT, name="mm_ff2_dw")
    ff1_dx = functools.partial(_matmul, dz, W["w_ff1_t"], mode="nn", tm=1024, tn=1024, tk=2048, out_dtype=F32,
                               name="mm_ff1_dx")
    if pending is None:
        dh2 = ff1_dx()
    else:
        dh2, got = ff1_dx(rider=_rs_chip_rider(chip, "_early"))
        landed["above_w_in_t"], landed["above_lat"] = got
    dw["w_ff1_t"] = _matmul(dz, s["h2"], mode="tn", tm=2048, tn=1024, tk=2048, out_dtype=CDT, name="mm_ff1_dw")
    dx1, dx1b, dg["ln2_g"] = _rms_bwd(s["x1"], G["ln2_g"], dh2, dx2, name="rms2_bwd")
    dmixed = _matmul(dx1b, W["w_out"], mode="nt", tm=1024, tn=1024, tk=2048, out_dtype=F32, name="mm_out_dx")
    out_dw = functools.partial(_matmul, s["mixed"], dx1b, mode="tn", tm=1024, tn=1024, tk=2048, out_dtype=CDT,
                               name="mm_out_dw")
    if not scatter:
        dw["w_out"] = out_dw()
        riders = [None, None, None]
    else:
        g_ff = [_by_destination(dw["w_ff2"], "w_ff2"), _by_destination(dw["w_ff1_t"], "w_ff1_t")]
        dw["w_out"], got = out_dw(rider=_rs_sibling_rider(g_ff, "_ff"))
        chip_ff2 = _rs_chip_sum(g_ff[0], got[0], name="rs_chip_sum_w_ff2")
        chip_ff1 = _rs_chip_sum(g_ff[1], got[1], name="rs_chip_sum_w_ff1_t")
        g_out = [_by_destination(dw["w_out"], "w_out")]
        riders = [_rs_chip_rider([chip_ff2], "_ff2"),
                  _join(_rs_chip_rider([chip_ff1], "_ff1"), _rs_sibling_rider(g_out, "_out")), None]
    doa, dob, doc, dla, dlb, dlc, dg["g_out"] = _outnorm_bwd(s["oa"], s["ob"], s["oc"], G["g_out"], dmixed)
    (dqa, dka, dva), got = _attn_bwd(s["qa"], s["ka"], s["va"], doa, s["lse_a"], dla, H=4, G=1, dk=256, dv=128,
                                     scale=SCALE_A, name="attn_a_bwd", rider=riders[0])
    if scatter:
        landed["w_ff2"] = got[0]
    (dqb, dkb, dvb), got = _attn_bwd(s["qb"], s["kb"], s["vb"], dob, s["lse_b"], dlb, H=6, G=3, dk=128, dv=128,
                                     scale=SCALE_H, name="attn_b_bwd", rider=riders[1])
    if scatter:
        landed["w_ff1_t"] = got[0]
        riders[2] = _rs_chip_rider([_rs_chip_sum(g_out[0], got[1], name="rs_chip_sum_w_out")], "_out")
    (dqc, dkc, dvc), got = _mixc_bwd(s["qc"], s["kc"], s["vc"], doc, s["lse_c"], dlc, rider=riders[2])
    if scatter:
        landed["w_out"] = got[0]
    dqkva, dkr = _prep_a2_bwd(dqa, dka, dva, tabs[0])
    dlat = _matmul(dqkva, W["w_lat"], mode="nt", tm=1024, tn=1024, tk=2048, out_dtype=F32, name="mm_lat_dx")
    dw["w_lat"] = _matmul(s["lat"], dqkva, mode="tn", tm=1024, tn=2048, tk=2048, out_dtype=CDT, name="mm_lat_dw")
    dproj, dg["glat"], dg["gqn"], dg["gkn"] = _prep_bwd(
        s["proj"], G["glat"], G["gqn"], G["gkn"], tabs, dlat, dkr, dqb, dkb, dvb, dqc, dkc, dvc)
    dh1 = _matmul(dproj, W["w_in_t"], mode="nn", tm=1024, tn=1024, tk=2304, out_dtype=F32, name="mm_in_dx")
    dw["w_in_t"] = _matmul(dproj, s["h1"], mode="tn", tm=1536, tn=1024, tk=2048, out_dtype=CDT, name="mm_in_dw")
    dx0, dx0b, dg["ln1_g"] = _rms_bwd(s["x0"], G["ln1_g"], dh1, dx1, name="rms1_bwd")
    return dx0, dx0b, dw, dg, landed


def _layer_gains(l, ln1_g, g_q_a, g_kv_a, g_qn_b, g_kn_b, g_out, ln2_g):
    return {"ln1_g": ln1_g[l], "ln2_g": ln2_g[l], "g_out": g_out[l].reshape(1, 2048),
            "glat": jnp.concatenate([g_q_a[l], g_kv_a[l], jnp.zeros((LAT_W - KV_HI,), F32)]).reshape(1, LAT_W),
            "gqn": g_qn_b[l].reshape(1, 128), "gkn": g_kn_b[l].reshape(1, 128)}


def _gain_grads(dg):
    glat = dg["glat"].reshape(-1)
    return {"ln1_g": dg["ln1_g"].reshape(-1), "g_q_a": glat[0:KV_LO], "g_kv_a": glat[KV_LO:KV_HI],
            "g_qn_b": dg["gqn"].reshape(-1), "g_kn_b": dg["gkn"].reshape(-1), "g_out": dg["g_out"].reshape(-1),
            "ln2_g": dg["ln2_g"].reshape(-1)}


def _local_step(x, tgt, weights, gains, ln_f_g):
    S = x.shape[0]
    tabs = _rope_tables(S)
    depth = len(weights)
    saved = []
    for l in range(depth):
        x, s, _, _, _ = _layer_fwd(x, weights[l], gains[l], tabs)
        saved.append(s)
    loss, dx, dxb, dlnf = _loss_head(x, ln_f_g, tgt)
    dws, dgs = [None] * depth, [None] * depth
    for l in reversed(range(depth)):
        dx, dxb, dws[l], dgs[l], _ = _layer_bwd(dx, dxb, saved[l], weights[l], gains[l], tabs)
    return loss, dx, dws, dgs, dlnf


SMALL_SIZES = (("ln1_g", 2048), ("g_q_a", 448), ("g_kv_a", 512), ("g_qn_b", 128), ("g_kn_b", 128), ("g_out", 2048),
               ("ln2_g", 2048))


def _pack_small(per_layer, ln_f):
    flat = jnp.concatenate([per_layer[n].reshape(-1) for n, _ in SMALL_SIZES] + [ln_f.reshape(-1)])
    rows = -(-flat.shape[0] // (8 * LANES)) * 8
    return jnp.concatenate([flat, jnp.zeros((rows * LANES - flat.shape[0],), F32)]).reshape(rows, LANES)


def _unpack_small(packed, depth):
    flat, out, lo = packed.reshape(-1), {}, 0
    for n, w in SMALL_SIZES:
        out[n] = flat[lo:lo + depth * w].reshape(depth, w)
        lo += depth * w
    out["ln_f_g"] = flat[lo:lo + 2048]
    return out


def kernel(x, ln1_g, w_in, g_q_a, w_uq, g_kv_a, w_ukv, g_qn_b, g_kn_b, g_out, w_out, ln2_g, w_ff1, w_ff2, ln_f_g, loss_target, m_ln1_g, m_w_in, m_g_q_a, m_w_uq, m_g_kv_a, m_w_ukv, m_g_qn_b, m_g_kn_b, m_g_out, m_w_out, m_ln2_g, m_w_ff1, m_w_ff2, m_ln_f_g, v_ln1_g, v_w_in, v_g_q_a, v_w_uq, v_g_kv_a, v_w_ukv, v_g_qn_b, v_g_kn_b, v_g_out, v_w_out, v_ln2_g, v_w_ff1, v_w_ff2, v_ln_f_g):
    depth = w_in.shape[0]
    S = x.shape[1]
    big_w = {"w_in": w_in, "w_uq": w_uq, "w_ukv": w_ukv, "w_out": w_out, "w_ff1": w_ff1, "w_ff2": w_ff2}
    big_m = {"w_in": m_w_in, "w_uq": m_w_uq, "w_ukv": m_w_ukv, "w_out": m_w_out, "w_ff1": m_w_ff1, "w_ff2": m_w_ff2}
    big_v = {"w_in": v_w_in, "w_uq": v_w_uq, "w_ukv": v_w_ukv, "w_out": v_w_out, "w_ff1": v_w_ff1, "w_ff2": v_w_ff2}
    small_w = {"ln1_g": ln1_g, "g_q_a": g_q_a, "g_kv_a": g_kv_a, "g_qn_b": g_qn_b, "g_kn_b": g_kn_b, "g_out": g_out,
               "ln2_g": ln2_g}
    small_m = {"ln1_g": m_ln1_g, "g_q_a": m_g_q_a, "g_kv_a": m_g_kv_a, "g_qn_b": m_g_qn_b, "g_kn_b": m_g_kn_b,
               "g_out": m_g_out, "ln2_g": m_ln2_g}
    small_v = {"ln1_g": v_ln1_g, "g_q_a": v_g_q_a, "g_kv_a": v_g_kv_a, "g_qn_b": v_g_qn_b, "g_kn_b": v_g_kn_b,
               "g_out": v_g_out, "ln2_g": v_ln2_g}

    shards = [dict(zip(COMM, _comm_shards(*[big_w[n][l].astype(CDT) for n in BIG]))) for l in range(depth)]
    early_shards = [[sh["w_in_t"], sh["lat"]] for sh in shards]
    early = _run_rider(_ag_second_rider(_run_rider(_ag_first_rider(early_shards[0], "_early")), "_early"))
    gains = [_layer_gains(l, ln1_g, g_q_a, g_kv_a, g_qn_b, g_kn_b, g_out, ln2_g) for l in range(depth)]
    tabs = _rope_tables(S)

    h = x.reshape(S, D_MODEL)
    saved, weights, f_half = [], [], None
    for l in range(depth):
        W = dict(zip(("w_in_t", "w_lat"), _early_weights(*early)))
        h, s, W, early, f_half = _layer_fwd(h, W, gains[l], tabs,
                                            plan=(shards[l], shards[l + 1] if l + 1 < depth else None, f_half))
        saved.append(s)
        weights.append(W)
    loss_part, dx, dxb, dlnf = _loss_head(h, ln_f_g, loss_target.reshape(S, D_MODEL))
    loss = lax.psum(loss_part[0, 0], ("x", "y", "c"))

    dgs, landed, pending = [None] * depth, [None] * depth, None
    for l in reversed(range(depth)):
        dx, dxb, dw, dgs[l], landed[l] = _layer_bwd(dx, dxb, saved[l], weights[l], gains[l], tabs, scatter=True,
                                                    pending=pending)
        if pending is not None:
            landed[l + 1]["w_in_t"], landed[l + 1]["lat"] = landed[l].pop("above_w_in_t"), landed[l].pop("above_lat")
        pending = _early_by_destination(dw["w_in_t"], dw["w_lat"])
    got = _run_rider(_rs_sibling_rider(pending, "_early"))
    chip = [_rs_chip_sum(g, r, name="rs_chip_sum_" + n) for g, r, n in zip(pending, got, ("w_in_t", "lat"))]
    landed[0]["w_in_t"], landed[0]["lat"] = _run_rider(_rs_chip_rider(chip, "_early"))
    grad_x = dx.reshape(1, S, D_MODEL)

    shard_grads = [_from_comm_shards([_rs_final_sum(landed[l][n], name="rs_final_sum_" + n) for n in COMM])
                   for l in range(depth)]
    big_g = {n: jnp.stack([shard_grads[l][n] for l in range(depth)]) for n in BIG}

    named = [_gain_grads(dgs[l]) for l in range(depth)]
    per_layer = {n: jnp.stack([named[l][n] for l in range(depth)]) for n, _ in SMALL_SIZES}
    small_g = _unpack_small(_all_reduce_small(_pack_small(per_layer, dlnf.reshape(-1))), depth)

    upd = {}
    for n in BIG:
        shp = big_w[n].shape
        two_d = (shp[0] * shp[1], shp[2])
        d, nm, nv = _adamw(big_w[n].reshape(two_d), big_g[n].reshape(two_d), big_m[n].reshape(two_d),
                           big_v[n].reshape(two_d), name="adamw_" + n)
        upd[n] = (d.reshape(shp), nm.reshape(shp), nv.reshape(shp))
    small_w["ln_f_g"], small_m["ln_f_g"], small_v["ln_f_g"] = ln_f_g, m_ln_f_g, v_ln_f_g
    names_small = [n for n, _ in SMALL_SIZES]
    pw = _pack_small({n: small_w[n] for n in names_small}, small_w["ln_f_g"])
    pg = _pack_small({n: small_g[n] for n in names_small}, small_g["ln_f_g"])
    pm = _pack_small({n: small_m[n] for n in names_small}, small_m["ln_f_g"])
    pv = _pack_small({n: small_v[n] for n in names_small}, small_v["ln_f_g"])
    d, nm, nv = _adamw(pw, pg, pm, pv, name="adamw_small")
    sd, snm, snv = _unpack_small(d, depth), _unpack_small(nm, depth), _unpack_small(nv, depth)
    for n in names_small + ["ln_f_g"]:
        upd[n] = (sd[n], snm[n], snv[n])

    order = ["ln1_g", "w_in", "g_q_a", "w_uq", "g_kv_a", "w_ukv", "g_qn_b", "g_kn_b", "g_out", "w_out", "ln2_g", "w_ff1",
             "w_ff2", "ln_f_g"]
    grads = {**big_g, **small_g}
    return (loss, grad_x, *[grads[n] for n in order], *[upd[n][0] for n in order], *[upd[n][1] for n in order],
            *[upd[n][2] for n in order])
```

```python
import functools
import math

import jax
import jax.numpy as jnp
from jax import lax
from jax.experimental import pallas as pl
from jax.experimental.pallas import tpu as pltpu

D_MODEL = 2048
D_FF = 8192
EPS = 1e-6
NEG_INF = -1e30
Q_LORA = 448
ROPE_THETA = 10000.0
GRID_W = 64
DILATIONS = (1, 4, 16)
BAND_HALF = 64
SCALE_A = 1.0 / math.sqrt(192.0)
SCALE_H = 1.0 / math.sqrt(128.0)
ADAM_LR, ADAM_B1, ADAM_B2, ADAM_EPS, ADAM_WD, ADAM_STEP = 0.001, 0.9, 0.999, 1e-08, 0.01, 10

CDT = jnp.bfloat16
F32 = jnp.float32
LANES = 128
VMEM_LIMIT = 56 * 1024 * 1024

PROJ_W = 4608
LAT_W = 1024
KV_LO, KV_HI = 448, 960
OFF_BQ, OFF_BK, OFF_BV, OFF_CQ, OFF_CK, OFF_CV = 1024, 1792, 2048, 2304, 3072, 3840

NN = ((1,), (0,))
NT = ((1,), (1,))
TN = ((0,), (0,))

BIG = ("w_in", "w_uq", "w_ukv", "w_out", "w_ff1", "w_ff2")
COMM = ("w_in_t", "w_ff1_t", "w_out", "w_ff2", "lat")
COMM_SHAPE = {"w_in_t": (576, 2048), "w_ff1_t": (1024, 2048), "w_out": (256, 2048), "w_ff2": (1024, 2048),
              "lat": (848, 128)}
UQ_ROWS = 448 * 96 // LANES


def _dot(a, b, dims):
    return lax.dot_general(a, b, (dims, ((), ())), preferred_element_type=F32)


def _cparams(dims=None):
    return pltpu.CompilerParams(dimension_semantics=dims, vmem_limit_bytes=VMEM_LIMIT)


def _sds(shape, dtype):
    return jax.ShapeDtypeStruct(shape, dtype)


MESH = pl.DeviceIdType.MESH
ANY = pl.BlockSpec(memory_space=pl.ANY)


class _Rider:
    def __init__(self, name, arrays, out_shape, scratch, aliases, start, finish):
        self.name, self.arrays, self.out_shape, self.scratch = name, list(arrays), list(out_shape), list(scratch)
        self.aliases, self.start, self.finish = dict(aliases), start, finish


def _join(a, b):
    if a is None or b is None:
        return a if b is None else b
    na, oa, sa = len(a.arrays), len(a.out_shape), len(a.scratch)
    aliases = dict(a.aliases)
    aliases.update({na + i: oa + o for i, o in b.aliases.items()})

    def start(ins, outs, sems):
        a.start(ins[:na], outs[:oa], sems[:sa])
        b.start(ins[na:], outs[oa:], sems[sa:])

    def finish(ins, outs, sems):
        a.finish(ins[:na], outs[:oa], sems[:sa])
        b.finish(ins[na:], outs[oa:], sems[sa:])

    return _Rider(a.name + "_" + b.name, a.arrays + b.arrays, a.out_shape + b.out_shape, a.scratch + b.scratch,
                  aliases, start, finish)


def _carried_call(body, *, name, grid, in_specs, out_specs, out_shape, scratch_shapes, dims, args, rider):
    in_specs, out_specs, out_shape = list(in_specs), list(out_specs), list(out_shape)
    scratch_shapes = list(scratch_shapes)
    if rider is None:
        res = pl.pallas_call(body, name=name, grid=grid, in_specs=in_specs, out_specs=out_specs, out_shape=out_shape,
                             scratch_shapes=scratch_shapes, compiler_params=_cparams(dims))(*args)
        return list(res), []
    n_in, n_out, n_scr = len(in_specs), len(out_specs), len(scratch_shapes)
    r_in, r_out = len(rider.arrays), len(rider.out_shape)

    def wrapped(*refs):
        o0 = n_in + r_in
        s0 = o0 + n_out + r_out
        ins, outs, sems = refs[n_in:o0], refs[o0 + n_out:s0], refs[s0 + n_scr:]
        ids = [pl.program_id(a) for a in range(len(grid))]
        first = functools.reduce(jnp.logical_and, [i == 0 for i in ids])
        last = functools.reduce(jnp.logical_and, [i == g - 1 for i, g in zip(ids, grid)])

        @pl.when(first)
        def _():
            rider.start(ins, outs, sems)

        body(*refs[:n_in], *refs[o0:o0 + n_out], *refs[s0:s0 + n_scr])

        @pl.when(last)
        def _():
            rider.finish(ins, outs, sems)

    res = pl.pallas_call(
        wrapped, name=name + "_" + rider.name, grid=grid, in_specs=in_specs + [ANY] * r_in,
        out_specs=out_specs + [ANY] * r_out, out_shape=out_shape + rider.out_shape,
        scratch_shapes=scratch_shapes + rider.scratch,
        input_output_aliases={n_in + i: n_out + o for i, o in rider.aliases.items()},
        compiler_params=_cparams(("arbitrary",) * len(grid)),
    )(*args, *rider.arrays)
    return list(res[:n_out]), list(res[n_out:])


def _run_rider(rider):
    def body(*refs):
        r_in, r_out = len(rider.arrays), len(rider.out_shape)
        ins, outs, sems = refs[:r_in], refs[r_in:r_in + r_out], refs[r_in + r_out:]
        rider.start(ins, outs, sems)
        rider.finish(ins, outs, sems)

    res = pl.pallas_call(
        body, name=rider.name, in_specs=[ANY] * len(rider.arrays), out_specs=[ANY] * len(rider.out_shape),
        out_shape=rider.out_shape, scratch_shapes=rider.scratch, input_output_aliases=rider.aliases,
    )(*rider.arrays)
    return list(res)


def _mesh_place():
    x, y, c = lax.axis_index("x"), lax.axis_index("y"), lax.axis_index("c")
    return x, y, c, [(1 - x, y), (x, 1 - y), (1 - x, 1 - y)]


def _remote(src, dst, send, recv, dev):
    return pltpu.make_async_remote_copy(src_ref=src, dst_ref=dst, send_sem=send, recv_sem=recv, device_id=dev,
                                        device_id_type=MESH)


def _ag_first_rider(shards, tag, part=(0, 1), into=None):
    n = len(shards)

    def copies(ins, outs, sems):
        send, recv, _ = sems
        x, y, c, chips = _mesh_place()
        me = 4 * x + 2 * y + c
        peers = [(x, y, 1 - c)] + [(cx, cy, c) for cx, cy in chips]
        out, mine = [], []
        for t in range(n):
            size = shards[t].shape[0] // part[1]
            rows = pl.ds(part[0] * size, size)
            for k, dev in enumerate(peers):
                theirs = 4 * dev[0] + 2 * dev[1] + dev[2]
                out.append((_remote(ins[t].at[rows], outs[t].at[me, rows], send.at[t, k], recv.at[t, k], dev),
                            _remote(ins[t].at[rows], outs[t].at[theirs, rows], send.at[t, k], recv.at[t, k], dev)))
            mine.append(pltpu.make_async_copy(ins[t].at[rows], outs[t].at[me, rows], sems[2].at[t]))
        return out, mine

    def start(ins, outs, sems):
        pairs, mine = copies(ins, outs, sems)
        for cp in mine:
            cp.start()
        for snd, _ in pairs:
            snd.start()

    def finish(ins, outs, sems):
        pairs, mine = copies(ins, outs, sems)
        for _, rcv in pairs:
            rcv.wait_recv()
        for snd, _ in pairs:
            snd.wait_send()
        for cp in mine:
            cp.wait()

    return _Rider("ag1" + tag, list(shards) + list(into or []), [_sds((8,) + s.shape, s.dtype) for s in shards],
                  [pltpu.SemaphoreType.DMA((n, 4)), pltpu.SemaphoreType.DMA((n, 4)), pltpu.SemaphoreType.DMA((n,))],
                  {} if into is None else {n + t: t for t in range(n)}, start, finish)


def _ag_second_rider(gathered, tag):
    n = len(gathered)

    def copies(ins, outs, sems):
        send, recv = sems
        x, y, c, chips = _mesh_place()
        out = []
        for t in range(n):
            for j, (cx, cy) in enumerate(chips):
                here, there = 4 * cx + 2 * cy + c, 4 * cx + 2 * cy + (1 - c)
                out.append((_remote(ins[t].at[here], outs[t].at[here], send.at[t, j], recv.at[t, j], (x, y, 1 - c)),
                            _remote(ins[t].at[here], outs[t].at[there], send.at[t, j], recv.at[t, j], (x, y, 1 - c))))
        return out

    def start(ins, outs, sems):
        for snd, _ in copies(ins, outs, sems):
            snd.start()

    def finish(ins, outs, sems):
        pairs = copies(ins, outs, sems)
        for _, rcv in pairs:
            rcv.wait_recv()
        for snd, _ in pairs:
            snd.wait_send()

    return _Rider("ag2" + tag, gathered, [_sds(g.shape, g.dtype) for g in gathered],
                  [pltpu.SemaphoreType.DMA((n, 3)), pltpu.SemaphoreType.DMA((n, 3))],
                  {t: t for t in range(n)}, start, finish)


def _rs_sibling_rider(gs, tag):
    n = len(gs)

    def copies(ins, outs, sems):
        send, recv = sems
        x, y, c, _ = _mesh_place()
        return [_remote(ins[t].at[k, 1 - c], outs[t].at[k], send.at[t, k], recv.at[t, k], (x, y, 1 - c))
                for t in range(n) for k in range(4)]

    def start(ins, outs, sems):
        for cp in copies(ins, outs, sems):
            cp.start()

    def finish(ins, outs, sems):
        for cp in copies(ins, outs, sems):
            cp.wait()

    return _Rider("rs1" + tag, gs, [_sds((4,) + g.shape[2:], g.dtype) for g in gs],
                  [pltpu.SemaphoreType.DMA((n, 4)), pltpu.SemaphoreType.DMA((n, 4))], {}, start, finish)


def _rs_chip_rider(ps, tag):
    n = len(ps)

    def copies(ins, outs, sems):
        send, recv, local = sems
        x, y, c, chips = _mesh_place()
        my_chip = 2 * x + y
        out = []
        for t in range(n):
            for j, (cx, cy) in enumerate(chips):
                dev = (cx, cy, c)
                out.append((_remote(ins[t].at[2 * cx + cy], outs[t].at[my_chip], send.at[t, j], recv.at[t, j], dev),
                            _remote(ins[t].at[my_chip], outs[t].at[2 * cx + cy], send.at[t, j], recv.at[t, j], dev)))
        mine = [pltpu.make_async_copy(ins[t].at[my_chip], outs[t].at[my_chip], local.at[t]) for t in range(n)]
        return out, mine

    def start(ins, outs, sems):
        pairs, mine = copies(ins, outs, sems)
        for cp in mine:
            cp.start()
        for snd, _ in pairs:
            snd.start()

    def finish(ins, outs, sems):
        pairs, mine = copies(ins, outs, sems)
        for _, rcv in pairs:
            rcv.wait_recv()
        for snd, _ in pairs:
            snd.wait_send()
        for cp in mine:
            cp.wait()

    return _Rider("rs2" + tag, ps, [_sds(p.shape, p.dtype) for p in ps],
                  [pltpu.SemaphoreType.DMA((n, 3)), pltpu.SemaphoreType.DMA((n, 3)), pltpu.SemaphoreType.DMA((n,))],
                  {}, start, finish)


def _matmul(a, b, *, mode, tm, tn, tk, out_dtype, name, epi=None, extra=None, rider=None):
    if mode == "nn":
        (M, K), (K2, N) = a.shape, b.shape
    elif mode == "nt":
        (M, K), (N, K2) = a.shape, b.shape
    else:
        (K, M), (K2, N) = a.shape, b.shape
    tm, tn, tk = min(tm, M), min(tn, N), min(tk, K)
    assert K == K2 and M % tm == 0 and N % tn == 0 and K % tk == 0, (name, a.shape, b.shape)
    nk = K // tk
    dims = {"nn": NN, "nt": NT, "tn": TN}[mode]
    if mode == "tn":
        a_spec = pl.BlockSpec((tk, tm), lambda i, j, k: (k, i))
    else:
        a_spec = pl.BlockSpec((tm, tk), lambda i, j, k: (i, k))
    if mode == "nt":
        b_spec = pl.BlockSpec((tn, tk), lambda i, j, k: (j, k))
    else:
        b_spec = pl.BlockSpec((tk, tn), lambda i, j, k: (k, j))
    tile = pl.BlockSpec((tm, tn), lambda i, j, k: (i, j))
    n_extra = 1 if epi in ("residual", "drelu2") else 0
    n_out = 2 if epi == "relu2" else 1

    def body(*refs):
        a_ref, b_ref = refs[0], refs[1]
        extra_refs = refs[2:2 + n_extra]
        out_refs = refs[2 + n_extra:2 + n_extra + n_out]

        def finish(acc):
            if epi is None:
                out_refs[0][...] = acc.astype(out_dtype)
            elif epi == "residual":
                out_refs[0][...] = (extra_refs[0][...] + acc).astype(out_dtype)
            elif epi == "relu2":
                out_refs[0][...] = acc.astype(out_dtype)
                r = jnp.maximum(acc, 0.0)
                out_refs[1][...] = (r * r).astype(out_dtype)
            else:
                z = extra_refs[0][...].astype(F32)
                out_refs[0][...] = (acc * (2.0 * jnp.maximum(z, 0.0))).astype(out_dtype)

        part = _dot(a_ref[...], b_ref[...], dims)
        if nk == 1:
            finish(part)
        else:
            acc_ref = refs[-1]
            k = pl.program_id(2)

            @pl.when(k == 0)
            def _():
                acc_ref[...] = part

            @pl.when(k > 0)
            def _():
                acc_ref[...] += part

            @pl.when(k == nk - 1)
            def _():
                finish(acc_ref[...])

    res, carried = _carried_call(
        body, name=name, grid=(M // tm, N // tn, nk), in_specs=[a_spec, b_spec] + [tile] * n_extra,
        out_specs=[tile] * n_out, out_shape=[_sds((M, N), out_dtype)] * n_out,
        scratch_shapes=[pltpu.VMEM((tm, tn), F32)] if nk > 1 else [],
        dims=("parallel", "parallel", "arbitrary"), args=[a, b] + ([extra] if n_extra else []), rider=rider)
    res = res if n_out > 1 else res[0]
    return res if rider is None else (res, carried)


def _rms_val(x, g, n):
    r = lax.rsqrt(jnp.sum(x * x, axis=-1, keepdims=True) * (1.0 / n) + EPS)
    y = x * r
    return (y if g is None else y * g), r


def _rms_bwd_val(x, g, dy, n):
    r = lax.rsqrt(jnp.sum(x * x, axis=-1, keepdims=True) * (1.0 / n) + EPS)
    xhat = x * r
    dyg = dy if g is None else dy * g
    dx = r * (dyg - xhat * (jnp.sum(dyg * xhat, axis=-1, keepdims=True) * (1.0 / n)))
    return dx, dy * xhat


def _rope_val(x, c, sa, sb, shift):
    return x * c + pltpu.roll(x, LANES - shift, 1) * sa + pltpu.roll(x, shift, 1) * sb


def _rope_t_val(dy, c, sa, sb, shift):
    return dy * c + pltpu.roll(dy * sa, shift, 1) + pltpu.roll(dy * sb, LANES - shift, 1)


def _colsum(x):
    return jnp.sum(x, axis=0, keepdims=True)


def _rope_tables(S):
    pos = lax.broadcasted_iota(jnp.int32, (S, LANES), 0)
    lane = lax.broadcasted_iota(jnp.int32, (S, LANES), 1)

    def tables(p, dim, active):
        half = dim // 2
        inv = jnp.power(ROPE_THETA, -(2 * (lane % half)).astype(F32) / dim)
        a = p.astype(F32) * inv
        first = (lane % dim) < half
        zero = jnp.zeros((S, LANES), F32)
        return (jnp.where(active, jnp.cos(a), zero), jnp.where(active & first, -jnp.sin(a), zero),
                jnp.where(active & ~first, jnp.sin(a), zero))

    tab_a = tables(pos, 64, lane >= 64)
    tab_b = tables(jnp.where(lane < 64, pos // GRID_W, pos % GRID_W), 64, lane >= 0)
    tab_c = tables(pos, 128, lane >= 0)
    return tab_a, tab_b, tab_c


ROPE_SHIFT_AB = 32
ROPE_SHIFT_C = 64


def _rms_fwd(x, g, *, name, tr=512):
    S, W = x.shape
    tr = min(tr, S)

    def body(x_ref, g_ref, o_ref):
        y, _ = _rms_val(x_ref[...], g_ref[...], W)
        o_ref[...] = y.astype(CDT)

    return pl.pallas_call(
        body, name=name, grid=(S // tr,),
        in_specs=[pl.BlockSpec((tr, W), lambda i: (i, 0)), pl.BlockSpec((1, W), lambda i: (0, 0))],
        out_specs=pl.BlockSpec((tr, W), lambda i: (i, 0)), out_shape=_sds((S, W), CDT),
        compiler_params=_cparams(("parallel",)),
    )(x, g.reshape(1, W))


def _rms_bwd(x, g, dy, res, *, name, tr=256):
    S, W = x.shape
    tr = min(tr, S)

    def body(x_ref, g_ref, dy_ref, res_ref, dx_ref, dxb_ref, dg_ref):
        dx, dgt = _rms_bwd_val(x_ref[...], g_ref[...], dy_ref[...], W)
        dx = res_ref[...] + dx
        dx_ref[...] = dx
        dxb_ref[...] = dx.astype(CDT)

        @pl.when(pl.program_id(0) == 0)
        def _():
            dg_ref[...] = jnp.zeros_like(dg_ref)

        dg_ref[...] += _colsum(dgt)

    row = pl.BlockSpec((tr, W), lambda i: (i, 0))
    vec = pl.BlockSpec((1, W), lambda i: (0, 0))
    return pl.pallas_call(
        body, name=name, grid=(S // tr,),
        in_specs=[row, vec, row, row], out_specs=[row, row, vec],
        out_shape=[_sds((S, W), F32), _sds((S, W), CDT), _sds((1, W), F32)],
        compiler_params=_cparams(("arbitrary",)),
    )(x, g.reshape(1, W), dy, res)


def _loss_head(x, g, tgt, *, tr=256):
    S, W = x.shape
    tr = min(tr, S)

    def body(x_ref, g_ref, t_ref, loss_ref, dx_ref, dxb_ref, dg_ref):
        xv, gv = x_ref[...], g_ref[...]
        y, _ = _rms_val(xv, gv, W)
        err = y - t_ref[...]
        part = 0.5 * jnp.sum(jnp.sum(err * err, axis=-1, keepdims=True) * (1.0 / W), axis=0, keepdims=True)
        dx, dgt = _rms_bwd_val(xv, gv, err * (1.0 / W), W)
        dx_ref[...] = dx
        dxb_ref[...] = dx.astype(CDT)

        @pl.when(pl.program_id(0) == 0)
        def _():
            dg_ref[...] = jnp.zeros_like(dg_ref)
            loss_ref[...] = jnp.zeros_like(loss_ref)

        dg_ref[...] += _colsum(dgt)
        loss_ref[...] += jnp.broadcast_to(part, (1, LANES))

    row = pl.BlockSpec((tr, W), lambda i: (i, 0))
    vec = pl.BlockSpec((1, W), lambda i: (0, 0))
    return pl.pallas_call(
        body, name="loss_head", grid=(S // tr,),
        in_specs=[row, vec, row], out_specs=[pl.BlockSpec((1, LANES), lambda i: (0, 0)), row, row, vec],
        out_shape=[_sds((1, LANES), F32), _sds((S, W), F32), _sds((S, W), CDT), _sds((1, W), F32)],
        compiler_params=_cparams(("arbitrary",)),
    )(x, g.reshape(1, W), tgt)


def _tab_specs(tr):
    return [pl.BlockSpec((tr, LANES), lambda i: (i, 0))] * 9


def _lat_masks(shape):
    lane = lax.broadcasted_iota(jnp.int32, shape, 1)
    return lane < KV_LO, (lane >= KV_LO) & (lane < KV_HI)


def _prep_fwd(proj, glat, gqn, gkn, tabs, *, tr=256):
    S = proj.shape[0]
    tr = min(tr, S)

    def body(p_ref, glat_ref, gqn_ref, gkn_ref, ac, aa, ab, bc, ba, bb, cc, ca, cb,
             lat_ref, kpe_ref, qb_ref, kb_ref, vb_ref, qc_ref, kc_ref, vc_ref):
        x = p_ref[:, 0:LAT_W].astype(F32)
        is_q, is_kv = _lat_masks(x.shape)
        yq, _ = _rms_val(jnp.where(is_q, x, 0.0), None, Q_LORA)
        ykv, _ = _rms_val(jnp.where(is_kv, x, 0.0), None, KV_HI - KV_LO)
        lat_ref[...] = ((yq + ykv) * glat_ref[...]).astype(CDT)
        kpe_ref[...] = _rope_val(x[:, LAT_W - LANES:LAT_W], ac[...], aa[...], ab[...], ROPE_SHIFT_AB).astype(CDT)
        for h in range(6):
            xh = p_ref[:, OFF_BQ + LANES * h:OFF_BQ + LANES * (h + 1)].astype(F32)
            y = _rope_val(_rms_val(xh, gqn_ref[...], LANES)[0], bc[...], ba[...], bb[...], ROPE_SHIFT_AB)
            qb_ref[:, LANES * h:LANES * (h + 1)] = y.astype(CDT)
        for h in range(2):
            xh = p_ref[:, OFF_BK + LANES * h:OFF_BK + LANES * (h + 1)].astype(F32)
            y = _rope_val(_rms_val(xh, gkn_ref[...], LANES)[0], bc[...], ba[...], bb[...], ROPE_SHIFT_AB)
            kb_ref[:, LANES * h:LANES * (h + 1)] = y.astype(CDT)
        vb_ref[...] = p_ref[:, OFF_BV:OFF_BV + 256].astype(CDT)
        for h in range(6):
            sl = slice(LANES * h, LANES * (h + 1))
            qc_ref[:, sl] = _rope_val(p_ref[:, OFF_CQ + LANES * h:OFF_CQ + LANES * (h + 1)].astype(F32), cc[...],
                                      ca[...], cb[...], ROPE_SHIFT_C)
            kc_ref[:, sl] = _rope_val(p_ref[:, OFF_CK + LANES * h:OFF_CK + LANES * (h + 1)].astype(F32), cc[...],
                                      ca[...], cb[...], ROPE_SHIFT_C)
        vc_ref[...] = p_ref[:, OFF_CV:OFF_CV + 768].astype(F32)

    vec = lambda w: pl.BlockSpec((1, w), lambda i: (0, 0))
    row = lambda w: pl.BlockSpec((tr, w), lambda i: (i, 0))
    return pl.pallas_call(
        body, name="prep_fwd", grid=(S // tr,),
        in_specs=[row(PROJ_W), vec(LAT_W), vec(128), vec(128)] + _tab_specs(tr),
        out_specs=[row(LAT_W), row(128), row(768), row(256), row(256), row(768), row(768), row(768)],
        out_shape=[_sds((S, LAT_W), CDT), _sds((S, 128), CDT), _sds((S, 768), CDT), _sds((S, 256), CDT),
                   _sds((S, 256), CDT), _sds((S, 768), F32), _sds((S, 768), F32), _sds((S, 768), F32)],
        compiler_params=_cparams(("parallel",)),
    )(proj, glat, gqn, gkn, *tabs[0], *tabs[1], *tabs[2])


def _prep_a2_fwd(qkva, kpe, tab_a, *, tr=512):
    S = qkva.shape[0]
    tr = min(tr, S)

    def body(x_ref, kpe_ref, ac, aa, ab, qa_ref, ka_ref, va_ref):
        for h in range(4):
            lo, hi = 2 * LANES * h, 2 * LANES * h + LANES
            qa_ref[:, lo:hi] = x_ref[:, lo:hi].astype(CDT)
            qa_ref[:, hi:hi + LANES] = _rope_val(x_ref[:, hi:hi + LANES], ac[...], aa[...], ab[...],
                                                 ROPE_SHIFT_AB).astype(CDT)
            ka_ref[:, lo:hi] = x_ref[:, 1024 + LANES * h:1024 + LANES * (h + 1)].astype(CDT)
            ka_ref[:, hi:hi + LANES] = kpe_ref[...]
        va_ref[...] = x_ref[:, 1536:2048].astype(CDT)

    row = lambda w: pl.BlockSpec((tr, w), lambda i: (i, 0))
    return pl.pallas_call(
        body, name="prep_a2_fwd", grid=(S // tr,),
        in_specs=[row(2048), row(128)] + _tab_specs(tr)[:3],
        out_specs=[row(1024), row(1024), row(512)],
        out_shape=[_sds((S, 1024), CDT), _sds((S, 1024), CDT), _sds((S, 512), CDT)],
        compiler_params=_cparams(("parallel",)),
    )(qkva, kpe, *tab_a)


def _prep_a2_bwd(dqa, dka, dva, tab_a, *, tr=512):
    S = dqa.shape[0]
    tr = min(tr, S)

    def body(dq_ref, dk_ref, dv_ref, ac, aa, ab, dx_ref, dkr_ref):
        dkpe = jnp.zeros((tr, LANES), F32)
        for h in range(4):
            lo, hi = 2 * LANES * h, 2 * LANES * h + LANES
            dx_ref[:, lo:hi] = dq_ref[:, lo:hi].astype(CDT)
            dx_ref[:, hi:hi + LANES] = _rope_t_val(dq_ref[:, hi:hi + LANES], ac[...], aa[...], ab[...],
                                                   ROPE_SHIFT_AB).astype(CDT)
            dx_ref[:, 1024 + LANES * h:1024 + LANES * (h + 1)] = dk_ref[:, lo:hi].astype(CDT)
            dkpe = dkpe + dk_ref[:, hi:hi + LANES]
        dx_ref[:, 1536:2048] = dv_ref[...].astype(CDT)
        dkr_ref[...] = _rope_t_val(dkpe, ac[...], aa[...], ab[...], ROPE_SHIFT_AB)

    row = lambda w: pl.BlockSpec((tr, w), lambda i: (i, 0))
    return pl.pallas_call(
        body, name="prep_a2_bwd", grid=(S // tr,),
        in_specs=[row(1024), row(1024), row(512)] + _tab_specs(tr)[:3],
        out_specs=[row(2048), row(128)],
        out_shape=[_sds((S, 2048), CDT), _sds((S, 128), F32)],
        compiler_params=_cparams(("parallel",)),
    )(dqa, dka, dva, *tab_a)


def _prep_bwd(proj, glat, gqn, gkn, tabs, dlat, dkr, dqb, dkb, dvb, dqc, dkc, dvc, *, tr=256):
    S = proj.shape[0]
    tr = min(tr, S)

    def body(p_ref, glat_ref, gqn_ref, gkn_ref, ac, aa, ab, bc, ba, bb, cc, ca, cb,
             dlat_ref, dkr_ref, dqb_ref, dkb_ref, dvb_ref, dqc_ref, dkc_ref, dvc_ref,
             dp_ref, dglat_ref, dgqn_ref, dgkn_ref):
        @pl.when(pl.program_id(0) == 0)
        def _():
            dglat_ref[...] = jnp.zeros_like(dglat_ref)
            dgqn_ref[...] = jnp.zeros_like(dgqn_ref)
            dgkn_ref[...] = jnp.zeros_like(dgkn_ref)

        x = p_ref[:, 0:LAT_W].astype(F32)
        is_q, is_kv = _lat_masks(x.shape)
        dy, g = dlat_ref[...], glat_ref[...]
        dxq, dgq = _rms_bwd_val(jnp.where(is_q, x, 0.0), g, jnp.where(is_q, dy, 0.0), Q_LORA)
        dxkv, dgkv = _rms_bwd_val(jnp.where(is_kv, x, 0.0), g, jnp.where(is_kv, dy, 0.0), KV_HI - KV_LO)
        dglat_ref[...] += _colsum(dgq + dgkv)
        dx = dxq + dxkv
        dp_ref[:, 0:LAT_W - LANES] = dx[:, 0:LAT_W - LANES].astype(CDT)
        dp_ref[:, LAT_W - LANES:LAT_W] = (dx[:, LAT_W - LANES:LAT_W] + dkr_ref[...]).astype(CDT)
        dgqn = jnp.zeros((1, LANES), F32)
        for h in range(6):
            sl = slice(LANES * h, LANES * (h + 1))
            po = slice(OFF_BQ + LANES * h, OFF_BQ + LANES * (h + 1))
            dyh = _rope_t_val(dqb_ref[:, sl], bc[...], ba[...], bb[...], ROPE_SHIFT_AB)
            dxh, dgt = _rms_bwd_val(p_ref[:, po].astype(F32), gqn_ref[...], dyh, LANES)
            dp_ref[:, po] = dxh.astype(CDT)
            dgqn = dgqn + _colsum(dgt)
        dgqn_ref[...] += dgqn
        dgkn = jnp.zeros((1, LANES), F32)
        for h in range(2):
            sl = slice(LANES * h, LANES * (h + 1))
            po = slice(OFF_BK + LANES * h, OFF_BK + LANES * (h + 1))
            dyh = _rope_t_val(dkb_ref[:, sl], bc[...], ba[...], bb[...], ROPE_SHIFT_AB)
            dxh, dgt = _rms_bwd_val(p_ref[:, po].astype(F32), gkn_ref[...], dyh, LANES)
            dp_ref[:, po] = dxh.astype(CDT)
            dgkn = dgkn + _colsum(dgt)
        dgkn_ref[...] += dgkn
        dp_ref[:, OFF_BV:OFF_BV + 256] = dvb_ref[...].astype(CDT)
        for h in range(6):
            sl = slice(LANES * h, LANES * (h + 1))
            dp_ref[:, OFF_CQ + LANES * h:OFF_CQ + LANES * (h + 1)] = _rope_t_val(
                dqc_ref[:, sl], cc[...], ca[...], cb[...], ROPE_SHIFT_C).astype(CDT)
            dp_ref[:, OFF_CK + LANES * h:OFF_CK + LANES * (h + 1)] = _rope_t_val(
                dkc_ref[:, sl], cc[...], ca[...], cb[...], ROPE_SHIFT_C).astype(CDT)
        dp_ref[:, OFF_CV:OFF_CV + 768] = dvc_ref[...].astype(CDT)

    vec = lambda w: pl.BlockSpec((1, w), lambda i: (0, 0))
    row = lambda w: pl.BlockSpec((tr, w), lambda i: (i, 0))
    return pl.pallas_call(
        body, name="prep_bwd", grid=(S // tr,),
        in_specs=[row(PROJ_W), vec(LAT_W), vec(128), vec(128)] + _tab_specs(tr)
        + [row(LAT_W), row(128), row(768), row(256), row(256), row(768), row(768), row(768)],
        out_specs=[row(PROJ_W), vec(LAT_W), vec(128), vec(128)],
        out_shape=[_sds((S, PROJ_W), CDT), _sds((1, LAT_W), F32), _sds((1, 128), F32), _sds((1, 128), F32)],
        compiler_params=_cparams(("arbitrary",)),
    )(proj, glat, gqn, gkn, *tabs[0], *tabs[1], *tabs[2], dlat, dkr, dqb, dkb, dvb, dqc, dkc, dvc)


ATTN_TK = 512
LOG2E = 1.4426950408889634
C2_H = SCALE_H * LOG2E


def _attn_fwd(q, k, v, *, H, G, dk, dv, scale, name, tq=512, rider=None):
    S = q.shape[0]
    tq = min(tq, S)
    tk = min(ATTN_TK, S)
    c2 = scale * LOG2E

    half = tq // 2

    def body(q_ref, k_ref, v_ref, o_ref, l_ref):
        chunks = [pl.ds(c * tk, tk) for c in range(S // tk)]
        qa, qb = q_ref[0:half, :], q_ref[half:tq, :]
        rowmax = lambda s: functools.reduce(jnp.maximum, [jnp.max(sc, axis=-1, keepdims=True) for sc in s])
        s_a = [_dot(qa, k_ref[rows, :], NT) for rows in chunks]
        m_a = rowmax(s_a)
        e_a, s_b = [], []
        for c, rows in enumerate(chunks):
            e_a.append(jnp.exp2((s_a[c] - m_a) * c2))
            s_b.append(_dot(qb, k_ref[rows, :], NT))
        m_b = rowmax(s_b)
        den_a, den_b = jnp.zeros((half, 1), F32), jnp.zeros((half, 1), F32)
        acc_a, acc_b = jnp.zeros((half, dv), F32), jnp.zeros((half, dv), F32)
        e_b = []
        for c, rows in enumerate(chunks):
            acc_a = acc_a + _dot(e_a[c].astype(CDT), v_ref[rows, :], NN)
            den_a = den_a + jnp.sum(e_a[c], axis=-1, keepdims=True)
            e_b.append(jnp.exp2((s_b[c] - m_b) * c2))
        for c, rows in enumerate(chunks):
            acc_b = acc_b + _dot(e_b[c].astype(CDT), v_ref[rows, :], NN)
            den_b = den_b + jnp.sum(e_b[c], axis=-1, keepdims=True)
        o_ref[0:half, :] = acc_a * (1.0 / den_a)
        o_ref[half:tq, :] = acc_b * (1.0 / den_b)
        l_ref[0:half, :] = jnp.broadcast_to(m_a * scale + jnp.log(den_a), (half, LANES))
        l_ref[half:tq, :] = jnp.broadcast_to(m_b * scale + jnp.log(den_b), (half, LANES))

    return _carried_call(
        body, name=name, grid=(H, S // tq),
        in_specs=[pl.BlockSpec((tq, dk), lambda h, i: (i, h)), pl.BlockSpec((S, dk), lambda h, i: (0, h // G)),
                  pl.BlockSpec((S, dv), lambda h, i: (0, h // G))],
        out_specs=[pl.BlockSpec((tq, dv), lambda h, i: (i, h)), pl.BlockSpec((tq, LANES), lambda h, i: (i, h))],
        out_shape=[_sds((S, H * dv), F32), _sds((S, H * LANES), F32)], scratch_shapes=[],
        dims=("parallel", "parallel"), args=[q, k, v], rider=rider)


def _attn_bwd(q, k, v, do, lse, delta, *, H, G, dk, dv, scale, name, tq=256, rider=None):
    S = q.shape[0]
    tq = min(tq, S)
    Hkv = H // G
    c2 = scale * LOG2E

    def body(q_ref, k_ref, v_ref, do_ref, l_ref, d_ref, dq_ref, dk_ref, dv_ref):
        @pl.when((pl.program_id(1) == 0) & (pl.program_id(2) == 0))
        def _():
            dk_ref[...] = jnp.zeros_like(dk_ref)
            dv_ref[...] = jnp.zeros_like(dv_ref)

        qv, kv, dov = q_ref[...], k_ref[...], do_ref[...]
        p = jnp.exp2(_dot(qv, kv, NT) * c2 - l_ref[:, 0:1] * LOG2E)
        dp = _dot(dov, v_ref[...], NT)
        ds = (p * (dp - d_ref[:, 0:1]) * scale).astype(CDT)
        dq_ref[...] = _dot(ds, kv, NN)
        dk_ref[...] += _dot(ds, qv, TN)
        dv_ref[...] += _dot(p.astype(CDT), dov, TN)

    qi = lambda hk, g, i: (i, hk * G + g)
    return _carried_call(
        body, name=name, grid=(Hkv, G, S // tq),
        in_specs=[pl.BlockSpec((tq, dk), qi), pl.BlockSpec((S, dk), lambda hk, g, i: (0, hk)),
                  pl.BlockSpec((S, dv), lambda hk, g, i: (0, hk)), pl.BlockSpec((tq, dv), qi),
                  pl.BlockSpec((tq, LANES), qi), pl.BlockSpec((tq, LANES), qi)],
        out_specs=[pl.BlockSpec((tq, dk), qi), pl.BlockSpec((S, dk), lambda hk, g, i: (0, hk)),
                   pl.BlockSpec((S, dv), lambda hk, g, i: (0, hk))],
        out_shape=[_sds((S, H * dk), F32), _sds((S, Hkv * dk), F32), _sds((S, Hkv * dv), F32)], scratch_shapes=[],
        dims=("parallel", "arbitrary", "arbitrary"), args=[q, k, v, do, lse, delta], rider=rider)


BAND_SUB = 128
BAND_WIN = 384
BAND_UNROLL = 8


def _band_blocks(S, d):
    L = S // d
    assert L % BAND_SUB == 0 and S % (BAND_SUB * BAND_UNROLL) == 0
    return L, L // BAND_SUB, min(BAND_WIN, L)


def _band_index(blk, d, nb, L, win):
    r, jb = blk // nb, blk % nb
    l0 = jb * BAND_SUB
    w0 = jnp.clip(l0 - BAND_SUB, 0, L - win)
    return r + d * l0, r + d * w0, l0, w0


def _band_rows(start, size, d):
    return pl.ds(pl.multiple_of(start, BAND_SUB), size) if d == 1 else pl.ds(start, size, stride=d)


def _band_mask(l0, w0, win):
    rpos = l0 + lax.broadcasted_iota(jnp.int32, (BAND_SUB, win), 0)
    cpos = w0 + lax.broadcasted_iota(jnp.int32, (BAND_SUB, win), 1)
    return jnp.abs(rpos - cpos) <= BAND_HALF


def _mixc_fwd(q, k, v, rider=None):
    S, W = q.shape

    def body(q_ref, k_ref, v_ref, o_ref, l_ref, *scratch):
        ob_refs, lb_refs = scratch[0:3], scratch[3:6]
        for b, d in enumerate(DILATIONS):
            L, nb, win = _band_blocks(S, d)

            def step(it, carry, b=b, d=d, L=L, nb=nb, win=win):
                idx = [_band_index(it * BAND_UNROLL + u, d, nb, L, win) for u in range(BAND_UNROLL)]
                qrows = [_band_rows(i[0], BAND_SUB, d) for i in idx]
                krows = [_band_rows(i[1], win, d) for i in idx]
                qv = [q_ref[r, :].astype(CDT) for r in qrows]
                kw = [k_ref[r, :].astype(CDT) for r in krows]
                vw = [v_ref[r, :].astype(CDT) for r in krows]
                s = [jnp.where(_band_mask(i[2], i[3], win), _dot(a, kk, NT), NEG_INF) for i, a, kk in zip(idx, qv, kw)]
                m = [jnp.max(x, axis=-1, keepdims=True) for x in s]
                e = [jnp.exp2((x - mm) * C2_H) for x, mm in zip(s, m)]
                den = [jnp.sum(x, axis=-1, keepdims=True) for x in e]
                o = [_dot((x * (1.0 / dd)).astype(CDT), vv, NN) for x, dd, vv in zip(e, den, vw)]
                for r, ou, mm, dd in zip(qrows, o, m, den):
                    ob_refs[b][r, :] = ou
                    lb_refs[b][r, :] = jnp.broadcast_to(mm * SCALE_H + jnp.log(dd), (BAND_SUB, LANES))
                return carry

            lax.fori_loop(0, S // (BAND_SUB * BAND_UNROLL), step, 0)

        def combine(c, carry):
            rows = pl.ds(pl.multiple_of(c * 256, 256), 256)
            l0, l1, l2 = lb_refs[0][rows, :], lb_refs[1][rows, :], lb_refs[2][rows, :]
            m = jnp.maximum(jnp.maximum(l0, l1), l2)
            e0, e1, e2 = jnp.exp(l0 - m), jnp.exp(l1 - m), jnp.exp(l2 - m)
            den = e0 + e1 + e2
            inv = 1.0 / den
            o_ref[rows, :] = ((e0 * inv) * ob_refs[0][rows, :] + (e1 * inv) * ob_refs[1][rows, :]
                              + (e2 * inv) * ob_refs[2][rows, :])
            l_ref[rows, :] = m + jnp.log(den)
            return carry

        lax.fori_loop(0, S // 256, combine, 0)

    head = pl.BlockSpec((S, LANES), lambda h: (0, h))
    return _carried_call(
        body, name="mixc_fwd", grid=(W // LANES,), in_specs=[head] * 3, out_specs=[head] * 2,
        out_shape=[_sds((S, W), F32)] * 2, scratch_shapes=[pltpu.VMEM((S, LANES), F32)] * 6,
        dims=("parallel",), args=[q, k, v], rider=rider)


def _mixc_bwd(q, k, v, do, lse, dd, rider=None):
    S, W = q.shape

    def body(q_ref, k_ref, v_ref, do_ref, l_ref, d_ref, dq_ref, dk_ref, dv_ref):
        dq_ref[...] = jnp.zeros_like(dq_ref)
        dk_ref[...] = jnp.zeros_like(dk_ref)
        dv_ref[...] = jnp.zeros_like(dv_ref)
        for d in DILATIONS:
            L, nb, win = _band_blocks(S, d)

            def step(it, carry, d=d, L=L, nb=nb, win=win):
                idx = [_band_index(it * BAND_UNROLL + u, d, nb, L, win) for u in range(BAND_UNROLL)]
                qrows = [_band_rows(i[0], BAND_SUB, d) for i in idx]
                krows = [_band_rows(i[1], win, d) for i in idx]
                qv = [q_ref[r, :].astype(CDT) for r in qrows]
                dov = [do_ref[r, :].astype(CDT) for r in qrows]
                kw = [k_ref[r, :].astype(CDT) for r in krows]
                vw = [v_ref[r, :].astype(CDT) for r in krows]
                lse2 = [l_ref[r, :][:, 0:1] * LOG2E for r in qrows]
                dd = [d_ref[r, :][:, 0:1] for r in qrows]
                s = [jnp.where(_band_mask(i[2], i[3], win), _dot(a, kk, NT), NEG_INF) for i, a, kk in zip(idx, qv, kw)]
                p = [jnp.exp2(x * C2_H - ll) for x, ll in zip(s, lse2)]
                dp = [_dot(a, vv, NT) for a, vv in zip(dov, vw)]
                ds = [(pp * (x - y) * SCALE_H).astype(CDT) for pp, x, y in zip(p, dp, dd)]
                dq = [_dot(x, kk, NN) for x, kk in zip(ds, kw)]
                dk = [_dot(x, a, TN) for x, a in zip(ds, qv)]
                dv = [_dot(pp.astype(CDT), a, TN) for pp, a in zip(p, dov)]
                for u in range(BAND_UNROLL):
                    dq_ref[qrows[u], :] += dq[u]
                    dk_ref[krows[u], :] += dk[u]
                    dv_ref[krows[u], :] += dv[u]
                return carry

            lax.fori_loop(0, S // (BAND_SUB * BAND_UNROLL), step, 0)

    head = pl.BlockSpec((S, LANES), lambda h: (0, h))
    return _carried_call(
        body, name="mixc_bwd", grid=(W // LANES,), in_specs=[head] * 6, out_specs=[head] * 3,
        out_shape=[_sds((S, W), F32)] * 3, scratch_shapes=[], dims=("parallel",), args=[q, k, v, do, lse, dd],
        rider=rider)


def _outnorm_fwd(oa, ob, oc, g, *, tr=256):
    S = oa.shape[0]
    tr = min(tr, S)

    def body(a_ref, b_ref, c_ref, g_ref, m_ref):
        m_ref[:, 0:512] = (_rms_val(a_ref[...], None, 512)[0] * g_ref[:, 0:512]).astype(CDT)
        m_ref[:, 512:1280] = (_rms_val(b_ref[...], None, 768)[0] * g_ref[:, 512:1280]).astype(CDT)
        m_ref[:, 1280:2048] = (_rms_val(c_ref[...], None, 768)[0] * g_ref[:, 1280:2048]).astype(CDT)

    row = lambda w: pl.BlockSpec((tr, w), lambda i: (i, 0))
    return pl.pallas_call(
        body, name="outnorm_fwd", grid=(S // tr,),
        in_specs=[row(512), row(768), row(768), pl.BlockSpec((1, 2048), lambda i: (0, 0))],
        out_specs=row(2048), out_shape=_sds((S, 2048), CDT), compiler_params=_cparams(("parallel",)),
    )(oa, ob, oc, g)


def _outnorm_bwd(oa, ob, oc, g, dm, *, tr=256):
    S = oa.shape[0]
    tr = min(tr, S)

    def body(a_ref, b_ref, c_ref, g_ref, dm_ref, doa_ref, dob_ref, doc_ref, da_ref, db_ref, dc_ref, dg_ref):
        @pl.when(pl.program_id(0) == 0)
        def _():
            dg_ref[...] = jnp.zeros_like(dg_ref)

        for o_ref, do_ref, d_ref, lo, w in ((a_ref, doa_ref, da_ref, 0, 512), (b_ref, dob_ref, db_ref, 512, 768),
                                            (c_ref, doc_ref, dc_ref, 1280, 768)):
            o = o_ref[...]
            dmv = dm_ref[:, lo:lo + w]
            do, _ = _rms_bwd_val(o, None, dmv * g_ref[:, lo:lo + w], w)
            r = lax.rsqrt(jnp.sum(o * o, axis=-1, keepdims=True) * (1.0 / w) + EPS)
            dg_ref[:, lo:lo + w] += _colsum(dmv * (o * r))
            do_ref[...] = do.astype(do_ref.dtype)
            for h in range(w // LANES):
                sl = slice(LANES * h, LANES * (h + 1))
                d_ref[:, sl] = jnp.broadcast_to(jnp.sum(do[:, sl] * o[:, sl], axis=-1, keepdims=True), (tr, LANES))

    row = lambda w: pl.BlockSpec((tr, w), lambda i: (i, 0))
    vec = pl.BlockSpec((1, 2048), lambda i: (0, 0))
    return pl.pallas_call(
        body, name="outnorm_bwd", grid=(S // tr,),
        in_specs=[row(512), row(768), row(768), vec, row(2048)],
        out_specs=[row(512), row(768), row(768), row(512), row(768), row(768), vec],
        out_shape=[_sds((S, 512), CDT), _sds((S, 768), CDT), _sds((S, 768), F32), _sds((S, 512), F32),
                   _sds((S, 768), F32), _sds((S, 768), F32), _sds((1, 2048), F32)],
        compiler_params=_cparams(("arbitrary",)),
    )(oa, ob, oc, g, dm)


def _row_tile(r, c, itemsize, limit=1 << 20):
    best = 16
    for t in range(16, r + 1, 16):
        if r % t == 0 and t * c * itemsize <= limit:
            best = t
    return best


def _rs_chip_sum(g, got, *, name):
    _, _, r, c = g.shape
    tr = _row_tile(r, c, 2)
    core = lax.axis_index("c").astype(jnp.int32).reshape(1)

    def body(c_ref, a_ref, b_ref, o_ref):
        o_ref[...] = (a_ref[...].astype(F32) + b_ref[...].astype(F32)).astype(o_ref.dtype)

    spec = pltpu.PrefetchScalarGridSpec(
        num_scalar_prefetch=1, grid=(4, r // tr),
        in_specs=[pl.BlockSpec((None, None, tr, c), lambda k, i, cr: (k, cr[0], i, 0)),
                  pl.BlockSpec((None, tr, c), lambda k, i, cr: (k, i, 0))],
        out_specs=pl.BlockSpec((None, tr, c), lambda k, i, cr: (k, i, 0)))
    return pl.pallas_call(body, name=name, grid_spec=spec, out_shape=_sds((4, r, c), g.dtype),
                          compiler_params=_cparams(("parallel", "parallel")))(core, g, got)


def _rs_final_sum(landed, *, name):
    depth = len(landed)
    _, r, c = landed[0].shape
    tr = _row_tile(r, c, 4)

    def body(*refs):
        o_ref = refs[depth]
        for k in range(depth):
            @pl.when(pl.program_id(0) == k)
            def _(r_ref=refs[k]):
                o_ref[...] = ((r_ref[0].astype(F32) + r_ref[1].astype(F32)) + r_ref[2].astype(F32)
                              ) + r_ref[3].astype(F32)

    in_specs = [pl.BlockSpec((4, tr, c), lambda l, i, k=k: (0, jnp.where(l == k, i, 0), 0)) for k in range(depth)]
    return pl.pallas_call(
        body, name=name, grid=(depth, r // tr), in_specs=in_specs,
        out_specs=pl.BlockSpec((None, tr, c), lambda l, i: (l, i, 0)), out_shape=_sds((depth, r, c), F32),
        compiler_params=_cparams(("arbitrary", "arbitrary")))(*landed)


def _all_reduce_small(v):
    R = v.shape[0]

    def body(v_ref, out_ref, buf_ref, send_sems, recv_sems):
        x, y, c = lax.axis_index("x"), lax.axis_index("y"), lax.axis_index("c")
        me = 4 * x + 2 * y + c
        buf_ref[me] = v_ref[...]
        peers = []
        for r in range(1, 8):
            px, py, pc = x ^ (r >> 2), y ^ ((r >> 1) & 1), c ^ (r & 1)
            peers.append((r, (px, py, pc), 4 * px + 2 * py + pc))
        sends = [pltpu.make_async_remote_copy(
            src_ref=v_ref, dst_ref=buf_ref.at[me], send_sem=send_sems.at[r - 1], recv_sem=recv_sems.at[r - 1],
            device_id=dev, device_id_type=MESH) for r, dev, _ in peers]
        for cp in sends:
            cp.start()
        for r, dev, idx in peers:
            pltpu.make_async_remote_copy(
                src_ref=v_ref, dst_ref=buf_ref.at[idx], send_sem=send_sems.at[r - 1], recv_sem=recv_sems.at[r - 1],
                device_id=dev, device_id_type=MESH).wait_recv()
        for cp in sends:
            cp.wait_send()
        acc = buf_ref[0]
        for k in range(1, 8):
            acc = acc + buf_ref[k]
        out_ref[...] = acc

    vm = pl.BlockSpec(memory_space=pltpu.VMEM)
    return pl.pallas_call(
        body, name="all_reduce_small", out_shape=_sds((R, LANES), F32), in_specs=[vm], out_specs=vm,
        scratch_shapes=[pltpu.VMEM((8, R, LANES), F32), pltpu.SemaphoreType.DMA((7,)), pltpu.SemaphoreType.DMA((7,))],
    )(v)


def _adamw(w, g, m, v, *, name):
    R, C = w.shape
    tr = R
    for cand in (1024, 512, 256, 128, 64, 32, 16, 8):
        if R % cand == 0 and cand * C * 4 <= 2 * 1024 * 1024:
            tr = cand
            break

    def body(w_ref, g_ref, m_ref, v_ref, d_ref, nm_ref, nv_ref):
        gv = g_ref[...]
        mn = ADAM_B1 * m_ref[...] + (1.0 - ADAM_B1) * gv
        vn = ADAM_B2 * v_ref[...] + (1.0 - ADAM_B2) * (gv * gv)
        m_hat = mn / (1.0 - ADAM_B1 ** ADAM_STEP)
        v_hat = vn / (1.0 - ADAM_B2 ** ADAM_STEP)
        d_ref[...] = -ADAM_LR * (m_hat / (jnp.sqrt(v_hat) + ADAM_EPS) + ADAM_WD * w_ref[...])
        nm_ref[...] = mn
        nv_ref[...] = vn

    blk = pl.BlockSpec((tr, C), lambda i: (i, 0))
    return pl.pallas_call(
        body, name=name, grid=(R // tr,), in_specs=[blk] * 4, out_specs=[blk] * 3,
        out_shape=[_sds((R, C), F32)] * 3, compiler_params=_cparams(("parallel",)))(w, g, m, v)


def _wuq_pad(w):
    w = w.reshape(448, 4, 192)
    z = jnp.zeros((448, 4, 64), w.dtype)
    return jnp.concatenate([w[:, :, 0:128], z, w[:, :, 128:192]], axis=2).reshape(448, 1024)


def _wuq_unpad(w):
    w = w.reshape(448, 4, 256)
    return jnp.concatenate([w[:, :, 0:128], w[:, :, 192:256]], axis=2).reshape(448, 768)


def _wukv_perm(w):
    return w.reshape(512, 4, 2, 128).transpose(0, 2, 1, 3).reshape(512, 1024)


def _wukv_unperm(w):
    return w.reshape(512, 2, 4, 128).transpose(0, 2, 1, 3).reshape(512, 1024)


def _lat_weight(w_uq, w_ukv):
    z = lambda r, c: jnp.zeros((r, c), w_uq.dtype)
    top = jnp.concatenate([_wuq_pad(w_uq), z(448, 1024)], axis=1)
    mid = jnp.concatenate([z(512, 1024), _wukv_perm(w_ukv)], axis=1)
    return jnp.concatenate([top, mid, z(64, 2048)], axis=0)


def _lat_weight_grads(dw):
    return _wuq_unpad(dw[0:KV_LO, 0:1024]), _wukv_unperm(dw[KV_LO:KV_HI, 1024:2048])


def _comm_shards(w_in, w_uq, w_ukv, w_out, w_ff1, w_ff2):
    lat = jnp.concatenate([w_uq.reshape(UQ_ROWS, LANES), w_ukv.reshape(512, LANES)], axis=0)
    return [w_in.T, w_ff1.T, w_out, w_ff2, lat]


def _from_comm_shards(parts):
    w_in_t, w_ff1_t, w_out, w_ff2, lat = parts
    lead = lat.shape[:-2]
    return {"w_in": jnp.swapaxes(w_in_t, -1, -2), "w_ff1": jnp.swapaxes(w_ff1_t, -1, -2), "w_out": w_out,
            "w_ff2": w_ff2, "w_uq": lat[..., 0:UQ_ROWS, :].reshape(lead + (448, 96)),
            "w_ukv": lat[..., UQ_ROWS:, :].reshape(lead + (512, 128))}


def _early_weights(g_in_t, g_lat):
    w_uq = g_lat[:, 0:UQ_ROWS].reshape(8, 448, 96).transpose(1, 0, 2).reshape(448, 768)
    w_ukv = g_lat[:, UQ_ROWS:].reshape(8, 512, 128).transpose(1, 0, 2).reshape(512, 1024)
    return g_in_t.reshape(PROJ_W, D_MODEL), _lat_weight(w_uq, w_ukv)


def _layer_fwd(x, W, G, tabs, plan=None):
    W = dict(W)
    sh, nxt, ff1_half = plan if plan is not None else (None, None, None)
    first = plan is not None and ff1_half is None
    early_next = None if nxt is None else [nxt["w_in_t"], nxt["lat"]]
    ag1 = lambda arrays, tag, **kw: None if (plan is None or not arrays) else _ag_first_rider(arrays, tag, **kw)
    ag2 = lambda arrays, tag: None if (plan is None or not arrays) else _ag_second_rider(arrays, tag)
    s = {"x0": x}
    s["h1"] = _rms_fwd(x, G["ln1_g"], name="rms1_fwd")
    mm_in = functools.partial(_matmul, s["h1"], W["w_in_t"], mode="nt", tm=1024, tn=768, tk=2048, out_dtype=CDT,
                              name="mm_in")
    if first:
        s["proj"], ff1_rows0 = mm_in(rider=ag1([sh["w_ff1_t"]], "_ff1a", part=(0, 2)))
    else:
        s["proj"] = mm_in()
    s["lat"], kpe, s["qb"], s["kb"], s["vb"], s["qc"], s["kc"], s["vc"] = _prep_fwd(
        s["proj"], G["glat"], G["gqn"], G["gkn"], tabs)
    qkva = _matmul(s["lat"], W["w_lat"], mode="nn", tm=1024, tn=1024, tk=1024, out_dtype=F32, name="mm_lat")
    s["qa"], s["ka"], s["va"] = _prep_a2_fwd(qkva, kpe, tabs[0])
    early, ff1_half_next = None, None
    attn_a = functools.partial(_attn_fwd, s["qa"], s["ka"], s["va"], H=4, G=1, dk=256, dv=128, scale=SCALE_A,
                               name="attn_a_fwd")
    attn_b = functools.partial(_attn_fwd, s["qb"], s["kb"], s["vb"], H=6, G=3, dk=128, dv=128, scale=SCALE_H,
                               name="attn_b_fwd")
    mixc = functools.partial(_mixc_fwd, s["qc"], s["kc"], s["vc"])
    if plan is None:
        (s["oa"], s["lse_a"]), _ = attn_a()
        (s["ob"], s["lse_b"]), _ = attn_b()
        (s["oc"], s["lse_c"]), _ = mixc()
    else:
        if first:
            (s["oa"], s["lse_a"]), got = attn_a(rider=ag1([sh["w_ff1_t"]], "_ff1b", part=(1, 2), into=ff1_rows0))
            ff1_half, early_half = got[0], []
        else:
            (s["oa"], s["lse_a"]), early_half = attn_a(rider=ag1(early_next, "_early"))
        (s["ob"], s["lse_b"]), got_b = attn_b(
            rider=_join(ag1([sh["w_out"]], "_out"), ag1([sh["w_ff2"]], "_ff2a", part=(0, 2))))
        (s["oc"], s["lse_c"]), got_c = mixc(
            rider=_join(_join(ag2(got_b[0:1], "_out"), ag1([sh["w_ff2"]], "_ff2b", part=(1, 2), into=got_b[1:2])),
                        ag2(early_half, "_early")))
        W["w_out"] = got_c[0].reshape(D_MODEL, D_MODEL)
        ff2_half, early = got_c[1], (got_c[2:4] or None)
    s["mixed"] = _outnorm_fwd(s["oa"], s["ob"], s["oc"], G["g_out"])
    mm_out = functools.partial(_matmul, s["mixed"], W["w_out"], mode="nn", tm=1024, tn=1024, tk=2048, out_dtype=F32,
                               name="mm_out", epi="residual", extra=x)
    if plan is None:
        s["x1"] = mm_out()
    else:
        s["x1"], got = mm_out(rider=ag2([ff1_half, ff2_half], "_ff"))
        W["w_ff1_t"], W["w_ff2"] = got[0].reshape(D_FF, D_MODEL), got[1].reshape(D_FF, D_MODEL)
    s["h2"] = _rms_fwd(s["x1"], G["ln2_g"], name="rms2_fwd")
    ff1 = functools.partial(_matmul, s["h2"], W["w_ff1_t"], mode="nt", tm=1024, tn=1024, tk=2048, out_dtype=CDT,
                            name="mm_ff1", epi="relu2")
    if nxt is None:
        s["z"], s["u"] = ff1()
        x2 = _matmul(s["u"], W["w_ff2"], mode="nn", tm=1024, tn=1024, tk=2048, out_dtype=F32, name="mm_ff2",
                     epi="residual", extra=s["x1"])
    else:
        (s["z"], s["u"]), got = ff1(rider=_join(ag1(early_next, "_early") if first else None,
                                                ag1([nxt["w_ff1_t"]], "_ff1a", part=(0, 2))))
        early_half, rows0 = (got[0:2], got[2:3]) if first else ([], got[0:1])
        x2, got = _matmul(s["u"], W["w_ff2"], mode="nn", tm=1024, tn=1024, tk=2048, out_dtype=F32, name="mm_ff2",
                          epi="residual", extra=s["x1"],
                          rider=_join(ag2(early_half, "_early"),
                                      ag1([nxt["w_ff1_t"]], "_ff1b", part=(1, 2), into=rows0)))
        if first:
            early, got = got[0:2], got[2:3]
        ff1_half_next = got[0]
    return x2, s, W, early, ff1_half_next


def _by_destination(dw, name):
    return dw.reshape((4, 2) + COMM_SHAPE[name])


def _early_by_destination(dw_in_t, dw_lat):
    dw_uq, dw_ukv = _lat_weight_grads(dw_lat)
    lat = jnp.concatenate([dw_uq.reshape(448, 8, 96).transpose(1, 0, 2).reshape(8, UQ_ROWS, LANES),
                           dw_ukv.reshape(512, 8, 128).transpose(1, 0, 2)], axis=1)
    return [_by_destination(dw_in_t, "w_in_t"), _by_destination(lat, "lat")]


def _layer_bwd(dx2, dx2b, s, W, G, tabs, scatter=False, pending=None):
    dw, dg, landed = {}, {}, {}
    ff2_dx = functools.partial(_matmul, dx2b, W["w_ff2"], mode="nt", tm=1024, tn=1024, tk=2048, out_dtype=CDT,
                               name="mm_ff2_dx", epi="drelu2", extra=s["z"])
    if pending is None:
        dz = ff2_dx()
    else:
        dz, got = ff2_dx(rider=_rs_sibling_rider(pending, "_early"))
        chip = [_rs_chip_sum(g, r, name="rs_chip_sum_" + n) for g, r, n in zip(pending, got, ("w_in_t", "lat"))]
    dw["w_ff2"] = _matmul(s["u"], dx2b, mode="tn", tm=2048, tn=1024, tk=2048, out_dtype=CDT, name="mm_ff2_dw")
    ff1_dx = functools.partial(_matmul, dz, W["w_ff1_t"], mode="nn", tm=1024, tn=1024, tk=2048, out_dtype=F32,
                               name="mm_ff1_dx")
    if pending is None:
        dh2 = ff1_dx()
    else:
        dh2, got = ff1_dx(rider=_rs_chip_rider(chip, "_early"))
        landed["above_w_in_t"], landed["above_lat"] = got
    dw["w_ff1_t"] = _matmul(dz, s["h2"], mode="tn", tm=2048, tn=1024, tk=2048, out_dtype=CDT, name="mm_ff1_dw")
    dx1, dx1b, dg["ln2_g"] = _rms_bwd(s["x1"], G["ln2_g"], dh2, dx2, name="rms2_bwd")
    dmixed = _matmul(dx1b, W["w_out"], mode="nt", tm=1024, tn=1024, tk=2048, out_dtype=F32, name="mm_out_dx")
    out_dw = functools.partial(_matmul, s["mixed"], dx1b, mode="tn", tm=1024, tn=1024, tk=2048, out_dtype=CDT,
                               name="mm_out_dw")
    if not scatter:
        dw["w_out"] = out_dw()
        riders = [None, None, None]
    else:
        g_ff = [_by_destination(dw["w_ff2"], "w_ff2"), _by_destination(dw["w_ff1_t"], "w_ff1_t")]
        dw["w_out"], got = out_dw(rider=_rs_sibling_rider(g_ff, "_ff"))
        chip_ff2 = _rs_chip_sum(g_ff[0], got[0], name="rs_chip_sum_w_ff2")
        chip_ff1 = _rs_chip_sum(g_ff[1], got[1], name="rs_chip_sum_w_ff1_t")
        g_out = [_by_destination(dw["w_out"], "w_out")]
        riders = [_rs_chip_rider([chip_ff2], "_ff2"),
                  _join(_rs_chip_rider([chip_ff1], "_ff1"), _rs_sibling_rider(g_out, "_out")), None]
    doa, dob, doc, dla, dlb, dlc, dg["g_out"] = _outnorm_bwd(s["oa"], s["ob"], s["oc"], G["g_out"], dmixed)
    (dqa, dka, dva), got = _attn_bwd(s["qa"], s["ka"], s["va"], doa, s["lse_a"], dla, H=4, G=1, dk=256, dv=128,
                                     scale=SCALE_A, name="attn_a_bwd", rider=riders[0])
    if scatter:
        landed["w_ff2"] = got[0]
    (dqb, dkb, dvb), got = _attn_bwd(s["qb"], s["kb"], s["vb"], dob, s["lse_b"], dlb, H=6, G=3, dk=128, dv=128,
                                     scale=SCALE_H, name="attn_b_bwd", rider=riders[1])
    if scatter:
        landed["w_ff1_t"] = got[0]
        riders[2] = _rs_chip_rider([_rs_chip_sum(g_out[0], got[1], name="rs_chip_sum_w_out")], "_out")
    (dqc, dkc, dvc), got = _mixc_bwd(s["qc"], s["kc"], s["vc"], doc, s["lse_c"], dlc, rider=riders[2])
    if scatter:
        landed["w_out"] = got[0]
    dqkva, dkr = _prep_a2_bwd(dqa, dka, dva, tabs[0])
    dlat = _matmul(dqkva, W["w_lat"], mode="nt", tm=1024, tn=1024, tk=2048, out_dtype=F32, name="mm_lat_dx")
    dw["w_lat"] = _matmul(s["lat"], dqkva, mode="tn", tm=1024, tn=2048, tk=2048, out_dtype=CDT, name="mm_lat_dw")
    dproj, dg["glat"], dg["gqn"], dg["gkn"] = _prep_bwd(
        s["proj"], G["glat"], G["gqn"], G["gkn"], tabs, dlat, dkr, dqb, dkb, dvb, dqc, dkc, dvc)
    dh1 = _matmul(dproj, W["w_in_t"], mode="nn", tm=1024, tn=1024, tk=2304, out_dtype=F32, name="mm_in_dx")
    dw["w_in_t"] = _matmul(dproj, s["h1"], mode="tn", tm=1536, tn=1024, tk=2048, out_dtype=CDT, name="mm_in_dw")
    dx0, dx0b, dg["ln1_g"] = _rms_bwd(s["x0"], G["ln1_g"], dh1, dx1, name="rms1_bwd")
    return dx0, dx0b, dw, dg, landed


def _layer_gains(l, ln1_g, g_q_a, g_kv_a, g_qn_b, g_kn_b, g_out, ln2_g):
    return {"ln1_g": ln1_g[l], "ln2_g": ln2_g[l], "g_out": g_out[l].reshape(1, 2048),
            "glat": jnp.concatenate([g_q_a[l], g_kv_a[l], jnp.zeros((LAT_W - KV_HI,), F32)]).reshape(1, LAT_W),
            "gqn": g_qn_b[l].reshape(1, 128), "gkn": g_kn_b[l].reshape(1, 128)}


def _gain_grads(dg):
    glat = dg["glat"].reshape(-1)
    return {"ln1_g": dg["ln1_g"].reshape(-1), "g_q_a": glat[0:KV_LO], "g_kv_a": glat[KV_LO:KV_HI],
            "g_qn_b": dg["gqn"].reshape(-1), "g_kn_b": dg["gkn"].reshape(-1), "g_out": dg["g_out"].reshape(-1),
            "ln2_g": dg["ln2_g"].reshape(-1)}


def _local_step(x, tgt, weights, gains, ln_f_g):
    S = x.shape[0]
    tabs = _rope_tables(S)
    depth = len(weights)
    saved = []
    for l in range(depth):
        x, s, _, _, _ = _layer_fwd(x, weights[l], gains[l], tabs)
        saved.append(s)
    loss, dx, dxb, dlnf = _loss_head(x, ln_f_g, tgt)
    dws, dgs = [None] * depth, [None] * depth
    for l in reversed(range(depth)):
        dx, dxb, dws[l], dgs[l], _ = _layer_bwd(dx, dxb, saved[l], weights[l], gains[l], tabs)
    return loss, dx, dws, dgs, dlnf


SMALL_SIZES = (("ln1_g", 2048), ("g_q_a", 448), ("g_kv_a", 512), ("g_qn_b", 128), ("g_kn_b", 128), ("g_out", 2048),
               ("ln2_g", 2048))


def _pack_small(per_layer, ln_f):
    flat = jnp.concatenate([per_layer[n].reshape(-1) for n, _ in SMALL_SIZES] + [ln_f.reshape(-1)])
    rows = -(-flat.shape[0] // (8 * LANES)) * 8
    return jnp.concatenate([flat, jnp.zeros((rows * LANES - flat.shape[0],), F32)]).reshape(rows, LANES)


def _unpack_small(packed, depth):
    flat, out, lo = packed.reshape(-1), {}, 0
    for n, w in SMALL_SIZES:
        out[n] = flat[lo:lo + depth * w].reshape(depth, w)
        lo += depth * w
    out["ln_f_g"] = flat[lo:lo + 2048]
    return out


def kernel(x, ln1_g, w_in, g_q_a, w_uq, g_kv_a, w_ukv, g_qn_b, g_kn_b, g_out, w_out, ln2_g, w_ff1, w_ff2, ln_f_g, loss_target, m_ln1_g, m_w_in, m_g_q_a, m_w_uq, m_g_kv_a, m_w_ukv, m_g_qn_b, m_g_kn_b, m_g_out, m_w_out, m_ln2_g, m_w_ff1, m_w_ff2, m_ln_f_g, v_ln1_g, v_w_in, v_g_q_a, v_w_uq, v_g_kv_a, v_w_ukv, v_g_qn_b, v_g_kn_b, v_g_out, v_w_out, v_ln2_g, v_w_ff1, v_w_ff2, v_ln_f_g):
    depth = w_in.shape[0]
    S = x.shape[1]
    big_w = {"w_in": w_in, "w_uq": w_uq, "w_ukv": w_ukv, "w_out": w_out, "w_ff1": w_ff1, "w_ff2": w_ff2}
    big_m = {"w_in": m_w_in, "w_uq": m_w_uq, "w_ukv": m_w_ukv, "w_out": m_w_out, "w_ff1": m_w_ff1, "w_ff2": m_w_ff2}
    big_v = {"w_in": v_w_in, "w_uq": v_w_uq, "w_ukv": v_w_ukv, "w_out": v_w_out, "w_ff1": v_w_ff1, "w_ff2": v_w_ff2}
    small_w = {"ln1_g": ln1_g, "g_q_a": g_q_a, "g_kv_a": g_kv_a, "g_qn_b": g_qn_b, "g_kn_b": g_kn_b, "g_out": g_out,
               "ln2_g": ln2_g}
    small_m = {"ln1_g": m_ln1_g, "g_q_a": m_g_q_a, "g_kv_a": m_g_kv_a, "g_qn_b": m_g_qn_b, "g_kn_b": m_g_kn_b,
               "g_out": m_g_out, "ln2_g": m_ln2_g}
    small_v = {"ln1_g": v_ln1_g, "g_q_a": v_g_q_a, "g_kv_a": v_g_kv_a, "g_qn_b": v_g_qn_b, "g_kn_b": v_g_kn_b,
               "g_out": v_g_out, "ln2_g": v_ln2_g}

    shards = [dict(zip(COMM, _comm_shards(*[big_w[n][l].astype(CDT) for n in BIG]))) for l in range(depth)]
    early_shards = [[sh["w_in_t"], sh["lat"]] for sh in shards]
    early = _run_rider(_ag_second_rider(_run_rider(_ag_first_rider(early_shards[0], "_early")), "_early"))
    gains = [_layer_gains(l, ln1_g, g_q_a, g_kv_a, g_qn_b, g_kn_b, g_out, ln2_g) for l in range(depth)]
    tabs = _rope_tables(S)

    h = x.reshape(S, D_MODEL)
    saved, weights, f_half = [], [], None
    for l in range(depth):
        W = dict(zip(("w_in_t", "w_lat"), _early_weights(*early)))
        h, s, W, early, f_half = _layer_fwd(h, W, gains[l], tabs,
                                            plan=(shards[l], shards[l + 1] if l + 1 < depth else None, f_half))
        saved.append(s)
        weights.append(W)
    loss_part, dx, dxb, dlnf = _loss_head(h, ln_f_g, loss_target.reshape(S, D_MODEL))
    loss = lax.psum(loss_part[0, 0], ("x", "y", "c"))

    dgs, landed, pending = [None] * depth, [None] * depth, None
    for l in reversed(range(depth)):
        dx, dxb, dw, dgs[l], landed[l] = _layer_bwd(dx, dxb, saved[l], weights[l], gains[l], tabs, scatter=True,
                                                    pending=pending)
        if pending is not None:
            landed[l + 1]["w_in_t"], landed[l + 1]["lat"] = landed[l].pop("above_w_in_t"), landed[l].pop("above_lat")
        pending = _early_by_destination(dw["w_in_t"], dw["w_lat"])
    got = _run_rider(_rs_sibling_rider(pending, "_early"))
    chip = [_rs_chip_sum(g, r, name="rs_chip_sum_" + n) for g, r, n in zip(pending, got, ("w_in_t", "lat"))]
    landed[0]["w_in_t"], landed[0]["lat"] = _run_rider(_rs_chip_rider(chip, "_early"))
    grad_x = dx.reshape(1, S, D_MODEL)

    big_g = _from_comm_shards([_rs_final_sum([landed[l][n] for l in range(depth)], name="rs_final_sum_" + n)
                               for n in COMM])

    named = [_gain_grads(dgs[l]) for l in range(depth)]
    per_layer = {n: jnp.stack([named[l][n] for l in range(depth)]) for n, _ in SMALL_SIZES}
    small_g = _unpack_small(_all_reduce_small(_pack_small(per_layer, dlnf.reshape(-1))), depth)

    upd = {}
    for n in BIG:
        shp = big_w[n].shape
        two_d = (shp[0] * shp[1], shp[2])
        d, nm, nv = _adamw(big_w[n].reshape(two_d), big_g[n].reshape(two_d), big_m[n].reshape(two_d),
                           big_v[n].reshape(two_d), name="adamw_" + n)
        upd[n] = (d.reshape(shp), nm.reshape(shp), nv.reshape(shp))
    small_w["ln_f_g"], small_m["ln_f_g"], small_v["ln_f_g"] = ln_f_g, m_ln_f_g, v_ln_f_g
    names_small = [n for n, _ in SMALL_SIZES]
    pw = _pack_small({n: small_w[n] for n in names_small}, small_w["ln_f_g"])
    pg = _pack_small({n: small_g[n] for n in names_small}, small_g["ln_f_g"])
    pm = _pack_small({n: small_m[n] for n in names_small}, small_m["ln_f_g"])
    pv = _pack_small({n: small_v[n] for n in names_small}, small_v["ln_f_g"])
    d, nm, nv = _adamw(pw, pg, pm, pv, name="adamw_small")
    sd, snm, snv = _unpack_small(d, depth), _unpack_small(nm, depth), _unpack_small(nv, depth)
    for n in names_small + ["ln_f_g"]:
        upd[n] = (sd[n], snm[n], snv[n])

    order = ["ln1_g", "w_in", "g_q_a", "w_uq", "g_kv_a", "w_ukv", "g_qn_b", "g_kn_b", "g_out", "w_out", "ln2_g", "w_ff1",
             "w_ff2", "ln_f_g"]
    grads = {**big_g, **small_g}
    return (loss, grad_x, *[grads[n] for n in order], *[upd[n][0] for n in order], *[upd[n][1] for n in order],
            *[upd[n][2] for n in order])
```

```python
import functools
import math

import jax
import jax.numpy as jnp
from jax import lax
from jax.experimental import pallas as pl
from jax.experimental.pallas import tpu as pltpu

D_MODEL = 2048
D_FF = 8192
EPS = 1e-6
NEG_INF = -1e30
Q_LORA = 448
ROPE_THETA = 10000.0
GRID_W = 64
DILATIONS = (1, 4, 16)
BAND_HALF = 64
SCALE_A = 1.0 / math.sqrt(192.0)
SCALE_H = 1.0 / math.sqrt(128.0)
ADAM_LR, ADAM_B1, ADAM_B2, ADAM_EPS, ADAM_WD, ADAM_STEP = 0.001, 0.9, 0.999, 1e-08, 0.01, 10

CDT = jnp.bfloat16
F32 = jnp.float32
LANES = 128
VMEM_LIMIT = 56 * 1024 * 1024

PROJ_W = 4608
LAT_W = 1024
KV_LO, KV_HI = 448, 960
OFF_BQ, OFF_BK, OFF_BV, OFF_CQ, OFF_CK, OFF_CV = 1024, 1792, 2048, 2304, 3072, 3840

NN = ((1,), (0,))
NT = ((1,), (1,))
TN = ((0,), (0,))

BIG = ("w_in", "w_uq", "w_ukv", "w_out", "w_ff1", "w_ff2")
COMM = ("w_in_t", "w_ff1_t", "w_out", "w_ff2", "lat")
COMM_SHAPE = {"w_in_t": (576, 2048), "w_ff1_t": (1024, 2048), "w_out": (256, 2048), "w_ff2": (1024, 2048),
              "lat": (848, 128)}
UQ_ROWS = 448 * 96 // LANES


def _dot(a, b, dims):
    return lax.dot_general(a, b, (dims, ((), ())), preferred_element_type=F32)


def _cparams(dims=None):
    return pltpu.CompilerParams(dimension_semantics=dims, vmem_limit_bytes=VMEM_LIMIT)


def _sds(shape, dtype):
    return jax.ShapeDtypeStruct(shape, dtype)


MESH = pl.DeviceIdType.MESH
ANY = pl.BlockSpec(memory_space=pl.ANY)


class _Rider:
    def __init__(self, name, arrays, out_shape, scratch, aliases, start, finish):
        self.name, self.arrays, self.out_shape, self.scratch = name, list(arrays), list(out_shape), list(scratch)
        self.aliases, self.start, self.finish = dict(aliases), start, finish


def _join(a, b):
    if a is None or b is None:
        return a if b is None else b
    na, oa, sa = len(a.arrays), len(a.out_shape), len(a.scratch)
    aliases = dict(a.aliases)
    aliases.update({na + i: oa + o for i, o in b.aliases.items()})

    def start(ins, outs, sems):
        a.start(ins[:na], outs[:oa], sems[:sa])
        b.start(ins[na:], outs[oa:], sems[sa:])

    def finish(ins, outs, sems):
        a.finish(ins[:na], outs[:oa], sems[:sa])
        b.finish(ins[na:], outs[oa:], sems[sa:])

    return _Rider(a.name + "_" + b.name, a.arrays + b.arrays, a.out_shape + b.out_shape, a.scratch + b.scratch,
                  aliases, start, finish)


def _carried_call(body, *, name, grid, in_specs, out_specs, out_shape, scratch_shapes, dims, args, rider):
    in_specs, out_specs, out_shape = list(in_specs), list(out_specs), list(out_shape)
    scratch_shapes = list(scratch_shapes)
    if rider is None:
        res = pl.pallas_call(body, name=name, grid=grid, in_specs=in_specs, out_specs=out_specs, out_shape=out_shape,
                             scratch_shapes=scratch_shapes, compiler_params=_cparams(dims))(*args)
        return list(res), []
    n_in, n_out, n_scr = len(in_specs), len(out_specs), len(scratch_shapes)
    r_in, r_out = len(rider.arrays), len(rider.out_shape)

    def wrapped(*refs):
        o0 = n_in + r_in
        s0 = o0 + n_out + r_out
        ins, outs, sems = refs[n_in:o0], refs[o0 + n_out:s0], refs[s0 + n_scr:]
        ids = [pl.program_id(a) for a in range(len(grid))]
        first = functools.reduce(jnp.logical_and, [i == 0 for i in ids])
        last = functools.reduce(jnp.logical_and, [i == g - 1 for i, g in zip(ids, grid)])

        @pl.when(first)
        def _():
            rider.start(ins, outs, sems)

        body(*refs[:n_in], *refs[o0:o0 + n_out], *refs[s0:s0 + n_scr])

        @pl.when(last)
        def _():
            rider.finish(ins, outs, sems)

    res = pl.pallas_call(
        wrapped, name=name + "_" + rider.name, grid=grid, in_specs=in_specs + [ANY] * r_in,
        out_specs=out_specs + [ANY] * r_out, out_shape=out_shape + rider.out_shape,
        scratch_shapes=scratch_shapes + rider.scratch,
        input_output_aliases={n_in + i: n_out + o for i, o in rider.aliases.items()},
        compiler_params=_cparams(("arbitrary",) * len(grid)),
    )(*args, *rider.arrays)
    return list(res[:n_out]), list(res[n_out:])


def _run_rider(rider):
    def body(*refs):
        r_in, r_out = len(rider.arrays), len(rider.out_shape)
        ins, outs, sems = refs[:r_in], refs[r_in:r_in + r_out], refs[r_in + r_out:]
        rider.start(ins, outs, sems)
        rider.finish(ins, outs, sems)

    res = pl.pallas_call(
        body, name=rider.name, in_specs=[ANY] * len(rider.arrays), out_specs=[ANY] * len(rider.out_shape),
        out_shape=rider.out_shape, scratch_shapes=rider.scratch, input_output_aliases=rider.aliases,
    )(*rider.arrays)
    return list(res)


def _mesh_place():
    x, y, c = lax.axis_index("x"), lax.axis_index("y"), lax.axis_index("c")
    return x, y, c, [(1 - x, y), (x, 1 - y), (1 - x, 1 - y)]


def _remote(src, dst, send, recv, dev):
    return pltpu.make_async_remote_copy(src_ref=src, dst_ref=dst, send_sem=send, recv_sem=recv, device_id=dev,
                                        device_id_type=MESH)


def _ag_first_rider(shards, tag, part=(0, 1), into=None):
    n = len(shards)

    def copies(ins, outs, sems):
        send, recv, _ = sems
        x, y, c, chips = _mesh_place()
        me = 4 * x + 2 * y + c
        peers = [(x, y, 1 - c)] + [(cx, cy, c) for cx, cy in chips]
        out, mine = [], []
        for t in range(n):
            size = shards[t].shape[0] // part[1]
            rows = pl.ds(part[0] * size, size)
            for k, dev in enumerate(peers):
                theirs = 4 * dev[0] + 2 * dev[1] + dev[2]
                out.append((_remote(ins[t].at[rows], outs[t].at[me, rows], send.at[t, k], recv.at[t, k], dev),
                            _remote(ins[t].at[rows], outs[t].at[theirs, rows], send.at[t, k], recv.at[t, k], dev)))
            mine.append(pltpu.make_async_copy(ins[t].at[rows], outs[t].at[me, rows], sems[2].at[t]))
        return out, mine

    def start(ins, outs, sems):
        pairs, mine = copies(ins, outs, sems)
        for cp in mine:
            cp.start()
        for snd, _ in pairs:
            snd.start()

    def finish(ins, outs, sems):
        pairs, mine = copies(ins, outs, sems)
        for _, rcv in pairs:
            rcv.wait_recv()
        for snd, _ in pairs:
            snd.wait_send()
        for cp in mine:
            cp.wait()

    return _Rider("ag1" + tag, list(shards) + list(into or []), [_sds((8,) + s.shape, s.dtype) for s in shards],
                  [pltpu.SemaphoreType.DMA((n, 4)), pltpu.SemaphoreType.DMA((n, 4)), pltpu.SemaphoreType.DMA((n,))],
                  {} if into is None else {n + t: t for t in range(n)}, start, finish)


def _ag_second_rider(gathered, tag):
    n = len(gathered)

    def copies(ins, outs, sems):
        send, recv = sems
        x, y, c, chips = _mesh_place()
        out = []
        for t in range(n):
            for j, (cx, cy) in enumerate(chips):
                here, there = 4 * cx + 2 * cy + c, 4 * cx + 2 * cy + (1 - c)
                out.append((_remote(ins[t].at[here], outs[t].at[here], send.at[t, j], recv.at[t, j], (x, y, 1 - c)),
                            _remote(ins[t].at[here], outs[t].at[there], send.at[t, j], recv.at[t, j], (x, y, 1 - c))))
        return out

    def start(ins, outs, sems):
        for snd, _ in copies(ins, outs, sems):
            snd.start()

    def finish(ins, outs, sems):
        pairs = copies(ins, outs, sems)
        for _, rcv in pairs:
            rcv.wait_recv()
        for snd, _ in pairs:
            snd.wait_send()

    return _Rider("ag2" + tag, gathered, [_sds(g.shape, g.dtype) for g in gathered],
                  [pltpu.SemaphoreType.DMA((n, 3)), pltpu.SemaphoreType.DMA((n, 3))],
                  {t: t for t in range(n)}, start, finish)


def _rs_sibling_rider(gs, tag):
    n = len(gs)

    def copies(ins, outs, sems):
        send, recv = sems
        x, y, c, _ = _mesh_place()
        return [_remote(ins[t].at[k, 1 - c], outs[t].at[k], send.at[t, k], recv.at[t, k], (x, y, 1 - c))
                for t in range(n) for k in range(4)]

    def start(ins, outs, sems):
        for cp in copies(ins, outs, sems):
            cp.start()

    def finish(ins, outs, sems):
        for cp in copies(ins, outs, sems):
            cp.wait()

    return _Rider("rs1" + tag, gs, [_sds((4,) + g.shape[2:], g.dtype) for g in gs],
                  [pltpu.SemaphoreType.DMA((n, 4)), pltpu.SemaphoreType.DMA((n, 4))], {}, start, finish)


def _rs_chip_rider(ps, tag):
    n = len(ps)

    def copies(ins, outs, sems):
        send, recv, local = sems
        x, y, c, chips = _mesh_place()
        my_chip = 2 * x + y
        out = []
        for t in range(n):
            for j, (cx, cy) in enumerate(chips):
                dev = (cx, cy, c)
                out.append((_remote(ins[t].at[2 * cx + cy], outs[t].at[my_chip], send.at[t, j], recv.at[t, j], dev),
                            _remote(ins[t].at[my_chip], outs[t].at[2 * cx + cy], send.at[t, j], recv.at[t, j], dev)))
        mine = [pltpu.make_async_copy(ins[t].at[my_chip], outs[t].at[my_chip], local.at[t]) for t in range(n)]
        return out, mine

    def start(ins, outs, sems):
        pairs, mine = copies(ins, outs, sems)
        for cp in mine:
            cp.start()
        for snd, _ in pairs:
            snd.start()

    def finish(ins, outs, sems):
        pairs, mine = copies(ins, outs, sems)
        for _, rcv in pairs:
            rcv.wait_recv()
        for snd, _ in pairs:
            snd.wait_send()
        for cp in mine:
            cp.wait()

    return _Rider("rs2" + tag, ps, [_sds(p.shape, p.dtype) for p in ps],
                  [pltpu.SemaphoreType.DMA((n, 3)), pltpu.SemaphoreType.DMA((n, 3)), pltpu.SemaphoreType.DMA((n,))],
                  {}, start, finish)


def _matmul(a, b, *, mode, tm, tn, tk, out_dtype, name, epi=None, extra=None, rider=None):
    if mode == "nn":
        (M, K), (K2, N) = a.shape, b.shape
    elif mode == "nt":
        (M, K), (N, K2) = a.shape, b.shape
    else:
        (K, M), (K2, N) = a.shape, b.shape
    tm, tn, tk = min(tm, M), min(tn, N), min(tk, K)
    assert K == K2 and M % tm == 0 and N % tn == 0 and K % tk == 0, (name, a.shape, b.shape)
    nk = K // tk
    dims = {"nn": NN, "nt": NT, "tn": TN}[mode]
    if mode == "tn":
        a_spec = pl.BlockSpec((tk, tm), lambda i, j, k: (k, i))
    else:
        a_spec = pl.BlockSpec((tm, tk), lambda i, j, k: (i, k))
    if mode == "nt":
        b_spec = pl.BlockSpec((tn, tk), lambda i, j, k: (j, k))
    else:
        b_spec = pl.BlockSpec((tk, tn), lambda i, j, k: (k, j))
    tile = pl.BlockSpec((tm, tn), lambda i, j, k: (i, j))
    n_extra = 1 if epi in ("residual", "drelu2") else 0
    n_out = 2 if epi == "relu2" else 1

    def body(*refs):
        a_ref, b_ref = refs[0], refs[1]
        extra_refs = refs[2:2 + n_extra]
        out_refs = refs[2 + n_extra:2 + n_extra + n_out]

        def finish(acc):
            if epi is None:
                out_refs[0][...] = acc.astype(out_dtype)
            elif epi == "residual":
                out_refs[0][...] = (extra_refs[0][...] + acc).astype(out_dtype)
            elif epi == "relu2":
                out_refs[0][...] = acc.astype(out_dtype)
                r = jnp.maximum(acc, 0.0)
                out_refs[1][...] = (r * r).astype(out_dtype)
            else:
                z = extra_refs[0][...].astype(F32)
                out_refs[0][...] = (acc * (2.0 * jnp.maximum(z, 0.0))).astype(out_dtype)

        part = _dot(a_ref[...], b_ref[...], dims)
        if nk == 1:
            finish(part)
        else:
            acc_ref = refs[-1]
            k = pl.program_id(2)

            @pl.when(k == 0)
            def _():
                acc_ref[...] = part

            @pl.when(k > 0)
            def _():
                acc_ref[...] += part

            @pl.when(k == nk - 1)
            def _():
                finish(acc_ref[...])

    res, carried = _carried_call(
        body, name=name, grid=(M // tm, N // tn, nk), in_specs=[a_spec, b_spec] + [tile] * n_extra,
        out_specs=[tile] * n_out, out_shape=[_sds((M, N), out_dtype)] * n_out,
        scratch_shapes=[pltpu.VMEM((tm, tn), F32)] if nk > 1 else [],
        dims=("parallel", "parallel", "arbitrary"), args=[a, b] + ([extra] if n_extra else []), rider=rider)
    res = res if n_out > 1 else res[0]
    return res if rider is None else (res, carried)


def _rms_val(x, g, n):
    r = lax.rsqrt(jnp.sum(x * x, axis=-1, keepdims=True) * (1.0 / n) + EPS)
    y = x * r
    return (y if g is None else y * g), r


def _rms_bwd_val(x, g, dy, n):
    r = lax.rsqrt(jnp.sum(x * x, axis=-1, keepdims=True) * (1.0 / n) + EPS)
    xhat = x * r
    dyg = dy if g is None else dy * g
    dx = r * (dyg - xhat * (jnp.sum(dyg * xhat, axis=-1, keepdims=True) * (1.0 / n)))
    return dx, dy * xhat


def _rope_val(x, c, sa, sb, shift):
    return x * c + pltpu.roll(x, LANES - shift, 1) * sa + pltpu.roll(x, shift, 1) * sb


def _rope_t_val(dy, c, sa, sb, shift):
    return dy * c + pltpu.roll(dy * sa, shift, 1) + pltpu.roll(dy * sb, LANES - shift, 1)


def _colsum(x):
    return jnp.sum(x, axis=0, keepdims=True)


def _rope_tables(S):
    pos = lax.broadcasted_iota(jnp.int32, (S, LANES), 0)
    lane = lax.broadcasted_iota(jnp.int32, (S, LANES), 1)

    def tables(p, dim, active):
        half = dim // 2
        inv = jnp.power(ROPE_THETA, -(2 * (lane % half)).astype(F32) / dim)
        a = p.astype(F32) * inv
        first = (lane % dim) < half
        zero = jnp.zeros((S, LANES), F32)
        return (jnp.where(active, jnp.cos(a), zero), jnp.where(active & first, -jnp.sin(a), zero),
                jnp.where(active & ~first, jnp.sin(a), zero))

    tab_a = tables(pos, 64, lane >= 64)
    tab_b = tables(jnp.where(lane < 64, pos // GRID_W, pos % GRID_W), 64, lane >= 0)
    tab_c = tables(pos, 128, lane >= 0)
    return tab_a, tab_b, tab_c


ROPE_SHIFT_AB = 32
ROPE_SHIFT_C = 64


def _rms_fwd(x, g, *, name, tr=512):
    S, W = x.shape
    tr = min(tr, S)

    def body(x_ref, g_ref, o_ref):
        y, _ = _rms_val(x_ref[...], g_ref[...], W)
        o_ref[...] = y.astype(CDT)

    return pl.pallas_call(
        body, name=name, grid=(S // tr,),
        in_specs=[pl.BlockSpec((tr, W), lambda i: (i, 0)), pl.BlockSpec((1, W), lambda i: (0, 0))],
        out_specs=pl.BlockSpec((tr, W), lambda i: (i, 0)), out_shape=_sds((S, W), CDT),
        compiler_params=_cparams(("parallel",)),
    )(x, g.reshape(1, W))


def _rms_bwd(x, g, dy, res, *, name, tr=256):
    S, W = x.shape
    tr = min(tr, S)

    def body(x_ref, g_ref, dy_ref, res_ref, dx_ref, dxb_ref, dg_ref):
        dx, dgt = _rms_bwd_val(x_ref[...], g_ref[...], dy_ref[...], W)
        dx = res_ref[...] + dx
        dx_ref[...] = dx
        dxb_ref[...] = dx.astype(CDT)

        @pl.when(pl.program_id(0) == 0)
        def _():
            dg_ref[...] = jnp.zeros_like(dg_ref)

        dg_ref[...] += _colsum(dgt)

    row = pl.BlockSpec((tr, W), lambda i: (i, 0))
    vec = pl.BlockSpec((1, W), lambda i: (0, 0))
    return pl.pallas_call(
        body, name=name, grid=(S // tr,),
        in_specs=[row, vec, row, row], out_specs=[row, row, vec],
        out_shape=[_sds((S, W), F32), _sds((S, W), CDT), _sds((1, W), F32)],
        compiler_params=_cparams(("arbitrary",)),
    )(x, g.reshape(1, W), dy, res)


def _loss_head(x, g, tgt, *, tr=256):
    S, W = x.shape
    tr = min(tr, S)

    def body(x_ref, g_ref, t_ref, loss_ref, dx_ref, dxb_ref, dg_ref):
        xv, gv = x_ref[...], g_ref[...]
        y, _ = _rms_val(xv, gv, W)
        err = y - t_ref[...]
        part = 0.5 * jnp.sum(jnp.sum(err * err, axis=-1, keepdims=True) * (1.0 / W), axis=0, keepdims=True)
        dx, dgt = _rms_bwd_val(xv, gv, err * (1.0 / W), W)
        dx_ref[...] = dx
        dxb_ref[...] = dx.astype(CDT)

        @pl.when(pl.program_id(0) == 0)
        def _():
            dg_ref[...] = jnp.zeros_like(dg_ref)
            loss_ref[...] = jnp.zeros_like(loss_ref)

        dg_ref[...] += _colsum(dgt)
        loss_ref[...] += jnp.broadcast_to(part, (1, LANES))

    row = pl.BlockSpec((tr, W), lambda i: (i, 0))
    vec = pl.BlockSpec((1, W), lambda i: (0, 0))
    return pl.pallas_call(
        body, name="loss_head", grid=(S // tr,),
        in_specs=[row, vec, row], out_specs=[pl.BlockSpec((1, LANES), lambda i: (0, 0)), row, row, vec],
        out_shape=[_sds((1, LANES), F32), _sds((S, W), F32), _sds((S, W), CDT), _sds((1, W), F32)],
        compiler_params=_cparams(("arbitrary",)),
    )(x, g.reshape(1, W), tgt)


def _tab_specs(tr):
    return [pl.BlockSpec((tr, LANES), lambda i: (i, 0))] * 9


def _lat_masks(shape):
    lane = lax.broadcasted_iota(jnp.int32, shape, 1)
    return lane < KV_LO, (lane >= KV_LO) & (lane < KV_HI)


def _prep_fwd(proj, glat, gqn, gkn, tabs, *, tr=256):
    S = proj.shape[0]
    tr = min(tr, S)

    def body(p_ref, glat_ref, gqn_ref, gkn_ref, ac, aa, ab, bc, ba, bb, cc, ca, cb,
             lat_ref, kpe_ref, qb_ref, kb_ref, vb_ref, qc_ref, kc_ref, vc_ref):
        x = p_ref[:, 0:LAT_W].astype(F32)
        is_q, is_kv = _lat_masks(x.shape)
        yq, _ = _rms_val(jnp.where(is_q, x, 0.0), None, Q_LORA)
        ykv, _ = _rms_val(jnp.where(is_kv, x, 0.0), None, KV_HI - KV_LO)
        lat_ref[...] = ((yq + ykv) * glat_ref[...]).astype(CDT)
        kpe_ref[...] = _rope_val(x[:, LAT_W - LANES:LAT_W], ac[...], aa[...], ab[...], ROPE_SHIFT_AB).astype(CDT)
        for h in range(6):
            xh = p_ref[:, OFF_BQ + LANES * h:OFF_BQ + LANES * (h + 1)].astype(F32)
            y = _rope_val(_rms_val(xh, gqn_ref[...], LANES)[0], bc[...], ba[...], bb[...], ROPE_SHIFT_AB)
            qb_ref[:, LANES * h:LANES * (h + 1)] = y.astype(CDT)
        for h in range(2):
            xh = p_ref[:, OFF_BK + LANES * h:OFF_BK + LANES * (h + 1)].astype(F32)
            y = _rope_val(_rms_val(xh, gkn_ref[...], LANES)[0], bc[...], ba[...], bb[...], ROPE_SHIFT_AB)
            kb_ref[:, LANES * h:LANES * (h + 1)] = y.astype(CDT)
        vb_ref[...] = p_ref[:, OFF_BV:OFF_BV + 256].astype(CDT)
        for h in range(6):
            sl = slice(LANES * h, LANES * (h + 1))
            qc_ref[:, sl] = _rope_val(p_ref[:, OFF_CQ + LANES * h:OFF_CQ + LANES * (h + 1)].astype(F32), cc[...],
                                      ca[...], cb[...], ROPE_SHIFT_C)
            kc_ref[:, sl] = _rope_val(p_ref[:, OFF_CK + LANES * h:OFF_CK + LANES * (h + 1)].astype(F32), cc[...],
                                      ca[...], cb[...], ROPE_SHIFT_C)
        vc_ref[...] = p_ref[:, OFF_CV:OFF_CV + 768].astype(F32)

    vec = lambda w: pl.BlockSpec((1, w), lambda i: (0, 0))
    row = lambda w: pl.BlockSpec((tr, w), lambda i: (i, 0))
    return pl.pallas_call(
        body, name="prep_fwd", grid=(S // tr,),
        in_specs=[row(PROJ_W), vec(LAT_W), vec(128), vec(128)] + _tab_specs(tr),
        out_specs=[row(LAT_W), row(128), row(768), row(256), row(256), row(768), row(768), row(768)],
        out_shape=[_sds((S, LAT_W), CDT), _sds((S, 128), CDT), _sds((S, 768), CDT), _sds((S, 256), CDT),
                   _sds((S, 256), CDT), _sds((S, 768), F32), _sds((S, 768), F32), _sds((S, 768), F32)],
        compiler_params=_cparams(("parallel",)),
    )(proj, glat, gqn, gkn, *tabs[0], *tabs[1], *tabs[2])


def _prep_a2_fwd(qkva, kpe, tab_a, *, tr=512):
    S = qkva.shape[0]
    tr = min(tr, S)

    def body(x_ref, kpe_ref, ac, aa, ab, qa_ref, ka_ref, va_ref):
        for h in range(4):
            lo, hi = 2 * LANES * h, 2 * LANES * h + LANES
            qa_ref[:, lo:hi] = x_ref[:, lo:hi].astype(CDT)
            qa_ref[:, hi:hi + LANES] = _rope_val(x_ref[:, hi:hi + LANES], ac[...], aa[...], ab[...],
                                                 ROPE_SHIFT_AB).astype(CDT)
            ka_ref[:, lo:hi] = x_ref[:, 1024 + LANES * h:1024 + LANES * (h + 1)].astype(CDT)
            ka_ref[:, hi:hi + LANES] = kpe_ref[...]
        va_ref[...] = x_ref[:, 1536:2048].astype(CDT)

    row = lambda w: pl.BlockSpec((tr, w), lambda i: (i, 0))
    return pl.pallas_call(
        body, name="prep_a2_fwd", grid=(S // tr,),
        in_specs=[row(2048), row(128)] + _tab_specs(tr)[:3],
        out_specs=[row(1024), row(1024), row(512)],
        out_shape=[_sds((S, 1024), CDT), _sds((S, 1024), CDT), _sds((S, 512), CDT)],
        compiler_params=_cparams(("parallel",)),
    )(qkva, kpe, *tab_a)


def _prep_a2_bwd(dqa, dka, dva, tab_a, *, tr=512):
    S = dqa.shape[0]
    tr = min(tr, S)

    def body(dq_ref, dk_ref, dv_ref, ac, aa, ab, dx_ref, dkr_ref):
        dkpe = jnp.zeros((tr, LANES), F32)
        for h in range(4):
            lo, hi = 2 * LANES * h, 2 * LANES * h + LANES
            dx_ref[:, lo:hi] = dq_ref[:, lo:hi].astype(CDT)
            dx_ref[:, hi:hi + LANES] = _rope_t_val(dq_ref[:, hi:hi + LANES], ac[...], aa[...], ab[...],
                                                   ROPE_SHIFT_AB).astype(CDT)
            dx_ref[:, 1024 + LANES * h:1024 + LANES * (h + 1)] = dk_ref[:, lo:hi].astype(CDT)
            dkpe = dkpe + dk_ref[:, hi:hi + LANES]
        dx_ref[:, 1536:2048] = dv_ref[...].astype(CDT)
        dkr_ref[...] = _rope_t_val(dkpe, ac[...], aa[...], ab[...], ROPE_SHIFT_AB)

    row = lambda w: pl.BlockSpec((tr, w), lambda i: (i, 0))
    return pl.pallas_call(
        body, name="prep_a2_bwd", grid=(S // tr,),
        in_specs=[row(1024), row(1024), row(512)] + _tab_specs(tr)[:3],
        out_specs=[row(2048), row(128)],
        out_shape=[_sds((S, 2048), CDT), _sds((S, 128), F32)],
        compiler_params=_cparams(("parallel",)),
    )(dqa, dka, dva, *tab_a)


def _prep_bwd(proj, glat, gqn, gkn, tabs, dlat, dkr, dqb, dkb, dvb, dqc, dkc, dvc, *, tr=256):
    S = proj.shape[0]
    tr = min(tr, S)

    def body(p_ref, glat_ref, gqn_ref, gkn_ref, ac, aa, ab, bc, ba, bb, cc, ca, cb,
             dlat_ref, dkr_ref, dqb_ref, dkb_ref, dvb_ref, dqc_ref, dkc_ref, dvc_ref,
             dp_ref, dglat_ref, dgqn_ref, dgkn_ref):
        @pl.when(pl.program_id(0) == 0)
        def _():
            dglat_ref[...] = jnp.zeros_like(dglat_ref)
            dgqn_ref[...] = jnp.zeros_like(dgqn_ref)
            dgkn_ref[...] = jnp.zeros_like(dgkn_ref)

        x = p_ref[:, 0:LAT_W].astype(F32)
        is_q, is_kv = _lat_masks(x.shape)
        dy, g = dlat_ref[...], glat_ref[...]
        dxq, dgq = _rms_bwd_val(jnp.where(is_q, x, 0.0), g, jnp.where(is_q, dy, 0.0), Q_LORA)
        dxkv, dgkv = _rms_bwd_val(jnp.where(is_kv, x, 0.0), g, jnp.where(is_kv, dy, 0.0), KV_HI - KV_LO)
        dglat_ref[...] += _colsum(dgq + dgkv)
        dx = dxq + dxkv
        dp_ref[:, 0:LAT_W - LANES] = dx[:, 0:LAT_W - LANES].astype(CDT)
        dp_ref[:, LAT_W - LANES:LAT_W] = (dx[:, LAT_W - LANES:LAT_W] + dkr_ref[...]).astype(CDT)
        dgqn = jnp.zeros((1, LANES), F32)
        for h in range(6):
            sl = slice(LANES * h, LANES * (h + 1))
            po = slice(OFF_BQ + LANES * h, OFF_BQ + LANES * (h + 1))
            dyh = _rope_t_val(dqb_ref[:, sl], bc[...], ba[...], bb[...], ROPE_SHIFT_AB)
            dxh, dgt = _rms_bwd_val(p_ref[:, po].astype(F32), gqn_ref[...], dyh, LANES)
            dp_ref[:, po] = dxh.astype(CDT)
            dgqn = dgqn + _colsum(dgt)
        dgqn_ref[...] += dgqn
        dgkn = jnp.zeros((1, LANES), F32)
        for h in range(2):
            sl = slice(LANES * h, LANES * (h + 1))
            po = slice(OFF_BK + LANES * h, OFF_BK + LANES * (h + 1))
            dyh = _rope_t_val(dkb_ref[:, sl], bc[...], ba[...], bb[...], ROPE_SHIFT_AB)
            dxh, dgt = _rms_bwd_val(p_ref[:, po].astype(F32), gkn_ref[...], dyh, LANES)
            dp_ref[:, po] = dxh.astype(CDT)
            dgkn = dgkn + _colsum(dgt)
        dgkn_ref[...] += dgkn
        dp_ref[:, OFF_BV:OFF_BV + 256] = dvb_ref[...].astype(CDT)
        for h in range(6):
            sl = slice(LANES * h, LANES * (h + 1))
            dp_ref[:, OFF_CQ + LANES * h:OFF_CQ + LANES * (h + 1)] = _rope_t_val(
                dqc_ref[:, sl], cc[...], ca[...], cb[...], ROPE_SHIFT_C).astype(CDT)
            dp_ref[:, OFF_CK + LANES * h:OFF_CK + LANES * (h + 1)] = _rope_t_val(
                dkc_ref[:, sl], cc[...], ca[...], cb[...], ROPE_SHIFT_C).astype(CDT)
        dp_ref[:, OFF_CV:OFF_CV + 768] = dvc_ref[...].astype(CDT)

    vec = lambda w: pl.BlockSpec((1, w), lambda i: (0, 0))
    row = lambda w: pl.BlockSpec((tr, w), lambda i: (i, 0))
    return pl.pallas_call(
        body, name="prep_bwd", grid=(S // tr,),
        in_specs=[row(PROJ_W), vec(LAT_W), vec(128), vec(128)] + _tab_specs(tr)
        + [row(LAT_W), row(128), row(768), row(256), row(256), row(768), row(768), row(768)],
        out_specs=[row(PROJ_W), vec(LAT_W), vec(128), vec(128)],
        out_shape=[_sds((S, PROJ_W), CDT), _sds((1, LAT_W), F32), _sds((1, 128), F32), _sds((1, 128), F32)],
        compiler_params=_cparams(("arbitrary",)),
    )(proj, glat, gqn, gkn, *tabs[0], *tabs[1], *tabs[2], dlat, dkr, dqb, dkb, dvb, dqc, dkc, dvc)


ATTN_TK = 512
LOG2E = 1.4426950408889634
C2_H = SCALE_H * LOG2E


def _attn_fwd(q, k, v, *, H, G, dk, dv, scale, name, tq=512, rider=None):
    S = q.shape[0]
    tq = min(tq, S)
    tk = min(ATTN_TK, S)
    c2 = scale * LOG2E

    half = tq // 2

    def body(q_ref, k_ref, v_ref, o_ref, l_ref):
        chunks = [pl.ds(c * tk, tk) for c in range(S // tk)]
        qa, qb = q_ref[0:half, :], q_ref[half:tq, :]
        rowmax = lambda s: functools.reduce(jnp.maximum, [jnp.max(sc, axis=-1, keepdims=True) for sc in s])
        s_a = [_dot(qa, k_ref[rows, :], NT) for rows in chunks]
        m_a = rowmax(s_a)
        e_a, s_b = [], []
        for c, rows in enumerate(chunks):
            e_a.append(jnp.exp2((s_a[c] - m_a) * c2))
            s_b.append(_dot(qb, k_ref[rows, :], NT))
        m_b = rowmax(s_b)
        den_a, den_b = jnp.zeros((half, 1), F32), jnp.zeros((half, 1), F32)
        acc_a, acc_b = jnp.zeros((half, dv), F32), jnp.zeros((half, dv), F32)
        e_b = []
        for c, rows in enumerate(chunks):
            acc_a = acc_a + _dot(e_a[c].astype(CDT), v_ref[rows, :], NN)
            den_a = den_a + jnp.sum(e_a[c], axis=-1, keepdims=True)
            e_b.append(jnp.exp2((s_b[c] - m_b) * c2))
        for c, rows in enumerate(chunks):
            acc_b = acc_b + _dot(e_b[c].astype(CDT), v_ref[rows, :], NN)
            den_b = den_b + jnp.sum(e_b[c], axis=-1, keepdims=True)
        o_ref[0:half, :] = acc_a * (1.0 / den_a)
        o_ref[half:tq, :] = acc_b * (1.0 / den_b)
        l_ref[0:half, :] = jnp.broadcast_to(m_a * scale + jnp.log(den_a), (half, LANES))
        l_ref[half:tq, :] = jnp.broadcast_to(m_b * scale + jnp.log(den_b), (half, LANES))

    return _carried_call(
        body, name=name, grid=(H, S // tq),
        in_specs=[pl.BlockSpec((tq, dk), lambda h, i: (i, h)), pl.BlockSpec((S, dk), lambda h, i: (0, h // G)),
                  pl.BlockSpec((S, dv), lambda h, i: (0, h // G))],
        out_specs=[pl.BlockSpec((tq, dv), lambda h, i: (i, h)), pl.BlockSpec((tq, LANES), lambda h, i: (i, h))],
        out_shape=[_sds((S, H * dv), F32), _sds((S, H * LANES), F32)], scratch_shapes=[],
        dims=("parallel", "parallel"), args=[q, k, v], rider=rider)


def _attn_bwd(q, k, v, do, lse, delta, *, H, G, dk, dv, scale, name, tq=512, rider=None):
    S = q.shape[0]
    tq = min(tq, S)
    Hkv = H // G
    c2 = scale * LOG2E

    def body(q_ref, k_ref, v_ref, do_ref, l_ref, d_ref, dq_ref, dk_ref, dv_ref):
        @pl.when((pl.program_id(1) == 0) & (pl.program_id(2) == 0))
        def _():
            dk_ref[...] = jnp.zeros_like(dk_ref)
            dv_ref[...] = jnp.zeros_like(dv_ref)

        qv, kv, dov = q_ref[...], k_ref[...], do_ref[...]
        p = jnp.exp2(_dot(qv, kv, NT) * c2 - l_ref[:, 0:1] * LOG2E)
        dp = _dot(dov, v_ref[...], NT)
        ds = (p * (dp - d_ref[:, 0:1]) * scale).astype(CDT)
        dq_ref[...] = _dot(ds, kv, NN)
        dk_ref[...] += _dot(ds, qv, TN)
        dv_ref[...] += _dot(p.astype(CDT), dov, TN)

    qi = lambda hk, g, i: (i, hk * G + g)
    return _carried_call(
        body, name=name, grid=(Hkv, G, S // tq),
        in_specs=[pl.BlockSpec((tq, dk), qi), pl.BlockSpec((S, dk), lambda hk, g, i: (0, hk)),
                  pl.BlockSpec((S, dv), lambda hk, g, i: (0, hk)), pl.BlockSpec((tq, dv), qi),
                  pl.BlockSpec((tq, LANES), qi), pl.BlockSpec((tq, LANES), qi)],
        out_specs=[pl.BlockSpec((tq, dk), qi), pl.BlockSpec((S, dk), lambda hk, g, i: (0, hk)),
                   pl.BlockSpec((S, dv), lambda hk, g, i: (0, hk))],
        out_shape=[_sds((S, H * dk), F32), _sds((S, Hkv * dk), F32), _sds((S, Hkv * dv), F32)], scratch_shapes=[],
        dims=("parallel", "arbitrary", "arbitrary"), args=[q, k, v, do, lse, delta], rider=rider)


BAND_SUB = 128
BAND_WIN = 384
BAND_UNROLL = 8


def _band_blocks(S, d):
    L = S // d
    assert L % BAND_SUB == 0 and S % (BAND_SUB * BAND_UNROLL) == 0
    return L, L // BAND_SUB, min(BAND_WIN, L)


def _band_index(blk, d, nb, L, win):
    r, jb = blk // nb, blk % nb
    l0 = jb * BAND_SUB
    w0 = jnp.clip(l0 - BAND_SUB, 0, L - win)
    return r + d * l0, r + d * w0, l0, w0


def _band_rows(start, size, d):
    return pl.ds(pl.multiple_of(start, BAND_SUB), size) if d == 1 else pl.ds(start, size, stride=d)


def _band_mask(l0, w0, win):
    rpos = l0 + lax.broadcasted_iota(jnp.int32, (BAND_SUB, win), 0)
    cpos = w0 + lax.broadcasted_iota(jnp.int32, (BAND_SUB, win), 1)
    return jnp.abs(rpos - cpos) <= BAND_HALF


def _mixc_fwd(q, k, v, rider=None):
    S, W = q.shape

    def body(q_ref, k_ref, v_ref, o_ref, l_ref, *scratch):
        ob_refs, lb_refs = scratch[0:3], scratch[3:6]
        for b, d in enumerate(DILATIONS):
            L, nb, win = _band_blocks(S, d)

            def step(it, carry, b=b, d=d, L=L, nb=nb, win=win):
                idx = [_band_index(it * BAND_UNROLL + u, d, nb, L, win) for u in range(BAND_UNROLL)]
                qrows = [_band_rows(i[0], BAND_SUB, d) for i in idx]
                krows = [_band_rows(i[1], win, d) for i in idx]
                qv = [q_ref[r, :].astype(CDT) for r in qrows]
                kw = [k_ref[r, :].astype(CDT) for r in krows]
                vw = [v_ref[r, :].astype(CDT) for r in krows]
                s = [jnp.where(_band_mask(i[2], i[3], win), _dot(a, kk, NT), NEG_INF) for i, a, kk in zip(idx, qv, kw)]
                m = [jnp.max(x, axis=-1, keepdims=True) for x in s]
                e = [jnp.exp2((x - mm) * C2_H) for x, mm in zip(s, m)]
                den = [jnp.sum(x, axis=-1, keepdims=True) for x in e]
                o = [_dot((x * (1.0 / dd)).astype(CDT), vv, NN) for x, dd, vv in zip(e, den, vw)]
                for r, ou, mm, dd in zip(qrows, o, m, den):
                    ob_refs[b][r, :] = ou
                    lb_refs[b][r, :] = jnp.broadcast_to(mm * SCALE_H + jnp.log(dd), (BAND_SUB, LANES))
                return carry

            lax.fori_loop(0, S // (BAND_SUB * BAND_UNROLL), step, 0)

        def combine(c, carry):
            rows = pl.ds(pl.multiple_of(c * 256, 256), 256)
            l0, l1, l2 = lb_refs[0][rows, :], lb_refs[1][rows, :], lb_refs[2][rows, :]
            m = jnp.maximum(jnp.maximum(l0, l1), l2)
            e0, e1, e2 = jnp.exp(l0 - m), jnp.exp(l1 - m), jnp.exp(l2 - m)
            den = e0 + e1 + e2
            inv = 1.0 / den
            o_ref[rows, :] = ((e0 * inv) * ob_refs[0][rows, :] + (e1 * inv) * ob_refs[1][rows, :]
                              + (e2 * inv) * ob_refs[2][rows, :])
            l_ref[rows, :] = m + jnp.log(den)
            return carry

        lax.fori_loop(0, S // 256, combine, 0)

    head = pl.BlockSpec((S, LANES), lambda h: (0, h))
    return _carried_call(
        body, name="mixc_fwd", grid=(W // LANES,), in_specs=[head] * 3, out_specs=[head] * 2,
        out_shape=[_sds((S, W), F32)] * 2, scratch_shapes=[pltpu.VMEM((S, LANES), F32)] * 6,
        dims=("parallel",), args=[q, k, v], rider=rider)


def _mixc_bwd(q, k, v, do, lse, dd, rider=None):
    S, W = q.shape

    def body(q_ref, k_ref, v_ref, do_ref, l_ref, d_ref, dq_ref, dk_ref, dv_ref):
        dq_ref[...] = jnp.zeros_like(dq_ref)
        dk_ref[...] = jnp.zeros_like(dk_ref)
        dv_ref[...] = jnp.zeros_like(dv_ref)
        for d in DILATIONS:
            L, nb, win = _band_blocks(S, d)

            def step(it, carry, d=d, L=L, nb=nb, win=win):
                idx = [_band_index(it * BAND_UNROLL + u, d, nb, L, win) for u in range(BAND_UNROLL)]
                qrows = [_band_rows(i[0], BAND_SUB, d) for i in idx]
                krows = [_band_rows(i[1], win, d) for i in idx]
                qv = [q_ref[r, :].astype(CDT) for r in qrows]
                dov = [do_ref[r, :].astype(CDT) for r in qrows]
                kw = [k_ref[r, :].astype(CDT) for r in krows]
                vw = [v_ref[r, :].astype(CDT) for r in krows]
                lse2 = [l_ref[r, :][:, 0:1] * LOG2E for r in qrows]
                dd = [d_ref[r, :][:, 0:1] for r in qrows]
                s = [jnp.where(_band_mask(i[2], i[3], win), _dot(a, kk, NT), NEG_INF) for i, a, kk in zip(idx, qv, kw)]
                p = [jnp.exp2(x * C2_H - ll) for x, ll in zip(s, lse2)]
                dp = [_dot(a, vv, NT) for a, vv in zip(dov, vw)]
                ds = [(pp * (x - y) * SCALE_H).astype(CDT) for pp, x, y in zip(p, dp, dd)]
                dq = [_dot(x, kk, NN) for x, kk in zip(ds, kw)]
                dk = [_dot(x, a, TN) for x, a in zip(ds, qv)]
                dv = [_dot(pp.astype(CDT), a, TN) for pp, a in zip(p, dov)]
                for u in range(BAND_UNROLL):
                    dq_ref[qrows[u], :] += dq[u]
                    dk_ref[krows[u], :] += dk[u]
                    dv_ref[krows[u], :] += dv[u]
                return carry

            lax.fori_loop(0, S // (BAND_SUB * BAND_UNROLL), step, 0)

    head = pl.BlockSpec((S, LANES), lambda h: (0, h))
    return _carried_call(
        body, name="mixc_bwd", grid=(W // LANES,), in_specs=[head] * 6, out_specs=[head] * 3,
        out_shape=[_sds((S, W), F32)] * 3, scratch_shapes=[], dims=("parallel",), args=[q, k, v, do, lse, dd],
        rider=rider)


def _outnorm_fwd(oa, ob, oc, g, *, tr=256):
    S = oa.shape[0]
    tr = min(tr, S)

    def body(a_ref, b_ref, c_ref, g_ref, m_ref):
        m_ref[:, 0:512] = (_rms_val(a_ref[...], None, 512)[0] * g_ref[:, 0:512]).astype(CDT)
        m_ref[:, 512:1280] = (_rms_val(b_ref[...], None, 768)[0] * g_ref[:, 512:1280]).astype(CDT)
        m_ref[:, 1280:2048] = (_rms_val(c_ref[...], None, 768)[0] * g_ref[:, 1280:2048]).astype(CDT)

    row = lambda w: pl.BlockSpec((tr, w), lambda i: (i, 0))
    return pl.pallas_call(
        body, name="outnorm_fwd", grid=(S // tr,),
        in_specs=[row(512), row(768), row(768), pl.BlockSpec((1, 2048), lambda i: (0, 0))],
        out_specs=row(2048), out_shape=_sds((S, 2048), CDT), compiler_params=_cparams(("parallel",)),
    )(oa, ob, oc, g)


def _outnorm_bwd(oa, ob, oc, g, dm, *, tr=256):
    S = oa.shape[0]
    tr = min(tr, S)

    def body(a_ref, b_ref, c_ref, g_ref, dm_ref, doa_ref, dob_ref, doc_ref, da_ref, db_ref, dc_ref, dg_ref):
        @pl.when(pl.program_id(0) == 0)
        def _():
            dg_ref[...] = jnp.zeros_like(dg_ref)

        for o_ref, do_ref, d_ref, lo, w in ((a_ref, doa_ref, da_ref, 0, 512), (b_ref, dob_ref, db_ref, 512, 768),
                                            (c_ref, doc_ref, dc_ref, 1280, 768)):
            o = o_ref[...]
            dmv = dm_ref[:, lo:lo + w]
            do, _ = _rms_bwd_val(o, None, dmv * g_ref[:, lo:lo + w], w)
            r = lax.rsqrt(jnp.sum(o * o, axis=-1, keepdims=True) * (1.0 / w) + EPS)
            dg_ref[:, lo:lo + w] += _colsum(dmv * (o * r))
            do_ref[...] = do.astype(do_ref.dtype)
            for h in range(w // LANES):
                sl = slice(LANES * h, LANES * (h + 1))
                d_ref[:, sl] = jnp.broadcast_to(jnp.sum(do[:, sl] * o[:, sl], axis=-1, keepdims=True), (tr, LANES))

    row = lambda w: pl.BlockSpec((tr, w), lambda i: (i, 0))
    vec = pl.BlockSpec((1, 2048), lambda i: (0, 0))
    return pl.pallas_call(
        body, name="outnorm_bwd", grid=(S // tr,),
        in_specs=[row(512), row(768), row(768), vec, row(2048)],
        out_specs=[row(512), row(768), row(768), row(512), row(768), row(768), vec],
        out_shape=[_sds((S, 512), CDT), _sds((S, 768), CDT), _sds((S, 768), F32), _sds((S, 512), F32),
                   _sds((S, 768), F32), _sds((S, 768), F32), _sds((1, 2048), F32)],
        compiler_params=_cparams(("arbitrary",)),
    )(oa, ob, oc, g, dm)


def _row_tile(r, c, itemsize, limit=1 << 20):
    best = 16
    for t in range(16, r + 1, 16):
        if r % t == 0 and t * c * itemsize <= limit:
            best = t
    return best


def _rs_chip_sum(g, got, *, name):
    _, _, r, c = g.shape
    tr = _row_tile(r, c, 2)
    core = lax.axis_index("c").astype(jnp.int32).reshape(1)

    def body(c_ref, a_ref, b_ref, o_ref):
        o_ref[...] = (a_ref[...].astype(F32) + b_ref[...].astype(F32)).astype(o_ref.dtype)

    spec = pltpu.PrefetchScalarGridSpec(
        num_scalar_prefetch=1, grid=(4, r // tr),
        in_specs=[pl.BlockSpec((None, None, tr, c), lambda k, i, cr: (k, cr[0], i, 0)),
                  pl.BlockSpec((None, tr, c), lambda k, i, cr: (k, i, 0))],
        out_specs=pl.BlockSpec((None, tr, c), lambda k, i, cr: (k, i, 0)))
    return pl.pallas_call(body, name=name, grid_spec=spec, out_shape=_sds((4, r, c), g.dtype),
                          compiler_params=_cparams(("parallel", "parallel")))(core, g, got)


def _rs_final_sum(landed, *, name):
    depth = len(landed)
    _, r, c = landed[0].shape
    tr = _row_tile(r, c, 4)

    def body(*refs):
        o_ref = refs[depth]
        for k in range(depth):
            @pl.when(pl.program_id(0) == k)
            def _(r_ref=refs[k]):
                o_ref[...] = ((r_ref[0].astype(F32) + r_ref[1].astype(F32)) + r_ref[2].astype(F32)
                              ) + r_ref[3].astype(F32)

    in_specs = [pl.BlockSpec((4, tr, c), lambda l, i, k=k: (0, jnp.where(l == k, i, 0), 0)) for k in range(depth)]
    return pl.pallas_call(
        body, name=name, grid=(depth, r // tr), in_specs=in_specs,
        out_specs=pl.BlockSpec((None, tr, c), lambda l, i: (l, i, 0)), out_shape=_sds((depth, r, c), F32),
        compiler_params=_cparams(("arbitrary", "arbitrary")))(*landed)


def _all_reduce_small(v):
    R = v.shape[0]

    def body(v_ref, out_ref, buf_ref, send_sems, recv_sems):
        x, y, c = lax.axis_index("x"), lax.axis_index("y"), lax.axis_index("c")
        me = 4 * x + 2 * y + c
        buf_ref[me] = v_ref[...]
        peers = []
        for r in range(1, 8):
            px, py, pc = x ^ (r >> 2), y ^ ((r >> 1) & 1), c ^ (r & 1)
            peers.append((r, (px, py, pc), 4 * px + 2 * py + pc))
        sends = [pltpu.make_async_remote_copy(
            src_ref=v_ref, dst_ref=buf_ref.at[me], send_sem=send_sems.at[r - 1], recv_sem=recv_sems.at[r - 1],
            device_id=dev, device_id_type=MESH) for r, dev, _ in peers]
        for cp in sends:
            cp.start()
        for r, dev, idx in peers:
            pltpu.make_async_remote_copy(
                src_ref=v_ref, dst_ref=buf_ref.at[idx], send_sem=send_sems.at[r - 1], recv_sem=recv_sems.at[r - 1],
                device_id=dev, device_id_type=MESH).wait_recv()
        for cp in sends:
            cp.wait_send()
        acc = buf_ref[0]
        for k in range(1, 8):
            acc = acc + buf_ref[k]
        out_ref[...] = acc

    vm = pl.BlockSpec(memory_space=pltpu.VMEM)
    return pl.pallas_call(
        body, name="all_reduce_small", out_shape=_sds((R, LANES), F32), in_specs=[vm], out_specs=vm,
        scratch_shapes=[pltpu.VMEM((8, R, LANES), F32), pltpu.SemaphoreType.DMA((7,)), pltpu.SemaphoreType.DMA((7,))],
    )(v)


def _adamw(w, g, m, v, *, name):
    R, C = w.shape
    tr = R
    for cand in (1024, 512, 256, 128, 64, 32, 16, 8):
        if R % cand == 0 and cand * C * 4 <= 2 * 1024 * 1024:
            tr = cand
            break

    def body(w_ref, g_ref, m_ref, v_ref, d_ref, nm_ref, nv_ref):
        gv = g_ref[...]
        mn = ADAM_B1 * m_ref[...] + (1.0 - ADAM_B1) * gv
        vn = ADAM_B2 * v_ref[...] + (1.0 - ADAM_B2) * (gv * gv)
        m_hat = mn / (1.0 - ADAM_B1 ** ADAM_STEP)
        v_hat = vn / (1.0 - ADAM_B2 ** ADAM_STEP)
        d_ref[...] = -ADAM_LR * (m_hat / (jnp.sqrt(v_hat) + ADAM_EPS) + ADAM_WD * w_ref[...])
        nm_ref[...] = mn
        nv_ref[...] = vn

    blk = pl.BlockSpec((tr, C), lambda i: (i, 0))
    return pl.pallas_call(
        body, name=name, grid=(R // tr,), in_specs=[blk] * 4, out_specs=[blk] * 3,
        out_shape=[_sds((R, C), F32)] * 3, compiler_params=_cparams(("parallel",)))(w, g, m, v)


def _wuq_pad(w):
    w = w.reshape(448, 4, 192)
    z = jnp.zeros((448, 4, 64), w.dtype)
    return jnp.concatenate([w[:, :, 0:128], z, w[:, :, 128:192]], axis=2).reshape(448, 1024)


def _wuq_unpad(w):
    w = w.reshape(448, 4, 256)
    return jnp.concatenate([w[:, :, 0:128], w[:, :, 192:256]], axis=2).reshape(448, 768)


def _wukv_perm(w):
    return w.reshape(512, 4, 2, 128).transpose(0, 2, 1, 3).reshape(512, 1024)


def _wukv_unperm(w):
    return w.reshape(512, 2, 4, 128).transpose(0, 2, 1, 3).reshape(512, 1024)


def _lat_weight(w_uq, w_ukv):
    z = lambda r, c: jnp.zeros((r, c), w_uq.dtype)
    top = jnp.concatenate([_wuq_pad(w_uq), z(448, 1024)], axis=1)
    mid = jnp.concatenate([z(512, 1024), _wukv_perm(w_ukv)], axis=1)
    return jnp.concatenate([top, mid, z(64, 2048)], axis=0)


def _lat_weight_grads(dw):
    return _wuq_unpad(dw[0:KV_LO, 0:1024]), _wukv_unperm(dw[KV_LO:KV_HI, 1024:2048])


def _comm_shards(w_in, w_uq, w_ukv, w_out, w_ff1, w_ff2):
    lat = jnp.concatenate([w_uq.reshape(UQ_ROWS, LANES), w_ukv.reshape(512, LANES)], axis=0)
    return [w_in.T, w_ff1.T, w_out, w_ff2, lat]


def _from_comm_shards(parts):
    w_in_t, w_ff1_t, w_out, w_ff2, lat = parts
    lead = lat.shape[:-2]
    return {"w_in": jnp.swapaxes(w_in_t, -1, -2), "w_ff1": jnp.swapaxes(w_ff1_t, -1, -2), "w_out": w_out,
            "w_ff2": w_ff2, "w_uq": lat[..., 0:UQ_ROWS, :].reshape(lead + (448, 96)),
            "w_ukv": lat[..., UQ_ROWS:, :].reshape(lead + (512, 128))}


def _early_weights(g_in_t, g_lat):
    w_uq = g_lat[:, 0:UQ_ROWS].reshape(8, 448, 96).transpose(1, 0, 2).reshape(448, 768)
    w_ukv = g_lat[:, UQ_ROWS:].reshape(8, 512, 128).transpose(1, 0, 2).reshape(512, 1024)
    return g_in_t.reshape(PROJ_W, D_MODEL), _lat_weight(w_uq, w_ukv)


def _layer_fwd(x, W, G, tabs, plan=None):
    W = dict(W)
    sh, nxt, ff1_half = plan if plan is not None else (None, None, None)
    first = plan is not None and ff1_half is None
    early_next = None if nxt is None else [nxt["w_in_t"], nxt["lat"]]
    ag1 = lambda arrays, tag, **kw: None if (plan is None or not arrays) else _ag_first_rider(arrays, tag, **kw)
    ag2 = lambda arrays, tag: None if (plan is None or not arrays) else _ag_second_rider(arrays, tag)
    s = {"x0": x}
    s["h1"] = _rms_fwd(x, G["ln1_g"], name="rms1_fwd")
    mm_in = functools.partial(_matmul, s["h1"], W["w_in_t"], mode="nt", tm=1024, tn=768, tk=2048, out_dtype=CDT,
                              name="mm_in")
    if first:
        s["proj"], ff1_rows0 = mm_in(rider=ag1([sh["w_ff1_t"]], "_ff1a", part=(0, 2)))
    else:
        s["proj"] = mm_in()
    s["lat"], kpe, s["qb"], s["kb"], s["vb"], s["qc"], s["kc"], s["vc"] = _prep_fwd(
        s["proj"], G["glat"], G["gqn"], G["gkn"], tabs)
    qkva = _matmul(s["lat"], W["w_lat"], mode="nn", tm=1024, tn=1024, tk=1024, out_dtype=F32, name="mm_lat")
    s["qa"], s["ka"], s["va"] = _prep_a2_fwd(qkva, kpe, tabs[0])
    early, ff1_half_next = None, None
    attn_a = functools.partial(_attn_fwd, s["qa"], s["ka"], s["va"], H=4, G=1, dk=256, dv=128, scale=SCALE_A,
                               name="attn_a_fwd")
    attn_b = functools.partial(_attn_fwd, s["qb"], s["kb"], s["vb"], H=6, G=3, dk=128, dv=128, scale=SCALE_H,
                               name="attn_b_fwd")
    mixc = functools.partial(_mixc_fwd, s["qc"], s["kc"], s["vc"])
    if plan is None:
        (s["oa"], s["lse_a"]), _ = attn_a()
        (s["ob"], s["lse_b"]), _ = attn_b()
        (s["oc"], s["lse_c"]), _ = mixc()
    else:
        if first:
            (s["oa"], s["lse_a"]), got = attn_a(rider=ag1([sh["w_ff1_t"]], "_ff1b", part=(1, 2), into=ff1_rows0))
            ff1_half, early_half = got[0], []
        else:
            (s["oa"], s["lse_a"]), early_half = attn_a(rider=ag1(early_next, "_early"))
        (s["ob"], s["lse_b"]), got_b = attn_b(
            rider=_join(_join(ag1([sh["w_out"]], "_out"), ag1([sh["w_ff2"]], "_ff2a", part=(0, 2))),
                        ag2([ff1_half], "_ff1")))
        W["w_ff1_t"] = got_b[2].reshape(D_FF, D_MODEL)
        (s["oc"], s["lse_c"]), got_c = mixc(
            rider=_join(_join(ag2(got_b[0:1], "_out"), ag1([sh["w_ff2"]], "_ff2b", part=(1, 2), into=got_b[1:2])),
                        ag2(early_half, "_early")))
        W["w_out"] = got_c[0].reshape(D_MODEL, D_MODEL)
        ff2_half, early = got_c[1], (got_c[2:4] or None)
    s["mixed"] = _outnorm_fwd(s["oa"], s["ob"], s["oc"], G["g_out"])
    mm_out = functools.partial(_matmul, s["mixed"], W["w_out"], mode="nn", tm=1024, tn=1024, tk=2048, out_dtype=F32,
                               name="mm_out", epi="residual", extra=x)
    if plan is None:
        s["x1"] = mm_out()
    else:
        s["x1"], got = mm_out(rider=ag2([ff2_half], "_ff2"))
        W["w_ff2"] = got[0].reshape(D_FF, D_MODEL)
    s["h2"] = _rms_fwd(s["x1"], G["ln2_g"], name="rms2_fwd")
    ff1 = functools.partial(_matmul, s["h2"], W["w_ff1_t"], mode="nt", tm=1024, tn=1024, tk=2048, out_dtype=CDT,
                            name="mm_ff1", epi="relu2")
    if nxt is None:
        s["z"], s["u"] = ff1()
        x2 = _matmul(s["u"], W["w_ff2"], mode="nn", tm=1024, tn=1024, tk=2048, out_dtype=F32, name="mm_ff2",
                     epi="residual", extra=s["x1"])
    else:
        (s["z"], s["u"]), got = ff1(rider=_join(ag1(early_next, "_early") if first else None,
                                                ag1([nxt["w_ff1_t"]], "_ff1a", part=(0, 2))))
        early_half, rows0 = (got[0:2], got[2:3]) if first else ([], got[0:1])
        x2, got = _matmul(s["u"], W["w_ff2"], mode="nn", tm=1024, tn=1024, tk=2048, out_dtype=F32, name="mm_ff2",
                          epi="residual", extra=s["x1"],
                          rider=_join(ag2(early_half, "_early"),
                                      ag1([nxt["w_ff1_t"]], "_ff1b", part=(1, 2), into=rows0)))
        if first:
            early, got = got[0:2], got[2:3]
        ff1_half_next = got[0]
    return x2, s, W, early, ff1_half_next


def _by_destination(dw, name):
    return dw.reshape((4, 2) + COMM_SHAPE[name])


def _early_by_destination(dw_in_t, dw_lat):
    dw_uq, dw_ukv = _lat_weight_grads(dw_lat)
    lat = jnp.concatenate([dw_uq.reshape(448, 8, 96).transpose(1, 0, 2).reshape(8, UQ_ROWS, LANES),
                           dw_ukv.reshape(512, 8, 128).transpose(1, 0, 2)], axis=1)
    return [_by_destination(dw_in_t, "w_in_t"), _by_destination(lat, "lat")]


def _layer_bwd(dx2, dx2b, s, W, G, tabs, scatter=False, pending=None):
    dw, dg, landed = {}, {}, {}
    ff2_dx = functools.partial(_matmul, dx2b, W["w_ff2"], mode="nt", tm=1024, tn=1024, tk=2048, out_dtype=CDT,
                               name="mm_ff2_dx", epi="drelu2", extra=s["z"])
    if pending is None:
        dz = ff2_dx()
    else:
        dz, got = ff2_dx(rider=_rs_sibling_rider(pending, "_early"))
        chip = [_rs_chip_sum(g, r, name="rs_chip_sum_" + n) for g, r, n in zip(pending, got, ("w_in_t", "lat"))]
    dw["w_ff2"] = _matmul(s["u"], dx2b, mode="tn", tm=2048, tn=1024, tk=2048, out_dtype=CDT, name="mm_ff2_dw")
    ff1_dx = functools.partial(_matmul, dz, W["w_ff1_t"], mode="nn", tm=1024, tn=1024, tk=2048, out_dtype=F32,
                               name="mm_ff1_dx")
    if pending is None:
        dh2 = ff1_dx()
    else:
        dh2, got = ff1_dx(rider=_rs_chip_rider(chip, "_early"))
        landed["above_w_in_t"], landed["above_lat"] = got
    dw["w_ff1_t"] = _matmul(dz, s["h2"], mode="tn", tm=2048, tn=1024, tk=2048, out_dtype=CDT, name="mm_ff1_dw")
    dx1, dx1b, dg["ln2_g"] = _rms_bwd(s["x1"], G["ln2_g"], dh2, dx2, name="rms2_bwd")
    dmixed = _matmul(dx1b, W["w_out"], mode="nt", tm=1024, tn=1024, tk=2048, out_dtype=F32, name="mm_out_dx")
    out_dw = functools.partial(_matmul, s["mixed"], dx1b, mode="tn", tm=1024, tn=1024, tk=2048, out_dtype=CDT,
                               name="mm_out_dw")
    if not scatter:
        dw["w_out"] = out_dw()
        riders = [None, None, None]
    else:
        g_ff = [_by_destination(dw["w_ff2"], "w_ff2"), _by_destination(dw["w_ff1_t"], "w_ff1_t")]
        dw["w_out"], got = out_dw(rider=_rs_sibling_rider(g_ff, "_ff"))
        chip_ff2 = _rs_chip_sum(g_ff[0], got[0], name="rs_chip_sum_w_ff2")
        chip_ff1 = _rs_chip_sum(g_ff[1], got[1], name="rs_chip_sum_w_ff1_t")
        g_out = [_by_destination(dw["w_out"], "w_out")]
        riders = [_rs_chip_rider([chip_ff2], "_ff2"),
                  _join(_rs_chip_rider([chip_ff1], "_ff1"), _rs_sibling_rider(g_out, "_out")), None]
    doa, dob, doc, dla, dlb, dlc, dg["g_out"] = _outnorm_bwd(s["oa"], s["ob"], s["oc"], G["g_out"], dmixed)
    (dqa, dka, dva), got = _attn_bwd(s["qa"], s["ka"], s["va"], doa, s["lse_a"], dla, H=4, G=1, dk=256, dv=128,
                                     scale=SCALE_A, name="attn_a_bwd", rider=riders[0])
    if scatter:
        landed["w_ff2"] = got[0]
    (dqb, dkb, dvb), got = _attn_bwd(s["qb"], s["kb"], s["vb"], dob, s["lse_b"], dlb, H=6, G=3, dk=128, dv=128,
                                     scale=SCALE_H, name="attn_b_bwd", rider=riders[1])
    if scatter:
        landed["w_ff1_t"] = got[0]
        riders[2] = _rs_chip_rider([_rs_chip_sum(g_out[0], got[1], name="rs_chip_sum_w_out")], "_out")
    (dqc, dkc, dvc), got = _mixc_bwd(s["qc"], s["kc"], s["vc"], doc, s["lse_c"], dlc, rider=riders[2])
    if scatter:
        landed["w_out"] = got[0]
    dqkva, dkr = _prep_a2_bwd(dqa, dka, dva, tabs[0])
    dlat = _matmul(dqkva, W["w_lat"], mode="nt", tm=1024, tn=1024, tk=2048, out_dtype=F32, name="mm_lat_dx")
    dw["w_lat"] = _matmul(s["lat"], dqkva, mode="tn", tm=1024, tn=2048, tk=2048, out_dtype=CDT, name="mm_lat_dw")
    dproj, dg["glat"], dg["gqn"], dg["gkn"] = _prep_bwd(
        s["proj"], G["glat"], G["gqn"], G["gkn"], tabs, dlat, dkr, dqb, dkb, dvb, dqc, dkc, dvc)
    dh1 = _matmul(dproj, W["w_in_t"], mode="nn", tm=1024, tn=1024, tk=2304, out_dtype=F32, name="mm_in_dx")
    dw["w_in_t"] = _matmul(dproj, s["h1"], mode="tn", tm=1536, tn=1024, tk=2048, out_dtype=CDT, name="mm_in_dw")
    dx0, dx0b, dg["ln1_g"] = _rms_bwd(s["x0"], G["ln1_g"], dh1, dx1, name="rms1_bwd")
    return dx0, dx0b, dw, dg, landed


def _layer_gains(l, ln1_g, g_q_a, g_kv_a, g_qn_b, g_kn_b, g_out, ln2_g):
    return {"ln1_g": ln1_g[l], "ln2_g": ln2_g[l], "g_out": g_out[l].reshape(1, 2048),
            "glat": jnp.concatenate([g_q_a[l], g_kv_a[l], jnp.zeros((LAT_W - KV_HI,), F32)]).reshape(1, LAT_W),
            "gqn": g_qn_b[l].reshape(1, 128), "gkn": g_kn_b[l].reshape(1, 128)}


def _gain_grads(dg):
    glat = dg["glat"].reshape(-1)
    return {"ln1_g": dg["ln1_g"].reshape(-1), "g_q_a": glat[0:KV_LO], "g_kv_a": glat[KV_LO:KV_HI],
            "g_qn_b": dg["gqn"].reshape(-1), "g_kn_b": dg["gkn"].reshape(-1), "g_out": dg["g_out"].reshape(-1),
            "ln2_g": dg["ln2_g"].reshape(-1)}


def _local_step(x, tgt, weights, gains, ln_f_g):
    S = x.shape[0]
    tabs = _rope_tables(S)
    depth = len(weights)
    saved = []
    for l in range(depth):
        x, s, _, _, _ = _layer_fwd(x, weights[l], gains[l], tabs)
        saved.append(s)
    loss, dx, dxb, dlnf = _loss_head(x, ln_f_g, tgt)
    dws, dgs = [None] * depth, [None] * depth
    for l in reversed(range(depth)):
        dx, dxb, dws[l], dgs[l], _ = _layer_bwd(dx, dxb, saved[l], weights[l], gains[l], tabs)
    return loss, dx, dws, dgs, dlnf


SMALL_SIZES = (("ln1_g", 2048), ("g_q_a", 448), ("g_kv_a", 512), ("g_qn_b", 128), ("g_kn_b", 128), ("g_out", 2048),
               ("ln2_g", 2048))


def _pack_small(per_layer, ln_f):
    flat = jnp.concatenate([per_layer[n].reshape(-1) for n, _ in SMALL_SIZES] + [ln_f.reshape(-1)])
    rows = -(-flat.shape[0] // (8 * LANES)) * 8
    return jnp.concatenate([flat, jnp.zeros((rows * LANES - flat.shape[0],), F32)]).reshape(rows, LANES)


def _unpack_small(packed, depth):
    flat, out, lo = packed.reshape(-1), {}, 0
    for n, w in SMALL_SIZES:
        out[n] = flat[lo:lo + depth * w].reshape(depth, w)
        lo += depth * w
    out["ln_f_g"] = flat[lo:lo + 2048]
    return out


def kernel(x, ln1_g, w_in, g_q_a, w_uq, g_kv_a, w_ukv, g_qn_b, g_kn_b, g_out, w_out, ln2_g, w_ff1, w_ff2, ln_f_g, loss_target, m_ln1_g, m_w_in, m_g_q_a, m_w_uq, m_g_kv_a, m_w_ukv, m_g_qn_b, m_g_kn_b, m_g_out, m_w_out, m_ln2_g, m_w_ff1, m_w_ff2, m_ln_f_g, v_ln1_g, v_w_in, v_g_q_a, v_w_uq, v_g_kv_a, v_w_ukv, v_g_qn_b, v_g_kn_b, v_g_out, v_w_out, v_ln2_g, v_w_ff1, v_w_ff2, v_ln_f_g):
    depth = w_in.shape[0]
    S = x.shape[1]
    big_w = {"w_in": w_in, "w_uq": w_uq, "w_ukv": w_ukv, "w_out": w_out, "w_ff1": w_ff1, "w_ff2": w_ff2}
    big_m = {"w_in": m_w_in, "w_uq": m_w_uq, "w_ukv": m_w_ukv, "w_out": m_w_out, "w_ff1": m_w_ff1, "w_ff2": m_w_ff2}
    big_v = {"w_in": v_w_in, "w_uq": v_w_uq, "w_ukv": v_w_ukv, "w_out": v_w_out, "w_ff1": v_w_ff1, "w_ff2": v_w_ff2}
    small_w = {"ln1_g": ln1_g, "g_q_a": g_q_a, "g_kv_a": g_kv_a, "g_qn_b": g_qn_b, "g_kn_b": g_kn_b, "g_out": g_out,
               "ln2_g": ln2_g}
    small_m = {"ln1_g": m_ln1_g, "g_q_a": m_g_q_a, "g_kv_a": m_g_kv_a, "g_qn_b": m_g_qn_b, "g_kn_b": m_g_kn_b,
               "g_out": m_g_out, "ln2_g": m_ln2_g}
    small_v = {"ln1_g": v_ln1_g, "g_q_a": v_g_q_a, "g_kv_a": v_g_kv_a, "g_qn_b": v_g_qn_b, "g_kn_b": v_g_kn_b,
               "g_out": v_g_out, "ln2_g": v_ln2_g}

    shards = [dict(zip(COMM, _comm_shards(*[big_w[n][l].astype(CDT) for n in BIG]))) for l in range(depth)]
    early_shards = [[sh["w_in_t"], sh["lat"]] for sh in shards]
    early = _run_rider(_ag_second_rider(_run_rider(_ag_first_rider(early_shards[0], "_early")), "_early"))
    gains = [_layer_gains(l, ln1_g, g_q_a, g_kv_a, g_qn_b, g_kn_b, g_out, ln2_g) for l in range(depth)]
    tabs = _rope_tables(S)

    h = x.reshape(S, D_MODEL)
    saved, weights, f_half = [], [], None
    for l in range(depth):
        W = dict(zip(("w_in_t", "w_lat"), _early_weights(*early)))
        h, s, W, early, f_half = _layer_fwd(h, W, gains[l], tabs,
                                            plan=(shards[l], shards[l + 1] if l + 1 < depth else None, f_half))
        saved.append(s)
        weights.append(W)
    loss_part, dx, dxb, dlnf = _loss_head(h, ln_f_g, loss_target.reshape(S, D_MODEL))
    loss = lax.psum(loss_part[0, 0], ("x", "y", "c"))

    dgs, landed, pending = [None] * depth, [None] * depth, None
    for l in reversed(range(depth)):
        dx, dxb, dw, dgs[l], landed[l] = _layer_bwd(dx, dxb, saved[l], weights[l], gains[l], tabs, scatter=True,
                                                    pending=pending)
        if pending is not None:
            landed[l + 1]["w_in_t"], landed[l + 1]["lat"] = landed[l].pop("above_w_in_t"), landed[l].pop("above_lat")
        pending = _early_by_destination(dw["w_in_t"], dw["w_lat"])
    got = _run_rider(_rs_sibling_rider(pending, "_early"))
    chip = [_rs_chip_sum(g, r, name="rs_chip_sum_" + n) for g, r, n in zip(pending, got, ("w_in_t", "lat"))]
    landed[0]["w_in_t"], landed[0]["lat"] = _run_rider(_rs_chip_rider(chip, "_early"))
    grad_x = dx.reshape(1, S, D_MODEL)

    big_g = _from_comm_shards([_rs_final_sum([landed[l][n] for l in range(depth)], name="rs_final_sum_" + n)
                               for n in COMM])

    named = [_gain_grads(dgs[l]) for l in range(depth)]
    per_layer = {n: jnp.stack([named[l][n] for l in range(depth)]) for n, _ in SMALL_SIZES}
    small_g = _unpack_small(_all_reduce_small(_pack_small(per_layer, dlnf.reshape(-1))), depth)

    upd = {}
    for n in BIG:
        shp = big_w[n].shape
        two_d = (shp[0] * shp[1], shp[2])
        d, nm, nv = _adamw(big_w[n].reshape(two_d), big_g[n].reshape(two_d), big_m[n].reshape(two_d),
                           big_v[n].reshape(two_d), name="adamw_" + n)
        upd[n] = (d.reshape(shp), nm.reshape(shp), nv.reshape(shp))
    small_w["ln_f_g"], small_m["ln_f_g"], small_v["ln_f_g"] = ln_f_g, m_ln_f_g, v_ln_f_g
    names_small = [n for n, _ in SMALL_SIZES]
    pw = _pack_small({n: small_w[n] for n in names_small}, small_w["ln_f_g"])
    pg = _pack_small({n: small_g[n] for n in names_small}, small_g["ln_f_g"])
    pm = _pack_small({n: small_m[n] for n in names_small}, small_m["ln_f_g"])
    pv = _pack_small({n: small_v[n] for n in names_small}, small_v["ln_f_g"])
    d, nm, nv = _adamw(pw, pg, pm, pv, name="adamw_small")
    sd, snm, snv = _unpack_small(d, depth), _unpack_small(nm, depth), _unpack_small(nv, depth)
    for n in names_small + ["ln_f_g"]:
        upd[n] = (sd[n], snm[n], snv[n])

    order = ["ln1_g", "w_in", "g_q_a", "w_uq", "g_kv_a", "w_ukv", "g_qn_b", "g_kn_b", "g_out", "w_out", "ln2_g", "w_ff1",
             "w_ff2", "ln_f_g"]
    grads = {**big_g, **small_g}
    return (loss, grad_x, *[grads[n] for n in order], *[upd[n][0] for n in order], *[upd[n][1] for n in order],
            *[upd[n][2] for n in order])
```

```python
import functools
import math

import jax
import jax.numpy as jnp
from jax import lax
from jax.experimental import pallas as pl
from jax.experimental.pallas import tpu as pltpu

D_MODEL = 2048
D_FF = 8192
EPS = 1e-6
NEG_INF = -1e30
Q_LORA = 448
ROPE_THETA = 10000.0
GRID_W = 64
DILATIONS = (1, 4, 16)
BAND_HALF = 64
SCALE_A = 1.0 / math.sqrt(192.0)
SCALE_H = 1.0 / math.sqrt(128.0)
ADAM_LR, ADAM_B1, ADAM_B2, ADAM_EPS, ADAM_WD, ADAM_STEP = 0.001, 0.9, 0.999, 1e-08, 0.01, 10

CDT = jnp.bfloat16
F32 = jnp.float32
LANES = 128
VMEM_LIMIT = 56 * 1024 * 1024

PROJ_W = 4608
LAT_W = 1024
KV_LO, KV_HI = 448, 960
OFF_BQ, OFF_BK, OFF_BV, OFF_CQ, OFF_CK, OFF_CV = 1024, 1792, 2048, 2304, 3072, 3840

NN = ((1,), (0,))
NT = ((1,), (1,))
TN = ((0,), (0,))

BIG = ("w_in", "w_uq", "w_ukv", "w_out", "w_ff1", "w_ff2")
COMM = ("w_in_t", "w_ff1_t", "w_out", "w_ff2", "lat")
COMM_SHAPE = {"w_in_t": (576, 2048), "w_ff1_t": (1024, 2048), "w_out": (256, 2048), "w_ff2": (1024, 2048),
              "lat": (848, 128)}
UQ_ROWS = 448 * 96 // LANES


def _dot(a, b, dims):
    return lax.dot_general(a, b, (dims, ((), ())), preferred_element_type=F32)


def _cparams(dims=None):
    return pltpu.CompilerParams(dimension_semantics=dims, vmem_limit_bytes=VMEM_LIMIT)


def _sds(shape, dtype):
    return jax.ShapeDtypeStruct(shape, dtype)


MESH = pl.DeviceIdType.MESH
ANY = pl.BlockSpec(memory_space=pl.ANY)


class _Rider:
    def __init__(self, name, arrays, out_shape, scratch, aliases, start, finish):
        self.name, self.arrays, self.out_shape, self.scratch = name, list(arrays), list(out_shape), list(scratch)
        self.aliases, self.start, self.finish = dict(aliases), start, finish


def _join(a, b):
    if a is None or b is None:
        return a if b is None else b
    na, oa, sa = len(a.arrays), len(a.out_shape), len(a.scratch)
    aliases = dict(a.aliases)
    aliases.update({na + i: oa + o for i, o in b.aliases.items()})

    def start(ins, outs, sems):
        a.start(ins[:na], outs[:oa], sems[:sa])
        b.start(ins[na:], outs[oa:], sems[sa:])

    def finish(ins, outs, sems):
        a.finish(ins[:na], outs[:oa], sems[:sa])
        b.finish(ins[na:], outs[oa:], sems[sa:])

    return _Rider(a.name + "_" + b.name, a.arrays + b.arrays, a.out_shape + b.out_shape, a.scratch + b.scratch,
                  aliases, start, finish)


def _carried_call(body, *, name, grid, in_specs, out_specs, out_shape, scratch_shapes, dims, args, rider):
    in_specs, out_specs, out_shape = list(in_specs), list(out_specs), list(out_shape)
    scratch_shapes = list(scratch_shapes)
    if rider is None:
        res = pl.pallas_call(body, name=name, grid=grid, in_specs=in_specs, out_specs=out_specs, out_shape=out_shape,
                             scratch_shapes=scratch_shapes, compiler_params=_cparams(dims))(*args)
        return list(res), []
    n_in, n_out, n_scr = len(in_specs), len(out_specs), len(scratch_shapes)
    r_in, r_out = len(rider.arrays), len(rider.out_shape)

    def wrapped(*refs):
        o0 = n_in + r_in
        s0 = o0 + n_out + r_out
        ins, outs, sems = refs[n_in:o0], refs[o0 + n_out:s0], refs[s0 + n_scr:]
        ids = [pl.program_id(a) for a in range(len(grid))]
        first = functools.reduce(jnp.logical_and, [i == 0 for i in ids])
        last = functools.reduce(jnp.logical_and, [i == g - 1 for i, g in zip(ids, grid)])

        @pl.when(first)
        def _():
            rider.start(ins, outs, sems)

        body(*refs[:n_in], *refs[o0:o0 + n_out], *refs[s0:s0 + n_scr])

        @pl.when(last)
        def _():
            rider.finish(ins, outs, sems)

    res = pl.pallas_call(
        wrapped, name=name + "_" + rider.name, grid=grid, in_specs=in_specs + [ANY] * r_in,
        out_specs=out_specs + [ANY] * r_out, out_shape=out_shape + rider.out_shape,
        scratch_shapes=scratch_shapes + rider.scratch,
        input_output_aliases={n_in + i: n_out + o for i, o in rider.aliases.items()},
        compiler_params=_cparams(("arbitrary",) * len(grid)),
    )(*args, *rider.arrays)
    return list(res[:n_out]), list(res[n_out:])


def _run_rider(rider):
    def body(*refs):
        r_in, r_out = len(rider.arrays), len(rider.out_shape)
        ins, outs, sems = refs[:r_in], refs[r_in:r_in + r_out], refs[r_in + r_out:]
        rider.start(ins, outs, sems)
        rider.finish(ins, outs, sems)

    res = pl.pallas_call(
        body, name=rider.name, in_specs=[ANY] * len(rider.arrays), out_specs=[ANY] * len(rider.out_shape),
        out_shape=rider.out_shape, scratch_shapes=rider.scratch, input_output_aliases=rider.aliases,
    )(*rider.arrays)
    return list(res)


def _mesh_place():
    x, y, c = lax.axis_index("x"), lax.axis_index("y"), lax.axis_index("c")
    return x, y, c, [(1 - x, y), (x, 1 - y), (1 - x, 1 - y)]


def _remote(src, dst, send, recv, dev):
    return pltpu.make_async_remote_copy(src_ref=src, dst_ref=dst, send_sem=send, recv_sem=recv, device_id=dev,
                                        device_id_type=MESH)


def _ag_first_rider(shards, tag, part=(0, 1), into=None):
    n = len(shards)

    def copies(ins, outs, sems):
        send, recv, _ = sems
        x, y, c, chips = _mesh_place()
        me = 4 * x + 2 * y + c
        peers = [(x, y, 1 - c)] + [(cx, cy, c) for cx, cy in chips]
        out, mine = [], []
        for t in range(n):
            size = shards[t].shape[0] // part[1]
            rows = pl.ds(part[0] * size, size)
            for k, dev in enumerate(peers):
                theirs = 4 * dev[0] + 2 * dev[1] + dev[2]
                out.append((_remote(ins[t].at[rows], outs[t].at[me, rows], send.at[t, k], recv.at[t, k], dev),
                            _remote(ins[t].at[rows], outs[t].at[theirs, rows], send.at[t, k], recv.at[t, k], dev)))
            mine.append(pltpu.make_async_copy(ins[t].at[rows], outs[t].at[me, rows], sems[2].at[t]))
        return out, mine

    def start(ins, outs, sems):
        pairs, mine = copies(ins, outs, sems)
        for cp in mine:
            cp.start()
        for snd, _ in pairs:
            snd.start()

    def finish(ins, outs, sems):
        pairs, mine = copies(ins, outs, sems)
        for _, rcv in pairs:
            rcv.wait_recv()
        for snd, _ in pairs:
            snd.wait_send()
        for cp in mine:
            cp.wait()

    return _Rider("ag1" + tag, list(shards) + list(into or []), [_sds((8,) + s.shape, s.dtype) for s in shards],
                  [pltpu.SemaphoreType.DMA((n, 4)), pltpu.SemaphoreType.DMA((n, 4)), pltpu.SemaphoreType.DMA((n,))],
                  {} if into is None else {n + t: t for t in range(n)}, start, finish)


def _ag_second_rider(gathered, tag):
    n = len(gathered)

    def copies(ins, outs, sems):
        send, recv = sems
        x, y, c, chips = _mesh_place()
        out = []
        for t in range(n):
            for j, (cx, cy) in enumerate(chips):
                here, there = 4 * cx + 2 * cy + c, 4 * cx + 2 * cy + (1 - c)
                out.append((_remote(ins[t].at[here], outs[t].at[here], send.at[t, j], recv.at[t, j], (x, y, 1 - c)),
                            _remote(ins[t].at[here], outs[t].at[there], send.at[t, j], recv.at[t, j], (x, y, 1 - c))))
        return out

    def start(ins, outs, sems):
        for snd, _ in copies(ins, outs, sems):
            snd.start()

    def finish(ins, outs, sems):
        pairs = copies(ins, outs, sems)
        for _, rcv in pairs:
            rcv.wait_recv()
        for snd, _ in pairs:
            snd.wait_send()

    return _Rider("ag2" + tag, gathered, [_sds(g.shape, g.dtype) for g in gathered],
                  [pltpu.SemaphoreType.DMA((n, 3)), pltpu.SemaphoreType.DMA((n, 3))],
                  {t: t for t in range(n)}, start, finish)


def _rs_sibling_rider(gs, tag):
    n = len(gs)

    def copies(ins, outs, sems):
        send, recv = sems
        x, y, c, _ = _mesh_place()
        return [_remote(ins[t].at[k, 1 - c], outs[t].at[k], send.at[t, k], recv.at[t, k], (x, y, 1 - c))
                for t in range(n) for k in range(4)]

    def start(ins, outs, sems):
        for cp in copies(ins, outs, sems):
            cp.start()

    def finish(ins, outs, sems):
        for cp in copies(ins, outs, sems):
            cp.wait()

    return _Rider("rs1" + tag, gs, [_sds((4,) + g.shape[2:], g.dtype) for g in gs],
                  [pltpu.SemaphoreType.DMA((n, 4)), pltpu.SemaphoreType.DMA((n, 4))], {}, start, finish)


def _rs_chip_rider(ps, tag):
    n = len(ps)

    def copies(ins, outs, sems):
        send, recv, local = sems
        x, y, c, chips = _mesh_place()
        my_chip = 2 * x + y
        out = []
        for t in range(n):
            for j, (cx, cy) in enumerate(chips):
                dev = (cx, cy, c)
                out.append((_remote(ins[t].at[2 * cx + cy], outs[t].at[my_chip], send.at[t, j], recv.at[t, j], dev),
                            _remote(ins[t].at[my_chip], outs[t].at[2 * cx + cy], send.at[t, j], recv.at[t, j], dev)))
        mine = [pltpu.make_async_copy(ins[t].at[my_chip], outs[t].at[my_chip], local.at[t]) for t in range(n)]
        return out, mine

    def start(ins, outs, sems):
        pairs, mine = copies(ins, outs, sems)
        for cp in mine:
            cp.start()
        for snd, _ in pairs:
            snd.start()

    def finish(ins, outs, sems):
        pairs, mine = copies(ins, outs, sems)
        for _, rcv in pairs:
            rcv.wait_recv()
        for snd, _ in pairs:
            snd.wait_send()
        for cp in mine:
            cp.wait()

    return _Rider("rs2" + tag, ps, [_sds(p.shape, p.dtype) for p in ps],
                  [pltpu.SemaphoreType.DMA((n, 3)), pltpu.SemaphoreType.DMA((n, 3)), pltpu.SemaphoreType.DMA((n,))],
                  {}, start, finish)


def _matmul(a, b, *, mode, tm, tn, tk, out_dtype, name, epi=None, extra=None, rider=None):
    if mode == "nn":
        (M, K), (K2, N) = a.shape, b.shape
    elif mode == "nt":
        (M, K), (N, K2) = a.shape, b.shape
    else:
        (K, M), (K2, N) = a.shape, b.shape
    tm, tn, tk = min(tm, M), min(tn, N), min(tk, K)
    assert K == K2 and M % tm == 0 and N % tn == 0 and K % tk == 0, (name, a.shape, b.shape)
    nk = K // tk
    dims = {"nn": NN, "nt": NT, "tn": TN}[mode]
    if mode == "tn":
        a_spec = pl.BlockSpec((tk, tm), lambda i, j, k: (k, i))
    else:
        a_spec = pl.BlockSpec((tm, tk), lambda i, j, k: (i, k))
    if mode == "nt":
        b_spec = pl.BlockSpec((tn, tk), lambda i, j, k: (j, k))
    else:
        b_spec = pl.BlockSpec((tk, tn), lambda i, j, k: (k, j))
    tile = pl.BlockSpec((tm, tn), lambda i, j, k: (i, j))
    n_extra = 1 if epi in ("residual", "drelu2") else 0
    n_out = 2 if epi == "relu2" else 1

    def body(*refs):
        a_ref, b_ref = refs[0], refs[1]
        extra_refs = refs[2:2 + n_extra]
        out_refs = refs[2 + n_extra:2 + n_extra + n_out]

        def finish(acc):
            if epi is None:
                out_refs[0][...] = acc.astype(out_dtype)
            elif epi == "residual":
                out_refs[0][...] = (extra_refs[0][...] + acc).astype(out_dtype)
            elif epi == "relu2":
                out_refs[0][...] = acc.astype(out_dtype)
                r = jnp.maximum(acc, 0.0)
                out_refs[1][...] = (r * r).astype(out_dtype)
            else:
                z = extra_refs[0][...].astype(F32)
                out_refs[0][...] = (acc * (2.0 * jnp.maximum(z, 0.0))).astype(out_dtype)

        part = _dot(a_ref[...], b_ref[...], dims)
        if nk == 1:
            finish(part)
        else:
            acc_ref = refs[-1]
            k = pl.program_id(2)

            @pl.when(k == 0)
            def _():
                acc_ref[...] = part

            @pl.when(k > 0)
            def _():
                acc_ref[...] += part

            @pl.when(k == nk - 1)
            def _():
                finish(acc_ref[...])

    res, carried = _carried_call(
        body, name=name, grid=(M // tm, N // tn, nk), in_specs=[a_spec, b_spec] + [tile] * n_extra,
        out_specs=[tile] * n_out, out_shape=[_sds((M, N), out_dtype)] * n_out,
        scratch_shapes=[pltpu.VMEM((tm, tn), F32)] if nk > 1 else [],
        dims=("parallel", "parallel", "arbitrary"), args=[a, b] + ([extra] if n_extra else []), rider=rider)
    res = res if n_out > 1 else res[0]
    return res if rider is None else (res, carried)


def _rms_val(x, g, n):
    r = lax.rsqrt(jnp.sum(x * x, axis=-1, keepdims=True) * (1.0 / n) + EPS)
    y = x * r
    return (y if g is None else y * g), r


def _rms_bwd_val(x, g, dy, n):
    r = lax.rsqrt(jnp.sum(x * x, axis=-1, keepdims=True) * (1.0 / n) + EPS)
    xhat = x * r
    dyg = dy if g is None else dy * g
    dx = r * (dyg - xhat * (jnp.sum(dyg * xhat, axis=-1, keepdims=True) * (1.0 / n)))
    return dx, dy * xhat


def _rope_val(x, c, sa, sb, shift):
    return x * c + pltpu.roll(x, LANES - shift, 1) * sa + pltpu.roll(x, shift, 1) * sb


def _rope_t_val(dy, c, sa, sb, shift):
    return dy * c + pltpu.roll(dy * sa, shift, 1) + pltpu.roll(dy * sb, LANES - shift, 1)


def _colsum(x):
    return jnp.sum(x, axis=0, keepdims=True)


def _rope_tables(S):
    pos = lax.broadcasted_iota(jnp.int32, (S, LANES), 0)
    lane = lax.broadcasted_iota(jnp.int32, (S, LANES), 1)

    def tables(p, dim, active):
        half = dim // 2
        inv = jnp.power(ROPE_THETA, -(2 * (lane % half)).astype(F32) / dim)
        a = p.astype(F32) * inv
        first = (lane % dim) < half
        zero = jnp.zeros((S, LANES), F32)
        return (jnp.where(active, jnp.cos(a), zero), jnp.where(active & first, -jnp.sin(a), zero),
                jnp.where(active & ~first, jnp.sin(a), zero))

    tab_a = tables(pos, 64, lane >= 64)
    tab_b = tables(jnp.where(lane < 64, pos // GRID_W, pos % GRID_W), 64, lane >= 0)
    tab_c = tables(pos, 128, lane >= 0)
    return tab_a, tab_b, tab_c


ROPE_SHIFT_AB = 32
ROPE_SHIFT_C = 64


def _rms_fwd(x, g, *, name, tr=512):
    S, W = x.shape
    tr = min(tr, S)

    def body(x_ref, g_ref, o_ref):
        y, _ = _rms_val(x_ref[...], g_ref[...], W)
        o_ref[...] = y.astype(CDT)

    return pl.pallas_call(
        body, name=name, grid=(S // tr,),
        in_specs=[pl.BlockSpec((tr, W), lambda i: (i, 0)), pl.BlockSpec((1, W), lambda i: (0, 0))],
        out_specs=pl.BlockSpec((tr, W), lambda i: (i, 0)), out_shape=_sds((S, W), CDT),
        compiler_params=_cparams(("parallel",)),
    )(x, g.reshape(1, W))


def _rms_bwd(x, g, dy, res, *, name, tr=256):
    S, W = x.shape
    tr = min(tr, S)

    def body(x_ref, g_ref, dy_ref, res_ref, dx_ref, dxb_ref, dg_ref):
        dx, dgt = _rms_bwd_val(x_ref[...], g_ref[...], dy_ref[...], W)
        dx = res_ref[...] + dx
        dx_ref[...] = dx
        dxb_ref[...] = dx.astype(CDT)

        @pl.when(pl.program_id(0) == 0)
        def _():
            dg_ref[...] = jnp.zeros_like(dg_ref)

        dg_ref[...] += _colsum(dgt)

    row = pl.BlockSpec((tr, W), lambda i: (i, 0))
    vec = pl.BlockSpec((1, W), lambda i: (0, 0))
    return pl.pallas_call(
        body, name=name, grid=(S // tr,),
        in_specs=[row, vec, row, row], out_specs=[row, row, vec],
        out_shape=[_sds((S, W), F32), _sds((S, W), CDT), _sds((1, W), F32)],
        compiler_params=_cparams(("arbitrary",)),
    )(x, g.reshape(1, W), dy, res)


def _loss_head(x, g, tgt, *, tr=256):
    S, W = x.shape
    tr = min(tr, S)

    def body(x_ref, g_ref, t_ref, loss_ref, dx_ref, dxb_ref, dg_ref):
        xv, gv = x_ref[...], g_ref[...]
        y, _ = _rms_val(xv, gv, W)
        err = y - t_ref[...]
        part = 0.5 * jnp.sum(jnp.sum(err * err, axis=-1, keepdims=True) * (1.0 / W), axis=0, keepdims=True)
        dx, dgt = _rms_bwd_val(xv, gv, err * (1.0 / W), W)
        dx_ref[...] = dx
        dxb_ref[...] = dx.astype(CDT)

        @pl.when(pl.program_id(0) == 0)
        def _():
            dg_ref[...] = jnp.zeros_like(dg_ref)
            loss_ref[...] = jnp.zeros_like(loss_ref)

        dg_ref[...] += _colsum(dgt)
        loss_ref[...] += jnp.broadcast_to(part, (1, LANES))

    row = pl.BlockSpec((tr, W), lambda i: (i, 0))
    vec = pl.BlockSpec((1, W), lambda i: (0, 0))
    return pl.pallas_call(
        body, name="loss_head", grid=(S // tr,),
        in_specs=[row, vec, row], out_specs=[pl.BlockSpec((1, LANES), lambda i: (0, 0)), row, row, vec],
        out_shape=[_sds((1, LANES), F32), _sds((S, W), F32), _sds((S, W), CDT), _sds((1, W), F32)],
        compiler_params=_cparams(("arbitrary",)),
    )(x, g.reshape(1, W), tgt)


def _tab_specs(tr):
    return [pl.BlockSpec((tr, LANES), lambda i: (i, 0))] * 9


def _lat_masks(shape):
    lane = lax.broadcasted_iota(jnp.int32, shape, 1)
    return lane < KV_LO, (lane >= KV_LO) & (lane < KV_HI)


def _prep_fwd(proj, glat, gqn, gkn, tabs, *, tr=256):
    S = proj.shape[0]
    tr = min(tr, S)

    def body(p_ref, glat_ref, gqn_ref, gkn_ref, ac, aa, ab, bc, ba, bb, cc, ca, cb,
             lat_ref, kpe_ref, qb_ref, kb_ref, vb_ref, qc_ref, kc_ref, vc_ref):
        x = p_ref[:, 0:LAT_W].astype(F32)
        is_q, is_kv = _lat_masks(x.shape)
        yq, _ = _rms_val(jnp.where(is_q, x, 0.0), None, Q_LORA)
        ykv, _ = _rms_val(jnp.where(is_kv, x, 0.0), None, KV_HI - KV_LO)
        lat_ref[...] = ((yq + ykv) * glat_ref[...]).astype(CDT)
        kpe_ref[...] = _rope_val(x[:, LAT_W - LANES:LAT_W], ac[...], aa[...], ab[...], ROPE_SHIFT_AB).astype(CDT)
        for h in range(6):
            xh = p_ref[:, OFF_BQ + LANES * h:OFF_BQ + LANES * (h + 1)].astype(F32)
            y = _rope_val(_rms_val(xh, gqn_ref[...], LANES)[0], bc[...], ba[...], bb[...], ROPE_SHIFT_AB)
            qb_ref[:, LANES * h:LANES * (h + 1)] = y.astype(CDT)
        for h in range(2):
            xh = p_ref[:, OFF_BK + LANES * h:OFF_BK + LANES * (h + 1)].astype(F32)
            y = _rope_val(_rms_val(xh, gkn_ref[...], LANES)[0], bc[...], ba[...], bb[...], ROPE_SHIFT_AB)
            kb_ref[:, LANES * h:LANES * (h + 1)] = y.astype(CDT)
        vb_ref[...] = p_ref[:, OFF_BV:OFF_BV + 256].astype(CDT)
        for h in range(6):
            sl = slice(LANES * h, LANES * (h + 1))
            qc_ref[:, sl] = _rope_val(p_ref[:, OFF_CQ + LANES * h:OFF_CQ + LANES * (h + 1)].astype(F32), cc[...],
                                      ca[...], cb[...], ROPE_SHIFT_C)
            kc_ref[:, sl] = _rope_val(p_ref[:, OFF_CK + LANES * h:OFF_CK + LANES * (h + 1)].astype(F32), cc[...],
                                      ca[...], cb[...], ROPE_SHIFT_C)
        vc_ref[...] = p_ref[:, OFF_CV:OFF_CV + 768].astype(F32)

    vec = lambda w: pl.BlockSpec((1, w), lambda i: (0, 0))
    row = lambda w: pl.BlockSpec((tr, w), lambda i: (i, 0))
    return pl.pallas_call(
        body, name="prep_fwd", grid=(S // tr,),
        in_specs=[row(PROJ_W), vec(LAT_W), vec(128), vec(128)] + _tab_specs(tr),
        out_specs=[row(LAT_W), row(128), row(768), row(256), row(256), row(768), row(768), row(768)],
        out_shape=[_sds((S, LAT_W), CDT), _sds((S, 128), CDT), _sds((S, 768), CDT), _sds((S, 256), CDT),
                   _sds((S, 256), CDT), _sds((S, 768), F32), _sds((S, 768), F32), _sds((S, 768), F32)],
        compiler_params=_cparams(("parallel",)),
    )(proj, glat, gqn, gkn, *tabs[0], *tabs[1], *tabs[2])


def _prep_a2_fwd(qkva, kpe, tab_a, *, tr=512):
    S = qkva.shape[0]
    tr = min(tr, S)

    def body(x_ref, kpe_ref, ac, aa, ab, qa_ref, ka_ref, va_ref):
        for h in range(4):
            lo, hi = 2 * LANES * h, 2 * LANES * h + LANES
            qa_ref[:, lo:hi] = x_ref[:, lo:hi].astype(CDT)
            qa_ref[:, hi:hi + LANES] = _rope_val(x_ref[:, hi:hi + LANES], ac[...], aa[...], ab[...],
                                                 ROPE_SHIFT_AB).astype(CDT)
            ka_ref[:, lo:hi] = x_ref[:, 1024 + LANES * h:1024 + LANES * (h + 1)].astype(CDT)
            ka_ref[:, hi:hi + LANES] = kpe_ref[...]
        va_ref[...] = x_ref[:, 1536:2048].astype(CDT)

    row = lambda w: pl.BlockSpec((tr, w), lambda i: (i, 0))
    return pl.pallas_call(
        body, name="prep_a2_fwd", grid=(S // tr,),
        in_specs=[row(2048), row(128)] + _tab_specs(tr)[:3],
        out_specs=[row(1024), row(1024), row(512)],
        out_shape=[_sds((S, 1024), CDT), _sds((S, 1024), CDT), _sds((S, 512), CDT)],
        compiler_params=_cparams(("parallel",)),
    )(qkva, kpe, *tab_a)


def _prep_a2_bwd(dqa, dka, dva, tab_a, *, tr=512):
    S = dqa.shape[0]
    tr = min(tr, S)

    def body(dq_ref, dk_ref, dv_ref, ac, aa, ab, dx_ref, dkr_ref):
        dkpe = jnp.zeros((tr, LANES), F32)
        for h in range(4):
            lo, hi = 2 * LANES * h, 2 * LANES * h + LANES
            dx_ref[:, lo:hi] = dq_ref[:, lo:hi].astype(CDT)
            dx_ref[:, hi:hi + LANES] = _rope_t_val(dq_ref[:, hi:hi + LANES], ac[...], aa[...], ab[...],
                                                   ROPE_SHIFT_AB).astype(CDT)
            dx_ref[:, 1024 + LANES * h:1024 + LANES * (h + 1)] = dk_ref[:, lo:hi].astype(CDT)
            dkpe = dkpe + dk_ref[:, hi:hi + LANES]
        dx_ref[:, 1536:2048] = dv_ref[...].astype(CDT)
        dkr_ref[...] = _rope_t_val(dkpe, ac[...], aa[...], ab[...], ROPE_SHIFT_AB)

    row = lambda w: pl.BlockSpec((tr, w), lambda i: (i, 0))
    return pl.pallas_call(
        body, name="prep_a2_bwd", grid=(S // tr,),
        in_specs=[row(1024), row(1024), row(512)] + _tab_specs(tr)[:3],
        out_specs=[row(2048), row(128)],
        out_shape=[_sds((S, 2048), CDT), _sds((S, 128), F32)],
        compiler_params=_cparams(("parallel",)),
    )(dqa, dka, dva, *tab_a)


def _prep_bwd(proj, glat, gqn, gkn, tabs, dlat, dkr, dqb, dkb, dvb, dqc, dkc, dvc, *, tr=256):
    S = proj.shape[0]
    tr = min(tr, S)

    def body(p_ref, glat_ref, gqn_ref, gkn_ref, ac, aa, ab, bc, ba, bb, cc, ca, cb,
             dlat_ref, dkr_ref, dqb_ref, dkb_ref, dvb_ref, dqc_ref, dkc_ref, dvc_ref,
             dp_ref, dglat_ref, dgqn_ref, dgkn_ref):
        @pl.when(pl.program_id(0) == 0)
        def _():
            dglat_ref[...] = jnp.zeros_like(dglat_ref)
            dgqn_ref[...] = jnp.zeros_like(dgqn_ref)
            dgkn_ref[...] = jnp.zeros_like(dgkn_ref)

        x = p_ref[:, 0:LAT_W].astype(F32)
        is_q, is_kv = _lat_masks(x.shape)
        dy, g = dlat_ref[...], glat_ref[...]
        dxq, dgq = _rms_bwd_val(jnp.where(is_q, x, 0.0), g, jnp.where(is_q, dy, 0.0), Q_LORA)
        dxkv, dgkv = _rms_bwd_val(jnp.where(is_kv, x, 0.0), g, jnp.where(is_kv, dy, 0.0), KV_HI - KV_LO)
        dglat_ref[...] += _colsum(dgq + dgkv)
        dx = dxq + dxkv
        dp_ref[:, 0:LAT_W - LANES] = dx[:, 0:LAT_W - LANES].astype(CDT)
        dp_ref[:, LAT_W - LANES:LAT_W] = (dx[:, LAT_W - LANES:LAT_W] + dkr_ref[...]).astype(CDT)
        dgqn = jnp.zeros((1, LANES), F32)
        for h in range(6):
            sl = slice(LANES * h, LANES * (h + 1))
            po = slice(OFF_BQ + LANES * h, OFF_BQ + LANES * (h + 1))
            dyh = _rope_t_val(dqb_ref[:, sl], bc[...], ba[...], bb[...], ROPE_SHIFT_AB)
            dxh, dgt = _rms_bwd_val(p_ref[:, po].astype(F32), gqn_ref[...], dyh, LANES)
            dp_ref[:, po] = dxh.astype(CDT)
            dgqn = dgqn + _colsum(dgt)
        dgqn_ref[...] += dgqn
        dgkn = jnp.zeros((1, LANES), F32)
        for h in range(2):
            sl = slice(LANES * h, LANES * (h + 1))
            po = slice(OFF_BK + LANES * h, OFF_BK + LANES * (h + 1))
            dyh = _rope_t_val(dkb_ref[:, sl], bc[...], ba[...], bb[...], ROPE_SHIFT_AB)
            dxh, dgt = _rms_bwd_val(p_ref[:, po].astype(F32), gkn_ref[...], dyh, LANES)
            dp_ref[:, po] = dxh.astype(CDT)
            dgkn = dgkn + _colsum(dgt)
        dgkn_ref[...] += dgkn
        dp_ref[:, OFF_BV:OFF_BV + 256] = dvb_ref[...].astype(CDT)
        for h in range(6):
            sl = slice(LANES * h, LANES * (h + 1))
            dp_ref[:, OFF_CQ + LANES * h:OFF_CQ + LANES * (h + 1)] = _rope_t_val(
                dqc_ref[:, sl], cc[...], ca[...], cb[...], ROPE_SHIFT_C).astype(CDT)
            dp_ref[:, OFF_CK + LANES * h:OFF_CK + LANES * (h + 1)] = _rope_t_val(
                dkc_ref[:, sl], cc[...], ca[...], cb[...], ROPE_SHIFT_C).astype(CDT)
        dp_ref[:, OFF_CV:OFF_CV + 768] = dvc_ref[...].astype(CDT)

    vec = lambda w: pl.BlockSpec((1, w), lambda i: (0, 0))
    row = lambda w: pl.BlockSpec((tr, w), lambda i: (i, 0))
    return pl.pallas_call(
        body, name="prep_bwd", grid=(S // tr,),
        in_specs=[row(PROJ_W), vec(LAT_W), vec(128), vec(128)] + _tab_specs(tr)
        + [row(LAT_W), row(128), row(768), row(256), row(256), row(768), row(768), row(768)],
        out_specs=[row(PROJ_W), vec(LAT_W), vec(128), vec(128)],
        out_shape=[_sds((S, PROJ_W), CDT), _sds((1, LAT_W), F32), _sds((1, 128), F32), _sds((1, 128), F32)],
        compiler_params=_cparams(("arbitrary",)),
    )(proj, glat, gqn, gkn, *tabs[0], *tabs[1], *tabs[2], dlat, dkr, dqb, dkb, dvb, dqc, dkc, dvc)


ATTN_TK = 512
LOG2E = 1.4426950408889634
C2_H = SCALE_H * LOG2E


def _attn_fwd(q, k, v, *, H, G, dk, dv, scale, name, tq=1024, rider=None):
    S = q.shape[0]
    tq = min(tq, S)
    tk = min(ATTN_TK, S)
    c2 = scale * LOG2E

    half = tq // 2

    def body(q_ref, k_ref, v_ref, o_ref, l_ref):
        chunks = [pl.ds(c * tk, tk) for c in range(S // tk)]
        qa, qb = q_ref[0:half, :], q_ref[half:tq, :]
        rowmax = lambda s: functools.reduce(jnp.maximum, [jnp.max(sc, axis=-1, keepdims=True) for sc in s])
        s_a = [_dot(qa, k_ref[rows, :], NT) for rows in chunks]
        m_a = rowmax(s_a)
        e_a, s_b = [], []
        for c, rows in enumerate(chunks):
            e_a.append(jnp.exp2((s_a[c] - m_a) * c2))
            s_b.append(_dot(qb, k_ref[rows, :], NT))
        m_b = rowmax(s_b)
        den_a, den_b = jnp.zeros((half, 1), F32), jnp.zeros((half, 1), F32)
        acc_a, acc_b = jnp.zeros((half, dv), F32), jnp.zeros((half, dv), F32)
        e_b = []
        for c, rows in enumerate(chunks):
            acc_a = acc_a + _dot(e_a[c].astype(CDT), v_ref[rows, :], NN)
            den_a = den_a + jnp.sum(e_a[c], axis=-1, keepdims=True)
            e_b.append(jnp.exp2((s_b[c] - m_b) * c2))
        for c, rows in enumerate(chunks):
            acc_b = acc_b + _dot(e_b[c].astype(CDT), v_ref[rows, :], NN)
            den_b = den_b + jnp.sum(e_b[c], axis=-1, keepdims=True)
        o_ref[0:half, :] = acc_a * (1.0 / den_a)
        o_ref[half:tq, :] = acc_b * (1.0 / den_b)
        l_ref[0:half, :] = jnp.broadcast_to(m_a * scale + jnp.log(den_a), (half, LANES))
        l_ref[half:tq, :] = jnp.broadcast_to(m_b * scale + jnp.log(den_b), (half, LANES))

    return _carried_call(
        body, name=name, grid=(H, S // tq),
        in_specs=[pl.BlockSpec((tq, dk), lambda h, i: (i, h)), pl.BlockSpec((S, dk), lambda h, i: (0, h // G)),
                  pl.BlockSpec((S, dv), lambda h, i: (0, h // G))],
        out_specs=[pl.BlockSpec((tq, dv), lambda h, i: (i, h)), pl.BlockSpec((tq, LANES), lambda h, i: (i, h))],
        out_shape=[_sds((S, H * dv), F32), _sds((S, H * LANES), F32)], scratch_shapes=[],
        dims=("parallel", "parallel"), args=[q, k, v], rider=rider)


def _attn_bwd(q, k, v, do, lse, delta, *, H, G, dk, dv, scale, name, tq=512, rider=None):
    S = q.shape[0]
    tq = min(tq, S)
    Hkv = H // G
    c2 = scale * LOG2E

    def body(q_ref, k_ref, v_ref, do_ref, l_ref, d_ref, dq_ref, dk_ref, dv_ref):
        @pl.when((pl.program_id(1) == 0) & (pl.program_id(2) == 0))
        def _():
            dk_ref[...] = jnp.zeros_like(dk_ref)
            dv_ref[...] = jnp.zeros_like(dv_ref)

        qv, kv, dov = q_ref[...], k_ref[...], do_ref[...]
        p = jnp.exp2(_dot(qv, kv, NT) * c2 - l_ref[:, 0:1] * LOG2E)
        dp = _dot(dov, v_ref[...], NT)
        ds = (p * (dp - d_ref[:, 0:1]) * scale).astype(CDT)
        dq_ref[...] = _dot(ds, kv, NN)
        dk_ref[...] += _dot(ds, qv, TN)
        dv_ref[...] += _dot(p.astype(CDT), dov, TN)

    qi = lambda hk, g, i: (i, hk * G + g)
    return _carried_call(
        body, name=name, grid=(Hkv, G, S // tq),
        in_specs=[pl.BlockSpec((tq, dk), qi), pl.BlockSpec((S, dk), lambda hk, g, i: (0, hk)),
                  pl.BlockSpec((S, dv), lambda hk, g, i: (0, hk)), pl.BlockSpec((tq, dv), qi),
                  pl.BlockSpec((tq, LANES), qi), pl.BlockSpec((tq, LANES), qi)],
        out_specs=[pl.BlockSpec((tq, dk), qi), pl.BlockSpec((S, dk), lambda hk, g, i: (0, hk)),
                   pl.BlockSpec((S, dv), lambda hk, g, i: (0, hk))],
        out_shape=[_sds((S, H * dk), F32), _sds((S, Hkv * dk), F32), _sds((S, Hkv * dv), F32)], scratch_shapes=[],
        dims=("parallel", "arbitrary", "arbitrary"), args=[q, k, v, do, lse, delta], rider=rider)


BAND_SUB = 128
BAND_WIN = 384
BAND_UNROLL = 16


def _band_blocks(S, d):
    L = S // d
    assert L % BAND_SUB == 0 and S % (BAND_SUB * BAND_UNROLL) == 0
    return L, L // BAND_SUB, min(BAND_WIN, L)


def _band_index(blk, d, nb, L, win):
    r, jb = blk // nb, blk % nb
    l0 = jb * BAND_SUB
    w0 = jnp.clip(l0 - BAND_SUB, 0, L - win)
    return r + d * l0, r + d * w0, l0, w0


def _band_rows(start, size, d):
    return pl.ds(pl.multiple_of(start, BAND_SUB), size) if d == 1 else pl.ds(start, size, stride=d)


def _band_mask(l0, w0, win):
    rpos = l0 + lax.broadcasted_iota(jnp.int32, (BAND_SUB, win), 0)
    cpos = w0 + lax.broadcasted_iota(jnp.int32, (BAND_SUB, win), 1)
    return jnp.abs(rpos - cpos) <= BAND_HALF


def _mixc_fwd(q, k, v, rider=None):
    S, W = q.shape

    def body(q_ref, k_ref, v_ref, o_ref, l_ref, *scratch):
        ob_refs, lb_refs = scratch[0:3], scratch[3:6]
        for b, d in enumerate(DILATIONS):
            L, nb, win = _band_blocks(S, d)

            def step(it, carry, b=b, d=d, L=L, nb=nb, win=win):
                idx = [_band_index(it * BAND_UNROLL + u, d, nb, L, win) for u in range(BAND_UNROLL)]
                qrows = [_band_rows(i[0], BAND_SUB, d) for i in idx]
                krows = [_band_rows(i[1], win, d) for i in idx]
                qv = [q_ref[r, :].astype(CDT) for r in qrows]
                kw = [k_ref[r, :].astype(CDT) for r in krows]
                vw = [v_ref[r, :].astype(CDT) for r in krows]
                s = [jnp.where(_band_mask(i[2], i[3], win), _dot(a, kk, NT), NEG_INF) for i, a, kk in zip(idx, qv, kw)]
                m = [jnp.max(x, axis=-1, keepdims=True) for x in s]
                e = [jnp.exp2((x - mm) * C2_H) for x, mm in zip(s, m)]
                den = [jnp.sum(x, axis=-1, keepdims=True) for x in e]
                o = [_dot((x * (1.0 / dd)).astype(CDT), vv, NN) for x, dd, vv in zip(e, den, vw)]
                for r, ou, mm, dd in zip(qrows, o, m, den):
                    ob_refs[b][r, :] = ou
                    lb_refs[b][r, :] = jnp.broadcast_to(mm * SCALE_H + jnp.log(dd), (BAND_SUB, LANES))
                return carry

            lax.fori_loop(0, S // (BAND_SUB * BAND_UNROLL), step, 0)

        def combine(c, carry):
            rows = pl.ds(pl.multiple_of(c * 256, 256), 256)
            l0, l1, l2 = lb_refs[0][rows, :], lb_refs[1][rows, :], lb_refs[2][rows, :]
            m = jnp.maximum(jnp.maximum(l0, l1), l2)
            e0, e1, e2 = jnp.exp(l0 - m), jnp.exp(l1 - m), jnp.exp(l2 - m)
            den = e0 + e1 + e2
            inv = 1.0 / den
            o_ref[rows, :] = ((e0 * inv) * ob_refs[0][rows, :] + (e1 * inv) * ob_refs[1][rows, :]
                              + (e2 * inv) * ob_refs[2][rows, :])
            l_ref[rows, :] = m + jnp.log(den)
            return carry

        lax.fori_loop(0, S // 256, combine, 0)

    head = pl.BlockSpec((S, LANES), lambda h: (0, h))
    return _carried_call(
        body, name="mixc_fwd", grid=(W // LANES,), in_specs=[head] * 3, out_specs=[head] * 2,
        out_shape=[_sds((S, W), F32)] * 2, scratch_shapes=[pltpu.VMEM((S, LANES), F32)] * 6,
        dims=("parallel",), args=[q, k, v], rider=rider)


def _mixc_bwd(q, k, v, do, lse, dd, rider=None):
    S, W = q.shape

    def body(q_ref, k_ref, v_ref, do_ref, l_ref, d_ref, dq_ref, dk_ref, dv_ref):
        dq_ref[...] = jnp.zeros_like(dq_ref)
        dk_ref[...] = jnp.zeros_like(dk_ref)
        dv_ref[...] = jnp.zeros_like(dv_ref)
        for d in DILATIONS:
            L, nb, win = _band_blocks(S, d)

            def step(it, carry, d=d, L=L, nb=nb, win=win):
                idx = [_band_index(it * BAND_UNROLL + u, d, nb, L, win) for u in range(BAND_UNROLL)]
                qrows = [_band_rows(i[0], BAND_SUB, d) for i in idx]
                krows = [_band_rows(i[1], win, d) for i in idx]
                qv = [q_ref[r, :].astype(CDT) for r in qrows]
                dov = [do_ref[r, :].astype(CDT) for r in qrows]
                kw = [k_ref[r, :].astype(CDT) for r in krows]
                vw = [v_ref[r, :].astype(CDT) for r in krows]
                lse2 = [l_ref[r, :][:, 0:1] * LOG2E for r in qrows]
                dd = [d_ref[r, :][:, 0:1] for r in qrows]
                s = [jnp.where(_band_mask(i[2], i[3], win), _dot(a, kk, NT), NEG_INF) for i, a, kk in zip(idx, qv, kw)]
                p = [jnp.exp2(x * C2_H - ll) for x, ll in zip(s, lse2)]
                dp = [_dot(a, vv, NT) for a, vv in zip(dov, vw)]
                ds = [(pp * (x - y) * SCALE_H).astype(CDT) for pp, x, y in zip(p, dp, dd)]
                dq = [_dot(x, kk, NN) for x, kk in zip(ds, kw)]
                dk = [_dot(x, a, TN) for x, a in zip(ds, qv)]
                dv = [_dot(pp.astype(CDT), a, TN) for pp, a in zip(p, dov)]
                for u in range(BAND_UNROLL):
                    dq_ref[qrows[u], :] += dq[u]
                    dk_ref[krows[u], :] += dk[u]
                    dv_ref[krows[u], :] += dv[u]
                return carry

            lax.fori_loop(0, S // (BAND_SUB * BAND_UNROLL), step, 0)

    head = pl.BlockSpec((S, LANES), lambda h: (0, h))
    return _carried_call(
        body, name="mixc_bwd", grid=(W // LANES,), in_specs=[head] * 6, out_specs=[head] * 3,
        out_shape=[_sds((S, W), F32)] * 3, scratch_shapes=[], dims=("parallel",), args=[q, k, v, do, lse, dd],
        rider=rider)


def _outnorm_fwd(oa, ob, oc, g, *, tr=256):
    S = oa.shape[0]
    tr = min(tr, S)

    def body(a_ref, b_ref, c_ref, g_ref, m_ref):
        m_ref[:, 0:512] = (_rms_val(a_ref[...], None, 512)[0] * g_ref[:, 0:512]).astype(CDT)
        m_ref[:, 512:1280] = (_rms_val(b_ref[...], None, 768)[0] * g_ref[:, 512:1280]).astype(CDT)
        m_ref[:, 1280:2048] = (_rms_val(c_ref[...], None, 768)[0] * g_ref[:, 1280:2048]).astype(CDT)

    row = lambda w: pl.BlockSpec((tr, w), lambda i: (i, 0))
    return pl.pallas_call(
        body, name="outnorm_fwd", grid=(S // tr,),
        in_specs=[row(512), row(768), row(768), pl.BlockSpec((1, 2048), lambda i: (0, 0))],
        out_specs=row(2048), out_shape=_sds((S, 2048), CDT), compiler_params=_cparams(("parallel",)),
    )(oa, ob, oc, g)


def _outnorm_bwd(oa, ob, oc, g, dm, *, tr=256):
    S = oa.shape[0]
    tr = min(tr, S)

    def body(a_ref, b_ref, c_ref, g_ref, dm_ref, doa_ref, dob_ref, doc_ref, da_ref, db_ref, dc_ref, dg_ref):
        @pl.when(pl.program_id(0) == 0)
        def _():
            dg_ref[...] = jnp.zeros_like(dg_ref)

        for o_ref, do_ref, d_ref, lo, w in ((a_ref, doa_ref, da_ref, 0, 512), (b_ref, dob_ref, db_ref, 512, 768),
                                            (c_ref, doc_ref, dc_ref, 1280, 768)):
            o = o_ref[...]
            dmv = dm_ref[:, lo:lo + w]
            do, _ = _rms_bwd_val(o, None, dmv * g_ref[:, lo:lo + w], w)
            r = lax.rsqrt(jnp.sum(o * o, axis=-1, keepdims=True) * (1.0 / w) + EPS)
            dg_ref[:, lo:lo + w] += _colsum(dmv * (o * r))
            do_ref[...] = do.astype(do_ref.dtype)
            for h in range(w // LANES):
                sl = slice(LANES * h, LANES * (h + 1))
                d_ref[:, sl] = jnp.broadcast_to(jnp.sum(do[:, sl] * o[:, sl], axis=-1, keepdims=True), (tr, LANES))

    row = lambda w: pl.BlockSpec((tr, w), lambda i: (i, 0))
    vec = pl.BlockSpec((1, 2048), lambda i: (0, 0))
    return pl.pallas_call(
        body, name="outnorm_bwd", grid=(S // tr,),
        in_specs=[row(512), row(768), row(768), vec, row(2048)],
        out_specs=[row(512), row(768), row(768), row(512), row(768), row(768), vec],
        out_shape=[_sds((S, 512), CDT), _sds((S, 768), CDT), _sds((S, 768), F32), _sds((S, 512), F32),
                   _sds((S, 768), F32), _sds((S, 768), F32), _sds((1, 2048), F32)],
        compiler_params=_cparams(("arbitrary",)),
    )(oa, ob, oc, g, dm)


def _row_tile(r, c, itemsize, limit=1 << 20):
    best = 16
    for t in range(16, r + 1, 16):
        if r % t == 0 and t * c * itemsize <= limit:
            best = t
    return best


def _rs_chip_sum(g, got, *, name):
    _, _, r, c = g.shape
    tr = _row_tile(r, c, 2)
    core = lax.axis_index("c").astype(jnp.int32).reshape(1)

    def body(c_ref, a_ref, b_ref, o_ref):
        o_ref[...] = (a_ref[...].astype(F32) + b_ref[...].astype(F32)).astype(o_ref.dtype)

    spec = pltpu.PrefetchScalarGridSpec(
        num_scalar_prefetch=1, grid=(4, r // tr),
        in_specs=[pl.BlockSpec((None, None, tr, c), lambda k, i, cr: (k, cr[0], i, 0)),
                  pl.BlockSpec((None, tr, c), lambda k, i, cr: (k, i, 0))],
        out_specs=pl.BlockSpec((None, tr, c), lambda k, i, cr: (k, i, 0)))
    return pl.pallas_call(body, name=name, grid_spec=spec, out_shape=_sds((4, r, c), g.dtype),
                          compiler_params=_cparams(("parallel", "parallel")))(core, g, got)


def _rs_final_sum(landed, *, name):
    depth = len(landed)
    _, r, c = landed[0].shape
    tr = _row_tile(r, c, 4)

    def body(*refs):
        o_ref = refs[depth]
        for k in range(depth):
            @pl.when(pl.program_id(0) == k)
            def _(r_ref=refs[k]):
                o_ref[...] = ((r_ref[0].astype(F32) + r_ref[1].astype(F32)) + r_ref[2].astype(F32)
                              ) + r_ref[3].astype(F32)

    in_specs = [pl.BlockSpec((4, tr, c), lambda l, i, k=k: (0, jnp.where(l == k, i, 0), 0)) for k in range(depth)]
    return pl.pallas_call(
        body, name=name, grid=(depth, r // tr), in_specs=in_specs,
        out_specs=pl.BlockSpec((None, tr, c), lambda l, i: (l, i, 0)), out_shape=_sds((depth, r, c), F32),
        compiler_params=_cparams(("arbitrary", "arbitrary")))(*landed)


def _all_reduce_small(v):
    R = v.shape[0]

    def body(v_ref, out_ref, buf_ref, send_sems, recv_sems):
        x, y, c = lax.axis_index("x"), lax.axis_index("y"), lax.axis_index("c")
        me = 4 * x + 2 * y + c
        buf_ref[me] = v_ref[...]
        peers = []
        for r in range(1, 8):
            px, py, pc = x ^ (r >> 2), y ^ ((r >> 1) & 1), c ^ (r & 1)
            peers.append((r, (px, py, pc), 4 * px + 2 * py + pc))
        sends = [pltpu.make_async_remote_copy(
            src_ref=v_ref, dst_ref=buf_ref.at[me], send_sem=send_sems.at[r - 1], recv_sem=recv_sems.at[r - 1],
            device_id=dev, device_id_type=MESH) for r, dev, _ in peers]
        for cp in sends:
            cp.start()
        for r, dev, idx in peers:
            pltpu.make_async_remote_copy(
                src_ref=v_ref, dst_ref=buf_ref.at[idx], send_sem=send_sems.at[r - 1], recv_sem=recv_sems.at[r - 1],
                device_id=dev, device_id_type=MESH).wait_recv()
        for cp in sends:
            cp.wait_send()
        acc = buf_ref[0]
        for k in range(1, 8):
            acc = acc + buf_ref[k]
        out_ref[...] = acc

    vm = pl.BlockSpec(memory_space=pltpu.VMEM)
    return pl.pallas_call(
        body, name="all_reduce_small", out_shape=_sds((R, LANES), F32), in_specs=[vm], out_specs=vm,
        scratch_shapes=[pltpu.VMEM((8, R, LANES), F32), pltpu.SemaphoreType.DMA((7,)), pltpu.SemaphoreType.DMA((7,))],
    )(v)


def _adamw(w, g, m, v, *, name):
    R, C = w.shape
    tr = R
    for cand in (1024, 512, 256, 128, 64, 32, 16, 8):
        if R % cand == 0 and cand * C * 4 <= 2 * 1024 * 1024:
            tr = cand
            break

    def body(w_ref, g_ref, m_ref, v_ref, d_ref, nm_ref, nv_ref):
        gv = g_ref[...]
        mn = ADAM_B1 * m_ref[...] + (1.0 - ADAM_B1) * gv
        vn = ADAM_B2 * v_ref[...] + (1.0 - ADAM_B2) * (gv * gv)
        m_hat = mn / (1.0 - ADAM_B1 ** ADAM_STEP)
        v_hat = vn / (1.0 - ADAM_B2 ** ADAM_STEP)
        d_ref[...] = -ADAM_LR * (m_hat / (jnp.sqrt(v_hat) + ADAM_EPS) + ADAM_WD * w_ref[...])
        nm_ref[...] = mn
        nv_ref[...] = vn

    blk = pl.BlockSpec((tr, C), lambda i: (i, 0))
    return pl.pallas_call(
        body, name=name, grid=(R // tr,), in_specs=[blk] * 4, out_specs=[blk] * 3,
        out_shape=[_sds((R, C), F32)] * 3, compiler_params=_cparams(("parallel",)))(w, g, m, v)


def _wuq_pad(w):
    w = w.reshape(448, 4, 192)
    z = jnp.zeros((448, 4, 64), w.dtype)
    return jnp.concatenate([w[:, :, 0:128], z, w[:, :, 128:192]], axis=2).reshape(448, 1024)


def _wuq_unpad(w):
    w = w.reshape(448, 4, 256)
    return jnp.concatenate([w[:, :, 0:128], w[:, :, 192:256]], axis=2).reshape(448, 768)


def _wukv_perm(w):
    return w.reshape(512, 4, 2, 128).transpose(0, 2, 1, 3).reshape(512, 1024)


def _wukv_unperm(w):
    return w.reshape(512, 2, 4, 128).transpose(0, 2, 1, 3).reshape(512, 1024)


def _lat_weight(w_uq, w_ukv):
    z = lambda r, c: jnp.zeros((r, c), w_uq.dtype)
    top = jnp.concatenate([_wuq_pad(w_uq), z(448, 1024)], axis=1)
    mid = jnp.concatenate([z(512, 1024), _wukv_perm(w_ukv)], axis=1)
    return jnp.concatenate([top, mid, z(64, 2048)], axis=0)


def _lat_weight_grads(dw):
    return _wuq_unpad(dw[0:KV_LO, 0:1024]), _wukv_unperm(dw[KV_LO:KV_HI, 1024:2048])


def _comm_shards(w_in, w_uq, w_ukv, w_out, w_ff1, w_ff2):
    lat = jnp.concatenate([w_uq.reshape(UQ_ROWS, LANES), w_ukv.reshape(512, LANES)], axis=0)
    return [w_in.T, w_ff1.T, w_out, w_ff2, lat]


def _from_comm_shards(parts):
    w_in_t, w_ff1_t, w_out, w_ff2, lat = parts
    lead = lat.shape[:-2]
    return {"w_in": jnp.swapaxes(w_in_t, -1, -2), "w_ff1": jnp.swapaxes(w_ff1_t, -1, -2), "w_out": w_out,
            "w_ff2": w_ff2, "w_uq": lat[..., 0:UQ_ROWS, :].reshape(lead + (448, 96)),
            "w_ukv": lat[..., UQ_ROWS:, :].reshape(lead + (512, 128))}


def _early_weights(g_in_t, g_lat):
    w_uq = g_lat[:, 0:UQ_ROWS].reshape(8, 448, 96).transpose(1, 0, 2).reshape(448, 768)
    w_ukv = g_lat[:, UQ_ROWS:].reshape(8, 512, 128).transpose(1, 0, 2).reshape(512, 1024)
    return g_in_t.reshape(PROJ_W, D_MODEL), _lat_weight(w_uq, w_ukv)


def _layer_fwd(x, W, G, tabs, plan=None):
    W = dict(W)
    sh, nxt, ff1_half = plan if plan is not None else (None, None, None)
    first = plan is not None and ff1_half is None
    early_next = None if nxt is None else [nxt["w_in_t"], nxt["lat"]]
    ag1 = lambda arrays, tag, **kw: None if (plan is None or not arrays) else _ag_first_rider(arrays, tag, **kw)
    ag2 = lambda arrays, tag: None if (plan is None or not arrays) else _ag_second_rider(arrays, tag)
    s = {"x0": x}
    s["h1"] = _rms_fwd(x, G["ln1_g"], name="rms1_fwd")
    mm_in = functools.partial(_matmul, s["h1"], W["w_in_t"], mode="nt", tm=1024, tn=768, tk=2048, out_dtype=CDT,
                              name="mm_in")
    if first:
        s["proj"], ff1_rows0 = mm_in(rider=ag1([sh["w_ff1_t"]], "_ff1a", part=(0, 2)))
    else:
        s["proj"] = mm_in()
    s["lat"], kpe, s["qb"], s["kb"], s["vb"], s["qc"], s["kc"], s["vc"] = _prep_fwd(
        s["proj"], G["glat"], G["gqn"], G["gkn"], tabs)
    qkva = _matmul(s["lat"], W["w_lat"], mode="nn", tm=1024, tn=1024, tk=1024, out_dtype=F32, name="mm_lat")
    s["qa"], s["ka"], s["va"] = _prep_a2_fwd(qkva, kpe, tabs[0])
    early, ff1_half_next = None, None
    attn_a = functools.partial(_attn_fwd, s["qa"], s["ka"], s["va"], H=4, G=1, dk=256, dv=128, scale=SCALE_A,
                               name="attn_a_fwd")
    attn_b = functools.partial(_attn_fwd, s["qb"], s["kb"], s["vb"], H=6, G=3, dk=128, dv=128, scale=SCALE_H,
                               name="attn_b_fwd")
    mixc = functools.partial(_mixc_fwd, s["qc"], s["kc"], s["vc"])
    if plan is None:
        (s["oa"], s["lse_a"]), _ = attn_a()
        (s["ob"], s["lse_b"]), _ = attn_b()
        (s["oc"], s["lse_c"]), _ = mixc()
    else:
        if first:
            (s["oa"], s["lse_a"]), got = attn_a(rider=ag1([sh["w_ff1_t"]], "_ff1b", part=(1, 2), into=ff1_rows0))
            ff1_half, early_half = got[0], []
        else:
            (s["oa"], s["lse_a"]), early_half = attn_a(rider=ag1(early_next, "_early"))
        (s["ob"], s["lse_b"]), got_b = attn_b(
            rider=_join(_join(ag1([sh["w_out"]], "_out"), ag1([sh["w_ff2"]], "_ff2a", part=(0, 2))),
                        ag2([ff1_half], "_ff1")))
        W["w_ff1_t"] = got_b[2].reshape(D_FF, D_MODEL)
        (s["oc"], s["lse_c"]), got_c = mixc(
            rider=_join(_join(ag2(got_b[0:1], "_out"), ag1([sh["w_ff2"]], "_ff2b", part=(1, 2), into=got_b[1:2])),
                        ag2(early_half, "_early")))
        W["w_out"] = got_c[0].reshape(D_MODEL, D_MODEL)
        ff2_half, early = got_c[1], (got_c[2:4] or None)
    s["mixed"] = _outnorm_fwd(s["oa"], s["ob"], s["oc"], G["g_out"])
    mm_out = functools.partial(_matmul, s["mixed"], W["w_out"], mode="nn", tm=1024, tn=1024, tk=2048, out_dtype=F32,
                               name="mm_out", epi="residual", extra=x)
    if plan is None:
        s["x1"] = mm_out()
    else:
        s["x1"], got = mm_out(rider=ag2([ff2_half], "_ff2"))
        W["w_ff2"] = got[0].reshape(D_FF, D_MODEL)
    s["h2"] = _rms_fwd(s["x1"], G["ln2_g"], name="rms2_fwd")
    ff1 = functools.partial(_matmul, s["h2"], W["w_ff1_t"], mode="nt", tm=1024, tn=1024, tk=2048, out_dtype=CDT,
                            name="mm_ff1", epi="relu2")
    if nxt is None:
        s["z"], s["u"] = ff1()
        x2 = _matmul(s["u"], W["w_ff2"], mode="nn", tm=1024, tn=1024, tk=2048, out_dtype=F32, name="mm_ff2",
                     epi="residual", extra=s["x1"])
    else:
        (s["z"], s["u"]), got = ff1(rider=_join(ag1(early_next, "_early") if first else None,
                                                ag1([nxt["w_ff1_t"]], "_ff1a", part=(0, 2))))
        early_half, rows0 = (got[0:2], got[2:3]) if first else ([], got[0:1])
        x2, got = _matmul(s["u"], W["w_ff2"], mode="nn", tm=1024, tn=1024, tk=2048, out_dtype=F32, name="mm_ff2",
                          epi="residual", extra=s["x1"],
                          rider=_join(ag2(early_half, "_early"),
                                      ag1([nxt["w_ff1_t"]], "_ff1b", part=(1, 2), into=rows0)))
        if first:
            early, got = got[0:2], got[2:3]
        ff1_half_next = got[0]
    return x2, s, W, early, ff1_half_next


def _by_destination(dw, name):
    return dw.reshape((4, 2) + COMM_SHAPE[name])


def _early_by_destination(dw_in_t, dw_lat):
    dw_uq, dw_ukv = _lat_weight_grads(dw_lat)
    lat = jnp.concatenate([dw_uq.reshape(448, 8, 96).transpose(1, 0, 2).reshape(8, UQ_ROWS, LANES),
                           dw_ukv.reshape(512, 8, 128).transpose(1, 0, 2)], axis=1)
    return [_by_destination(dw_in_t, "w_in_t"), _by_destination(lat, "lat")]


def _layer_bwd(dx2, dx2b, s, W, G, tabs, scatter=False, pending=None):
    dw, dg, landed = {}, {}, {}
    ff2_dx = functools.partial(_matmul, dx2b, W["w_ff2"], mode="nt", tm=1024, tn=1024, tk=2048, out_dtype=CDT,
                               name="mm_ff2_dx", epi="drelu2", extra=s["z"])
    if pending is None:
        dz = ff2_dx()
    else:
        dz, got = ff2_dx(rider=_rs_sibling_rider(pending, "_early"))
        chip = [_rs_chip_sum(g, r, name="rs_chip_sum_" + n) for g, r, n in zip(pending, got, ("w_in_t", "lat"))]
    dw["w_ff2"] = _matmul(s["u"], dx2b, mode="tn", tm=2048, tn=1024, tk=2048, out_dtype=CDT, name="mm_ff2_dw")
    ff1_dx = functools.partial(_matmul, dz, W["w_ff1_t"], mode="nn", tm=1024, tn=1024, tk=2048, out_dtype=F32,
                               name="mm_ff1_dx")
    if pending is None:
        dh2 = ff1_dx()
    else:
        dh2, got = ff1_dx(rider=_rs_chip_rider(chip, "_early"))
        landed["above_w_in_t"], landed["above_lat"] = got
    dw["w_ff1_t"] = _matmul(dz, s["h2"], mode="tn", tm=2048, tn=1024, tk=2048, out_dtype=CDT, name="mm_ff1_dw")
    dx1, dx1b, dg["ln2_g"] = _rms_bwd(s["x1"], G["ln2_g"], dh2, dx2, name="rms2_bwd")
    dmixed = _matmul(dx1b, W["w_out"], mode="nt", tm=1024, tn=1024, tk=2048, out_dtype=F32, name="mm_out_dx")
    out_dw = functools.partial(_matmul, s["mixed"], dx1b, mode="tn", tm=1024, tn=1024, tk=2048, out_dtype=CDT,
                               name="mm_out_dw")
    if not scatter:
        dw["w_out"] = out_dw()
        riders = [None, None, None]
    else:
        g_ff = [_by_destination(dw["w_ff2"], "w_ff2"), _by_destination(dw["w_ff1_t"], "w_ff1_t")]
        dw["w_out"], got = out_dw(rider=_rs_sibling_rider(g_ff, "_ff"))
        chip_ff2 = _rs_chip_sum(g_ff[0], got[0], name="rs_chip_sum_w_ff2")
        chip_ff1 = _rs_chip_sum(g_ff[1], got[1], name="rs_chip_sum_w_ff1_t")
        g_out = [_by_destination(dw["w_out"], "w_out")]
        riders = [_rs_chip_rider([chip_ff2], "_ff2"),
                  _join(_rs_chip_rider([chip_ff1], "_ff1"), _rs_sibling_rider(g_out, "_out")), None]
    doa, dob, doc, dla, dlb, dlc, dg["g_out"] = _outnorm_bwd(s["oa"], s["ob"], s["oc"], G["g_out"], dmixed)
    (dqa, dka, dva), got = _attn_bwd(s["qa"], s["ka"], s["va"], doa, s["lse_a"], dla, H=4, G=1, dk=256, dv=128,
                                     scale=SCALE_A, name="attn_a_bwd", rider=riders[0])
    if scatter:
        landed["w_ff2"] = got[0]
    (dqb, dkb, dvb), got = _attn_bwd(s["qb"], s["kb"], s["vb"], dob, s["lse_b"], dlb, H=6, G=3, dk=128, dv=128,
                                     scale=SCALE_H, name="attn_b_bwd", rider=riders[1])
    if scatter:
        landed["w_ff1_t"] = got[0]
        riders[2] = _rs_chip_rider([_rs_chip_sum(g_out[0], got[1], name="rs_chip_sum_w_out")], "_out")
    (dqc, dkc, dvc), got = _mixc_bwd(s["qc"], s["kc"], s["vc"], doc, s["lse_c"], dlc, rider=riders[2])
    if scatter:
        landed["w_out"] = got[0]
    dqkva, dkr = _prep_a2_bwd(dqa, dka, dva, tabs[0])
    dlat = _matmul(dqkva, W["w_lat"], mode="nt", tm=1024, tn=1024, tk=2048, out_dtype=F32, name="mm_lat_dx")
    dw["w_lat"] = _matmul(s["lat"], dqkva, mode="tn", tm=1024, tn=2048, tk=2048, out_dtype=CDT, name="mm_lat_dw")
    dproj, dg["glat"], dg["gqn"], dg["gkn"] = _prep_bwd(
        s["proj"], G["glat"], G["gqn"], G["gkn"], tabs, dlat, dkr, dqb, dkb, dvb, dqc, dkc, dvc)
    dh1 = _matmul(dproj, W["w_in_t"], mode="nn", tm=1024, tn=1024, tk=2304, out_dtype=F32, name="mm_in_dx")
    dw["w_in_t"] = _matmul(dproj, s["h1"], mode="tn", tm=1536, tn=1024, tk=2048, out_dtype=CDT, name="mm_in_dw")
    dx0, dx0b, dg["ln1_g"] = _rms_bwd(s["x0"], G["ln1_g"], dh1, dx1, name="rms1_bwd")
    return dx0, dx0b, dw, dg, landed


def _layer_gains(l, ln1_g, g_q_a, g_kv_a, g_qn_b, g_kn_b, g_out, ln2_g):
    return {"ln1_g": ln1_g[l], "ln2_g": ln2_g[l], "g_out": g_out[l].reshape(1, 2048),
            "glat": jnp.concatenate([g_q_a[l], g_kv_a[l], jnp.zeros((LAT_W - KV_HI,), F32)]).reshape(1, LAT_W),
            "gqn": g_qn_b[l].reshape(1, 128), "gkn": g_kn_b[l].reshape(1, 128)}


def _gain_grads(dg):
    glat = dg["glat"].reshape(-1)
    return {"ln1_g": dg["ln1_g"].reshape(-1), "g_q_a": glat[0:KV_LO], "g_kv_a": glat[KV_LO:KV_HI],
            "g_qn_b": dg["gqn"].reshape(-1), "g_kn_b": dg["gkn"].reshape(-1), "g_out": dg["g_out"].reshape(-1),
            "ln2_g": dg["ln2_g"].reshape(-1)}


def _local_step(x, tgt, weights, gains, ln_f_g):
    S = x.shape[0]
    tabs = _rope_tables(S)
    depth = len(weights)
    saved = []
    for l in range(depth):
        x, s, _, _, _ = _layer_fwd(x, weights[l], gains[l], tabs)
        saved.append(s)
    loss, dx, dxb, dlnf = _loss_head(x, ln_f_g, tgt)
    dws, dgs = [None] * depth, [None] * depth
    for l in reversed(range(depth)):
        dx, dxb, dws[l], dgs[l], _ = _layer_bwd(dx, dxb, saved[l], weights[l], gains[l], tabs)
    return loss, dx, dws, dgs, dlnf


SMALL_SIZES = (("ln1_g", 2048), ("g_q_a", 448), ("g_kv_a", 512), ("g_qn_b", 128), ("g_kn_b", 128), ("g_out", 2048),
               ("ln2_g", 2048))


def _pack_small(per_layer, ln_f):
    flat = jnp.concatenate([per_layer[n].reshape(-1) for n, _ in SMALL_SIZES] + [ln_f.reshape(-1)])
    rows = -(-flat.shape[0] // (8 * LANES)) * 8
    return jnp.concatenate([flat, jnp.zeros((rows * LANES - flat.shape[0],), F32)]).reshape(rows, LANES)


def _unpack_small(packed, depth):
    flat, out, lo = packed.reshape(-1), {}, 0
    for n, w in SMALL_SIZES:
        out[n] = flat[lo:lo + depth * w].reshape(depth, w)
        lo += depth * w
    out["ln_f_g"] = flat[lo:lo + 2048]
    return out


def kernel(x, ln1_g, w_in, g_q_a, w_uq, g_kv_a, w_ukv, g_qn_b, g_kn_b, g_out, w_out, ln2_g, w_ff1, w_ff2, ln_f_g, loss_target, m_ln1_g, m_w_in, m_g_q_a, m_w_uq, m_g_kv_a, m_w_ukv, m_g_qn_b, m_g_kn_b, m_g_out, m_w_out, m_ln2_g, m_w_ff1, m_w_ff2, m_ln_f_g, v_ln1_g, v_w_in, v_g_q_a, v_w_uq, v_g_kv_a, v_w_ukv, v_g_qn_b, v_g_kn_b, v_g_out, v_w_out, v_ln2_g, v_w_ff1, v_w_ff2, v_ln_f_g):
    depth = w_in.shape[0]
    S = x.shape[1]
    big_w = {"w_in": w_in, "w_uq": w_uq, "w_ukv": w_ukv, "w_out": w_out, "w_ff1": w_ff1, "w_ff2": w_ff2}
    big_m = {"w_in": m_w_in, "w_uq": m_w_uq, "w_ukv": m_w_ukv, "w_out": m_w_out, "w_ff1": m_w_ff1, "w_ff2": m_w_ff2}
    big_v = {"w_in": v_w_in, "w_uq": v_w_uq, "w_ukv": v_w_ukv, "w_out": v_w_out, "w_ff1": v_w_ff1, "w_ff2": v_w_ff2}
    small_w = {"ln1_g": ln1_g, "g_q_a": g_q_a, "g_kv_a": g_kv_a, "g_qn_b": g_qn_b, "g_kn_b": g_kn_b, "g_out": g_out,
               "ln2_g": ln2_g}
    small_m = {"ln1_g": m_ln1_g, "g_q_a": m_g_q_a, "g_kv_a": m_g_kv_a, "g_qn_b": m_g_qn_b, "g_kn_b": m_g_kn_b,
               "g_out": m_g_out, "ln2_g": m_ln2_g}
    small_v = {"ln1_g": v_ln1_g, "g_q_a": v_g_q_a, "g_kv_a": v_g_kv_a, "g_qn_b": v_g_qn_b, "g_kn_b": v_g_kn_b,
               "g_out": v_g_out, "ln2_g": v_ln2_g}

    shards = [dict(zip(COMM, _comm_shards(*[big_w[n][l].astype(CDT) for n in BIG]))) for l in range(depth)]
    early_shards = [[sh["w_in_t"], sh["lat"]] for sh in shards]
    early = _run_rider(_ag_second_rider(_run_rider(_ag_first_rider(early_shards[0], "_early")), "_early"))
    gains = [_layer_gains(l, ln1_g, g_q_a, g_kv_a, g_qn_b, g_kn_b, g_out, ln2_g) for l in range(depth)]
    tabs = _rope_tables(S)

    h = x.reshape(S, D_MODEL)
    saved, weights, f_half = [], [], None
    for l in range(depth):
        W = dict(zip(("w_in_t", "w_lat"), _early_weights(*early)))
        h, s, W, early, f_half = _layer_fwd(h, W, gains[l], tabs,
                                            plan=(shards[l], shards[l + 1] if l + 1 < depth else None, f_half))
        saved.append(s)
        weights.append(W)
    loss_part, dx, dxb, dlnf = _loss_head(h, ln_f_g, loss_target.reshape(S, D_MODEL))
    loss = lax.psum(loss_part[0, 0], ("x", "y", "c"))

    dgs, landed, pending = [None] * depth, [None] * depth, None
    for l in reversed(range(depth)):
        dx, dxb, dw, dgs[l], landed[l] = _layer_bwd(dx, dxb, saved[l], weights[l], gains[l], tabs, scatter=True,
                                                    pending=pending)
        if pending is not None:
            landed[l + 1]["w_in_t"], landed[l + 1]["lat"] = landed[l].pop("above_w_in_t"), landed[l].pop("above_lat")
        pending = _early_by_destination(dw["w_in_t"], dw["w_lat"])
    got = _run_rider(_rs_sibling_rider(pending, "_early"))
    chip = [_rs_chip_sum(g, r, name="rs_chip_sum_" + n) for g, r, n in zip(pending, got, ("w_in_t", "lat"))]
    landed[0]["w_in_t"], landed[0]["lat"] = _run_rider(_rs_chip_rider(chip, "_early"))
    grad_x = dx.reshape(1, S, D_MODEL)

    big_g = _from_comm_shards([_rs_final_sum([landed[l][n] for l in range(depth)], name="rs_final_sum_" + n)
                               for n in COMM])

    named = [_gain_grads(dgs[l]) for l in range(depth)]
    per_layer = {n: jnp.stack([named[l][n] for l in range(depth)]) for n, _ in SMALL_SIZES}
    small_g = _unpack_small(_all_reduce_small(_pack_small(per_layer, dlnf.reshape(-1))), depth)

    upd = {}
    for n in BIG:
        shp = big_w[n].shape
        two_d = (shp[0] * shp[1], shp[2])
        d, nm, nv = _adamw(big_w[n].reshape(two_d), big_g[n].reshape(two_d), big_m[n].reshape(two_d),
                           big_v[n].reshape(two_d), name="adamw_" + n)
        upd[n] = (d.reshape(shp), nm.reshape(shp), nv.reshape(shp))
    small_w["ln_f_g"], small_m["ln_f_g"], small_v["ln_f_g"] = ln_f_g, m_ln_f_g, v_ln_f_g
    names_small = [n for n, _ in SMALL_SIZES]
    pw = _pack_small({n: small_w[n] for n in names_small}, small_w["ln_f_g"])
    pg = _pack_small({n: small_g[n] for n in names_small}, small_g["ln_f_g"])
    pm = _pack_small({n: small_m[n] for n in names_small}, small_m["ln_f_g"])
    pv = _pack_small({n: small_v[n] for n in names_small}, small_v["ln_f_g"])
    d, nm, nv = _adamw(pw, pg, pm, pv, name="adamw_small")
    sd, snm, snv = _unpack_small(d, depth), _unpack_small(nm, depth), _unpack_small(nv, depth)
    for n in names_small + ["ln_f_g"]:
        upd[n] = (sd[n], snm[n], snv[n])

    order = ["ln1_g", "w_in", "g_q_a", "w_uq", "g_kv_a", "w_ukv", "g_qn_b", "g_kn_b", "g_out", "w_out", "ln2_g", "w_ff1",
             "w_ff2", "ln_f_g"]
    grads = {**big_g, **small_g}
    return (loss, grad_x, *[grads[n] for n in order], *[upd[n][0] for n in order], *[upd[n][1] for n in order],
            *[upd[n][2] for n in order])
```
